```python
import jax, jax.numpy as jnp
from jax import lax
import numpy as np

D_MODEL = 1024
BATCH = 8
SEQ = 8192
DEPTH = 1

N_HEADS_A = 8
N_KV_HEADS_A = 2
HEAD_DIM_A = 64
IDX_HEADS = 8
IDX_DIM = 64
TOPK_MAX = 256
Q_BLOCK = 128
CONV_CH = 512
CONV_WIDTH = 31
N_MEM = 256
MEM_HEADS = 4
MEM_HEAD_DIM = 128
N_BRANCHES = 3
N_GROUPS = 4
EXPERTS_PER_GROUP = 8
N_EXPERTS = N_GROUPS * EXPERTS_PER_GROUP
TOP_K_INNER = 2
D_FF_EXPERT = 256

ROPE_THETA = 10000.0
EPS = 1e-6

SPLIT_SIZES = (
    N_HEADS_A * HEAD_DIM_A,
    N_KV_HEADS_A * HEAD_DIM_A,
    N_KV_HEADS_A * HEAD_DIM_A,
    IDX_HEADS * IDX_DIM,
    IDX_DIM,
    IDX_HEADS,
    2 * CONV_CH,
    MEM_HEADS * MEM_HEAD_DIM,
    N_BRANCHES * D_MODEL,
)
IN_COLS = sum(SPLIT_SIZES)

kernel_name = "hybrid_dsa_conformer_memory_hiermoe"


def rmsnorm(x, g):
    xf = x.astype(jnp.float32)
    y = xf * lax.rsqrt(jnp.mean(xf * xf, axis=-1, keepdims=True) + EPS)
    return (y * g.astype(jnp.float32)).astype(x.dtype)


def layernorm(x, g, b):
    xf = x.astype(jnp.float32)
    mu = jnp.mean(xf, axis=-1, keepdims=True)
    var = jnp.mean(jnp.square(xf - mu), axis=-1, keepdims=True)
    y = (xf - mu) * lax.rsqrt(var + EPS)
    return (y * g.astype(jnp.float32) + b.astype(jnp.float32)).astype(x.dtype)


def rope(x, pos):
    d = x.shape[-1]
    half = d // 2
    inv = ROPE_THETA ** (-jnp.arange(half, dtype=jnp.float32) / half)
    ang = pos.astype(jnp.float32)[:, None] * inv[None, :]
    cos = jnp.cos(ang)[None, :, None, :]
    sin = jnp.sin(ang)[None, :, None, :]
    xf = x.astype(jnp.float32)
    x1, x2 = xf[..., :half], xf[..., half:]
    out = jnp.concatenate([x1 * cos - x2 * sin, x2 * cos + x1 * sin], axis=-1)
    return out.astype(x.dtype)


def dsa_attention(q, k, v, q_idx, k_idx, w_idx, topk):
    B, S = q.shape[0], q.shape[1]
    G = N_HEADS_A // N_KV_HEADS_A
    n_blocks = S // Q_BLOCK
    key_pos = jnp.arange(S)
    w_idx = w_idx.astype(jnp.float32) * (IDX_HEADS ** -0.5)
    scale = HEAD_DIM_A ** -0.5

    def block(i):
        start = i * Q_BLOCK
        qb = lax.dynamic_slice_in_dim(q, start, Q_BLOCK, axis=1)
        qib = lax.dynamic_slice_in_dim(q_idx, start, Q_BLOCK, axis=1)
        wib = lax.dynamic_slice_in_dim(w_idx, start, Q_BLOCK, axis=1)
        tpos = start + jnp.arange(Q_BLOCK)
        logits = jnp.einsum('bqhd,bsd->bqhs', qib, k_idx,
                            preferred_element_type=jnp.float32) * (IDX_DIM ** -0.5)
        score = jnp.einsum('bqh,bqhs->bqs', wib, jax.nn.relu(logits))
        causal = key_pos[None, :] <= tpos[:, None]
        score = jnp.where(causal[None], score, -jnp.inf)
        _, sel = lax.top_k(score, topk)
        valid = sel <= tpos[None, :, None]
        kg = jax.vmap(lambda kk, ii: kk[ii])(k, sel)
        vg = jax.vmap(lambda vv, ii: vv[ii])(v, sel)
        qg = qb.reshape(B, Q_BLOCK, N_KV_HEADS_A, G, HEAD_DIM_A)
        s = jnp.einsum('bqhgd,bqkhd->bqhgk', qg, kg,
                       preferred_element_type=jnp.float32) * scale
        s = jnp.where(valid[:, :, None, None, :], s, -jnp.inf)
        p = jax.nn.softmax(s, axis=-1)
        o = jnp.einsum('bqhgk,bqkhd->bqhgd', p.astype(vg.dtype), vg)
        return o.reshape(B, Q_BLOCK, N_HEADS_A * HEAD_DIM_A)

    out = lax.map(block, jnp.arange(n_blocks))
    return jnp.moveaxis(out, 0, 1).reshape(B, S, N_HEADS_A * HEAD_DIM_A)


def conformer_conv(conv_in, conv_w, conv_b, ln_g, ln_b):
    a, gte = jnp.split(conv_in, 2, axis=-1)
    u = a * jax.nn.sigmoid(gte)
    y = lax.conv_general_dilated(
        u, conv_w, window_strides=(1,), padding=[(CONV_WIDTH - 1, 0)],
        dimension_numbers=('NWC', 'WIO', 'NWC'), feature_group_count=CONV_CH)
    y = y + conv_b
    return jax.nn.silu(layernorm(y, ln_g, ln_b))


def memory_attention(q_m, mem, g_mem, w_mem_kv, g_qm, g_km):
    B, S = q_m.shape[0], q_m.shape[1]
    M = mem.shape[1]
    kv = jnp.einsum('bmd,dc->bmc', rmsnorm(mem, g_mem), w_mem_kv)
    k_m, v_m = jnp.split(kv, 2, axis=-1)
    k_m = rmsnorm(k_m.reshape(B, M, MEM_HEADS, MEM_HEAD_DIM), g_km)
    v_m = v_m.reshape(B, M, MEM_HEADS, MEM_HEAD_DIM)
    q = rmsnorm(q_m.reshape(B, S, MEM_HEADS, MEM_HEAD_DIM), g_qm)
    s = jnp.einsum('bshd,bmhd->bhsm', q, k_m,
                   preferred_element_type=jnp.float32) * (MEM_HEAD_DIM ** -0.5)
    p = jax.nn.softmax(s, axis=-1)
    o = jnp.einsum('bhsm,bmhd->bshd', p.astype(v_m.dtype), v_m)
    return o.reshape(B, S, MEM_HEADS * MEM_HEAD_DIM)


def hier_moe(h, w_rg, b_rg, w_re, b_re, w_up, w_down):
    T = h.shape[0]
    g_logits = (h @ w_rg + b_rg).astype(jnp.float32)
    p_g = jax.nn.softmax(g_logits, axis=-1)
    g_sel = jnp.argmax(g_logits, axis=-1)
    p_sel = jnp.take_along_axis(p_g, g_sel[:, None], axis=1)
    e_logits = (h @ w_re + b_re).astype(jnp.float32).reshape(T, N_GROUPS, EXPERTS_PER_GROUP)
    e_in = jnp.take_along_axis(e_logits, g_sel[:, None, None], axis=1)[:, 0]
    top_v, top_i = lax.top_k(e_in, TOP_K_INNER)
    w2 = jax.nn.softmax(top_v, axis=-1) * p_sel
    expert_id = g_sel[:, None] * EXPERTS_PER_GROUP + top_i
    combine = jnp.sum(jax.nn.one_hot(expert_id, N_EXPERTS, dtype=jnp.float32)
                      * w2[..., None], axis=1).astype(h.dtype)
    out = jnp.zeros_like(h)
    for n in range(N_EXPERTS):
        a, b = jnp.split(h @ w_up[n], 2, axis=-1)
        out = out + combine[:, n:n + 1] * ((jax.nn.silu(a) * b) @ w_down[n])
    return out


def setup_inputs(seed: int = 0) -> dict:
    key = jax.random.key(seed)
    ks = jax.random.split(key, 28)
    f32 = jnp.float32
    L = DEPTH

    def nrm(k, shape, fan_in):
        return jax.random.normal(k, shape, f32) * (fan_in ** -0.5)

    def gain(k, shape):
        return 1.0 + 0.05 * jax.random.normal(k, shape, f32)

    def small(k, shape, s):
        return s * jax.random.normal(k, shape, f32)

    return {
        "x": jax.random.normal(ks[0], (BATCH, SEQ, D_MODEL), f32),
        "mem": jax.random.normal(ks[1], (BATCH, N_MEM, D_MODEL), f32),
        "g_mix": gain(ks[2], (L, D_MODEL)),
        "w_in": nrm(ks[3], (L, D_MODEL, IN_COLS), D_MODEL),
        "b_gate": small(ks[4], (L, N_BRANCHES * D_MODEL), 0.02),
        "g_qa": gain(ks[5], (L, HEAD_DIM_A)),
        "g_ka": gain(ks[6], (L, HEAD_DIM_A)),
        "g_idx_k": gain(ks[7], (L, IDX_DIM)),
        "conv_w": nrm(ks[8], (L, CONV_WIDTH, 1, CONV_CH), CONV_WIDTH),
        "conv_b": small(ks[9], (L, CONV_CH), 0.02),
        "ln_g": gain(ks[10], (L, CONV_CH)),
        "ln_b": small(ks[11], (L, CONV_CH), 0.02),
        "g_mem": gain(ks[12], (L, D_MODEL)),
        "w_mem_kv": nrm(ks[13], (L, D_MODEL, 2 * MEM_HEADS * MEM_HEAD_DIM), D_MODEL),
        "g_qm": gain(ks[14], (L, MEM_HEAD_DIM)),
        "g_km": gain(ks[15], (L, MEM_HEAD_DIM)),
        "w_br_a": nrm(ks[16], (L, N_HEADS_A * HEAD_DIM_A, D_MODEL), N_HEADS_A * HEAD_DIM_A),
        "w_br_b": nrm(ks[17], (L, CONV_CH, D_MODEL), CONV_CH),
        "w_br_m": nrm(ks[18], (L, MEM_HEADS * MEM_HEAD_DIM, D_MODEL), MEM_HEADS * MEM_HEAD_DIM),
        "w_o": nrm(ks[19], (L, D_MODEL, D_MODEL), D_MODEL),
        "g_ffn": gain(ks[20], (L, D_MODEL)),
        "w_rg": nrm(ks[21], (L, D_MODEL, N_GROUPS), D_MODEL),
        "b_rg": small(ks[22], (L, N_GROUPS), 0.01),
        "w_re": nrm(ks[23], (L, D_MODEL, N_EXPERTS), D_MODEL),
        "b_re": small(ks[24], (L, N_EXPERTS), 0.01),
        "w_up": nrm(ks[25], (L, N_EXPERTS, D_MODEL, 2 * D_FF_EXPERT), D_MODEL),
        "w_down": nrm(ks[26], (L, N_EXPERTS, D_FF_EXPERT, D_MODEL), D_FF_EXPERT),
    }


def reference(x, mem, g_mix, w_in, b_gate, g_qa, g_ka, g_idx_k, conv_w, conv_b, ln_g, ln_b,
              g_mem, w_mem_kv, g_qm, g_km, w_br_a, w_br_b, w_br_m, w_o, g_ffn,
              w_rg, b_rg, w_re, b_re, w_up, w_down):
    B, S, D = x.shape
    topk = min(TOPK_MAX, S // 4)
    pos = jnp.arange(S, dtype=jnp.int32)
    split_points = []
    acc = 0
    for sz in SPLIT_SIZES[:-1]:
        acc += sz
        split_points.append(acc)

    for l in range(DEPTH):
        h = rmsnorm(x, g_mix[l])
        proj = jnp.einsum('bsd,dc->bsc', h, w_in[l])
        q_a, k_a, v_a, q_i, k_i, w_i, conv_in, q_m, gate_logits = jnp.split(
            proj, split_points, axis=-1)

        q_a = rope(rmsnorm(q_a.reshape(B, S, N_HEADS_A, HEAD_DIM_A), g_qa[l]), pos)
        k_a = rope(rmsnorm(k_a.reshape(B, S, N_KV_HEADS_A, HEAD_DIM_A), g_ka[l]), pos)
        v_a = v_a.reshape(B, S, N_KV_HEADS_A, HEAD_DIM_A)
        q_i = rope(q_i.reshape(B, S, IDX_HEADS, IDX_DIM), pos)
        k_i = rope(rmsnorm(k_i, g_idx_k[l])[:, :, None, :], pos)[:, :, 0, :]
        o_a = dsa_attention(q_a, k_a, v_a, q_i, k_i, w_i, topk)

        o_b = conformer_conv(conv_in, conv_w[l], conv_b[l], ln_g[l], ln_b[l])

        o_m = memory_attention(q_m, mem, g_mem[l], w_mem_kv[l], g_qm[l], g_km[l])

        gates = jax.nn.sigmoid((gate_logits + b_gate[l]).astype(jnp.float32)).astype(x.dtype)
        gates = gates.reshape(B, S, N_BRANCHES, D)
        merged = (gates[:, :, 0] * (o_a @ w_br_a[l])
                  + gates[:, :, 1] * (o_b @ w_br_b[l])
                  + gates[:, :, 2] * (o_m @ w_br_m[l]))
        x = x + merged @ w_o[l]

        h2 = rmsnorm(x, g_ffn[l]).reshape(B * S, D)
        moe = hier_moe(h2, w_rg[l], b_rg[l], w_re[l], b_re[l], w_up[l], w_down[l])
        x = x + moe.reshape(B, S, D)
    return x
```

```python
import functools

import jax
import jax.numpy as jnp
import numpy as np
from jax import lax
from jax.experimental import pallas as pl
from jax.experimental.pallas import tpu as pltpu

N_HEADS_A = 8
N_KV_HEADS_A = 2
HEAD_DIM_A = 64
IDX_HEADS = 8
IDX_DIM = 64
TOPK_MAX = 256
CONV_CH = 512
CONV_WIDTH = 31
MEM_HEADS = 4
MEM_HEAD_DIM = 128
N_BRANCHES = 3
N_GROUPS = 4
EXPERTS_PER_GROUP = 8
N_EXPERTS = N_GROUPS * EXPERTS_PER_GROUP
D_FF_EXPERT = 256
ROPE_THETA = 10000.0
EPS = 1e-6

LANES = 128
VMEM_LIMIT = 56 * 1024 * 1024

INT_MIN = -2 ** 31
NEG_BIG = -1e30
CONV_HALO = 32

BF16 = jnp.bfloat16
F32 = jnp.float32


def _dot(a, b):
    return jnp.dot(a, b, preferred_element_type=F32)


def _split_bf16(x):
    hi = x.astype(BF16)
    lo = (x - hi.astype(F32)).astype(BF16)
    return hi, lo


def _seg_sum(xsq, ones_bd):
    hi, lo = _split_bf16(xsq)
    return _dot(hi, ones_bd) + _dot(lo, ones_bd)


def _rot_half(x, head_dim):
    n = x.shape[-1]
    half = head_dim // 2
    lane = lax.broadcasted_iota(jnp.int32, x.shape, x.ndim - 1)
    is_lo = (lane & (head_dim - 1)) < half
    return jnp.where(is_lo, pltpu.roll(x, n - half, x.ndim - 1), pltpu.roll(x, half, x.ndim - 1))


def _tile_lanes(x, reps):
    return jnp.concatenate([x] * reps, axis=-1) if reps > 1 else x


def _mem_kv_kernel(mem_ref, g_mem_ref, w_ref, g_km_ref, kt_ref, v_ref):
    m = mem_ref[0]
    hm = m * lax.rsqrt(jnp.mean(m * m, axis=-1, keepdims=True) + EPS) * g_mem_ref[...]
    kv = _dot(hm.astype(BF16), w_ref[...])
    hd = MEM_HEADS * MEM_HEAD_DIM
    for h in range(MEM_HEADS):
        k = kv[:, h * MEM_HEAD_DIM:(h + 1) * MEM_HEAD_DIM]
        k = k * lax.rsqrt(jnp.mean(k * k, axis=-1, keepdims=True) + EPS) * g_km_ref[...]
        kt_ref[0, h] = k.T.astype(BF16)
        v_ref[0, h] = kv[:, hd + h * MEM_HEAD_DIM: hd + (h + 1) * MEM_HEAD_DIM].astype(BF16)


def _mem_kv(mem, g_mem, w_mem_kv, g_km):
    B, M, D = mem.shape
    hd2 = 2 * MEM_HEADS * MEM_HEAD_DIM
    return pl.pallas_call(
        _mem_kv_kernel,
        grid=(B,),
        in_specs=[
            pl.BlockSpec((1, M, D), lambda b: (b, 0, 0)),
            pl.BlockSpec((1, D), lambda b: (0, 0)),
            pl.BlockSpec((D, hd2), lambda b: (0, 0)),
            pl.BlockSpec((1, MEM_HEAD_DIM), lambda b: (0, 0)),
        ],
        out_specs=[
            pl.BlockSpec((1, MEM_HEADS, MEM_HEAD_DIM, M), lambda b: (b, 0, 0, 0)),
            pl.BlockSpec((1, MEM_HEADS, M, MEM_HEAD_DIM), lambda b: (b, 0, 0, 0)),
        ],
        out_shape=[
            jax.ShapeDtypeStruct((B, MEM_HEADS, MEM_HEAD_DIM, M), BF16),
            jax.ShapeDtypeStruct((B, MEM_HEADS, M, MEM_HEAD_DIM), BF16),
        ],
        compiler_params=pltpu.CompilerParams(dimension_semantics=("arbitrary",)),
        name="mem_kv",
    )(mem, g_mem.reshape(1, D), w_mem_kv.astype(BF16), g_km.reshape(1, MEM_HEAD_DIM))


QA_W = N_HEADS_A * HEAD_DIM_A
QI_W = IDX_HEADS * IDX_DIM
KA_W = 2 * N_KV_HEADS_A * HEAD_DIM_A
VA_W = 2 * N_KV_HEADS_A * HEAD_DIM_A
KI_W = 2 * IDX_DIM
WI_W = LANES
DSA_COLS = QA_W + QI_W + KA_W + VA_W + KI_W + WI_W


def _dsa_proj_kernel(x_ref, g_mix_ref, w_ref, vbias_ref, gq_ref, gk_ref, gki_ref, cos_ref, sin_ref, bd_ref,
                     qa_ref, qi_ref, kt_ref, v_ref, kit_ref, wi_ref):
    x = x_ref[0]
    h = x * lax.rsqrt(jnp.mean(x * x, axis=-1, keepdims=True) + EPS) * g_mix_ref[...]
    p = _dot(h.astype(BF16), w_ref[...])
    cos1 = cos_ref[...]
    sin1 = sin_ref[...]
    cos4, sin4 = _tile_lanes(cos1, 4), _tile_lanes(sin1, 4)
    cos2, sin2 = _tile_lanes(cos1, 2), _tile_lanes(sin1, 2)
    bd = bd_ref[...]
    inv_hd = 1.0 / HEAD_DIM_A

    def rope(v, c, s):
        return v * c + _rot_half(v, HEAD_DIM_A) * s

    o = 0
    qa = p[:, o:o + QA_W]; o += QA_W
    qa = qa * lax.rsqrt(_seg_sum(qa * qa, bd) * inv_hd + EPS) * gq_ref[...]
    qa_ref[0] = (rope(qa, cos4, sin4) * (HEAD_DIM_A ** -0.5)).astype(BF16)

    qi = p[:, o:o + QI_W]; o += QI_W
    qi_ref[0] = (rope(qi, cos4, sin4) * (IDX_DIM ** -0.5)).astype(BF16)

    ka = p[:, o:o + KA_W]; o += KA_W
    ka = ka * lax.rsqrt(_seg_sum(ka * ka, bd[:KA_W, :KA_W]) * inv_hd + EPS) * gk_ref[...]
    kt_ref[0, 0] = rope(ka, cos2, sin2).T.astype(BF16)

    v_ref[0] = (p[:, o:o + VA_W] + vbias_ref[...]).astype(BF16); o += VA_W

    ki = p[:, o:o + KI_W]; o += KI_W
    ki = ki * lax.rsqrt(_seg_sum(ki * ki, bd[:KI_W, :KI_W]) * (1.0 / IDX_DIM) + EPS) * gki_ref[...]
    kit_ref[0, 0] = rope(ki, cos1, sin1).T.astype(BF16)

    wi_ref[0] = p[:, o:o + WI_W] * (IDX_HEADS ** -0.5)


def _dsa_proj(x, g_mix, w_dsa, vbias, g_qa, g_ka, g_idx_k, cos_t, sin_t, bd, tt):
    B, S, D = x.shape
    nt = S // tt
    const = lambda b, s: (0, 0)
    return pl.pallas_call(
        _dsa_proj_kernel,
        grid=(B, nt),
        in_specs=[
            pl.BlockSpec((1, tt, D), lambda b, s: (b, s, 0)),
            pl.BlockSpec((1, D), const),
            pl.BlockSpec((D, DSA_COLS), const),
            pl.BlockSpec((1, VA_W), const),
            pl.BlockSpec((1, QA_W), const),
            pl.BlockSpec((1, KA_W), const),
            pl.BlockSpec((1, KI_W), const),
            pl.BlockSpec((tt, LANES), lambda b, s: (s, 0)),
            pl.BlockSpec((tt, LANES), lambda b, s: (s, 0)),
            pl.BlockSpec((QA_W, QA_W), const),
        ],
        out_specs=[
            pl.BlockSpec((1, tt, QA_W), lambda b, s: (b, s, 0)),
            pl.BlockSpec((1, tt, QI_W), lambda b, s: (b, s, 0)),
            pl.BlockSpec((1, 1, KA_W, tt), lambda b, s: (b, s, 0, 0)),
            pl.BlockSpec((1, tt, VA_W), lambda b, s: (b, s, 0)),
            pl.BlockSpec((1, 1, KI_W, tt), lambda b, s: (b, s, 0, 0)),
            pl.BlockSpec((1, tt, WI_W), lambda b, s: (b, s, 0)),
        ],
        out_shape=[
            jax.ShapeDtypeStruct((B, S, QA_W), BF16),
            jax.ShapeDtypeStruct((B, S, QI_W), BF16),
            jax.ShapeDtypeStruct((B, nt, KA_W, tt), BF16),
            jax.ShapeDtypeStruct((B, S, VA_W), BF16),
            jax.ShapeDtypeStruct((B, nt, KI_W, tt), BF16),
            jax.ShapeDtypeStruct((B, S, WI_W), F32),
        ],
        compiler_params=pltpu.CompilerParams(dimension_semantics=("arbitrary", "arbitrary"),
                                             vmem_limit_bytes=VMEM_LIMIT),
        name="dsa_proj",
    )(x, g_mix, w_dsa, vbias, g_qa, g_ka, g_idx_k, cos_t, sin_t, bd)


def _dsa_attn_kernel(qa_ref, qi_ref, wi_ref, kt_ref, v_ref, kit_ref, o_ref,
                     qa_s, qi_s, keys_s, tau_s, jcut_s, m_s, acc_s, *, tq, tk, topk, idx_bits):
    qt = pl.program_id(1)
    row0 = qt * tq
    nchunks = row0 // tk + 1
    hpg = N_HEADS_A // N_KV_HEADS_A
    lane = lax.broadcasted_iota(jnp.int32, (tq, LANES), 1)
    lo_half = lane < HEAD_DIM_A

    for j in range(IDX_HEADS // 2):
        pair = qi_ref[0, :, j * LANES:(j + 1) * LANES]
        qi_s[(2 * j) * tq:(2 * j + 1) * tq, :] = jnp.where(lo_half, pair, jnp.zeros_like(pair))
        qi_s[(2 * j + 1) * tq:(2 * j + 2) * tq, :] = jnp.where(lo_half, jnp.zeros_like(pair), pair)
    for g in range(N_KV_HEADS_A):
        for jj in range(hpg // 2):
            pair = qa_ref[0, :, (g * hpg // 2 + jj) * LANES:(g * hpg // 2 + jj + 1) * LANES]
            qa_s[g, jj * tq:(jj + 1) * tq, :] = jnp.where(lo_half, pair, jnp.zeros_like(pair))
            qa_s[g, (hpg // 2 + jj) * tq:(hpg // 2 + jj + 1) * tq, :] = jnp.where(lo_half, jnp.zeros_like(pair), pair)

    rows = row0 + lax.broadcasted_iota(jnp.int32, (tq, tk), 0)
    col_in_chunk = lax.broadcasted_iota(jnp.int32, (tq, tk), 1)

    def score_chunk(c, carry):
        logits = _dot(qi_s[...], kit_ref[0, c])
        score = jnp.zeros((tq, tk), F32)
        for h in range(IDX_HEADS):
            score = score + wi_ref[0, :, h:h + 1] * jnp.maximum(logits[h * tq:(h + 1) * tq, :], 0.0)
        bits = lax.bitcast_convert_type(score, jnp.int32)
        key = bits ^ ((bits >> 31) & jnp.int32(0x7FFFFFFF))
        keys_s[c] = jnp.where(col_in_chunk + c * tk <= rows, key, jnp.int32(INT_MIN))
        return carry

    lax.fori_loop(0, nchunks, score_chunk, 0)

    def count_rows(pred_fn):
        def body(c, acc):
            k = keys_s[c]
            for j in range(tk // LANES):
                acc = acc + pred_fn(k[:, j * LANES:(j + 1) * LANES], c * tk + j * LANES).astype(jnp.int32)
            return acc
        part = lax.fori_loop(0, nchunks, body, jnp.zeros((tq, LANES), jnp.int32))
        return jnp.sum(part, axis=-1, keepdims=True)

    def radix_step(i, tau):
        cand = tau + (jnp.int32(1) << (31 - i))
        cnt = count_rows(lambda k, _: k >= cand)
        return jnp.where(cnt >= topk, cand, tau)

    tau = lax.fori_loop(0, 32, radix_step, jnp.full((tq, 1), INT_MIN, jnp.int32))
    tau_s[...] = tau

    cnt_gt = count_rows(lambda k, _: k > tau)
    cnt_eq = count_rows(lambda k, _: k == tau)
    need = topk - cnt_gt
    has_tau = tau > INT_MIN
    jcut_s[...] = jnp.where(has_tau, jnp.int32(2 ** 30), jnp.int32(-1))
    split_tie = jnp.max(jnp.where(has_tau & (cnt_eq > need), 1, 0)) > 0

    @pl.when(split_tie)
    def _():
        def idx_step(i, j):
            bit = jnp.int32(1) << (idx_bits - 1 - i)
            test = j + bit - 1
            f = count_rows(lambda k, base: (k == tau) & (base + lane <= test))
            return jnp.where(f < need, j + bit, j)
        j = lax.fori_loop(0, idx_bits, idx_step, jnp.zeros((tq, 1), jnp.int32))
        jcut_s[...] = jnp.where(has_tau, j, jnp.int32(-1))

    m_s[...] = jnp.full(m_s.shape, NEG_BIG, F32)
    acc_s[...] = jnp.zeros(acc_s.shape, F32)

    def attn_chunk(c, carry):
        k = keys_s[c]
        tau_b = tau_s[...]
        jcut_b = jcut_s[...]
        cols = col_in_chunk + c * tk
        bias = jnp.where(k > tau_b, 0.0, jnp.where(k == tau_b, jnp.where(cols <= jcut_b, 0.0, NEG_BIG), NEG_BIG))
        bias4 = jnp.concatenate([bias] * hpg, axis=0)
        for g in range(N_KV_HEADS_A):
            s = _dot(qa_s[g], kt_ref[0, c, g * LANES:(g + 1) * LANES, :]) + bias4
            m_old = m_s[g]
            m_new = jnp.maximum(m_old, jnp.max(s, axis=-1, keepdims=True))
            p = jnp.exp(s - m_new)
            alpha = jnp.exp(m_old - m_new)
            pv = _dot(p.astype(BF16), v_ref[0, pl.ds(pl.multiple_of(c * tk, tk), tk), g * LANES:(g + 1) * LANES])
            acc_s[g] = alpha * acc_s[g] + pv
            m_s[g] = m_new
        return carry

    lax.fori_loop(0, nchunks, attn_chunk, 0)

    for g in range(N_KV_HEADS_A):
        for jj in range(hpg // 2):
            a_e = acc_s[g, jj * tq:(jj + 1) * tq, :]
            a_o = acc_s[g, (hpg // 2 + jj) * tq:(hpg // 2 + jj + 1) * tq, :]
            o_pair = jnp.where(lo_half, a_e / pltpu.roll(a_e, HEAD_DIM_A, 1), pltpu.roll(a_o, HEAD_DIM_A, 1) / a_o)
            j = g * hpg // 2 + jj
            o_ref[0, :, j * LANES:(j + 1) * LANES] = o_pair.astype(BF16)


def _dsa_attn(qa, qi, wi, kt, v, kit, tq, topk):
    B, S, _ = qa.shape
    nc, tk = kt.shape[1], kt.shape[3]
    hpg = N_HEADS_A // N_KV_HEADS_A
    idx_bits = max(1, int(np.ceil(np.log2(S))))
    kernel = functools.partial(_dsa_attn_kernel, tq=tq, tk=tk, topk=topk, idx_bits=idx_bits)
    return pl.pallas_call(
        kernel,
        grid=(B, S // tq),
        in_specs=[
            pl.BlockSpec((1, tq, QA_W), lambda b, q: (b, q, 0)),
            pl.BlockSpec((1, tq, QI_W), lambda b, q: (b, q, 0)),
            pl.BlockSpec((1, tq, WI_W), lambda b, q: (b, q, 0)),
            pl.BlockSpec((1, nc, KA_W, tk), lambda b, q: (b, 0, 0, 0)),
            pl.BlockSpec((1, S, VA_W), lambda b, q: (b, 0, 0)),
            pl.BlockSpec((1, nc, KI_W, tk), lambda b, q: (b, 0, 0, 0)),
        ],
        out_specs=pl.BlockSpec((1, tq, QA_W), lambda b, q: (b, q, 0)),
        out_shape=jax.ShapeDtypeStruct((B, S, QA_W), BF16),
        scratch_shapes=[
            pltpu.VMEM((N_KV_HEADS_A, hpg * tq, LANES), BF16),
            pltpu.VMEM((IDX_HEADS * tq, LANES), BF16),
            pltpu.VMEM((nc, tq, tk), jnp.int32),
            pltpu.VMEM((tq, 1), jnp.int32),
            pltpu.VMEM((tq, 1), jnp.int32),
            pltpu.VMEM((N_KV_HEADS_A, hpg * tq, 1), F32),
            pltpu.VMEM((N_KV_HEADS_A, hpg * tq, LANES), F32),
        ],
        compiler_params=pltpu.CompilerParams(dimension_semantics=("arbitrary", "arbitrary"),
                                             vmem_limit_bytes=VMEM_LIMIT),
        name="dsa_attn",
    )(qa, qi, wi, kt, v, kit)


CONV_IN_W = 2 * CONV_CH
QM_W = MEM_HEADS * MEM_HEAD_DIM


def _mixer_kernel(x_ref, oa_ref, g_mix_ref, wc_ref, wq_ref, wg_ref, bg_ref, cw_ref, cb_ref, lng_ref, lnb_ref,
                  gqm_ref, kmt_ref, vm_ref, wa_ref, wb_ref, wm_ref, wo_ref, gffn_ref, wrh_ref, wrl_ref, br_ref,
                  x1_ref, h2_ref, comb_ref, ubuf, *, tt):
    D = x_ref.shape[-1]
    x = x_ref[0]
    h = (x * lax.rsqrt(jnp.mean(x * x, axis=-1, keepdims=True) + EPS) * g_mix_ref[...]).astype(BF16)

    cin = _dot(h, wc_ref[...])
    u = cin[:, :CONV_CH] * jax.nn.sigmoid(cin[:, CONV_CH:])

    @pl.when(pl.program_id(1) == 0)
    def _():
        ubuf[0:CONV_HALO, :] = jnp.zeros((CONV_HALO, CONV_CH), F32)

    ubuf[CONV_HALO:CONV_HALO + tt, :] = u
    y = jnp.zeros((tt, CONV_CH), F32) + cb_ref[...]
    for j in range(CONV_WIDTH):
        off = CONV_HALO - (CONV_WIDTH - 1) + j
        y = y + cw_ref[j:j + 1, :] * ubuf[off:off + tt, :]
    halo = ubuf[tt:tt + CONV_HALO, :]
    ubuf[0:CONV_HALO, :] = halo
    mu = jnp.mean(y, axis=-1, keepdims=True)
    yc = y - mu
    var = jnp.mean(yc * yc, axis=-1, keepdims=True)
    yn = yc * lax.rsqrt(var + EPS) * lng_ref[...] + lnb_ref[...]
    o_b = (yn * jax.nn.sigmoid(yn)).astype(BF16)

    qm = _dot(h, wq_ref[...])
    o_m = []
    for hh in range(MEM_HEADS):
        q = qm[:, hh * MEM_HEAD_DIM:(hh + 1) * MEM_HEAD_DIM]
        q = q * lax.rsqrt(jnp.mean(q * q, axis=-1, keepdims=True) + EPS) * gqm_ref[...] * (MEM_HEAD_DIM ** -0.5)
        s = _dot(q.astype(BF16), kmt_ref[0, hh])
        p = jnp.exp(s - jnp.max(s, axis=-1, keepdims=True))
        l = jnp.sum(p, axis=-1, keepdims=True)
        o_m.append((_dot(p.astype(BF16), vm_ref[0, hh]) / l).astype(BF16))
    o_m = jnp.concatenate(o_m, axis=-1)

    gates = jax.nn.sigmoid(_dot(h, wg_ref[...]) + bg_ref[...])
    merged = (gates[:, :D] * _dot(oa_ref[0], wa_ref[...])
              + gates[:, D:2 * D] * _dot(o_b, wb_ref[...])
              + gates[:, 2 * D:] * _dot(o_m, wm_ref[...]))
    x1 = x + _dot(merged.astype(BF16), wo_ref[...])
    x1_ref[0] = x1

    h2 = x1 * lax.rsqrt(jnp.mean(x1 * x1, axis=-1, keepdims=True) + EPS) * gffn_ref[...]
    h2_ref[0] = h2.astype(BF16)
    hi, lo = _split_bf16(h2)
    logits = _dot(hi, wrh_ref[...]) + _dot(hi, wrl_ref[...]) + _dot(lo, wrh_ref[...]) + br_ref[...]
    lane = lax.broadcasted_iota(jnp.int32, logits.shape, 1)
    ninf = jnp.float32(-jnp.inf)
    big = jnp.int32(1 << 20)
    is_g = (lane >= N_EXPERTS) & (lane < N_EXPERTS + N_GROUPS)
    gl = jnp.where(is_g, logits, ninf)
    gmax = jnp.max(gl, axis=-1, keepdims=True)
    gsel = jnp.min(jnp.where(gl == gmax, lane, big), axis=-1, keepdims=True) - N_EXPERTS
    p_sel = 1.0 / jnp.sum(jnp.where(is_g, jnp.exp(gl - gmax), 0.0), axis=-1, keepdims=True)
    in_grp = (lane < N_EXPERTS) & ((lane >> 3) == gsel)
    el = jnp.where(in_grp, logits, ninf)
    v1 = jnp.max(el, axis=-1, keepdims=True)
    i1 = jnp.min(jnp.where(el == v1, lane, big), axis=-1, keepdims=True)
    el2 = jnp.where(lane == i1, ninf, el)
    v2 = jnp.max(el2, axis=-1, keepdims=True)
    i2 = jnp.min(jnp.where(el2 == v2, lane, big), axis=-1, keepdims=True)
    e2 = jnp.exp(v2 - v1)
    den = 1.0 + e2
    comb_ref[0] = jnp.where(lane == i1, (1.0 / den) * p_sel, jnp.where(lane == i2, (e2 / den) * p_sel, 0.0))


def _mixer(x, oa, g_mix, wc, wq, wg, bg, cw, cb, lng, lnb, gqm, kmt, vm, wa, wb, wm, wo, gffn, wrh, wrl, br, tt):
    B, S, D = x.shape
    M = kmt.shape[-1]
    const = lambda b, s: (0, 0)
    tile = lambda w: pl.BlockSpec((1, tt, w), lambda b, s: (b, s, 0))
    full = lambda a: pl.BlockSpec(a.shape, const)
    return pl.pallas_call(
        functools.partial(_mixer_kernel, tt=tt),
        grid=(B, S // tt),
        in_specs=[
            tile(D), tile(QA_W), full(g_mix), full(wc), full(wq), full(wg), full(bg), full(cw), full(cb),
            full(lng), full(lnb), full(gqm),
            pl.BlockSpec((1, MEM_HEADS, MEM_HEAD_DIM, M), lambda b, s: (b, 0, 0, 0)),
            pl.BlockSpec((1, MEM_HEADS, M, MEM_HEAD_DIM), lambda b, s: (b, 0, 0, 0)),
            full(wa), full(wb), full(wm), full(wo), full(gffn), full(wrh), full(wrl), full(br),
        ],
        out_specs=[tile(D), tile(D), tile(LANES)],
        out_shape=[
            jax.ShapeDtypeStruct((B, S, D), F32),
            jax.ShapeDtypeStruct((B, S, D), BF16),
            jax.ShapeDtypeStruct((B, S, LANES), F32),
        ],
        scratch_shapes=[pltpu.VMEM((CONV_HALO + tt, CONV_CH), F32)],
        compiler_params=pltpu.CompilerParams(dimension_semantics=("arbitrary", "arbitrary"),
                                             vmem_limit_bytes=VMEM_LIMIT),
        name="mixer",
    )(x, oa, g_mix, wc, wq, wg, bg, cw, cb, lng, lnb, gqm, kmt, vm, wa, wb, wm, wo, gffn, wrh, wrl, br)


def _moe_kernel(h2_ref, x1_ref, comb_ref, wup_ref, wdn_ref, o_ref):
    n = pl.program_id(1)

    @pl.when(n == 0)
    def _():
        o_ref[...] = x1_ref[...]

    up = _dot(h2_ref[...], wup_ref[0])
    act = jax.nn.silu(up[:, :D_FF_EXPERT]) * up[:, D_FF_EXPERT:]
    comb = comb_ref[...]
    lane = lax.broadcasted_iota(jnp.int32, comb.shape, 1)
    c = jnp.sum(jnp.where(lane == n, comb, 0.0), axis=-1, keepdims=True)
    o_ref[...] += _dot((act * c).astype(BF16), wdn_ref[0])


def _moe(h2, x1, comb, w_up, w_down, tm):
    T, D = h2.shape
    return pl.pallas_call(
        _moe_kernel,
        grid=(T // tm, N_EXPERTS),
        in_specs=[
            pl.BlockSpec((tm, D), lambda t, n: (t, 0)),
            pl.BlockSpec((tm, D), lambda t, n: (t, 0)),
            pl.BlockSpec((tm, LANES), lambda t, n: (t, 0)),
            pl.BlockSpec((1, D, 2 * D_FF_EXPERT), lambda t, n: (n, 0, 0)),
            pl.BlockSpec((1, D_FF_EXPERT, D), lambda t, n: (n, 0, 0)),
        ],
        out_specs=pl.BlockSpec((tm, D), lambda t, n: (t, 0)),
        out_shape=jax.ShapeDtypeStruct((T, D), F32),
        compiler_params=pltpu.CompilerParams(dimension_semantics=("arbitrary", "arbitrary"),
                                             vmem_limit_bytes=VMEM_LIMIT),
        name="moe",
    )(h2, x1, comb, w_up, w_down)


def _pick_tile(n, pref):
    t = min(n, pref)
    assert n % t == 0, (n, t)
    return t


def _rope_tables(S):
    half = HEAD_DIM_A // 2
    inv = ROPE_THETA ** (-jnp.arange(half, dtype=F32) / half)
    ang = jnp.arange(S, dtype=jnp.int32).astype(F32)[:, None] * inv[None, :]
    cos, sin = jnp.cos(ang), jnp.sin(ang)
    cos_t = jnp.concatenate([cos, cos, cos, cos], axis=-1)
    sin_t = jnp.concatenate([-sin, sin, -sin, sin], axis=-1)
    return cos_t, sin_t


def _layer(x, mem, g_mix, w_in, b_gate, g_qa, g_ka, g_idx_k, conv_w, conv_b, ln_g, ln_b,
           g_mem, w_mem_kv, g_qm, g_km, w_br_a, w_br_b, w_br_m, w_o, g_ffn,
           w_rg, b_rg, w_re, b_re, w_up, w_down):
    B, S, D = x.shape
    topk = min(TOPK_MAX, S // 4)
    tk = _pick_tile(S, 512)
    tq = _pick_tile(tk, 128)
    tt = _pick_tile(S, 256)
    tm = _pick_tile(B * S, 1024)

    sizes = (QA_W, N_KV_HEADS_A * HEAD_DIM_A, N_KV_HEADS_A * HEAD_DIM_A, QI_W, IDX_DIM, IDX_HEADS,
             CONV_IN_W, QM_W, N_BRANCHES * D)
    offs = np.concatenate([[0], np.cumsum(sizes)])
    col = lambda i: w_in[:, offs[i]:offs[i + 1]]
    w_qa, w_ka, w_va, w_qi, w_ki, w_wi, w_conv, w_qm, w_gate = (col(i) for i in range(9))
    hd = HEAD_DIM_A
    zeros_hd = jnp.zeros((D, hd), F32)
    ka_dup = [w_ka[:, g * hd:(g + 1) * hd] for g in range(N_KV_HEADS_A) for _ in range(2)]
    va_aug = [w for g in range(N_KV_HEADS_A) for w in (w_va[:, g * hd:(g + 1) * hd], zeros_hd)]
    w_dsa = jnp.concatenate(
        [w_qa, w_qi] + ka_dup + va_aug + [w_ki, w_ki, w_wi, jnp.zeros((D, WI_W - IDX_HEADS), F32)], axis=1).astype(BF16)
    vbias = jnp.tile(jnp.concatenate([jnp.zeros((hd,), F32), jnp.ones((hd,), F32)]), N_KV_HEADS_A).reshape(1, VA_W)
    seg = np.arange(QA_W) // hd
    bd = jnp.asarray(seg[:, None] == seg[None, :], BF16)
    cos_t, sin_t = _rope_tables(S)

    kmt, vm = _mem_kv(mem, g_mem, w_mem_kv, g_km)
    qa, qi, kt, v, kit, wi = _dsa_proj(
        x, g_mix.reshape(1, D), w_dsa, vbias,
        jnp.tile(g_qa, N_HEADS_A).reshape(1, QA_W), jnp.tile(g_ka, KA_W // hd).reshape(1, KA_W),
        jnp.tile(g_idx_k, KI_W // IDX_DIM).reshape(1, KI_W), cos_t, sin_t, bd, tk)
    o_a = _dsa_attn(qa, qi, wi, kt, v, kit, tq, topk)

    w_r = jnp.concatenate([w_re, w_rg, jnp.zeros((D, LANES - N_EXPERTS - N_GROUPS), F32)], axis=1)
    b_r = jnp.concatenate([b_re, b_rg, jnp.zeros((LANES - N_EXPERTS - N_GROUPS,), F32)]).reshape(1, LANES)
    w_r_hi = w_r.astype(BF16)
    w_r_lo = (w_r - w_r_hi.astype(F32)).astype(BF16)
    x1, h2, comb = _mixer(
        x, o_a, g_mix.reshape(1, D), w_conv.astype(BF16), w_qm.astype(BF16), w_gate.astype(BF16),
        b_gate.reshape(1, -1), conv_w.reshape(CONV_WIDTH, CONV_CH), conv_b.reshape(1, CONV_CH),
        ln_g.reshape(1, CONV_CH), ln_b.reshape(1, CONV_CH), g_qm.reshape(1, MEM_HEAD_DIM), kmt, vm,
        w_br_a.astype(BF16), w_br_b.astype(BF16), w_br_m.astype(BF16), w_o.astype(BF16),
        g_ffn.reshape(1, D), w_r_hi, w_r_lo, b_r, tt)

    out = _moe(h2.reshape(B * S, D), x1.reshape(B * S, D), comb.reshape(B * S, LANES),
               w_up.astype(BF16), w_down.astype(BF16), tm)
    return out.reshape(B, S, D)


def kernel(x, mem, g_mix, w_in, b_gate, g_qa, g_ka, g_idx_k, conv_w, conv_b, ln_g, ln_b, g_mem, w_mem_kv, g_qm, g_km,
           w_br_a, w_br_b, w_br_m, w_o, g_ffn, w_rg, b_rg, w_re, b_re, w_up, w_down):
    params = (g_mix, w_in, b_gate, g_qa, g_ka, g_idx_k, conv_w, conv_b, ln_g, ln_b, g_mem, w_mem_kv, g_qm, g_km,
              w_br_a, w_br_b, w_br_m, w_o, g_ffn, w_rg, b_rg, w_re, b_re, w_up, w_down)
    for l in range(g_mix.shape[0]):
        x = _layer(x, mem, *(p[l] for p in params))
    return x
```

```python
import functools

import jax
import jax.numpy as jnp
import numpy as np
from jax import lax
from jax.experimental import pallas as pl
from jax.experimental.pallas import tpu as pltpu

N_HEADS_A = 8
N_KV_HEADS_A = 2
HEAD_DIM_A = 64
IDX_HEADS = 8
IDX_DIM = 64
TOPK_MAX = 256
CONV_CH = 512
CONV_WIDTH = 31
MEM_HEADS = 4
MEM_HEAD_DIM = 128
N_BRANCHES = 3
N_GROUPS = 4
EXPERTS_PER_GROUP = 8
N_EXPERTS = N_GROUPS * EXPERTS_PER_GROUP
D_FF_EXPERT = 256
ROPE_THETA = 10000.0
EPS = 1e-6

LANES = 128
VMEM_LIMIT = 56 * 1024 * 1024

INT_MIN = -2 ** 31
NEG_BIG = -1e30
CONV_HALO = 32

BF16 = jnp.bfloat16
F32 = jnp.float32


def _dot(a, b):
    return jnp.dot(a, b, preferred_element_type=F32)


def _split_bf16(x):
    hi = x.astype(BF16)
    lo = (x - hi.astype(F32)).astype(BF16)
    return hi, lo


def _seg_sum(xsq, ones_bd):
    hi, lo = _split_bf16(xsq)
    return _dot(hi, ones_bd) + _dot(lo, ones_bd)


def _rot_half(x, head_dim):
    n = x.shape[-1]
    half = head_dim // 2
    lane = lax.broadcasted_iota(jnp.int32, x.shape, x.ndim - 1)
    is_lo = (lane & (head_dim - 1)) < half
    return jnp.where(is_lo, pltpu.roll(x, n - half, x.ndim - 1), pltpu.roll(x, half, x.ndim - 1))


def _tile_lanes(x, reps):
    return jnp.concatenate([x] * reps, axis=-1) if reps > 1 else x


def _mem_kv_kernel(mem_ref, g_mem_ref, w_ref, g_km_ref, kt_ref, v_ref):
    m = mem_ref[0]
    hm = m * lax.rsqrt(jnp.mean(m * m, axis=-1, keepdims=True) + EPS) * g_mem_ref[...]
    kv = _dot(hm.astype(BF16), w_ref[...])
    hd = MEM_HEADS * MEM_HEAD_DIM
    for h in range(MEM_HEADS):
        k = kv[:, h * MEM_HEAD_DIM:(h + 1) * MEM_HEAD_DIM]
        k = k * lax.rsqrt(jnp.mean(k * k, axis=-1, keepdims=True) + EPS) * g_km_ref[...]
        kt_ref[0, h] = k.T.astype(BF16)
        v_ref[0, h] = kv[:, hd + h * MEM_HEAD_DIM: hd + (h + 1) * MEM_HEAD_DIM].astype(BF16)


def _mem_kv(mem, g_mem, w_mem_kv, g_km):
    B, M, D = mem.shape
    hd2 = 2 * MEM_HEADS * MEM_HEAD_DIM
    return pl.pallas_call(
        _mem_kv_kernel,
        grid=(B,),
        in_specs=[
            pl.BlockSpec((1, M, D), lambda b: (b, 0, 0)),
            pl.BlockSpec((1, D), lambda b: (0, 0)),
            pl.BlockSpec((D, hd2), lambda b: (0, 0)),
            pl.BlockSpec((1, MEM_HEAD_DIM), lambda b: (0, 0)),
        ],
        out_specs=[
            pl.BlockSpec((1, MEM_HEADS, MEM_HEAD_DIM, M), lambda b: (b, 0, 0, 0)),
            pl.BlockSpec((1, MEM_HEADS, M, MEM_HEAD_DIM), lambda b: (b, 0, 0, 0)),
        ],
        out_shape=[
            jax.ShapeDtypeStruct((B, MEM_HEADS, MEM_HEAD_DIM, M), BF16),
            jax.ShapeDtypeStruct((B, MEM_HEADS, M, MEM_HEAD_DIM), BF16),
        ],
        compiler_params=pltpu.CompilerParams(dimension_semantics=("arbitrary",)),
        name="mem_kv",
    )(mem, g_mem.reshape(1, D), w_mem_kv.astype(BF16), g_km.reshape(1, MEM_HEAD_DIM))


QA_W = N_HEADS_A * HEAD_DIM_A
QI_W = IDX_HEADS * IDX_DIM
KA_W = 2 * N_KV_HEADS_A * HEAD_DIM_A
VA_W = 2 * N_KV_HEADS_A * HEAD_DIM_A
KI_W = 2 * IDX_DIM
WI_W = LANES
DSA_COLS = QA_W + QI_W + KA_W + VA_W + KI_W + WI_W


def _dsa_proj_kernel(x_ref, g_mix_ref, w_ref, vbias_ref, gq_ref, gk_ref, gki_ref, cos_ref, sin_ref, bd_ref,
                     qat_ref, qit_ref, k_ref, vt_ref, ki_ref, wit_ref):
    x = x_ref[0]
    h = x * lax.rsqrt(jnp.mean(x * x, axis=-1, keepdims=True) + EPS) * g_mix_ref[...]
    p = _dot(h.astype(BF16), w_ref[...])
    cos1 = cos_ref[...]
    sin1 = sin_ref[...]
    cos4, sin4 = _tile_lanes(cos1, 4), _tile_lanes(sin1, 4)
    cos2, sin2 = _tile_lanes(cos1, 2), _tile_lanes(sin1, 2)
    bd = bd_ref[...]
    inv_hd = 1.0 / HEAD_DIM_A

    def rope(v, c, s):
        return v * c + _rot_half(v, HEAD_DIM_A) * s

    o = 0
    qa = p[:, o:o + QA_W]; o += QA_W
    qa = qa * lax.rsqrt(_seg_sum(qa * qa, bd) * inv_hd + EPS) * gq_ref[...]
    qat_ref[0] = (rope(qa, cos4, sin4) * (HEAD_DIM_A ** -0.5)).T.astype(BF16)

    qi = p[:, o:o + QI_W]; o += QI_W
    qit_ref[0] = (rope(qi, cos4, sin4) * (IDX_DIM ** -0.5)).T.astype(BF16)

    ka = p[:, o:o + KA_W]; o += KA_W
    ka = ka * lax.rsqrt(_seg_sum(ka * ka, bd[:KA_W, :KA_W]) * inv_hd + EPS) * gk_ref[...]
    k_ref[0] = rope(ka, cos2, sin2).astype(BF16)

    vt_ref[0, 0] = (p[:, o:o + VA_W] + vbias_ref[...]).T.astype(BF16); o += VA_W

    ki = p[:, o:o + KI_W]; o += KI_W
    ki = ki * lax.rsqrt(_seg_sum(ki * ki, bd[:KI_W, :KI_W]) * (1.0 / IDX_DIM) + EPS) * gki_ref[...]
    ki_ref[0] = rope(ki, cos1, sin1).astype(BF16)

    wit_ref[0] = (p[:, o:o + WI_W] * (IDX_HEADS ** -0.5)).T[:IDX_HEADS, :]


def _dsa_proj(x, g_mix, w_dsa, vbias, g_qa, g_ka, g_idx_k, cos_t, sin_t, bd, tt):
    B, S, D = x.shape
    nt = S // tt
    const = lambda b, s: (0, 0)
    return pl.pallas_call(
        _dsa_proj_kernel,
        grid=(B, nt),
        in_specs=[
            pl.BlockSpec((1, tt, D), lambda b, s: (b, s, 0)),
            pl.BlockSpec((1, D), const),
            pl.BlockSpec((D, DSA_COLS), const),
            pl.BlockSpec((1, VA_W), const),
            pl.BlockSpec((1, QA_W), const),
            pl.BlockSpec((1, KA_W), const),
            pl.BlockSpec((1, KI_W), const),
            pl.BlockSpec((tt, LANES), lambda b, s: (s, 0)),
            pl.BlockSpec((tt, LANES), lambda b, s: (s, 0)),
            pl.BlockSpec((QA_W, QA_W), const),
        ],
        out_specs=[
            pl.BlockSpec((1, QA_W, tt), lambda b, s: (b, 0, s)),
            pl.BlockSpec((1, QI_W, tt), lambda b, s: (b, 0, s)),
            pl.BlockSpec((1, tt, KA_W), lambda b, s: (b, s, 0)),
            pl.BlockSpec((1, 1, VA_W, tt), lambda b, s: (b, s, 0, 0)),
            pl.BlockSpec((1, tt, KI_W), lambda b, s: (b, s, 0)),
            pl.BlockSpec((1, IDX_HEADS, tt), lambda b, s: (b, 0, s)),
        ],
        out_shape=[
            jax.ShapeDtypeStruct((B, QA_W, S), BF16),
            jax.ShapeDtypeStruct((B, QI_W, S), BF16),
            jax.ShapeDtypeStruct((B, S, KA_W), BF16),
            jax.ShapeDtypeStruct((B, nt, VA_W, tt), BF16),
            jax.ShapeDtypeStruct((B, S, KI_W), BF16),
            jax.ShapeDtypeStruct((B, IDX_HEADS, S), F32),
        ],
        compiler_params=pltpu.CompilerParams(dimension_semantics=("arbitrary", "arbitrary"),
                                             vmem_limit_bytes=VMEM_LIMIT),
        name="dsa_proj",
    )(x, g_mix, w_dsa, vbias, g_qa, g_ka, g_idx_k, cos_t, sin_t, bd)


def _dsa_attn_kernel(qat_ref, qit_ref, wit_ref, k_ref, vt_ref, ki_ref, o_ref,
                     qa_s, qi_s, keys_s, m_s, acc_s, *, tq, tk, topk, idx_bits):
    qt = pl.program_id(1)
    q0 = qt * tq
    nchunks = q0 // tk + 1
    hpg = N_HEADS_A // N_KV_HEADS_A
    half_pairs = hpg // 2
    dim = lax.broadcasted_iota(jnp.int32, (LANES, tq), 0)
    lo_half = dim < HEAD_DIM_A

    for j in range(IDX_HEADS // 2):
        pair = qit_ref[0, j * LANES:(j + 1) * LANES, :]
        qi_s[:, (2 * j) * tq:(2 * j + 1) * tq] = jnp.where(lo_half, pair, jnp.zeros_like(pair))
        qi_s[:, (2 * j + 1) * tq:(2 * j + 2) * tq] = jnp.where(lo_half, jnp.zeros_like(pair), pair)
    for g in range(N_KV_HEADS_A):
        for jj in range(half_pairs):
            pair = qat_ref[0, (g * half_pairs + jj) * LANES:(g * half_pairs + jj + 1) * LANES, :]
            qa_s[g, :, jj * tq:(jj + 1) * tq] = jnp.where(lo_half, pair, jnp.zeros_like(pair))
            qa_s[g, :, (half_pairs + jj) * tq:(half_pairs + jj + 1) * tq] = jnp.where(lo_half, jnp.zeros_like(pair), pair)

    key_in_chunk = lax.broadcasted_iota(jnp.int32, (tk, tq), 0)
    q_pos = q0 + lax.broadcasted_iota(jnp.int32, (tk, tq), 1)

    def score_chunk(c, carry):
        r0 = pl.multiple_of(c * tk, tk)
        logits = _dot(ki_ref[0, pl.ds(r0, tk), :], qi_s[...])
        score = jnp.zeros((tk, tq), F32)
        for h in range(IDX_HEADS):
            score = score + wit_ref[0, h:h + 1, :] * jnp.maximum(logits[:, h * tq:(h + 1) * tq], 0.0)
        bits = lax.bitcast_convert_type(score, jnp.int32)
        key = bits ^ ((bits >> 31) & jnp.int32(0x7FFFFFFF))
        keys_s[c] = jnp.where(key_in_chunk + c * tk <= q_pos, key, jnp.int32(INT_MIN))
        return carry

    lax.fori_loop(0, nchunks, score_chunk, 0)

    def count_keys(pred_fn):
        def body(c, acc):
            hit = pred_fn(keys_s[c], c * tk).astype(jnp.int32)
            return acc + jnp.sum(hit.reshape(tk // 8, 8, tq), axis=0)
        part = lax.fori_loop(0, nchunks, body, jnp.zeros((8, tq), jnp.int32))
        return jnp.sum(part, axis=0, keepdims=True)

    def radix_step(i, tau):
        cand = tau + (jnp.int32(1) << (31 - i))
        cnt = count_keys(lambda k, _: k >= cand)
        return jnp.where(cnt >= topk, cand, tau)

    tau = lax.fori_loop(0, 32, radix_step, jnp.full((1, tq), INT_MIN, jnp.int32))

    cnt_gt = count_keys(lambda k, _: k > tau)
    cnt_eq = count_keys(lambda k, _: k == tau)
    need = topk - cnt_gt
    has_tau = tau > INT_MIN
    split_tie = jnp.max(jnp.where(has_tau & (cnt_eq > need), 1, 0)) > 0

    def tie_cut():
        def idx_step(i, j):
            bit = jnp.int32(1) << (idx_bits - 1 - i)
            test = j + bit - 1
            f = count_keys(lambda k, base: (k == tau) & (base + key_in_chunk <= test))
            return jnp.where(f < need, j + bit, j)
        return lax.fori_loop(0, idx_bits, idx_step, jnp.zeros((1, tq), jnp.int32))

    jcut = lax.cond(split_tie, tie_cut, lambda: jnp.full((1, tq), 2 ** 30, jnp.int32))
    jcut = jnp.where(has_tau, jcut, jnp.int32(-1))

    m_s[...] = jnp.full(m_s.shape, NEG_BIG, F32)
    acc_s[...] = jnp.zeros(acc_s.shape, F32)

    def attn_chunk(c, carry):
        r0 = pl.multiple_of(c * tk, tk)
        k = keys_s[c]
        kpos = key_in_chunk + c * tk
        bias = jnp.where(k > tau, 0.0, jnp.where(k == tau, jnp.where(kpos <= jcut, 0.0, NEG_BIG), NEG_BIG))
        bias4 = jnp.concatenate([bias] * hpg, axis=1)
        for g in range(N_KV_HEADS_A):
            s = _dot(k_ref[0, pl.ds(r0, tk), g * LANES:(g + 1) * LANES], qa_s[g]) + bias4
            m_old = m_s[g]
            m_new = jnp.maximum(m_old, jnp.max(s, axis=0, keepdims=True))
            p = jnp.exp(s - m_new)
            alpha = jnp.exp(m_old - m_new)
            pv = _dot(vt_ref[0, c, g * LANES:(g + 1) * LANES, :], p.astype(BF16))
            acc_s[g] = alpha * acc_s[g] + pv
            m_s[g] = m_new
        return carry

    lax.fori_loop(0, nchunks, attn_chunk, 0)

    for g in range(N_KV_HEADS_A):
        for jj in range(half_pairs):
            a_e = acc_s[g, :, jj * tq:(jj + 1) * tq]
            a_o = acc_s[g, :, (half_pairs + jj) * tq:(half_pairs + jj + 1) * tq]
            o_pair = jnp.concatenate([a_e[:HEAD_DIM_A] / a_e[HEAD_DIM_A:], a_o[:HEAD_DIM_A] / a_o[HEAD_DIM_A:]], axis=0)
            j = g * half_pairs + jj
            o_ref[0, :, j * LANES:(j + 1) * LANES] = o_pair.T.astype(BF16)


def _dsa_attn(qat, qit, wit, k, vt, ki, tq, topk):
    B, _, S = qat.shape
    nc, tk = vt.shape[1], vt.shape[3]
    hpg = N_HEADS_A // N_KV_HEADS_A
    idx_bits = max(1, int(np.ceil(np.log2(S))))
    kernel = functools.partial(_dsa_attn_kernel, tq=tq, tk=tk, topk=topk, idx_bits=idx_bits)
    return pl.pallas_call(
        kernel,
        grid=(B, S // tq),
        in_specs=[
            pl.BlockSpec((1, QA_W, tq), lambda b, q: (b, 0, q)),
            pl.BlockSpec((1, QI_W, tq), lambda b, q: (b, 0, q)),
            pl.BlockSpec((1, IDX_HEADS, tq), lambda b, q: (b, 0, q)),
            pl.BlockSpec((1, S, KA_W), lambda b, q: (b, 0, 0)),
            pl.BlockSpec((1, nc, VA_W, tk), lambda b, q: (b, 0, 0, 0)),
            pl.BlockSpec((1, S, KI_W), lambda b, q: (b, 0, 0)),
        ],
        out_specs=pl.BlockSpec((1, tq, QA_W), lambda b, q: (b, q, 0)),
        out_shape=jax.ShapeDtypeStruct((B, S, QA_W), BF16),
        scratch_shapes=[
            pltpu.VMEM((N_KV_HEADS_A, LANES, hpg * tq), BF16),
            pltpu.VMEM((LANES, IDX_HEADS * tq), BF16),
            pltpu.VMEM((nc, tk, tq), jnp.int32),
            pltpu.VMEM((N_KV_HEADS_A, 1, hpg * tq), F32),
            pltpu.VMEM((N_KV_HEADS_A, LANES, hpg * tq), F32),
        ],
        compiler_params=pltpu.CompilerParams(dimension_semantics=("arbitrary", "arbitrary"),
                                             vmem_limit_bytes=VMEM_LIMIT),
        name="dsa_attn",
    )(qat, qit, wit, k, vt, ki)


CONV_IN_W = 2 * CONV_CH
QM_W = MEM_HEADS * MEM_HEAD_DIM


def _mixer_kernel(x_ref, oa_ref, g_mix_ref, wc_ref, wq_ref, wg_ref, bg_ref, cw_ref, cb_ref, lng_ref, lnb_ref,
                  gqm_ref, kmt_ref, vm_ref, wa_ref, wb_ref, wm_ref, wo_ref, gffn_ref, wrh_ref, wrl_ref, br_ref,
                  x1_ref, h2_ref, comb_ref, ubuf, *, tt):
    D = x_ref.shape[-1]
    x = x_ref[0]
    h = (x * lax.rsqrt(jnp.mean(x * x, axis=-1, keepdims=True) + EPS) * g_mix_ref[...]).astype(BF16)

    cin = _dot(h, wc_ref[...])
    u = cin[:, :CONV_CH] * jax.nn.sigmoid(cin[:, CONV_CH:])

    @pl.when(pl.program_id(1) == 0)
    def _():
        ubuf[0:CONV_HALO, :] = jnp.zeros((CONV_HALO, CONV_CH), F32)

    ubuf[CONV_HALO:CONV_HALO + tt, :] = u
    y = jnp.zeros((tt, CONV_CH), F32) + cb_ref[...]
    for j in range(CONV_WIDTH):
        off = CONV_HALO - (CONV_WIDTH - 1) + j
        y = y + cw_ref[j:j + 1, :] * ubuf[off:off + tt, :]
    halo = ubuf[tt:tt + CONV_HALO, :]
    ubuf[0:CONV_HALO, :] = halo
    mu = jnp.mean(y, axis=-1, keepdims=True)
    yc = y - mu
    var = jnp.mean(yc * yc, axis=-1, keepdims=True)
    yn = yc * lax.rsqrt(var + EPS) * lng_ref[...] + lnb_ref[...]
    o_b = (yn * jax.nn.sigmoid(yn)).astype(BF16)

    qm = _dot(h, wq_ref[...])
    o_m = []
    for hh in range(MEM_HEADS):
        q = qm[:, hh * MEM_HEAD_DIM:(hh + 1) * MEM_HEAD_DIM]
        q = q * lax.rsqrt(jnp.mean(q * q, axis=-1, keepdims=True) + EPS) * gqm_ref[...] * (MEM_HEAD_DIM ** -0.5)
        s = _dot(q.astype(BF16), kmt_ref[0, hh])
        p = jnp.exp(s - jnp.max(s, axis=-1, keepdims=True))
        l = jnp.sum(p, axis=-1, keepdims=True)
        o_m.append((_dot(p.astype(BF16), vm_ref[0, hh]) / l).astype(BF16))
    o_m = jnp.concatenate(o_m, axis=-1)

    gates = jax.nn.sigmoid(_dot(h, wg_ref[...]) + bg_ref[...])
    merged = (gates[:, :D] * _dot(oa_ref[0], wa_ref[...])
              + gates[:, D:2 * D] * _dot(o_b, wb_ref[...])
              + gates[:, 2 * D:] * _dot(o_m, wm_ref[...]))
    x1 = x + _dot(merged.astype(BF16), wo_ref[...])
    x1_ref[0] = x1

    h2 = x1 * lax.rsqrt(jnp.mean(x1 * x1, axis=-1, keepdims=True) + EPS) * gffn_ref[...]
    h2_ref[0] = h2.astype(BF16)
    hi, lo = _split_bf16(h2)
    logits = _dot(hi, wrh_ref[...]) + _dot(hi, wrl_ref[...]) + _dot(lo, wrh_ref[...]) + br_ref[...]
    lane = lax.broadcasted_iota(jnp.int32, logits.shape, 1)
    ninf = jnp.float32(-jnp.inf)
    big = jnp.int32(1 << 20)
    is_g = (lane >= N_EXPERTS) & (lane < N_EXPERTS + N_GROUPS)
    gl = jnp.where(is_g, logits, ninf)
    gmax = jnp.max(gl, axis=-1, keepdims=True)
    gsel = jnp.min(jnp.where(gl == gmax, lane, big), axis=-1, keepdims=True) - N_EXPERTS
    p_sel = 1.0 / jnp.sum(jnp.where(is_g, jnp.exp(gl - gmax), 0.0), axis=-1, keepdims=True)
    in_grp = (lane < N_EXPERTS) & ((lane >> 3) == gsel)
    el = jnp.where(in_grp, logits, ninf)
    v1 = jnp.max(el, axis=-1, keepdims=True)
    i1 = jnp.min(jnp.where(el == v1, lane, big), axis=-1, keepdims=True)
    el2 = jnp.where(lane == i1, ninf, el)
    v2 = jnp.max(el2, axis=-1, keepdims=True)
    i2 = jnp.min(jnp.where(el2 == v2, lane, big), axis=-1, keepdims=True)
    e2 = jnp.exp(v2 - v1)
    den = 1.0 + e2
    comb_ref[0] = jnp.where(lane == i1, (1.0 / den) * p_sel, jnp.where(lane == i2, (e2 / den) * p_sel, 0.0))


def _mixer(x, oa, g_mix, wc, wq, wg, bg, cw, cb, lng, lnb, gqm, kmt, vm, wa, wb, wm, wo, gffn, wrh, wrl, br, tt):
    B, S, D = x.shape
    M = kmt.shape[-1]
    const = lambda b, s: (0, 0)
    tile = lambda w: pl.BlockSpec((1, tt, w), lambda b, s: (b, s, 0))
    full = lambda a: pl.BlockSpec(a.shape, const)
    return pl.pallas_call(
        functools.partial(_mixer_kernel, tt=tt),
        grid=(B, S // tt),
        in_specs=[
            tile(D), tile(QA_W), full(g_mix), full(wc), full(wq), full(wg), full(bg), full(cw), full(cb),
            full(lng), full(lnb), full(gqm),
            pl.BlockSpec((1, MEM_HEADS, MEM_HEAD_DIM, M), lambda b, s: (b, 0, 0, 0)),
            pl.BlockSpec((1, MEM_HEADS, M, MEM_HEAD_DIM), lambda b, s: (b, 0, 0, 0)),
            full(wa), full(wb), full(wm), full(wo), full(gffn), full(wrh), full(wrl), full(br),
        ],
        out_specs=[tile(D), tile(D), tile(LANES)],
        out_shape=[
            jax.ShapeDtypeStruct((B, S, D), F32),
            jax.ShapeDtypeStruct((B, S, D), BF16),
            jax.ShapeDtypeStruct((B, S, LANES), F32),
        ],
        scratch_shapes=[pltpu.VMEM((CONV_HALO + tt, CONV_CH), F32)],
        compiler_params=pltpu.CompilerParams(dimension_semantics=("arbitrary", "arbitrary"),
                                             vmem_limit_bytes=VMEM_LIMIT),
        name="mixer",
    )(x, oa, g_mix, wc, wq, wg, bg, cw, cb, lng, lnb, gqm, kmt, vm, wa, wb, wm, wo, gffn, wrh, wrl, br)


def _moe_kernel(h2_ref, x1_ref, comb_ref, wup_ref, wdn_ref, o_ref):
    n = pl.program_id(1)

    @pl.when(n == 0)
    def _():
        o_ref[...] = x1_ref[...]

    up = _dot(h2_ref[...], wup_ref[0])
    act = jax.nn.silu(up[:, :D_FF_EXPERT]) * up[:, D_FF_EXPERT:]
    comb = comb_ref[...]
    lane = lax.broadcasted_iota(jnp.int32, comb.shape, 1)
    c = jnp.sum(jnp.where(lane == n, comb, 0.0), axis=-1, keepdims=True)
    o_ref[...] += _dot((act * c).astype(BF16), wdn_ref[0])


def _moe(h2, x1, comb, w_up, w_down, tm):
    T, D = h2.shape
    return pl.pallas_call(
        _moe_kernel,
        grid=(T // tm, N_EXPERTS),
        in_specs=[
            pl.BlockSpec((tm, D), lambda t, n: (t, 0)),
            pl.BlockSpec((tm, D), lambda t, n: (t, 0)),
            pl.BlockSpec((tm, LANES), lambda t, n: (t, 0)),
            pl.BlockSpec((1, D, 2 * D_FF_EXPERT), lambda t, n: (n, 0, 0)),
            pl.BlockSpec((1, D_FF_EXPERT, D), lambda t, n: (n, 0, 0)),
        ],
        out_specs=pl.BlockSpec((tm, D), lambda t, n: (t, 0)),
        out_shape=jax.ShapeDtypeStruct((T, D), F32),
        compiler_params=pltpu.CompilerParams(dimension_semantics=("arbitrary", "arbitrary"),
                                             vmem_limit_bytes=VMEM_LIMIT),
        name="moe",
    )(h2, x1, comb, w_up, w_down)


def _pick_tile(n, pref):
    t = min(n, pref)
    assert n % t == 0, (n, t)
    return t


def _rope_tables(S):
    half = HEAD_DIM_A // 2
    inv = ROPE_THETA ** (-jnp.arange(half, dtype=F32) / half)
    ang = jnp.arange(S, dtype=jnp.int32).astype(F32)[:, None] * inv[None, :]
    cos, sin = jnp.cos(ang), jnp.sin(ang)
    cos_t = jnp.concatenate([cos, cos, cos, cos], axis=-1)
    sin_t = jnp.concatenate([-sin, sin, -sin, sin], axis=-1)
    return cos_t, sin_t


def _layer(x, mem, g_mix, w_in, b_gate, g_qa, g_ka, g_idx_k, conv_w, conv_b, ln_g, ln_b,
           g_mem, w_mem_kv, g_qm, g_km, w_br_a, w_br_b, w_br_m, w_o, g_ffn,
           w_rg, b_rg, w_re, b_re, w_up, w_down):
    B, S, D = x.shape
    topk = min(TOPK_MAX, S // 4)
    tk = _pick_tile(S, 512)
    tq = _pick_tile(tk, 128)
    tt = _pick_tile(S, 256)
    tm = _pick_tile(B * S, 1024)

    sizes = (QA_W, N_KV_HEADS_A * HEAD_DIM_A, N_KV_HEADS_A * HEAD_DIM_A, QI_W, IDX_DIM, IDX_HEADS,
             CONV_IN_W, QM_W, N_BRANCHES * D)
    offs = np.concatenate([[0], np.cumsum(sizes)])
    col = lambda i: w_in[:, offs[i]:offs[i + 1]]
    w_qa, w_ka, w_va, w_qi, w_ki, w_wi, w_conv, w_qm, w_gate = (col(i) for i in range(9))
    hd = HEAD_DIM_A
    zeros_hd = jnp.zeros((D, hd), F32)
    ka_dup = [w_ka[:, g * hd:(g + 1) * hd] for g in range(N_KV_HEADS_A) for _ in range(2)]
    va_aug = [w for g in range(N_KV_HEADS_A) for w in (w_va[:, g * hd:(g + 1) * hd], zeros_hd)]
    w_dsa = jnp.concatenate(
        [w_qa, w_qi] + ka_dup + va_aug + [w_ki, w_ki, w_wi, jnp.zeros((D, WI_W - IDX_HEADS), F32)], axis=1).astype(BF16)
    vbias = jnp.tile(jnp.concatenate([jnp.zeros((hd,), F32), jnp.ones((hd,), F32)]), N_KV_HEADS_A).reshape(1, VA_W)
    seg = np.arange(QA_W) // hd
    bd = jnp.asarray(seg[:, None] == seg[None, :], BF16)
    cos_t, sin_t = _rope_tables(S)

    kmt, vm = _mem_kv(mem, g_mem, w_mem_kv, g_km)
    qat, qit, k, vt, ki, wit = _dsa_proj(
        x, g_mix.reshape(1, D), w_dsa, vbias,
        jnp.tile(g_qa, N_HEADS_A).reshape(1, QA_W), jnp.tile(g_ka, KA_W // hd).reshape(1, KA_W),
        jnp.tile(g_idx_k, KI_W // IDX_DIM).reshape(1, KI_W), cos_t, sin_t, bd, tk)
    o_a = _dsa_attn(qat, qit, wit, k, vt, ki, tq, topk)

    w_r = jnp.concatenate([w_re, w_rg, jnp.zeros((D, LANES - N_EXPERTS - N_GROUPS), F32)], axis=1)
    b_r = jnp.concatenate([b_re, b_rg, jnp.zeros((LANES - N_EXPERTS - N_GROUPS,), F32)]).reshape(1, LANES)
    w_r_hi = w_r.astype(BF16)
    w_r_lo = (w_r - w_r_hi.astype(F32)).astype(BF16)
    x1, h2, comb = _mixer(
        x, o_a, g_mix.reshape(1, D), w_conv.astype(BF16), w_qm.astype(BF16), w_gate.astype(BF16),
        b_gate.reshape(1, -1), conv_w.reshape(CONV_WIDTH, CONV_CH), conv_b.reshape(1, CONV_CH),
        ln_g.reshape(1, CONV_CH), ln_b.reshape(1, CONV_CH), g_qm.reshape(1, MEM_HEAD_DIM), kmt, vm,
        w_br_a.astype(BF16), w_br_b.astype(BF16), w_br_m.astype(BF16), w_o.astype(BF16),
        g_ffn.reshape(1, D), w_r_hi, w_r_lo, b_r, tt)

    out = _moe(h2.reshape(B * S, D), x1.reshape(B * S, D), comb.reshape(B * S, LANES),
               w_up.astype(BF16), w_down.astype(BF16), tm)
    return out.reshape(B, S, D)


def kernel(x, mem, g_mix, w_in, b_gate, g_qa, g_ka, g_idx_k, conv_w, conv_b, ln_g, ln_b, g_mem, w_mem_kv, g_qm, g_km,
           w_br_a, w_br_b, w_br_m, w_o, g_ffn, w_rg, b_rg, w_re, b_re, w_up, w_down):
    params = (g_mix, w_in, b_gate, g_qa, g_ka, g_idx_k, conv_w, conv_b, ln_g, ln_b, g_mem, w_mem_kv, g_qm, g_km,
              w_br_a, w_br_b, w_br_m, w_o, g_ffn, w_rg, b_rg, w_re, b_re, w_up, w_down)
    for l in range(g_mix.shape[0]):
        x = _layer(x, mem, *(p[l] for p in params))
    return x
```

```python
import functools

import jax
import jax.numpy as jnp
import numpy as np
from jax import lax
from jax.experimental import pallas as pl
from jax.experimental.pallas import tpu as pltpu

N_HEADS_A = 8
N_KV_HEADS_A = 2
HEAD_DIM_A = 64
IDX_HEADS = 8
IDX_DIM = 64
TOPK_MAX = 256
CONV_CH = 512
CONV_WIDTH = 31
MEM_HEADS = 4
MEM_HEAD_DIM = 128
N_BRANCHES = 3
N_GROUPS = 4
EXPERTS_PER_GROUP = 8
N_EXPERTS = N_GROUPS * EXPERTS_PER_GROUP
D_FF_EXPERT = 256
ROPE_THETA = 10000.0
EPS = 1e-6

LANES = 128
SUBLANES = 8
VMEM_LIMIT = 56 * 1024 * 1024

INT_MIN = -2 ** 31
LOG2_E = 1.4426950408889634
NEG_BIG = -1e30
CONV_HALO = 32

BF16 = jnp.bfloat16
F32 = jnp.float32


def _dot(a, b):
    return jnp.dot(a, b, preferred_element_type=F32)


def _sigmoid(x):
    return 0.5 * jnp.tanh(0.5 * x) + 0.5


def _split_bf16(x):
    hi = x.astype(BF16)
    lo = (x - hi.astype(F32)).astype(BF16)
    return hi, lo


def _seg_sum(xsq, ones_bd):
    hi, lo = _split_bf16(xsq)
    return _dot(hi, ones_bd) + _dot(lo, ones_bd)


def _rot_half(x, head_dim):
    n = x.shape[-1]
    half = head_dim // 2
    lane = lax.broadcasted_iota(jnp.int32, x.shape, x.ndim - 1)
    is_lo = (lane & (head_dim - 1)) < half
    return jnp.where(is_lo, pltpu.roll(x, n - half, x.ndim - 1), pltpu.roll(x, half, x.ndim - 1))


def _tile_lanes(x, reps):
    return jnp.concatenate([x] * reps, axis=-1) if reps > 1 else x


def _mem_kv_kernel(mem_ref, g_mem_ref, w_ref, g_km_ref, kt_ref, v_ref):
    m = mem_ref[0]
    hm = m * lax.rsqrt(jnp.mean(m * m, axis=-1, keepdims=True) + EPS) * g_mem_ref[...]
    kv = _dot(hm.astype(BF16), w_ref[...])
    hd = MEM_HEADS * MEM_HEAD_DIM
    for h in range(MEM_HEADS):
        k = kv[:, h * MEM_HEAD_DIM:(h + 1) * MEM_HEAD_DIM]
        k = k * lax.rsqrt(jnp.mean(k * k, axis=-1, keepdims=True) + EPS) * g_km_ref[...]
        kt_ref[0, h] = k.T.astype(BF16)
        v_ref[0, h] = kv[:, hd + h * MEM_HEAD_DIM: hd + (h + 1) * MEM_HEAD_DIM].astype(BF16)


def _mem_kv(mem, g_mem, w_mem_kv, g_km):
    B, M, D = mem.shape
    hd2 = 2 * MEM_HEADS * MEM_HEAD_DIM
    return pl.pallas_call(
        _mem_kv_kernel,
        grid=(B,),
        in_specs=[
            pl.BlockSpec((1, M, D), lambda b: (b, 0, 0)),
            pl.BlockSpec((1, D), lambda b: (0, 0)),
            pl.BlockSpec((D, hd2), lambda b: (0, 0)),
            pl.BlockSpec((1, MEM_HEAD_DIM), lambda b: (0, 0)),
        ],
        out_specs=[
            pl.BlockSpec((1, MEM_HEADS, MEM_HEAD_DIM, M), lambda b: (b, 0, 0, 0)),
            pl.BlockSpec((1, MEM_HEADS, M, MEM_HEAD_DIM), lambda b: (b, 0, 0, 0)),
        ],
        out_shape=[
            jax.ShapeDtypeStruct((B, MEM_HEADS, MEM_HEAD_DIM, M), BF16),
            jax.ShapeDtypeStruct((B, MEM_HEADS, M, MEM_HEAD_DIM), BF16),
        ],
        compiler_params=pltpu.CompilerParams(dimension_semantics=("arbitrary",)),
        name="mem_kv",
    )(mem, g_mem.reshape(1, D), w_mem_kv.astype(BF16), g_km.reshape(1, MEM_HEAD_DIM))


QA_W = N_HEADS_A * HEAD_DIM_A
QI_W = IDX_HEADS * IDX_DIM
KA_W = 2 * N_KV_HEADS_A * HEAD_DIM_A
VA_W = 2 * N_KV_HEADS_A * HEAD_DIM_A
KI_W = 2 * IDX_DIM
WI_W = LANES
DSA_COLS = QA_W + QI_W + KA_W + VA_W + KI_W + WI_W


def _dsa_proj_kernel(x_ref, g_mix_ref, w_ref, vbias_ref, gq_ref, gk_ref, gki_ref, cos_ref, sin_ref, bd_ref,
                     qat_ref, qit_ref, k_ref, vt_ref, ki_ref, wit_ref):
    x = x_ref[0]
    h = x * lax.rsqrt(jnp.mean(x * x, axis=-1, keepdims=True) + EPS) * g_mix_ref[...]
    p = _dot(h.astype(BF16), w_ref[...])
    cos1 = cos_ref[...]
    sin1 = sin_ref[...]
    cos4, sin4 = _tile_lanes(cos1, 4), _tile_lanes(sin1, 4)
    cos2, sin2 = _tile_lanes(cos1, 2), _tile_lanes(sin1, 2)
    bd = bd_ref[...]
    inv_hd = 1.0 / HEAD_DIM_A

    def rope(v, c, s):
        return v * c + _rot_half(v, HEAD_DIM_A) * s

    o = 0
    qa = p[:, o:o + QA_W]; o += QA_W
    qa = qa * lax.rsqrt(_seg_sum(qa * qa, bd) * inv_hd + EPS) * gq_ref[...]
    qat_ref[0] = (rope(qa, cos4, sin4) * (HEAD_DIM_A ** -0.5 * LOG2_E)).T.astype(BF16)

    qi = p[:, o:o + QI_W]; o += QI_W
    qit_ref[0] = (rope(qi, cos4, sin4) * (IDX_DIM ** -0.5)).T.astype(BF16)

    ka = p[:, o:o + KA_W]; o += KA_W
    ka = ka * lax.rsqrt(_seg_sum(ka * ka, bd[:KA_W, :KA_W]) * inv_hd + EPS) * gk_ref[...]
    k_ref[0] = rope(ka, cos2, sin2).astype(BF16)

    vt_ref[0, 0] = (p[:, o:o + VA_W] + vbias_ref[...]).T.astype(BF16); o += VA_W

    ki = p[:, o:o + KI_W]; o += KI_W
    ki = ki * lax.rsqrt(_seg_sum(ki * ki, bd[:KI_W, :KI_W]) * (1.0 / IDX_DIM) + EPS) * gki_ref[...]
    ki_ref[0] = rope(ki, cos1, sin1).astype(BF16)

    wit_ref[0] = (p[:, o:o + WI_W] * (IDX_HEADS ** -0.5)).T[:IDX_HEADS, :]


def _dsa_proj(x, g_mix, w_dsa, vbias, g_qa, g_ka, g_idx_k, cos_t, sin_t, bd, tt):
    B, S, D = x.shape
    nt = S // tt
    const = lambda b, s: (0, 0)
    return pl.pallas_call(
        _dsa_proj_kernel,
        grid=(B, nt),
        in_specs=[
            pl.BlockSpec((1, tt, D), lambda b, s: (b, s, 0)),
            pl.BlockSpec((1, D), const),
            pl.BlockSpec((D, DSA_COLS), const),
            pl.BlockSpec((1, VA_W), const),
            pl.BlockSpec((1, QA_W), const),
            pl.BlockSpec((1, KA_W), const),
            pl.BlockSpec((1, KI_W), const),
            pl.BlockSpec((tt, LANES), lambda b, s: (s, 0)),
            pl.BlockSpec((tt, LANES), lambda b, s: (s, 0)),
            pl.BlockSpec((QA_W, QA_W), const),
        ],
        out_specs=[
            pl.BlockSpec((1, QA_W, tt), lambda b, s: (b, 0, s)),
            pl.BlockSpec((1, QI_W, tt), lambda b, s: (b, 0, s)),
            pl.BlockSpec((1, tt, KA_W), lambda b, s: (b, s, 0)),
            pl.BlockSpec((1, 1, VA_W, tt), lambda b, s: (b, s, 0, 0)),
            pl.BlockSpec((1, tt, KI_W), lambda b, s: (b, s, 0)),
            pl.BlockSpec((1, IDX_HEADS, tt), lambda b, s: (b, 0, s)),
        ],
        out_shape=[
            jax.ShapeDtypeStruct((B, QA_W, S), BF16),
            jax.ShapeDtypeStruct((B, QI_W, S), BF16),
            jax.ShapeDtypeStruct((B, S, KA_W), BF16),
            jax.ShapeDtypeStruct((B, nt, VA_W, tt), BF16),
            jax.ShapeDtypeStruct((B, S, KI_W), BF16),
            jax.ShapeDtypeStruct((B, IDX_HEADS, S), F32),
        ],
        compiler_params=pltpu.CompilerParams(dimension_semantics=("arbitrary", "arbitrary"),
                                             vmem_limit_bytes=VMEM_LIMIT),
        name="dsa_proj",
    )(x, g_mix, w_dsa, vbias, g_qa, g_ka, g_idx_k, cos_t, sin_t, bd)


def _key_of(f):
    bits = lax.bitcast_convert_type(f, jnp.int32)
    return bits ^ ((bits >> 31) & jnp.int32(0x7FFFFFFF))


def _pipelined_chunks(nchunks, last_chunk, matmul, finish):
    matmul(0, 0)

    def pair(i, carry):
        c0 = 2 * i
        matmul(jnp.minimum(c0 + 1, last_chunk), 1)
        finish(c0, 0)

        @pl.when(c0 + 1 < nchunks)
        def _():
            matmul(jnp.minimum(c0 + 2, last_chunk), 0)
            finish(c0 + 1, 1)

        return carry

    lax.fori_loop(0, (nchunks + 1) // 2, pair, 0)


def _dsa_attn_kernel(qat_ref, qit_ref, wit_ref, k_ref, vt_ref, ki_ref, o_ref,
                     qa_s, qi_s, keys_s, buf0, buf1, m_s, acc_s, *, tq, tk, topk, idx_bits):
    qt = pl.program_id(1)
    q0 = qt * tq
    nchunks = q0 // tk + 1
    last_chunk = nchunks - 1
    hpg = N_HEADS_A // N_KV_HEADS_A
    gw = hpg * tq
    half_pairs = hpg // 2
    bufs = (buf0, buf1)
    dim = lax.broadcasted_iota(jnp.int32, (LANES, tq), 0)
    lo_half = dim < HEAD_DIM_A

    for j in range(IDX_HEADS // 2):
        pair = qit_ref[0, j * LANES:(j + 1) * LANES, :]
        qi_s[:, (2 * j) * tq:(2 * j + 1) * tq] = jnp.where(lo_half, pair, jnp.zeros_like(pair))
        qi_s[:, (2 * j + 1) * tq:(2 * j + 2) * tq] = jnp.where(lo_half, jnp.zeros_like(pair), pair)
    for g in range(N_KV_HEADS_A):
        for jj in range(half_pairs):
            pair = qat_ref[0, (g * half_pairs + jj) * LANES:(g * half_pairs + jj + 1) * LANES, :]
            qa_s[:, g * gw + jj * tq:g * gw + (jj + 1) * tq] = jnp.where(lo_half, pair, jnp.zeros_like(pair))
            qa_s[:, g * gw + (half_pairs + jj) * tq:g * gw + (half_pairs + jj + 1) * tq] = (
                jnp.where(lo_half, jnp.zeros_like(pair), pair))

    key_in_chunk = lax.broadcasted_iota(jnp.int32, (tk, tq), 0)
    q_pos = q0 + lax.broadcasted_iota(jnp.int32, (tk, tq), 1)

    def score_chunk(c):
        r0 = pl.multiple_of(c * tk, tk)
        logits = _dot(ki_ref[0, pl.ds(r0, tk), :], qi_s[...])
        score = jnp.zeros((tk, tq), F32)
        for h in range(IDX_HEADS):
            score = score + wit_ref[0, h:h + 1, :] * jnp.maximum(logits[:, h * tq:(h + 1) * tq], 0.0)
        keys_s[c] = jnp.where(key_in_chunk + c * tk <= q_pos, _key_of(score), jnp.int32(INT_MIN))

    def score_pair(i, carry):
        score_chunk(2 * i)
        score_chunk(2 * i + 1)
        return carry

    lax.fori_loop(0, nchunks // 2, score_pair, 0)

    @pl.when(nchunks % 2 == 1)
    def _():
        score_chunk(last_chunk)

    def count_keys(pred_fn):
        def body(c, acc):
            hit = pred_fn(keys_s[c], c * tk).astype(jnp.int32)
            return acc + jnp.sum(hit.reshape(tk // 8, 8, tq), axis=0)
        part = lax.fori_loop(0, nchunks, body, jnp.zeros((8, tq), jnp.int32))
        return jnp.sum(part, axis=0, keepdims=True)

    total = q0 + lax.broadcasted_iota(jnp.int32, (1, tq), 1) + 1
    has_tau = total >= topk

    def unsettled(cnt):
        return jnp.max(jnp.where(has_tau & (cnt != topk), 1, 0))

    def radix_step(state):
        tau, cnt, i, _ = state
        cand = tau + (jnp.int32(1) << (31 - i))
        c = count_keys(lambda k, _: k >= cand)
        up = c >= topk
        tau, cnt = jnp.where(up, cand, tau), jnp.where(up, c, cnt)
        return tau, cnt, i + 1, unsettled(cnt)

    tau, cnt, _, _ = lax.while_loop(
        lambda s: (s[2] < 32) & (s[3] > 0), radix_step,
        (jnp.full((1, tq), INT_MIN, jnp.int32), total, jnp.int32(0), unsettled(total)))
    tau = jnp.where(has_tau, jnp.maximum(tau, jnp.int32(INT_MIN + 1)), jnp.int32(INT_MIN))

    tied = has_tau & (cnt > topk)

    def tie_cut():
        need = topk - count_keys(lambda k, _: k > tau)

        def idx_step(i, j):
            bit = jnp.int32(1) << (idx_bits - 1 - i)
            test = j + bit - 1
            f = count_keys(lambda k, base: (k == tau) & (base + key_in_chunk <= test))
            return jnp.where(f < need, j + bit, j)
        return lax.fori_loop(0, idx_bits, idx_step, jnp.zeros((1, tq), jnp.int32))

    jcut = lax.cond(jnp.max(jnp.where(tied, 1, 0)) > 0, tie_cut, lambda: jnp.zeros((1, tq), jnp.int32))
    jcut = jnp.where(tied, jcut, jnp.where(has_tau, jnp.int32(2 ** 30), jnp.int32(-1)))

    m_s[...] = jnp.full(m_s.shape, NEG_BIG, F32)
    acc_s[...] = jnp.zeros(acc_s.shape, F32)

    def attn_matmul(c, slot):
        r0 = pl.multiple_of(c * tk, tk)
        for g in range(N_KV_HEADS_A):
            bufs[slot][:, g * gw:(g + 1) * gw] = _dot(k_ref[0, pl.ds(r0, tk), g * LANES:(g + 1) * LANES],
                                                      qa_s[:, g * gw:(g + 1) * gw])

    def attn_finish(c, slot):
        k = keys_s[c]
        kpos = key_in_chunk + c * tk
        bias = jnp.where(k > tau, 0.0, jnp.where(k == tau, jnp.where(kpos <= jcut, 0.0, NEG_BIG), NEG_BIG))
        bias4 = jnp.concatenate([bias] * hpg, axis=1)
        for g in range(N_KV_HEADS_A):
            s = bufs[slot][:, g * gw:(g + 1) * gw] + bias4
            m_old = m_s[g]
            m_new = jnp.maximum(m_old, jnp.max(s, axis=0, keepdims=True))
            p = jnp.exp2(s - m_new)
            alpha = jnp.exp2(m_old - m_new)
            pv = _dot(vt_ref[0, c, g * LANES:(g + 1) * LANES, :], p.astype(BF16))
            acc_s[g] = alpha * acc_s[g] + pv
            m_s[g] = m_new

    _pipelined_chunks(nchunks, last_chunk, attn_matmul, attn_finish)

    for g in range(N_KV_HEADS_A):
        for jj in range(half_pairs):
            a_e = acc_s[g, :, jj * tq:(jj + 1) * tq]
            a_o = acc_s[g, :, (half_pairs + jj) * tq:(half_pairs + jj + 1) * tq]
            o_pair = jnp.concatenate([a_e[:HEAD_DIM_A] / a_e[HEAD_DIM_A:], a_o[:HEAD_DIM_A] / a_o[HEAD_DIM_A:]], axis=0)
            j = g * half_pairs + jj
            o_ref[0, :, j * LANES:(j + 1) * LANES] = o_pair.T.astype(BF16)


def _dsa_attn(qat, qit, wit, k, vt, ki, tq, topk):
    B, _, S = qat.shape
    nc, tk = vt.shape[1], vt.shape[3]
    hpg = N_HEADS_A // N_KV_HEADS_A
    idx_bits = max(1, int(np.ceil(np.log2(S))))
    assert IDX_HEADS == N_HEADS_A
    kernel = functools.partial(_dsa_attn_kernel, tq=tq, tk=tk, topk=topk, idx_bits=idx_bits)
    return pl.pallas_call(
        kernel,
        grid=(B, S // tq),
        in_specs=[
            pl.BlockSpec((1, QA_W, tq), lambda b, q: (b, 0, q)),
            pl.BlockSpec((1, QI_W, tq), lambda b, q: (b, 0, q)),
            pl.BlockSpec((1, IDX_HEADS, tq), lambda b, q: (b, 0, q)),
            pl.BlockSpec((1, S, KA_W), lambda b, q: (b, 0, 0)),
            pl.BlockSpec((1, nc, VA_W, tk), lambda b, q: (b, 0, 0, 0)),
            pl.BlockSpec((1, S, KI_W), lambda b, q: (b, 0, 0)),
        ],
        out_specs=pl.BlockSpec((1, tq, QA_W), lambda b, q: (b, q, 0)),
        out_shape=jax.ShapeDtypeStruct((B, S, QA_W), BF16),
        scratch_shapes=[
            pltpu.VMEM((LANES, N_HEADS_A * tq), BF16),
            pltpu.VMEM((LANES, IDX_HEADS * tq), BF16),
            pltpu.VMEM((nc, tk, tq), jnp.int32),
            pltpu.VMEM((tk, IDX_HEADS * tq), F32),
            pltpu.VMEM((tk, IDX_HEADS * tq), F32),
            pltpu.VMEM((N_KV_HEADS_A, 1, hpg * tq), F32),
            pltpu.VMEM((N_KV_HEADS_A, LANES, hpg * tq), F32),
        ],
        compiler_params=pltpu.CompilerParams(dimension_semantics=("arbitrary", "arbitrary"),
                                             vmem_limit_bytes=VMEM_LIMIT),
        name="dsa_attn",
    )(qat, qit, wit, k, vt, ki)


CONV_IN_W = 2 * CONV_CH
QM_W = MEM_HEADS * MEM_HEAD_DIM


def _mixer_kernel(x_ref, oa_ref, g_mix_ref, wc_ref, wq_ref, wg_ref, bg_ref, cw_ref, cb_ref, lng_ref, lnb_ref,
                  gqm_ref, kmt_ref, vm_ref, wa_ref, wb_ref, wm_ref, wo_ref, gffn_ref, wrh_ref, wrl_ref, br_ref,
                  x1_ref, h2_ref, comb_ref, ubuf, wbuf, *, tt):
    D = x_ref.shape[-1]
    x = x_ref[0]
    h = (x * lax.rsqrt(jnp.mean(x * x, axis=-1, keepdims=True) + EPS) * g_mix_ref[...]).astype(BF16)

    cin = _dot(h, wc_ref[...])
    u = cin[:, :CONV_CH] * _sigmoid(cin[:, CONV_CH:])

    @pl.when(pl.program_id(1) == 0)
    def _():
        ubuf[0:CONV_HALO, :] = jnp.zeros((CONV_HALO, CONV_CH), F32)

    ubuf[CONV_HALO:CONV_HALO + tt, :] = u
    y = jnp.zeros((tt, CONV_CH), F32) + cb_ref[...]
    first = CONV_HALO - (CONV_WIDTH - 1)
    for b in range(SUBLANES):
        taps = [j for j in range(CONV_WIDTH) if (first + j) % SUBLANES == b]
        if not taps:
            continue
        base = first + taps[0]
        if b == 0:
            window, w0 = ubuf, base
        else:
            span = taps[-1] - taps[0] + tt
            wbuf[0:span, :] = ubuf[base:base + span, :]
            window, w0 = wbuf, 0
        for j in taps:
            off = w0 + j - taps[0]
            y = y + cw_ref[j:j + 1, :] * window[off:off + tt, :]
    halo = ubuf[tt:tt + CONV_HALO, :]
    ubuf[0:CONV_HALO, :] = halo
    mu = jnp.mean(y, axis=-1, keepdims=True)
    yc = y - mu
    var = jnp.mean(yc * yc, axis=-1, keepdims=True)
    yn = yc * lax.rsqrt(var + EPS) * lng_ref[...] + lnb_ref[...]
    o_b = (yn * _sigmoid(yn)).astype(BF16)

    qm = _dot(h, wq_ref[...])
    o_m = []
    for hh in range(MEM_HEADS):
        q = qm[:, hh * MEM_HEAD_DIM:(hh + 1) * MEM_HEAD_DIM]
        q = q * lax.rsqrt(jnp.mean(q * q, axis=-1, keepdims=True) + EPS) * gqm_ref[...] * (MEM_HEAD_DIM ** -0.5)
        s = _dot(q.astype(BF16), kmt_ref[0, hh])
        p = jnp.exp(s - jnp.max(s, axis=-1, keepdims=True))
        l = jnp.sum(p, axis=-1, keepdims=True)
        o_m.append((_dot(p.astype(BF16), vm_ref[0, hh]) / l).astype(BF16))
    o_m = jnp.concatenate(o_m, axis=-1)

    gates = _sigmoid(_dot(h, wg_ref[...]) + bg_ref[...])
    merged = (gates[:, :D] * _dot(oa_ref[0], wa_ref[...])
              + gates[:, D:2 * D] * _dot(o_b, wb_ref[...])
              + gates[:, 2 * D:] * _dot(o_m, wm_ref[...]))
    x1 = x + _dot(merged.astype(BF16), wo_ref[...])
    x1_ref[0] = x1

    h2 = x1 * lax.rsqrt(jnp.mean(x1 * x1, axis=-1, keepdims=True) + EPS) * gffn_ref[...]
    h2_ref[0] = h2.astype(BF16)
    hi, lo = _split_bf16(h2)
    logits = _dot(hi, wrh_ref[...]) + _dot(hi, wrl_ref[...]) + _dot(lo, wrh_ref[...]) + br_ref[...]
    lane = lax.broadcasted_iota(jnp.int32, logits.shape, 1)
    ninf = jnp.float32(-jnp.inf)
    big = jnp.int32(1 << 20)
    is_g = (lane >= N_EXPERTS) & (lane < N_EXPERTS + N_GROUPS)
    gl = jnp.where(is_g, logits, ninf)
    gmax = jnp.max(gl, axis=-1, keepdims=True)
    gsel = jnp.min(jnp.where(gl == gmax, lane, big), axis=-1, keepdims=True) - N_EXPERTS
    p_sel = 1.0 / jnp.sum(jnp.where(is_g, jnp.exp(gl - gmax), 0.0), axis=-1, keepdims=True)
    in_grp = (lane < N_EXPERTS) & ((lane >> 3) == gsel)
    el = jnp.where(in_grp, logits, ninf)
    v1 = jnp.max(el, axis=-1, keepdims=True)
    i1 = jnp.min(jnp.where(el == v1, lane, big), axis=-1, keepdims=True)
    el2 = jnp.where(lane == i1, ninf, el)
    v2 = jnp.max(el2, axis=-1, keepdims=True)
    i2 = jnp.min(jnp.where(el2 == v2, lane, big), axis=-1, keepdims=True)
    e2 = jnp.exp(v2 - v1)
    den = 1.0 + e2
    comb_ref[0] = jnp.where(lane == i1, (1.0 / den) * p_sel, jnp.where(lane == i2, (e2 / den) * p_sel, 0.0))


def _mixer(x, oa, g_mix, wc, wq, wg, bg, cw, cb, lng, lnb, gqm, kmt, vm, wa, wb, wm, wo, gffn, wrh, wrl, br, tt):
    B, S, D = x.shape
    M = kmt.shape[-1]
    const = lambda b, s: (0, 0)
    tile = lambda w: pl.BlockSpec((1, tt, w), lambda b, s: (b, s, 0))
    full = lambda a: pl.BlockSpec(a.shape, const)
    return pl.pallas_call(
        functools.partial(_mixer_kernel, tt=tt),
        grid=(B, S // tt),
        in_specs=[
            tile(D), tile(QA_W), full(g_mix), full(wc), full(wq), full(wg), full(bg), full(cw), full(cb),
            full(lng), full(lnb), full(gqm),
            pl.BlockSpec((1, MEM_HEADS, MEM_HEAD_DIM, M), lambda b, s: (b, 0, 0, 0)),
            pl.BlockSpec((1, MEM_HEADS, M, MEM_HEAD_DIM), lambda b, s: (b, 0, 0, 0)),
            full(wa), full(wb), full(wm), full(wo), full(gffn), full(wrh), full(wrl), full(br),
        ],
        out_specs=[tile(D), tile(D), tile(LANES)],
        out_shape=[
            jax.ShapeDtypeStruct((B, S, D), F32),
            jax.ShapeDtypeStruct((B, S, D), BF16),
            jax.ShapeDtypeStruct((B, S, LANES), F32),
        ],
        scratch_shapes=[pltpu.VMEM((CONV_HALO + tt, CONV_CH), F32),
                        pltpu.VMEM((CONV_HALO + tt, CONV_CH), F32)],
        compiler_params=pltpu.CompilerParams(dimension_semantics=("arbitrary", "arbitrary"),
                                             vmem_limit_bytes=VMEM_LIMIT),
        name="mixer",
    )(x, oa, g_mix, wc, wq, wg, bg, cw, cb, lng, lnb, gqm, kmt, vm, wa, wb, wm, wo, gffn, wrh, wrl, br)


def _moe_kernel(h2_ref, x1_ref, comb_ref, wup_ref, wdn_ref, o_ref):
    n = pl.program_id(1)

    @pl.when(n == 0)
    def _():
        o_ref[...] = x1_ref[...]

    up = _dot(h2_ref[...], wup_ref[0])
    a = up[:, :D_FF_EXPERT]
    act = a * _sigmoid(a) * up[:, D_FF_EXPERT:]
    comb = comb_ref[...]
    lane = lax.broadcasted_iota(jnp.int32, comb.shape, 1)
    c = jnp.sum(jnp.where(lane == n, comb, 0.0), axis=-1, keepdims=True)
    o_ref[...] += _dot((act * c).astype(BF16), wdn_ref[0])


def _moe(h2, x1, comb, w_up, w_down, tm):
    T, D = h2.shape
    return pl.pallas_call(
        _moe_kernel,
        grid=(T // tm, N_EXPERTS),
        in_specs=[
            pl.BlockSpec((tm, D), lambda t, n: (t, 0)),
            pl.BlockSpec((tm, D), lambda t, n: (t, 0)),
            pl.BlockSpec((tm, LANES), lambda t, n: (t, 0)),
            pl.BlockSpec((1, D, 2 * D_FF_EXPERT), lambda t, n: (n, 0, 0)),
            pl.BlockSpec((1, D_FF_EXPERT, D), lambda t, n: (n, 0, 0)),
        ],
        out_specs=pl.BlockSpec((tm, D), lambda t, n: (t, 0)),
        out_shape=jax.ShapeDtypeStruct((T, D), F32),
        compiler_params=pltpu.CompilerParams(dimension_semantics=("arbitrary", "arbitrary"),
                                             vmem_limit_bytes=VMEM_LIMIT),
        name="moe",
    )(h2, x1, comb, w_up, w_down)


def _pick_tile(n, pref):
    t = min(n, pref)
    assert n % t == 0, (n, t)
    return t


def _rope_tables(S):
    half = HEAD_DIM_A // 2
    inv = ROPE_THETA ** (-jnp.arange(half, dtype=F32) / half)
    ang = jnp.arange(S, dtype=jnp.int32).astype(F32)[:, None] * inv[None, :]
    cos, sin = jnp.cos(ang), jnp.sin(ang)
    cos_t = jnp.concatenate([cos, cos, cos, cos], axis=-1)
    sin_t = jnp.concatenate([-sin, sin, -sin, sin], axis=-1)
    return cos_t, sin_t


def _layer(x, mem, g_mix, w_in, b_gate, g_qa, g_ka, g_idx_k, conv_w, conv_b, ln_g, ln_b,
           g_mem, w_mem_kv, g_qm, g_km, w_br_a, w_br_b, w_br_m, w_o, g_ffn,
           w_rg, b_rg, w_re, b_re, w_up, w_down):
    B, S, D = x.shape
    topk = min(TOPK_MAX, S // 4)
    tk = _pick_tile(S, 512)
    tq = _pick_tile(tk, 128)
    tt = _pick_tile(S, 512)
    tm = _pick_tile(B * S, 1024)

    sizes = (QA_W, N_KV_HEADS_A * HEAD_DIM_A, N_KV_HEADS_A * HEAD_DIM_A, QI_W, IDX_DIM, IDX_HEADS,
             CONV_IN_W, QM_W, N_BRANCHES * D)
    offs = np.concatenate([[0], np.cumsum(sizes)])
    col = lambda i: w_in[:, offs[i]:offs[i + 1]]
    w_qa, w_ka, w_va, w_qi, w_ki, w_wi, w_conv, w_qm, w_gate = (col(i) for i in range(9))
    hd = HEAD_DIM_A
    zeros_hd = jnp.zeros((D, hd), F32)
    ka_dup = [w_ka[:, g * hd:(g + 1) * hd] for g in range(N_KV_HEADS_A) for _ in range(2)]
    va_aug = [w for g in range(N_KV_HEADS_A) for w in (w_va[:, g * hd:(g + 1) * hd], zeros_hd)]
    w_dsa = jnp.concatenate(
        [w_qa, w_qi] + ka_dup + va_aug + [w_ki, w_ki, w_wi, jnp.zeros((D, WI_W - IDX_HEADS), F32)], axis=1).astype(BF16)
    vbias = jnp.tile(jnp.concatenate([jnp.zeros((hd,), F32), jnp.ones((hd,), F32)]), N_KV_HEADS_A).reshape(1, VA_W)
    seg = np.arange(QA_W) // hd
    bd = jnp.asarray(seg[:, None] == seg[None, :], BF16)
    cos_t, sin_t = _rope_tables(S)

    kmt, vm = _mem_kv(mem, g_mem, w_mem_kv, g_km)
    qat, qit, k, vt, ki, wit = _dsa_proj(
        x, g_mix.reshape(1, D), w_dsa, vbias,
        jnp.tile(g_qa, N_HEADS_A).reshape(1, QA_W), jnp.tile(g_ka, KA_W // hd).reshape(1, KA_W),
        jnp.tile(g_idx_k, KI_W // IDX_DIM).reshape(1, KI_W), cos_t, sin_t, bd, tk)
    o_a = _dsa_attn(qat, qit, wit, k, vt, ki, tq, topk)

    w_r = jnp.concatenate([w_re, w_rg, jnp.zeros((D, LANES - N_EXPERTS - N_GROUPS), F32)], axis=1)
    b_r = jnp.concatenate([b_re, b_rg, jnp.zeros((LANES - N_EXPERTS - N_GROUPS,), F32)]).reshape(1, LANES)
    w_r_hi = w_r.astype(BF16)
    w_r_lo = (w_r - w_r_hi.astype(F32)).astype(BF16)
    x1, h2, comb = _mixer(
        x, o_a, g_mix.reshape(1, D), w_conv.astype(BF16), w_qm.astype(BF16), w_gate.astype(BF16),
        b_gate.reshape(1, -1), conv_w.reshape(CONV_WIDTH, CONV_CH), conv_b.reshape(1, CONV_CH),
        ln_g.reshape(1, CONV_CH), ln_b.reshape(1, CONV_CH), g_qm.reshape(1, MEM_HEAD_DIM), kmt, vm,
        w_br_a.astype(BF16), w_br_b.astype(BF16), w_br_m.astype(BF16), w_o.astype(BF16),
        g_ffn.reshape(1, D), w_r_hi, w_r_lo, b_r, tt)

    out = _moe(h2.reshape(B * S, D), x1.reshape(B * S, D), comb.reshape(B * S, LANES),
               w_up.astype(BF16), w_down.astype(BF16), tm)
    return out.reshape(B, S, D)


def kernel(x, mem, g_mix, w_in, b_gate, g_qa, g_ka, g_idx_k, conv_w, conv_b, ln_g, ln_b, g_mem, w_mem_kv, g_qm, g_km,
           w_br_a, w_br_b, w_br_m, w_o, g_ffn, w_rg, b_rg, w_re, b_re, w_up, w_down):
    params = (g_mix, w_in, b_gate, g_qa, g_ka, g_idx_k, conv_w, conv_b, ln_g, ln_b, g_mem, w_mem_kv, g_qm, g_km,
              w_br_a, w_br_b, w_br_m, w_o, g_ffn, w_rg, b_rg, w_re, b_re, w_up, w_down)
    for l in range(g_mix.shape[0]):
        x = _layer(x, mem, *(p[l] for p in params))
    return x
```

```python
import functools

import jax
import jax.numpy as jnp
import numpy as np
from jax import lax
from jax.experimental import pallas as pl
from jax.experimental.pallas import tpu as pltpu

N_HEADS_A = 8
N_KV_HEADS_A = 2
HEAD_DIM_A = 64
IDX_HEADS = 8
IDX_DIM = 64
TOPK_MAX = 256
CONV_CH = 512
CONV_WIDTH = 31
MEM_HEADS = 4
MEM_HEAD_DIM = 128
N_BRANCHES = 3
N_GROUPS = 4
EXPERTS_PER_GROUP = 8
N_EXPERTS = N_GROUPS * EXPERTS_PER_GROUP
D_FF_EXPERT = 256
ROPE_THETA = 10000.0
EPS = 1e-6

LANES = 128
SUBLANES = 8
VMEM_LIMIT = 56 * 1024 * 1024

INT_MIN = -2 ** 31
LOG2_E = 1.4426950408889634
RADIX_CHECK_BIT = 26
NEG_BIG = -1e30
CONV_HALO = 32

BF16 = jnp.bfloat16
F32 = jnp.float32


def _dot(a, b):
    return jnp.dot(a, b, preferred_element_type=F32)


def _sigmoid(x):
    return 0.5 * jnp.tanh(0.5 * x) + 0.5


def _split_bf16(x):
    hi = x.astype(BF16)
    lo = (x - hi.astype(F32)).astype(BF16)
    return hi, lo


def _seg_sum(xsq, ones_bd):
    hi, lo = _split_bf16(xsq)
    return _dot(hi, ones_bd) + _dot(lo, ones_bd)


def _rot_half(x, head_dim):
    n = x.shape[-1]
    half = head_dim // 2
    lane = lax.broadcasted_iota(jnp.int32, x.shape, x.ndim - 1)
    is_lo = (lane & (head_dim - 1)) < half
    return jnp.where(is_lo, pltpu.roll(x, n - half, x.ndim - 1), pltpu.roll(x, half, x.ndim - 1))


def _tile_lanes(x, reps):
    return jnp.concatenate([x] * reps, axis=-1) if reps > 1 else x


def _mem_kv_kernel(mem_ref, g_mem_ref, w_ref, g_km_ref, kt_ref, v_ref):
    m = mem_ref[0]
    hm = m * lax.rsqrt(jnp.mean(m * m, axis=-1, keepdims=True) + EPS) * g_mem_ref[...]
    kv = _dot(hm.astype(BF16), w_ref[...])
    hd = MEM_HEADS * MEM_HEAD_DIM
    for h in range(MEM_HEADS):
        k = kv[:, h * MEM_HEAD_DIM:(h + 1) * MEM_HEAD_DIM]
        k = k * lax.rsqrt(jnp.mean(k * k, axis=-1, keepdims=True) + EPS) * g_km_ref[...]
        kt_ref[0, h] = k.T.astype(BF16)
        v_ref[0, h] = kv[:, hd + h * MEM_HEAD_DIM: hd + (h + 1) * MEM_HEAD_DIM].astype(BF16)


def _mem_kv(mem, g_mem, w_mem_kv, g_km):
    B, M, D = mem.shape
    hd2 = 2 * MEM_HEADS * MEM_HEAD_DIM
    return pl.pallas_call(
        _mem_kv_kernel,
        grid=(B,),
        in_specs=[
            pl.BlockSpec((1, M, D), lambda b: (b, 0, 0)),
            pl.BlockSpec((1, D), lambda b: (0, 0)),
            pl.BlockSpec((D, hd2), lambda b: (0, 0)),
            pl.BlockSpec((1, MEM_HEAD_DIM), lambda b: (0, 0)),
        ],
        out_specs=[
            pl.BlockSpec((1, MEM_HEADS, MEM_HEAD_DIM, M), lambda b: (b, 0, 0, 0)),
            pl.BlockSpec((1, MEM_HEADS, M, MEM_HEAD_DIM), lambda b: (b, 0, 0, 0)),
        ],
        out_shape=[
            jax.ShapeDtypeStruct((B, MEM_HEADS, MEM_HEAD_DIM, M), BF16),
            jax.ShapeDtypeStruct((B, MEM_HEADS, M, MEM_HEAD_DIM), BF16),
        ],
        compiler_params=pltpu.CompilerParams(dimension_semantics=("arbitrary",)),
        name="mem_kv",
    )(mem, g_mem.reshape(1, D), w_mem_kv.astype(BF16), g_km.reshape(1, MEM_HEAD_DIM))


QA_W = N_HEADS_A * HEAD_DIM_A
QI_W = IDX_HEADS * IDX_DIM
KA_W = 2 * N_KV_HEADS_A * HEAD_DIM_A
VA_W = 2 * N_KV_HEADS_A * HEAD_DIM_A
KI_W = 2 * IDX_DIM
WI_W = LANES
DSA_COLS = QA_W + QI_W + KA_W + VA_W + KI_W + WI_W


def _dsa_proj_kernel(x_ref, g_mix_ref, w_ref, vbias_ref, gq_ref, gk_ref, gki_ref, cos_ref, sin_ref, bd_ref,
                     qat_ref, qit_ref, k_ref, vt_ref, ki_ref, wit_ref):
    x = x_ref[0]
    h = x * lax.rsqrt(jnp.mean(x * x, axis=-1, keepdims=True) + EPS) * g_mix_ref[...]
    p = _dot(h.astype(BF16), w_ref[...])
    cos1 = cos_ref[...]
    sin1 = sin_ref[...]
    cos4, sin4 = _tile_lanes(cos1, 4), _tile_lanes(sin1, 4)
    cos2, sin2 = _tile_lanes(cos1, 2), _tile_lanes(sin1, 2)
    bd = bd_ref[...]
    inv_hd = 1.0 / HEAD_DIM_A

    def rope(v, c, s):
        return v * c + _rot_half(v, HEAD_DIM_A) * s

    o = 0
    qa = p[:, o:o + QA_W]; o += QA_W
    qa = qa * lax.rsqrt(_seg_sum(qa * qa, bd) * inv_hd + EPS) * gq_ref[...]
    qat_ref[0] = (rope(qa, cos4, sin4) * (HEAD_DIM_A ** -0.5 * LOG2_E)).T.astype(BF16)

    qi = p[:, o:o + QI_W]; o += QI_W
    qit_ref[0] = (rope(qi, cos4, sin4) * (IDX_DIM ** -0.5)).T.astype(BF16)

    ka = p[:, o:o + KA_W]; o += KA_W
    ka = ka * lax.rsqrt(_seg_sum(ka * ka, bd[:KA_W, :KA_W]) * inv_hd + EPS) * gk_ref[...]
    k_ref[0] = rope(ka, cos2, sin2).astype(BF16)

    vt_ref[0, 0] = (p[:, o:o + VA_W] + vbias_ref[...]).T.astype(BF16); o += VA_W

    ki = p[:, o:o + KI_W]; o += KI_W
    ki = ki * lax.rsqrt(_seg_sum(ki * ki, bd[:KI_W, :KI_W]) * (1.0 / IDX_DIM) + EPS) * gki_ref[...]
    ki_ref[0] = rope(ki, cos1, sin1).astype(BF16)

    wit_ref[0] = (p[:, o:o + WI_W] * (IDX_HEADS ** -0.5)).T[:IDX_HEADS, :]


def _dsa_proj(x, g_mix, w_dsa, vbias, g_qa, g_ka, g_idx_k, cos_t, sin_t, bd, tt):
    B, S, D = x.shape
    nt = S // tt
    const = lambda b, s: (0, 0)
    return pl.pallas_call(
        _dsa_proj_kernel,
        grid=(B, nt),
        in_specs=[
            pl.BlockSpec((1, tt, D), lambda b, s: (b, s, 0)),
            pl.BlockSpec((1, D), const),
            pl.BlockSpec((D, DSA_COLS), const),
            pl.BlockSpec((1, VA_W), const),
            pl.BlockSpec((1, QA_W), const),
            pl.BlockSpec((1, KA_W), const),
            pl.BlockSpec((1, KI_W), const),
            pl.BlockSpec((tt, LANES), lambda b, s: (s, 0)),
            pl.BlockSpec((tt, LANES), lambda b, s: (s, 0)),
            pl.BlockSpec((QA_W, QA_W), const),
        ],
        out_specs=[
            pl.BlockSpec((1, QA_W, tt), lambda b, s: (b, 0, s)),
            pl.BlockSpec((1, QI_W, tt), lambda b, s: (b, 0, s)),
            pl.BlockSpec((1, tt, KA_W), lambda b, s: (b, s, 0)),
            pl.BlockSpec((1, 1, VA_W, tt), lambda b, s: (b, s, 0, 0)),
            pl.BlockSpec((1, tt, KI_W), lambda b, s: (b, s, 0)),
            pl.BlockSpec((1, IDX_HEADS, tt), lambda b, s: (b, 0, s)),
        ],
        out_shape=[
            jax.ShapeDtypeStruct((B, QA_W, S), BF16),
            jax.ShapeDtypeStruct((B, QI_W, S), BF16),
            jax.ShapeDtypeStruct((B, S, KA_W), BF16),
            jax.ShapeDtypeStruct((B, nt, VA_W, tt), BF16),
            jax.ShapeDtypeStruct((B, S, KI_W), BF16),
            jax.ShapeDtypeStruct((B, IDX_HEADS, S), F32),
        ],
        compiler_params=pltpu.CompilerParams(dimension_semantics=("arbitrary", "arbitrary"),
                                             vmem_limit_bytes=VMEM_LIMIT),
        name="dsa_proj",
    )(x, g_mix, w_dsa, vbias, g_qa, g_ka, g_idx_k, cos_t, sin_t, bd)


def _key_of(f):
    bits = lax.bitcast_convert_type(f, jnp.int32)
    return bits ^ ((bits >> 31) & jnp.int32(0x7FFFFFFF))


def _pipelined_chunks(nchunks, last_chunk, matmul, finish):
    matmul(0, 0)

    def pair(i, carry):
        c0 = 2 * i
        matmul(jnp.minimum(c0 + 1, last_chunk), 1)
        finish(c0, 0)

        @pl.when(c0 + 1 < nchunks)
        def _():
            matmul(jnp.minimum(c0 + 2, last_chunk), 0)
            finish(c0 + 1, 1)

        return carry

    lax.fori_loop(0, (nchunks + 1) // 2, pair, 0)


def _dsa_attn_kernel(qat_ref, qit_ref, wit_ref, k_ref, vt_ref, ki_ref, o_ref,
                     qa_s, qi_s, keys_s, buf0, buf1, m_s, acc_s, *, tq, tk, topk, idx_bits):
    qt = pl.program_id(1)
    q0 = qt * tq
    nchunks = q0 // tk + 1
    last_chunk = nchunks - 1
    hpg = N_HEADS_A // N_KV_HEADS_A
    gw = hpg * tq
    half_pairs = hpg // 2
    bufs = (buf0, buf1)
    dim = lax.broadcasted_iota(jnp.int32, (LANES, tq), 0)
    lo_half = dim < HEAD_DIM_A

    for j in range(IDX_HEADS // 2):
        pair = qit_ref[0, j * LANES:(j + 1) * LANES, :]
        qi_s[:, (2 * j) * tq:(2 * j + 1) * tq] = jnp.where(lo_half, pair, jnp.zeros_like(pair))
        qi_s[:, (2 * j + 1) * tq:(2 * j + 2) * tq] = jnp.where(lo_half, jnp.zeros_like(pair), pair)
    for g in range(N_KV_HEADS_A):
        for jj in range(half_pairs):
            pair = qat_ref[0, (g * half_pairs + jj) * LANES:(g * half_pairs + jj + 1) * LANES, :]
            qa_s[:, g * gw + jj * tq:g * gw + (jj + 1) * tq] = jnp.where(lo_half, pair, jnp.zeros_like(pair))
            qa_s[:, g * gw + (half_pairs + jj) * tq:g * gw + (half_pairs + jj + 1) * tq] = (
                jnp.where(lo_half, jnp.zeros_like(pair), pair))

    key_in_chunk = lax.broadcasted_iota(jnp.int32, (tk, tq), 0)
    q_pos = q0 + lax.broadcasted_iota(jnp.int32, (tk, tq), 1)

    def score_chunk(c):
        r0 = pl.multiple_of(c * tk, tk)
        logits = _dot(ki_ref[0, pl.ds(r0, tk), :], qi_s[...])
        score = jnp.zeros((tk, tq), F32)
        for h in range(IDX_HEADS):
            score = score + wit_ref[0, h:h + 1, :] * jnp.maximum(logits[:, h * tq:(h + 1) * tq], 0.0)
        keys_s[c] = jnp.where(key_in_chunk + c * tk <= q_pos, _key_of(score), jnp.int32(INT_MIN))

    def score_pair(i, carry):
        score_chunk(2 * i)
        score_chunk(2 * i + 1)
        return carry

    lax.fori_loop(0, nchunks // 2, score_pair, 0)

    @pl.when(nchunks % 2 == 1)
    def _():
        score_chunk(last_chunk)

    def count_keys(pred_fn):
        def body(c, acc):
            hit = pred_fn(keys_s[c], c * tk).astype(jnp.int32)
            return acc + jnp.sum(hit.reshape(tk // 8, 8, tq), axis=0)
        part = lax.fori_loop(0, nchunks, body, jnp.zeros((8, tq), jnp.int32))
        return jnp.sum(part, axis=0, keepdims=True)

    total = q0 + lax.broadcasted_iota(jnp.int32, (1, tq), 1) + 1
    has_tau = total >= topk

    def unsettled(cnt):
        return jnp.max(jnp.where(has_tau & (cnt != topk), 1, 0))

    def radix_step(i, state):
        tau, cnt = state
        cand = tau + (jnp.int32(1) << (31 - i))
        c = count_keys(lambda k, _: k >= cand)
        up = c >= topk
        return jnp.where(up, cand, tau), jnp.where(up, c, cnt)

    state = lax.fori_loop(0, RADIX_CHECK_BIT, radix_step, (jnp.full((1, tq), INT_MIN, jnp.int32), total))
    tau, cnt = lax.cond(unsettled(state[1]) > 0,
                        lambda s: lax.fori_loop(RADIX_CHECK_BIT, 32, radix_step, s), lambda s: s, state)
    tau = jnp.where(has_tau, jnp.maximum(tau, jnp.int32(INT_MIN + 1)), jnp.int32(INT_MIN))

    tied = has_tau & (cnt > topk)

    def tie_cut():
        need = topk - count_keys(lambda k, _: k > tau)

        def idx_step(i, j):
            bit = jnp.int32(1) << (idx_bits - 1 - i)
            test = j + bit - 1
            f = count_keys(lambda k, base: (k == tau) & (base + key_in_chunk <= test))
            return jnp.where(f < need, j + bit, j)
        return lax.fori_loop(0, idx_bits, idx_step, jnp.zeros((1, tq), jnp.int32))

    jcut = lax.cond(jnp.max(jnp.where(tied, 1, 0)) > 0, tie_cut, lambda: jnp.zeros((1, tq), jnp.int32))
    jcut = jnp.where(tied, jcut, jnp.where(has_tau, jnp.int32(2 ** 30), jnp.int32(-1)))

    m_s[...] = jnp.full(m_s.shape, NEG_BIG, F32)
    acc_s[...] = jnp.zeros(acc_s.shape, F32)

    def attn_matmul(c, slot):
        r0 = pl.multiple_of(c * tk, tk)
        for g in range(N_KV_HEADS_A):
            bufs[slot][:, g * gw:(g + 1) * gw] = _dot(k_ref[0, pl.ds(r0, tk), g * LANES:(g + 1) * LANES],
                                                      qa_s[:, g * gw:(g + 1) * gw])

    def attn_finish(c, slot):
        k = keys_s[c]
        kpos = key_in_chunk + c * tk
        bias = jnp.where(k > tau, 0.0, jnp.where(k == tau, jnp.where(kpos <= jcut, 0.0, NEG_BIG), NEG_BIG))
        bias4 = jnp.concatenate([bias] * hpg, axis=1)
        for g in range(N_KV_HEADS_A):
            s = bufs[slot][:, g * gw:(g + 1) * gw] + bias4
            m_old = m_s[g]
            m_new = jnp.maximum(m_old, jnp.max(s, axis=0, keepdims=True))
            p = jnp.exp2(s - m_new)
            alpha = jnp.exp2(m_old - m_new)
            pv = _dot(vt_ref[0, c, g * LANES:(g + 1) * LANES, :], p.astype(BF16))
            acc_s[g] = alpha * acc_s[g] + pv
            m_s[g] = m_new

    _pipelined_chunks(nchunks, last_chunk, attn_matmul, attn_finish)

    for g in range(N_KV_HEADS_A):
        for jj in range(half_pairs):
            a_e = acc_s[g, :, jj * tq:(jj + 1) * tq]
            a_o = acc_s[g, :, (half_pairs + jj) * tq:(half_pairs + jj + 1) * tq]
            o_pair = jnp.concatenate([a_e[:HEAD_DIM_A] / a_e[HEAD_DIM_A:], a_o[:HEAD_DIM_A] / a_o[HEAD_DIM_A:]], axis=0)
            j = g * half_pairs + jj
            o_ref[0, :, j * LANES:(j + 1) * LANES] = o_pair.T.astype(BF16)


def _dsa_attn(qat, qit, wit, k, vt, ki, tq, topk):
    B, _, S = qat.shape
    nc, tk = vt.shape[1], vt.shape[3]
    hpg = N_HEADS_A // N_KV_HEADS_A
    idx_bits = max(1, int(np.ceil(np.log2(S))))
    assert IDX_HEADS == N_HEADS_A
    kernel = functools.partial(_dsa_attn_kernel, tq=tq, tk=tk, topk=topk, idx_bits=idx_bits)
    return pl.pallas_call(
        kernel,
        grid=(B, S // tq),
        in_specs=[
            pl.BlockSpec((1, QA_W, tq), lambda b, q: (b, 0, q)),
            pl.BlockSpec((1, QI_W, tq), lambda b, q: (b, 0, q)),
            pl.BlockSpec((1, IDX_HEADS, tq), lambda b, q: (b, 0, q)),
            pl.BlockSpec((1, S, KA_W), lambda b, q: (b, 0, 0)),
            pl.BlockSpec((1, nc, VA_W, tk), lambda b, q: (b, 0, 0, 0)),
            pl.BlockSpec((1, S, KI_W), lambda b, q: (b, 0, 0)),
        ],
        out_specs=pl.BlockSpec((1, tq, QA_W), lambda b, q: (b, q, 0)),
        out_shape=jax.ShapeDtypeStruct((B, S, QA_W), BF16),
        scratch_shapes=[
            pltpu.VMEM((LANES, N_HEADS_A * tq), BF16),
            pltpu.VMEM((LANES, IDX_HEADS * tq), BF16),
            pltpu.VMEM((nc, tk, tq), jnp.int32),
            pltpu.VMEM((tk, IDX_HEADS * tq), F32),
            pltpu.VMEM((tk, IDX_HEADS * tq), F32),
            pltpu.VMEM((N_KV_HEADS_A, 1, hpg * tq), F32),
            pltpu.VMEM((N_KV_HEADS_A, LANES, hpg * tq), F32),
        ],
        compiler_params=pltpu.CompilerParams(dimension_semantics=("arbitrary", "arbitrary"),
                                             vmem_limit_bytes=VMEM_LIMIT),
        name="dsa_attn",
    )(qat, qit, wit, k, vt, ki)


CONV_IN_W = 2 * CONV_CH
QM_W = MEM_HEADS * MEM_HEAD_DIM


def _mixer_kernel(x_ref, oa_ref, g_mix_ref, wc_ref, wq_ref, wg_ref, bg_ref, cw_ref, cb_ref, lng_ref, lnb_ref,
                  gqm_ref, kmt_ref, vm_ref, wa_ref, wb_ref, wm_ref, wo_ref, gffn_ref, wrh_ref, wrl_ref, br_ref,
                  x1_ref, h2_ref, comb_ref, ubuf, wbuf, *, tt):
    D = x_ref.shape[-1]
    x = x_ref[0]
    h = (x * lax.rsqrt(jnp.mean(x * x, axis=-1, keepdims=True) + EPS) * g_mix_ref[...]).astype(BF16)

    cin = _dot(h, wc_ref[...])
    u = cin[:, :CONV_CH] * _sigmoid(cin[:, CONV_CH:])

    @pl.when(pl.program_id(1) == 0)
    def _():
        ubuf[0:CONV_HALO, :] = jnp.zeros((CONV_HALO, CONV_CH), F32)

    ubuf[CONV_HALO:CONV_HALO + tt, :] = u
    y = jnp.zeros((tt, CONV_CH), F32) + cb_ref[...]
    first = CONV_HALO - (CONV_WIDTH - 1)
    for b in range(SUBLANES):
        taps = [j for j in range(CONV_WIDTH) if (first + j) % SUBLANES == b]
        if not taps:
            continue
        base = first + taps[0]
        if b == 0:
            window, w0 = ubuf, base
        else:
            span = taps[-1] - taps[0] + tt
            wbuf[0:span, :] = ubuf[base:base + span, :]
            window, w0 = wbuf, 0
        for j in taps:
            off = w0 + j - taps[0]
            y = y + cw_ref[j:j + 1, :] * window[off:off + tt, :]
    halo = ubuf[tt:tt + CONV_HALO, :]
    ubuf[0:CONV_HALO, :] = halo
    mu = jnp.mean(y, axis=-1, keepdims=True)
    yc = y - mu
    var = jnp.mean(yc * yc, axis=-1, keepdims=True)
    yn = yc * lax.rsqrt(var + EPS) * lng_ref[...] + lnb_ref[...]
    o_b = (yn * _sigmoid(yn)).astype(BF16)

    qm = _dot(h, wq_ref[...])
    o_m = []
    for hh in range(MEM_HEADS):
        q = qm[:, hh * MEM_HEAD_DIM:(hh + 1) * MEM_HEAD_DIM]
        q = q * lax.rsqrt(jnp.mean(q * q, axis=-1, keepdims=True) + EPS) * gqm_ref[...] * (MEM_HEAD_DIM ** -0.5)
        s = _dot(q.astype(BF16), kmt_ref[0, hh])
        p = jnp.exp(s - jnp.max(s, axis=-1, keepdims=True))
        l = jnp.sum(p, axis=-1, keepdims=True)
        o_m.append((_dot(p.astype(BF16), vm_ref[0, hh]) / l).astype(BF16))
    o_m = jnp.concatenate(o_m, axis=-1)

    gates = _sigmoid(_dot(h, wg_ref[...]) + bg_ref[...])
    merged = (gates[:, :D] * _dot(oa_ref[0], wa_ref[...])
              + gates[:, D:2 * D] * _dot(o_b, wb_ref[...])
              + gates[:, 2 * D:] * _dot(o_m, wm_ref[...]))
    x1 = x + _dot(merged.astype(BF16), wo_ref[...])
    x1_ref[0] = x1

    h2 = x1 * lax.rsqrt(jnp.mean(x1 * x1, axis=-1, keepdims=True) + EPS) * gffn_ref[...]
    h2_ref[0] = h2.astype(BF16)
    hi, lo = _split_bf16(h2)
    logits = _dot(hi, wrh_ref[...]) + _dot(hi, wrl_ref[...]) + _dot(lo, wrh_ref[...]) + br_ref[...]
    lane = lax.broadcasted_iota(jnp.int32, logits.shape, 1)
    ninf = jnp.float32(-jnp.inf)
    big = jnp.int32(1 << 20)
    is_g = (lane >= N_EXPERTS) & (lane < N_EXPERTS + N_GROUPS)
    gl = jnp.where(is_g, logits, ninf)
    gmax = jnp.max(gl, axis=-1, keepdims=True)
    gsel = jnp.min(jnp.where(gl == gmax, lane, big), axis=-1, keepdims=True) - N_EXPERTS
    p_sel = 1.0 / jnp.sum(jnp.where(is_g, jnp.exp(gl - gmax), 0.0), axis=-1, keepdims=True)
    in_grp = (lane < N_EXPERTS) & ((lane >> 3) == gsel)
    el = jnp.where(in_grp, logits, ninf)
    v1 = jnp.max(el, axis=-1, keepdims=True)
    i1 = jnp.min(jnp.where(el == v1, lane, big), axis=-1, keepdims=True)
    el2 = jnp.where(lane == i1, ninf, el)
    v2 = jnp.max(el2, axis=-1, keepdims=True)
    i2 = jnp.min(jnp.where(el2 == v2, lane, big), axis=-1, keepdims=True)
    e2 = jnp.exp(v2 - v1)
    den = 1.0 + e2
    comb_ref[0] = jnp.where(lane == i1, (1.0 / den) * p_sel, jnp.where(lane == i2, (e2 / den) * p_sel, 0.0))


def _mixer(x, oa, g_mix, wc, wq, wg, bg, cw, cb, lng, lnb, gqm, kmt, vm, wa, wb, wm, wo, gffn, wrh, wrl, br, tt):
    B, S, D = x.shape
    M = kmt.shape[-1]
    const = lambda b, s: (0, 0)
    tile = lambda w: pl.BlockSpec((1, tt, w), lambda b, s: (b, s, 0))
    full = lambda a: pl.BlockSpec(a.shape, const)
    return pl.pallas_call(
        functools.partial(_mixer_kernel, tt=tt),
        grid=(B, S // tt),
        in_specs=[
            tile(D), tile(QA_W), full(g_mix), full(wc), full(wq), full(wg), full(bg), full(cw), full(cb),
            full(lng), full(lnb), full(gqm),
            pl.BlockSpec((1, MEM_HEADS, MEM_HEAD_DIM, M), lambda b, s: (b, 0, 0, 0)),
            pl.BlockSpec((1, MEM_HEADS, M, MEM_HEAD_DIM), lambda b, s: (b, 0, 0, 0)),
            full(wa), full(wb), full(wm), full(wo), full(gffn), full(wrh), full(wrl), full(br),
        ],
        out_specs=[tile(D), tile(D), tile(LANES)],
        out_shape=[
            jax.ShapeDtypeStruct((B, S, D), F32),
            jax.ShapeDtypeStruct((B, S, D), BF16),
            jax.ShapeDtypeStruct((B, S, LANES), F32),
        ],
        scratch_shapes=[pltpu.VMEM((CONV_HALO + tt, CONV_CH), F32),
                        pltpu.VMEM((CONV_HALO + tt, CONV_CH), F32)],
        compiler_params=pltpu.CompilerParams(dimension_semantics=("arbitrary", "arbitrary"),
                                             vmem_limit_bytes=VMEM_LIMIT),
        name="mixer",
    )(x, oa, g_mix, wc, wq, wg, bg, cw, cb, lng, lnb, gqm, kmt, vm, wa, wb, wm, wo, gffn, wrh, wrl, br)


def _moe_kernel(h2_ref, x1_ref, comb_ref, wup_ref, wdn_ref, o_ref):
    n = pl.program_id(1)

    @pl.when(n == 0)
    def _():
        o_ref[...] = x1_ref[...]

    up = _dot(h2_ref[...], wup_ref[0])
    a = up[:, :D_FF_EXPERT]
    act = a * _sigmoid(a) * up[:, D_FF_EXPERT:]
    comb = comb_ref[...]
    lane = lax.broadcasted_iota(jnp.int32, comb.shape, 1)
    c = jnp.sum(jnp.where(lane == n, comb, 0.0), axis=-1, keepdims=True)
    o_ref[...] += _dot((act * c).astype(BF16), wdn_ref[0])


def _moe(h2, x1, comb, w_up, w_down, tm):
    T, D = h2.shape
    return pl.pallas_call(
        _moe_kernel,
        grid=(T // tm, N_EXPERTS),
        in_specs=[
            pl.BlockSpec((tm, D), lambda t, n: (t, 0)),
            pl.BlockSpec((tm, D), lambda t, n: (t, 0)),
            pl.BlockSpec((tm, LANES), lambda t, n: (t, 0)),
            pl.BlockSpec((1, D, 2 * D_FF_EXPERT), lambda t, n: (n, 0, 0)),
            pl.BlockSpec((1, D_FF_EXPERT, D), lambda t, n: (n, 0, 0)),
        ],
        out_specs=pl.BlockSpec((tm, D), lambda t, n: (t, 0)),
        out_shape=jax.ShapeDtypeStruct((T, D), F32),
        compiler_params=pltpu.CompilerParams(dimension_semantics=("arbitrary", "arbitrary"),
                                             vmem_limit_bytes=VMEM_LIMIT),
        name="moe",
    )(h2, x1, comb, w_up, w_down)


def _pick_tile(n, pref):
    t = min(n, pref)
    assert n % t == 0, (n, t)
    return t


def _rope_tables(S):
    half = HEAD_DIM_A // 2
    inv = ROPE_THETA ** (-jnp.arange(half, dtype=F32) / half)
    ang = jnp.arange(S, dtype=jnp.int32).astype(F32)[:, None] * inv[None, :]
    cos, sin = jnp.cos(ang), jnp.sin(ang)
    cos_t = jnp.concatenate([cos, cos, cos, cos], axis=-1)
    sin_t = jnp.concatenate([-sin, sin, -sin, sin], axis=-1)
    return cos_t, sin_t


def _layer(x, mem, g_mix, w_in, b_gate, g_qa, g_ka, g_idx_k, conv_w, conv_b, ln_g, ln_b,
           g_mem, w_mem_kv, g_qm, g_km, w_br_a, w_br_b, w_br_m, w_o, g_ffn,
           w_rg, b_rg, w_re, b_re, w_up, w_down):
    B, S, D = x.shape
    topk = min(TOPK_MAX, S // 4)
    tk = _pick_tile(S, 512)
    tq = _pick_tile(tk, 128)
    tt = _pick_tile(S, 512)
    tm = _pick_tile(B * S, 1024)

    sizes = (QA_W, N_KV_HEADS_A * HEAD_DIM_A, N_KV_HEADS_A * HEAD_DIM_A, QI_W, IDX_DIM, IDX_HEADS,
             CONV_IN_W, QM_W, N_BRANCHES * D)
    offs = np.concatenate([[0], np.cumsum(sizes)])
    col = lambda i: w_in[:, offs[i]:offs[i + 1]]
    w_qa, w_ka, w_va, w_qi, w_ki, w_wi, w_conv, w_qm, w_gate = (col(i) for i in range(9))
    hd = HEAD_DIM_A
    zeros_hd = jnp.zeros((D, hd), F32)
    ka_dup = [w_ka[:, g * hd:(g + 1) * hd] for g in range(N_KV_HEADS_A) for _ in range(2)]
    va_aug = [w for g in range(N_KV_HEADS_A) for w in (w_va[:, g * hd:(g + 1) * hd], zeros_hd)]
    w_dsa = jnp.concatenate(
        [w_qa, w_qi] + ka_dup + va_aug + [w_ki, w_ki, w_wi, jnp.zeros((D, WI_W - IDX_HEADS), F32)], axis=1).astype(BF16)
    vbias = jnp.tile(jnp.concatenate([jnp.zeros((hd,), F32), jnp.ones((hd,), F32)]), N_KV_HEADS_A).reshape(1, VA_W)
    seg = np.arange(QA_W) // hd
    bd = jnp.asarray(seg[:, None] == seg[None, :], BF16)
    cos_t, sin_t = _rope_tables(S)

    kmt, vm = _mem_kv(mem, g_mem, w_mem_kv, g_km)
    qat, qit, k, vt, ki, wit = _dsa_proj(
        x, g_mix.reshape(1, D), w_dsa, vbias,
        jnp.tile(g_qa, N_HEADS_A).reshape(1, QA_W), jnp.tile(g_ka, KA_W // hd).reshape(1, KA_W),
        jnp.tile(g_idx_k, KI_W // IDX_DIM).reshape(1, KI_W), cos_t, sin_t, bd, tk)
    o_a = _dsa_attn(qat, qit, wit, k, vt, ki, tq, topk)

    w_r = jnp.concatenate([w_re, w_rg, jnp.zeros((D, LANES - N_EXPERTS - N_GROUPS), F32)], axis=1)
    b_r = jnp.concatenate([b_re, b_rg, jnp.zeros((LANES - N_EXPERTS - N_GROUPS,), F32)]).reshape(1, LANES)
    w_r_hi = w_r.astype(BF16)
    w_r_lo = (w_r - w_r_hi.astype(F32)).astype(BF16)
    x1, h2, comb = _mixer(
        x, o_a, g_mix.reshape(1, D), w_conv.astype(BF16), w_qm.astype(BF16), w_gate.astype(BF16),
        b_gate.reshape(1, -1), conv_w.reshape(CONV_WIDTH, CONV_CH), conv_b.reshape(1, CONV_CH),
        ln_g.reshape(1, CONV_CH), ln_b.reshape(1, CONV_CH), g_qm.reshape(1, MEM_HEAD_DIM), kmt, vm,
        w_br_a.astype(BF16), w_br_b.astype(BF16), w_br_m.astype(BF16), w_o.astype(BF16),
        g_ffn.reshape(1, D), w_r_hi, w_r_lo, b_r, tt)

    out = _moe(h2.reshape(B * S, D), x1.reshape(B * S, D), comb.reshape(B * S, LANES),
               w_up.astype(BF16), w_down.astype(BF16), tm)
    return out.reshape(B, S, D)


def kernel(x, mem, g_mix, w_in, b_gate, g_qa, g_ka, g_idx_k, conv_w, conv_b, ln_g, ln_b, g_mem, w_mem_kv, g_qm, g_km,
           w_br_a, w_br_b, w_br_m, w_o, g_ffn, w_rg, b_rg, w_re, b_re, w_up, w_down):
    params = (g_mix, w_in, b_gate, g_qa, g_ka, g_idx_k, conv_w, conv_b, ln_g, ln_b, g_mem, w_mem_kv, g_qm, g_km,
              w_br_a, w_br_b, w_br_m, w_o, g_ffn, w_rg, b_rg, w_re, b_re, w_up, w_down)
    for l in range(g_mix.shape[0]):
        x = _layer(x, mem, *(p[l] for p in params))
    return x
```

```python
import functools

import jax
import jax.numpy as jnp
import numpy as np
from jax import lax
from jax.experimental import pallas as pl
from jax.experimental.pallas import tpu as pltpu

N_HEADS_A = 8
N_KV_HEADS_A = 2
HEAD_DIM_A = 64
IDX_HEADS = 8
IDX_DIM = 64
TOPK_MAX = 256
CONV_CH = 512
CONV_WIDTH = 31
MEM_HEADS = 4
MEM_HEAD_DIM = 128
N_BRANCHES = 3
N_GROUPS = 4
EXPERTS_PER_GROUP = 8
N_EXPERTS = N_GROUPS * EXPERTS_PER_GROUP
D_FF_EXPERT = 256
ROPE_THETA = 10000.0
EPS = 1e-6

LANES = 128
SUBLANES = 8
VMEM_LIMIT = 56 * 1024 * 1024

INT_MIN = -2 ** 31
LOG2_E = 1.4426950408889634
RADIX_CHECK_BIT = 26
NEG_BIG = -1e30
CONV_HALO = 32

BF16 = jnp.bfloat16
F32 = jnp.float32


def _dot(a, b):
    return jnp.dot(a, b, preferred_element_type=F32)


def _sigmoid(x):
    return 0.5 * jnp.tanh(0.5 * x) + 0.5


def _split_bf16(x):
    hi = x.astype(BF16)
    lo = (x - hi.astype(F32)).astype(BF16)
    return hi, lo


def _seg_sum(xsq, ones_bd):
    hi, lo = _split_bf16(xsq)
    return _dot(hi, ones_bd) + _dot(lo, ones_bd)


def _rot_half(x, head_dim):
    n = x.shape[-1]
    half = head_dim // 2
    lane = lax.broadcasted_iota(jnp.int32, x.shape, x.ndim - 1)
    is_lo = (lane & (head_dim - 1)) < half
    return jnp.where(is_lo, pltpu.roll(x, n - half, x.ndim - 1), pltpu.roll(x, half, x.ndim - 1))


def _tile_lanes(x, reps):
    return jnp.concatenate([x] * reps, axis=-1) if reps > 1 else x


def _mem_kv_kernel(mem_ref, g_mem_ref, w_ref, g_km_ref, kt_ref, v_ref):
    m = mem_ref[0]
    hm = m * lax.rsqrt(jnp.mean(m * m, axis=-1, keepdims=True) + EPS) * g_mem_ref[...]
    kv = _dot(hm.astype(BF16), w_ref[...])
    hd = MEM_HEADS * MEM_HEAD_DIM
    for h in range(MEM_HEADS):
        k = kv[:, h * MEM_HEAD_DIM:(h + 1) * MEM_HEAD_DIM]
        k = k * lax.rsqrt(jnp.mean(k * k, axis=-1, keepdims=True) + EPS) * g_km_ref[...]
        kt_ref[0, h] = k.T.astype(BF16)
        v_ref[0, h] = kv[:, hd + h * MEM_HEAD_DIM: hd + (h + 1) * MEM_HEAD_DIM].astype(BF16)


def _mem_kv(mem, g_mem, w_mem_kv, g_km):
    B, M, D = mem.shape
    hd2 = 2 * MEM_HEADS * MEM_HEAD_DIM
    return pl.pallas_call(
        _mem_kv_kernel,
        grid=(B,),
        in_specs=[
            pl.BlockSpec((1, M, D), lambda b: (b, 0, 0)),
            pl.BlockSpec((1, D), lambda b: (0, 0)),
            pl.BlockSpec((D, hd2), lambda b: (0, 0)),
            pl.BlockSpec((1, MEM_HEAD_DIM), lambda b: (0, 0)),
        ],
        out_specs=[
            pl.BlockSpec((1, MEM_HEADS, MEM_HEAD_DIM, M), lambda b: (b, 0, 0, 0)),
            pl.BlockSpec((1, MEM_HEADS, M, MEM_HEAD_DIM), lambda b: (b, 0, 0, 0)),
        ],
        out_shape=[
            jax.ShapeDtypeStruct((B, MEM_HEADS, MEM_HEAD_DIM, M), BF16),
            jax.ShapeDtypeStruct((B, MEM_HEADS, M, MEM_HEAD_DIM), BF16),
        ],
        compiler_params=pltpu.CompilerParams(dimension_semantics=("arbitrary",)),
        name="mem_kv",
    )(mem, g_mem.reshape(1, D), w_mem_kv.astype(BF16), g_km.reshape(1, MEM_HEAD_DIM))


QA_W = N_HEADS_A * HEAD_DIM_A
QI_W = IDX_HEADS * IDX_DIM
KA_W = 2 * N_KV_HEADS_A * HEAD_DIM_A
VA_W = 2 * N_KV_HEADS_A * HEAD_DIM_A
KI_W = 2 * IDX_DIM
WI_W = LANES
DSA_COLS = QA_W + QI_W + KA_W + VA_W + KI_W + WI_W


def _dsa_proj_kernel(x_ref, g_mix_ref, w_ref, vbias_ref, gq_ref, gk_ref, gki_ref, cos_ref, sin_ref, bd_ref,
                     qat_ref, qit_ref, k_ref, vt_ref, ki_ref, wit_ref):
    x = x_ref[0]
    h = x * lax.rsqrt(jnp.mean(x * x, axis=-1, keepdims=True) + EPS) * g_mix_ref[...]
    p = _dot(h.astype(BF16), w_ref[...])
    cos1 = cos_ref[...]
    sin1 = sin_ref[...]
    cos4, sin4 = _tile_lanes(cos1, 4), _tile_lanes(sin1, 4)
    cos2, sin2 = _tile_lanes(cos1, 2), _tile_lanes(sin1, 2)
    bd = bd_ref[...]
    inv_hd = 1.0 / HEAD_DIM_A

    def rope(v, c, s):
        return v * c + _rot_half(v, HEAD_DIM_A) * s

    o = 0
    qa = p[:, o:o + QA_W]; o += QA_W
    qa = qa * lax.rsqrt(_seg_sum(qa * qa, bd) * inv_hd + EPS) * gq_ref[...]
    qat_ref[0] = (rope(qa, cos4, sin4) * (HEAD_DIM_A ** -0.5 * LOG2_E)).T.astype(BF16)

    qi = p[:, o:o + QI_W]; o += QI_W
    qit_ref[0] = (rope(qi, cos4, sin4) * (IDX_DIM ** -0.5)).T.astype(BF16)

    ka = p[:, o:o + KA_W]; o += KA_W
    ka = ka * lax.rsqrt(_seg_sum(ka * ka, bd[:KA_W, :KA_W]) * inv_hd + EPS) * gk_ref[...]
    k_ref[0] = rope(ka, cos2, sin2).astype(BF16)

    vt_ref[0, 0] = (p[:, o:o + VA_W] + vbias_ref[...]).T.astype(BF16); o += VA_W

    ki = p[:, o:o + KI_W]; o += KI_W
    ki = ki * lax.rsqrt(_seg_sum(ki * ki, bd[:KI_W, :KI_W]) * (1.0 / IDX_DIM) + EPS) * gki_ref[...]
    ki_ref[0] = rope(ki, cos1, sin1).astype(BF16)

    wit_ref[0] = (p[:, o:o + WI_W] * (IDX_HEADS ** -0.5)).T[:IDX_HEADS, :]


def _dsa_proj(x, g_mix, w_dsa, vbias, g_qa, g_ka, g_idx_k, cos_t, sin_t, bd, tt):
    B, S, D = x.shape
    nt = S // tt
    const = lambda b, s: (0, 0)
    return pl.pallas_call(
        _dsa_proj_kernel,
        grid=(B, nt),
        in_specs=[
            pl.BlockSpec((1, tt, D), lambda b, s: (b, s, 0)),
            pl.BlockSpec((1, D), const),
            pl.BlockSpec((D, DSA_COLS), const),
            pl.BlockSpec((1, VA_W), const),
            pl.BlockSpec((1, QA_W), const),
            pl.BlockSpec((1, KA_W), const),
            pl.BlockSpec((1, KI_W), const),
            pl.BlockSpec((tt, LANES), lambda b, s: (s, 0)),
            pl.BlockSpec((tt, LANES), lambda b, s: (s, 0)),
            pl.BlockSpec((QA_W, QA_W), const),
        ],
        out_specs=[
            pl.BlockSpec((1, QA_W, tt), lambda b, s: (b, 0, s)),
            pl.BlockSpec((1, QI_W, tt), lambda b, s: (b, 0, s)),
            pl.BlockSpec((1, tt, KA_W), lambda b, s: (b, s, 0)),
            pl.BlockSpec((1, 1, VA_W, tt), lambda b, s: (b, s, 0, 0)),
            pl.BlockSpec((1, tt, KI_W), lambda b, s: (b, s, 0)),
            pl.BlockSpec((1, IDX_HEADS, tt), lambda b, s: (b, 0, s)),
        ],
        out_shape=[
            jax.ShapeDtypeStruct((B, QA_W, S), BF16),
            jax.ShapeDtypeStruct((B, QI_W, S), BF16),
            jax.ShapeDtypeStruct((B, S, KA_W), BF16),
            jax.ShapeDtypeStruct((B, nt, VA_W, tt), BF16),
            jax.ShapeDtypeStruct((B, S, KI_W), BF16),
            jax.ShapeDtypeStruct((B, IDX_HEADS, S), F32),
        ],
        compiler_params=pltpu.CompilerParams(dimension_semantics=("arbitrary", "arbitrary"),
                                             vmem_limit_bytes=VMEM_LIMIT),
        name="dsa_proj",
    )(x, g_mix, w_dsa, vbias, g_qa, g_ka, g_idx_k, cos_t, sin_t, bd)


def _key_of(f):
    bits = lax.bitcast_convert_type(f, jnp.int32)
    return bits ^ ((bits >> 31) & jnp.int32(0x7FFFFFFF))


def _pipelined_chunks(nchunks, last_chunk, matmul, finish):
    matmul(0, 0)

    def pair(i, carry):
        c0 = 2 * i
        matmul(jnp.minimum(c0 + 1, last_chunk), 1)
        finish(c0, 0)

        @pl.when(c0 + 1 < nchunks)
        def _():
            matmul(jnp.minimum(c0 + 2, last_chunk), 0)
            finish(c0 + 1, 1)

        return carry

    lax.fori_loop(0, (nchunks + 1) // 2, pair, 0)


def _dsa_attn_kernel(qat_ref, qit_ref, wit_ref, k_ref, vt_ref, ki_ref, o_ref,
                     qa_s, qi_s, keys_s, buf0, buf1, m_s, acc_s, *, tq, tk, topk, idx_bits):
    qt = pl.program_id(1)
    q0 = qt * tq
    nchunks = q0 // tk + 1
    last_chunk = nchunks - 1
    hpg = N_HEADS_A // N_KV_HEADS_A
    gw = hpg * tq
    half_pairs = hpg // 2
    bufs = (buf0, buf1)
    dim = lax.broadcasted_iota(jnp.int32, (LANES, tq), 0)
    lo_half = dim < HEAD_DIM_A

    for j in range(IDX_HEADS // 2):
        pair = qit_ref[0, j * LANES:(j + 1) * LANES, :]
        qi_s[:, (2 * j) * tq:(2 * j + 1) * tq] = jnp.where(lo_half, pair, jnp.zeros_like(pair))
        qi_s[:, (2 * j + 1) * tq:(2 * j + 2) * tq] = jnp.where(lo_half, jnp.zeros_like(pair), pair)
    for g in range(N_KV_HEADS_A):
        for jj in range(half_pairs):
            pair = qat_ref[0, (g * half_pairs + jj) * LANES:(g * half_pairs + jj + 1) * LANES, :]
            qa_s[:, g * gw + jj * tq:g * gw + (jj + 1) * tq] = jnp.where(lo_half, pair, jnp.zeros_like(pair))
            qa_s[:, g * gw + (half_pairs + jj) * tq:g * gw + (half_pairs + jj + 1) * tq] = (
                jnp.where(lo_half, jnp.zeros_like(pair), pair))

    key_in_chunk = lax.broadcasted_iota(jnp.int32, (tk, tq), 0)
    q_pos = q0 + lax.broadcasted_iota(jnp.int32, (tk, tq), 1)

    def score_chunk(c):
        r0 = pl.multiple_of(c * tk, tk)
        logits = _dot(ki_ref[0, pl.ds(r0, tk), :], qi_s[...])
        score = jnp.zeros((tk, tq), F32)
        for h in range(IDX_HEADS):
            score = score + wit_ref[0, h:h + 1, :] * jnp.maximum(logits[:, h * tq:(h + 1) * tq], 0.0)
        keys_s[c] = jnp.where(key_in_chunk + c * tk <= q_pos, _key_of(score), jnp.int32(INT_MIN))

    def score_pair(i, carry):
        score_chunk(2 * i)
        score_chunk(2 * i + 1)
        return carry

    lax.fori_loop(0, nchunks // 2, score_pair, 0)

    @pl.when(nchunks % 2 == 1)
    def _():
        score_chunk(last_chunk)

    def count_keys(pred_fn):
        def body(c, acc):
            hit = pred_fn(keys_s[c], c * tk).astype(jnp.int32)
            return acc + jnp.sum(hit.reshape(tk // 8, 8, tq), axis=0)
        part = lax.fori_loop(0, nchunks, body, jnp.zeros((8, tq), jnp.int32))
        return jnp.sum(part, axis=0, keepdims=True)

    total = q0 + lax.broadcasted_iota(jnp.int32, (1, tq), 1) + 1
    has_tau = total >= topk

    def unsettled(cnt):
        return jnp.max(jnp.where(has_tau & (cnt != topk), 1, 0))

    def radix_step(i, state):
        tau, cnt = state
        cand = tau + (jnp.int32(1) << (31 - i))
        c = count_keys(lambda k, _: k >= cand)
        up = c >= topk
        return jnp.where(up, cand, tau), jnp.where(up, c, cnt)

    state = lax.fori_loop(0, RADIX_CHECK_BIT, radix_step, (jnp.full((1, tq), INT_MIN, jnp.int32), total))
    tau, cnt = lax.cond(unsettled(state[1]) > 0,
                        lambda s: lax.fori_loop(RADIX_CHECK_BIT, 32, radix_step, s), lambda s: s, state)
    tau = jnp.where(has_tau, jnp.maximum(tau, jnp.int32(INT_MIN + 1)), jnp.int32(INT_MIN))

    tied = has_tau & (cnt > topk)

    def tie_cut():
        need = topk - count_keys(lambda k, _: k > tau)

        def idx_step(i, j):
            bit = jnp.int32(1) << (idx_bits - 1 - i)
            test = j + bit - 1
            f = count_keys(lambda k, base: (k == tau) & (base + key_in_chunk <= test))
            return jnp.where(f < need, j + bit, j)
        return lax.fori_loop(0, idx_bits, idx_step, jnp.zeros((1, tq), jnp.int32))

    jcut = lax.cond(jnp.max(jnp.where(tied, 1, 0)) > 0, tie_cut, lambda: jnp.zeros((1, tq), jnp.int32))
    jcut = jnp.where(tied, jcut, jnp.where(has_tau, jnp.int32(2 ** 30), jnp.int32(-1)))

    m_s[...] = jnp.full(m_s.shape, NEG_BIG, F32)
    acc_s[...] = jnp.zeros(acc_s.shape, F32)

    def attn_matmul(c, slot):
        r0 = pl.multiple_of(c * tk, tk)
        for g in range(N_KV_HEADS_A):
            bufs[slot][:, g * gw:(g + 1) * gw] = _dot(k_ref[0, pl.ds(r0, tk), g * LANES:(g + 1) * LANES],
                                                      qa_s[:, g * gw:(g + 1) * gw])

    def attn_finish(c, slot):
        k = keys_s[c]
        kpos = key_in_chunk + c * tk
        bias = jnp.where(k > tau, 0.0, jnp.where(k == tau, jnp.where(kpos <= jcut, 0.0, NEG_BIG), NEG_BIG))
        bias4 = jnp.concatenate([bias] * hpg, axis=1)
        for g in range(N_KV_HEADS_A):
            s = bufs[slot][:, g * gw:(g + 1) * gw] + bias4
            m_old = m_s[g]
            m_new = jnp.maximum(m_old, jnp.max(s, axis=0, keepdims=True))
            p = jnp.exp2(s - m_new)
            alpha = jnp.exp2(m_old - m_new)
            pv = _dot(vt_ref[0, c, g * LANES:(g + 1) * LANES, :], p.astype(BF16))
            acc_s[g] = alpha * acc_s[g] + pv
            m_s[g] = m_new

    _pipelined_chunks(nchunks, last_chunk, attn_matmul, attn_finish)

    for g in range(N_KV_HEADS_A):
        for jj in range(half_pairs):
            a_e = acc_s[g, :, jj * tq:(jj + 1) * tq]
            a_o = acc_s[g, :, (half_pairs + jj) * tq:(half_pairs + jj + 1) * tq]
            o_pair = jnp.concatenate([a_e[:HEAD_DIM_A] / a_e[HEAD_DIM_A:], a_o[:HEAD_DIM_A] / a_o[HEAD_DIM_A:]], axis=0)
            j = g * half_pairs + jj
            o_ref[0, :, j * LANES:(j + 1) * LANES] = o_pair.T.astype(BF16)


def _dsa_attn(qat, qit, wit, k, vt, ki, tq, topk):
    B, _, S = qat.shape
    nc, tk = vt.shape[1], vt.shape[3]
    hpg = N_HEADS_A // N_KV_HEADS_A
    idx_bits = max(1, int(np.ceil(np.log2(S))))
    assert IDX_HEADS == N_HEADS_A
    kernel = functools.partial(_dsa_attn_kernel, tq=tq, tk=tk, topk=topk, idx_bits=idx_bits)
    return pl.pallas_call(
        kernel,
        grid=(B, S // tq),
        in_specs=[
            pl.BlockSpec((1, QA_W, tq), lambda b, q: (b, 0, q)),
            pl.BlockSpec((1, QI_W, tq), lambda b, q: (b, 0, q)),
            pl.BlockSpec((1, IDX_HEADS, tq), lambda b, q: (b, 0, q)),
            pl.BlockSpec((1, S, KA_W), lambda b, q: (b, 0, 0)),
            pl.BlockSpec((1, nc, VA_W, tk), lambda b, q: (b, 0, 0, 0)),
            pl.BlockSpec((1, S, KI_W), lambda b, q: (b, 0, 0)),
        ],
        out_specs=pl.BlockSpec((1, tq, QA_W), lambda b, q: (b, q, 0)),
        out_shape=jax.ShapeDtypeStruct((B, S, QA_W), BF16),
        scratch_shapes=[
            pltpu.VMEM((LANES, N_HEADS_A * tq), BF16),
            pltpu.VMEM((LANES, IDX_HEADS * tq), BF16),
            pltpu.VMEM((nc, tk, tq), jnp.int32),
            pltpu.VMEM((tk, IDX_HEADS * tq), F32),
            pltpu.VMEM((tk, IDX_HEADS * tq), F32),
            pltpu.VMEM((N_KV_HEADS_A, 1, hpg * tq), F32),
            pltpu.VMEM((N_KV_HEADS_A, LANES, hpg * tq), F32),
        ],
        compiler_params=pltpu.CompilerParams(dimension_semantics=("arbitrary", "arbitrary"),
                                             vmem_limit_bytes=VMEM_LIMIT),
        name="dsa_attn",
    )(qat, qit, wit, k, vt, ki)


CONV_IN_W = 2 * CONV_CH
QM_W = MEM_HEADS * MEM_HEAD_DIM


def _mixer_kernel(x_ref, oa_ref, g_mix_ref, wc_ref, wq_ref, wg_ref, bg_ref, cw_ref, cb_ref, lng_ref, lnb_ref,
                  gqm_ref, kmt_ref, vm_ref, wa_ref, wb_ref, wm_ref, wo_ref, gffn_ref, wrh_ref, wrl_ref, br_ref,
                  x1_ref, h2_ref, comb_ref, ubuf, wbuf, *, tt):
    D = x_ref.shape[-1]
    x = x_ref[0]
    h = (x * lax.rsqrt(jnp.mean(x * x, axis=-1, keepdims=True) + EPS) * g_mix_ref[...]).astype(BF16)

    cin = _dot(h, wc_ref[...])
    u = cin[:, :CONV_CH] * _sigmoid(cin[:, CONV_CH:])

    @pl.when(pl.program_id(1) == 0)
    def _():
        ubuf[0:CONV_HALO, :] = jnp.zeros((CONV_HALO, CONV_CH), F32)

    ubuf[CONV_HALO:CONV_HALO + tt, :] = u
    y = jnp.zeros((tt, CONV_CH), F32) + cb_ref[...]
    first = CONV_HALO - (CONV_WIDTH - 1)
    for b in range(SUBLANES):
        taps = [j for j in range(CONV_WIDTH) if (first + j) % SUBLANES == b]
        if not taps:
            continue
        base = first + taps[0]
        if b == 0:
            window, w0 = ubuf, base
        else:
            span = taps[-1] - taps[0] + tt
            wbuf[0:span, :] = ubuf[base:base + span, :]
            window, w0 = wbuf, 0
        for j in taps:
            off = w0 + j - taps[0]
            y = y + cw_ref[j:j + 1, :] * window[off:off + tt, :]
    halo = ubuf[tt:tt + CONV_HALO, :]
    ubuf[0:CONV_HALO, :] = halo
    mu = jnp.mean(y, axis=-1, keepdims=True)
    yc = y - mu
    var = jnp.mean(yc * yc, axis=-1, keepdims=True)
    yn = yc * lax.rsqrt(var + EPS) * lng_ref[...] + lnb_ref[...]
    o_b = (yn * _sigmoid(yn)).astype(BF16)

    qm = _dot(h, wq_ref[...])
    o_m = []
    for hh in range(MEM_HEADS):
        q = qm[:, hh * MEM_HEAD_DIM:(hh + 1) * MEM_HEAD_DIM]
        q = q * lax.rsqrt(jnp.mean(q * q, axis=-1, keepdims=True) + EPS) * gqm_ref[...] * (MEM_HEAD_DIM ** -0.5)
        s = _dot(q.astype(BF16), kmt_ref[0, hh])
        p = jnp.exp(s - jnp.max(s, axis=-1, keepdims=True))
        l = jnp.sum(p, axis=-1, keepdims=True)
        o_m.append((_dot(p.astype(BF16), vm_ref[0, hh]) / l).astype(BF16))
    o_m = jnp.concatenate(o_m, axis=-1)

    gates = _sigmoid(_dot(h, wg_ref[...]) + bg_ref[...])
    merged = (gates[:, :D] * _dot(oa_ref[0], wa_ref[...])
              + gates[:, D:2 * D] * _dot(o_b, wb_ref[...])
              + gates[:, 2 * D:] * _dot(o_m, wm_ref[...]))
    x1 = x + _dot(merged.astype(BF16), wo_ref[...])
    x1_ref[0] = x1

    h2 = x1 * lax.rsqrt(jnp.mean(x1 * x1, axis=-1, keepdims=True) + EPS) * gffn_ref[...]
    h2_ref[0] = h2.astype(BF16)
    hi, lo = _split_bf16(h2)
    logits = _dot(hi, wrh_ref[...]) + _dot(hi, wrl_ref[...]) + _dot(lo, wrh_ref[...]) + br_ref[...]
    lane = lax.broadcasted_iota(jnp.int32, logits.shape, 1)
    ninf = jnp.float32(-jnp.inf)
    big = jnp.int32(1 << 20)
    is_g = (lane >= N_EXPERTS) & (lane < N_EXPERTS + N_GROUPS)
    gl = jnp.where(is_g, logits, ninf)
    gmax = jnp.max(gl, axis=-1, keepdims=True)
    gsel = jnp.min(jnp.where(gl == gmax, lane, big), axis=-1, keepdims=True) - N_EXPERTS
    p_sel = 1.0 / jnp.sum(jnp.where(is_g, jnp.exp(gl - gmax), 0.0), axis=-1, keepdims=True)
    in_grp = (lane < N_EXPERTS) & ((lane >> 3) == gsel)
    el = jnp.where(in_grp, logits, ninf)
    v1 = jnp.max(el, axis=-1, keepdims=True)
    i1 = jnp.min(jnp.where(el == v1, lane, big), axis=-1, keepdims=True)
    el2 = jnp.where(lane == i1, ninf, el)
    v2 = jnp.max(el2, axis=-1, keepdims=True)
    i2 = jnp.min(jnp.where(el2 == v2, lane, big), axis=-1, keepdims=True)
    e2 = jnp.exp(v2 - v1)
    den = 1.0 + e2
    comb_ref[0] = jnp.where(lane == i1, (1.0 / den) * p_sel,
                            jnp.where(lane == i2, (e2 / den) * p_sel,
                                      jnp.where(lane == N_EXPERTS, gsel.astype(F32), 0.0)))


def _mixer(x, oa, g_mix, wc, wq, wg, bg, cw, cb, lng, lnb, gqm, kmt, vm, wa, wb, wm, wo, gffn, wrh, wrl, br, tt):
    B, S, D = x.shape
    M = kmt.shape[-1]
    const = lambda b, s: (0, 0)
    tile = lambda w: pl.BlockSpec((1, tt, w), lambda b, s: (b, s, 0))
    full = lambda a: pl.BlockSpec(a.shape, const)
    return pl.pallas_call(
        functools.partial(_mixer_kernel, tt=tt),
        grid=(B, S // tt),
        in_specs=[
            tile(D), tile(QA_W), full(g_mix), full(wc), full(wq), full(wg), full(bg), full(cw), full(cb),
            full(lng), full(lnb), full(gqm),
            pl.BlockSpec((1, MEM_HEADS, MEM_HEAD_DIM, M), lambda b, s: (b, 0, 0, 0)),
            pl.BlockSpec((1, MEM_HEADS, M, MEM_HEAD_DIM), lambda b, s: (b, 0, 0, 0)),
            full(wa), full(wb), full(wm), full(wo), full(gffn), full(wrh), full(wrl), full(br),
        ],
        out_specs=[tile(D), tile(D), tile(LANES)],
        out_shape=[
            jax.ShapeDtypeStruct((B, S, D), F32),
            jax.ShapeDtypeStruct((B, S, D), BF16),
            jax.ShapeDtypeStruct((B, S, LANES), F32),
        ],
        scratch_shapes=[pltpu.VMEM((CONV_HALO + tt, CONV_CH), F32),
                        pltpu.VMEM((CONV_HALO + tt, CONV_CH), F32)],
        compiler_params=pltpu.CompilerParams(dimension_semantics=("arbitrary", "arbitrary"),
                                             vmem_limit_bytes=VMEM_LIMIT),
        name="mixer",
    )(x, oa, g_mix, wc, wq, wg, bg, cw, cb, lng, lnb, gqm, kmt, vm, wa, wb, wm, wo, gffn, wrh, wrl, br)


MOE_BLK = 256


def _moe_kernel(h2_ref, x1_ref, comb_ref, ltri_ref, wup_ref, wdn_ref, o_ref,
                hs_s, combs_s, ys_s, pt_s, seg_s, *, tm, rows):
    n = pl.program_id(1)
    lane = lax.broadcasted_iota(jnp.int32, (tm, LANES), 1)

    @pl.when(n == 0)
    def _():
        comb = comb_ref[...]
        gid = comb[:, N_EXPERTS:N_EXPERTS + 1].astype(jnp.int32)
        member = lane == gid
        onehot = jnp.where(member, 1.0, 0.0)
        before = _dot(ltri_ref[...], onehot.astype(BF16))
        rank = jnp.sum(jnp.where(member, before, 0.0), axis=-1, keepdims=True)
        count = jnp.sum(onehot, axis=0, keepdims=True)
        lane1 = lax.broadcasted_iota(jnp.int32, (1, LANES), 1)
        start_blk = jnp.int32(0)
        start_row = jnp.zeros((1, LANES), F32)
        for g in range(N_GROUPS):
            cnt_g = jnp.sum(jnp.where(lane1 == g, count, 0.0)).astype(jnp.int32)
            nblk_g = (cnt_g + (MOE_BLK - 1)) // MOE_BLK
            seg_s[g] = start_blk
            seg_s[N_GROUPS + g] = nblk_g
            start_row = jnp.where(lane1 == g, (start_blk * MOE_BLK).astype(F32), start_row)
            start_blk = start_blk + nblk_g
        dest = jnp.sum(jnp.where(member, start_row, 0.0), axis=-1, keepdims=True) + rank
        seg_s[2 * N_GROUPS] = start_blk
        dest_i = dest.astype(jnp.int32)
        dest_row = jnp.broadcast_to(dest, (tm, LANES)).T[0:1, :].astype(jnp.int32)
        c_hi, c_lo = _split_bf16(comb)
        col = lax.broadcasted_iota(jnp.int32, (tm, MOE_BLK), 1)
        row = lax.broadcasted_iota(jnp.int32, (MOE_BLK, tm), 0)

        def permute_block(b, carry):
            r0 = pl.multiple_of(b * MOE_BLK, MOE_BLK)
            pt_s[b] = jnp.where(col + r0 == dest_i, 1.0, 0.0).astype(BF16)
            perm = jnp.where(row + r0 == dest_row, 1.0, 0.0).astype(BF16)
            hs_s[pl.ds(r0, MOE_BLK), :] = _dot(perm, h2_ref[...]).astype(BF16)
            combs_s[pl.ds(r0, MOE_BLK), :] = _dot(perm, c_hi) + _dot(perm, c_lo)
            ys_s[pl.ds(r0, MOE_BLK), :] = jnp.zeros((MOE_BLK, ys_s.shape[1]), F32)
            return carry

        lax.fori_loop(0, start_blk, permute_block, 0)

    g = n // EXPERTS_PER_GROUP
    first_blk = seg_s[g]
    lane_b = lax.broadcasted_iota(jnp.int32, (MOE_BLK, LANES), 1)

    def expert_block(b, carry):
        r0 = pl.multiple_of((first_blk + b) * MOE_BLK, MOE_BLK)
        up = _dot(hs_s[pl.ds(r0, MOE_BLK), :], wup_ref[0])
        a = up[:, :D_FF_EXPERT]
        act = a * _sigmoid(a) * up[:, D_FF_EXPERT:]
        c = jnp.sum(jnp.where(lane_b == n, combs_s[pl.ds(r0, MOE_BLK), :], 0.0), axis=-1, keepdims=True)
        ys_s[pl.ds(r0, MOE_BLK), :] += _dot((act * c).astype(BF16), wdn_ref[0])
        return carry

    lax.fori_loop(0, seg_s[N_GROUPS + g], expert_block, 0)

    @pl.when(n == N_EXPERTS - 1)
    def _():
        o_ref[...] = x1_ref[...]

        def unpermute_block(b, carry):
            r0 = pl.multiple_of(b * MOE_BLK, MOE_BLK)
            o_ref[...] += _dot(pt_s[b], ys_s[pl.ds(r0, MOE_BLK), :].astype(BF16))
            return carry

        lax.fori_loop(0, seg_s[2 * N_GROUPS], unpermute_block, 0)


def _moe(h2, x1, comb, w_up, w_down, tm):
    T, D = h2.shape
    rows = tm + N_GROUPS * MOE_BLK
    ltri = jnp.asarray(np.tril(np.ones((tm, tm), np.float32), -1), BF16)
    return pl.pallas_call(
        functools.partial(_moe_kernel, tm=tm, rows=rows),
        grid=(T // tm, N_EXPERTS),
        in_specs=[
            pl.BlockSpec((tm, D), lambda t, n: (t, 0)),
            pl.BlockSpec((tm, D), lambda t, n: (t, 0)),
            pl.BlockSpec((tm, LANES), lambda t, n: (t, 0)),
            pl.BlockSpec((tm, tm), lambda t, n: (0, 0)),
            pl.BlockSpec((1, D, 2 * D_FF_EXPERT), lambda t, n: (n, 0, 0)),
            pl.BlockSpec((1, D_FF_EXPERT, D), lambda t, n: (n, 0, 0)),
        ],
        out_specs=pl.BlockSpec((tm, D), lambda t, n: (t, 0)),
        out_shape=jax.ShapeDtypeStruct((T, D), F32),
        scratch_shapes=[
            pltpu.VMEM((rows, D), BF16),
            pltpu.VMEM((rows, LANES), F32),
            pltpu.VMEM((rows, D), F32),
            pltpu.VMEM((rows // MOE_BLK, tm, MOE_BLK), BF16),
            pltpu.SMEM((2 * N_GROUPS + 1,), jnp.int32),
        ],
        compiler_params=pltpu.CompilerParams(dimension_semantics=("arbitrary", "arbitrary"),
                                             vmem_limit_bytes=VMEM_LIMIT),
        name="moe",
    )(h2, x1, comb, ltri, w_up, w_down)


def _pick_tile(n, pref):
    t = min(n, pref)
    assert n % t == 0, (n, t)
    return t


def _rope_tables(S):
    half = HEAD_DIM_A // 2
    inv = ROPE_THETA ** (-jnp.arange(half, dtype=F32) / half)
    ang = jnp.arange(S, dtype=jnp.int32).astype(F32)[:, None] * inv[None, :]
    cos, sin = jnp.cos(ang), jnp.sin(ang)
    cos_t = jnp.concatenate([cos, cos, cos, cos], axis=-1)
    sin_t = jnp.concatenate([-sin, sin, -sin, sin], axis=-1)
    return cos_t, sin_t


def _layer(x, mem, g_mix, w_in, b_gate, g_qa, g_ka, g_idx_k, conv_w, conv_b, ln_g, ln_b,
           g_mem, w_mem_kv, g_qm, g_km, w_br_a, w_br_b, w_br_m, w_o, g_ffn,
           w_rg, b_rg, w_re, b_re, w_up, w_down):
    B, S, D = x.shape
    topk = min(TOPK_MAX, S // 4)
    tk = _pick_tile(S, 512)
    tq = _pick_tile(tk, 128)
    tt = _pick_tile(S, 512)
    tm = _pick_tile(B * S, 1024)

    sizes = (QA_W, N_KV_HEADS_A * HEAD_DIM_A, N_KV_HEADS_A * HEAD_DIM_A, QI_W, IDX_DIM, IDX_HEADS,
             CONV_IN_W, QM_W, N_BRANCHES * D)
    offs = np.concatenate([[0], np.cumsum(sizes)])
    col = lambda i: w_in[:, offs[i]:offs[i + 1]]
    w_qa, w_ka, w_va, w_qi, w_ki, w_wi, w_conv, w_qm, w_gate = (col(i) for i in range(9))
    hd = HEAD_DIM_A
    zeros_hd = jnp.zeros((D, hd), F32)
    ka_dup = [w_ka[:, g * hd:(g + 1) * hd] for g in range(N_KV_HEADS_A) for _ in range(2)]
    va_aug = [w for g in range(N_KV_HEADS_A) for w in (w_va[:, g * hd:(g + 1) * hd], zeros_hd)]
    w_dsa = jnp.concatenate(
        [w_qa, w_qi] + ka_dup + va_aug + [w_ki, w_ki, w_wi, jnp.zeros((D, WI_W - IDX_HEADS), F32)], axis=1).astype(BF16)
    vbias = jnp.tile(jnp.concatenate([jnp.zeros((hd,), F32), jnp.ones((hd,), F32)]), N_KV_HEADS_A).reshape(1, VA_W)
    seg = np.arange(QA_W) // hd
    bd = jnp.asarray(seg[:, None] == seg[None, :], BF16)
    cos_t, sin_t = _rope_tables(S)

    kmt, vm = _mem_kv(mem, g_mem, w_mem_kv, g_km)
    qat, qit, k, vt, ki, wit = _dsa_proj(
        x, g_mix.reshape(1, D), w_dsa, vbias,
        jnp.tile(g_qa, N_HEADS_A).reshape(1, QA_W), jnp.tile(g_ka, KA_W // hd).reshape(1, KA_W),
        jnp.tile(g_idx_k, KI_W // IDX_DIM).reshape(1, KI_W), cos_t, sin_t, bd, tk)
    o_a = _dsa_attn(qat, qit, wit, k, vt, ki, tq, topk)

    w_r = jnp.concatenate([w_re, w_rg, jnp.zeros((D, LANES - N_EXPERTS - N_GROUPS), F32)], axis=1)
    b_r = jnp.concatenate([b_re, b_rg, jnp.zeros((LANES - N_EXPERTS - N_GROUPS,), F32)]).reshape(1, LANES)
    w_r_hi = w_r.astype(BF16)
    w_r_lo = (w_r - w_r_hi.astype(F32)).astype(BF16)
    x1, h2, comb = _mixer(
        x, o_a, g_mix.reshape(1, D), w_conv.astype(BF16), w_qm.astype(BF16), w_gate.astype(BF16),
        b_gate.reshape(1, -1), conv_w.reshape(CONV_WIDTH, CONV_CH), conv_b.reshape(1, CONV_CH),
        ln_g.reshape(1, CONV_CH), ln_b.reshape(1, CONV_CH), g_qm.reshape(1, MEM_HEAD_DIM), kmt, vm,
        w_br_a.astype(BF16), w_br_b.astype(BF16), w_br_m.astype(BF16), w_o.astype(BF16),
        g_ffn.reshape(1, D), w_r_hi, w_r_lo, b_r, tt)

    out = _moe(h2.reshape(B * S, D), x1.reshape(B * S, D), comb.reshape(B * S, LANES),
               w_up.astype(BF16), w_down.astype(BF16), tm)
    return out.reshape(B, S, D)


def kernel(x, mem, g_mix, w_in, b_gate, g_qa, g_ka, g_idx_k, conv_w, conv_b, ln_g, ln_b, g_mem, w_mem_kv, g_qm, g_km,
           w_br_a, w_br_b, w_br_m, w_o, g_ffn, w_rg, b_rg, w_re, b_re, w_up, w_down):
    params = (g_mix, w_in, b_gate, g_qa, g_ka, g_idx_k, conv_w, conv_b, ln_g, ln_b, g_mem, w_mem_kv, g_qm, g_km,
              w_br_a, w_br_b, w_br_m, w_o, g_ffn, w_rg, b_rg, w_re, b_re, w_up, w_down)
    for l in range(g_mix.shape[0]):
        x = _layer(x, mem, *(p[l] for p in params))
    return x
```

```python
import functools

import jax
import jax.numpy as jnp
import numpy as np
from jax import lax
from jax.experimental import pallas as pl
from jax.experimental.pallas import tpu as pltpu

N_HEADS_A = 8
N_KV_HEADS_A = 2
HEAD_DIM_A = 64
IDX_HEADS = 8
IDX_DIM = 64
TOPK_MAX = 256
CONV_CH = 512
CONV_WIDTH = 31
MEM_HEADS = 4
MEM_HEAD_DIM = 128
N_BRANCHES = 3
N_GROUPS = 4
EXPERTS_PER_GROUP = 8
N_EXPERTS = N_GROUPS * EXPERTS_PER_GROUP
D_FF_EXPERT = 256
ROPE_THETA = 10000.0
EPS = 1e-6

LANES = 128
SUBLANES = 8
VMEM_LIMIT = 56 * 1024 * 1024

INT_MIN = -2 ** 31
LOG2_E = 1.4426950408889634
RADIX_CHECK_BIT = 26
NEG_BIG = -1e30
CONV_HALO = 32

BF16 = jnp.bfloat16
F32 = jnp.float32


def _dot(a, b):
    return jnp.dot(a, b, preferred_element_type=F32)


def _sigmoid(x):
    return 0.5 * jnp.tanh(0.5 * x) + 0.5


def _split_bf16(x):
    hi = x.astype(BF16)
    lo = (x - hi.astype(F32)).astype(BF16)
    return hi, lo


def _seg_sum(xsq, ones_bd):
    hi, lo = _split_bf16(xsq)
    return _dot(hi, ones_bd) + _dot(lo, ones_bd)


def _rot_half(x, head_dim):
    n = x.shape[-1]
    half = head_dim // 2
    lane = lax.broadcasted_iota(jnp.int32, x.shape, x.ndim - 1)
    is_lo = (lane & (head_dim - 1)) < half
    return jnp.where(is_lo, pltpu.roll(x, n - half, x.ndim - 1), pltpu.roll(x, half, x.ndim - 1))


def _tile_lanes(x, reps):
    return jnp.concatenate([x] * reps, axis=-1) if reps > 1 else x


def _mem_kv_kernel(mem_ref, g_mem_ref, w_ref, g_km_ref, kt_ref, v_ref):
    m = mem_ref[0]
    hm = m * lax.rsqrt(jnp.mean(m * m, axis=-1, keepdims=True) + EPS) * g_mem_ref[...]
    kv = _dot(hm.astype(BF16), w_ref[...])
    hd = MEM_HEADS * MEM_HEAD_DIM
    for h in range(MEM_HEADS):
        k = kv[:, h * MEM_HEAD_DIM:(h + 1) * MEM_HEAD_DIM]
        k = k * lax.rsqrt(jnp.mean(k * k, axis=-1, keepdims=True) + EPS) * g_km_ref[...]
        kt_ref[0, h] = k.T.astype(BF16)
        v_ref[0, h] = kv[:, hd + h * MEM_HEAD_DIM: hd + (h + 1) * MEM_HEAD_DIM].astype(BF16)


def _mem_kv(mem, g_mem, w_mem_kv, g_km):
    B, M, D = mem.shape
    hd2 = 2 * MEM_HEADS * MEM_HEAD_DIM
    return pl.pallas_call(
        _mem_kv_kernel,
        grid=(B,),
        in_specs=[
            pl.BlockSpec((1, M, D), lambda b: (b, 0, 0)),
            pl.BlockSpec((1, D), lambda b: (0, 0)),
            pl.BlockSpec((D, hd2), lambda b: (0, 0)),
            pl.BlockSpec((1, MEM_HEAD_DIM), lambda b: (0, 0)),
        ],
        out_specs=[
            pl.BlockSpec((1, MEM_HEADS, MEM_HEAD_DIM, M), lambda b: (b, 0, 0, 0)),
            pl.BlockSpec((1, MEM_HEADS, M, MEM_HEAD_DIM), lambda b: (b, 0, 0, 0)),
        ],
        out_shape=[
            jax.ShapeDtypeStruct((B, MEM_HEADS, MEM_HEAD_DIM, M), BF16),
            jax.ShapeDtypeStruct((B, MEM_HEADS, M, MEM_HEAD_DIM), BF16),
        ],
        compiler_params=pltpu.CompilerParams(dimension_semantics=("arbitrary",)),
        name="mem_kv",
    )(mem, g_mem.reshape(1, D), w_mem_kv.astype(BF16), g_km.reshape(1, MEM_HEAD_DIM))


QA_W = N_HEADS_A * HEAD_DIM_A
QI_W = IDX_HEADS * IDX_DIM
KA_W = 2 * N_KV_HEADS_A * HEAD_DIM_A
VA_W = 2 * N_KV_HEADS_A * HEAD_DIM_A
KI_W = 2 * IDX_DIM
WI_W = LANES
DSA_COLS = QA_W + QI_W + KA_W + VA_W + KI_W + WI_W


def _dsa_proj_kernel(x_ref, g_mix_ref, w_ref, vbias_ref, gq_ref, gk_ref, gki_ref, cos_ref, sin_ref, bd_ref,
                     qat_ref, qit_ref, k_ref, vt_ref, ki_ref, wit_ref):
    x = x_ref[0]
    h = x * lax.rsqrt(jnp.mean(x * x, axis=-1, keepdims=True) + EPS) * g_mix_ref[...]
    p = _dot(h.astype(BF16), w_ref[...])
    cos1 = cos_ref[...]
    sin1 = sin_ref[...]
    cos4, sin4 = _tile_lanes(cos1, 4), _tile_lanes(sin1, 4)
    cos2, sin2 = _tile_lanes(cos1, 2), _tile_lanes(sin1, 2)
    bd = bd_ref[...]
    inv_hd = 1.0 / HEAD_DIM_A

    def rope(v, c, s):
        return v * c + _rot_half(v, HEAD_DIM_A) * s

    o = 0
    qa = p[:, o:o + QA_W]; o += QA_W
    qa = qa * lax.rsqrt(_seg_sum(qa * qa, bd) * inv_hd + EPS) * gq_ref[...]
    qat_ref[0] = (rope(qa, cos4, sin4) * (HEAD_DIM_A ** -0.5 * LOG2_E)).T.astype(BF16)

    qi = p[:, o:o + QI_W]; o += QI_W
    qit_ref[0] = (rope(qi, cos4, sin4) * (IDX_DIM ** -0.5)).T.astype(BF16)

    ka = p[:, o:o + KA_W]; o += KA_W
    ka = ka * lax.rsqrt(_seg_sum(ka * ka, bd[:KA_W, :KA_W]) * inv_hd + EPS) * gk_ref[...]
    k_ref[0] = rope(ka, cos2, sin2).astype(BF16)

    vt_ref[0, 0] = (p[:, o:o + VA_W] + vbias_ref[...]).T.astype(BF16); o += VA_W

    ki = p[:, o:o + KI_W]; o += KI_W
    ki = ki * lax.rsqrt(_seg_sum(ki * ki, bd[:KI_W, :KI_W]) * (1.0 / IDX_DIM) + EPS) * gki_ref[...]
    ki_ref[0] = rope(ki, cos1, sin1).astype(BF16)

    wit_ref[0] = (p[:, o:o + WI_W] * (IDX_HEADS ** -0.5)).T[:IDX_HEADS, :]


def _dsa_proj(x, g_mix, w_dsa, vbias, g_qa, g_ka, g_idx_k, cos_t, sin_t, bd, tt):
    B, S, D = x.shape
    nt = S // tt
    const = lambda b, s: (0, 0)
    return pl.pallas_call(
        _dsa_proj_kernel,
        grid=(B, nt),
        in_specs=[
            pl.BlockSpec((1, tt, D), lambda b, s: (b, s, 0)),
            pl.BlockSpec((1, D), const),
            pl.BlockSpec((D, DSA_COLS), const),
            pl.BlockSpec((1, VA_W), const),
            pl.BlockSpec((1, QA_W), const),
            pl.BlockSpec((1, KA_W), const),
            pl.BlockSpec((1, KI_W), const),
            pl.BlockSpec((tt, LANES), lambda b, s: (s, 0)),
            pl.BlockSpec((tt, LANES), lambda b, s: (s, 0)),
            pl.BlockSpec((QA_W, QA_W), const),
        ],
        out_specs=[
            pl.BlockSpec((1, QA_W, tt), lambda b, s: (b, 0, s)),
            pl.BlockSpec((1, QI_W, tt), lambda b, s: (b, 0, s)),
            pl.BlockSpec((1, tt, KA_W), lambda b, s: (b, s, 0)),
            pl.BlockSpec((1, 1, VA_W, tt), lambda b, s: (b, s, 0, 0)),
            pl.BlockSpec((1, tt, KI_W), lambda b, s: (b, s, 0)),
            pl.BlockSpec((1, IDX_HEADS, tt), lambda b, s: (b, 0, s)),
        ],
        out_shape=[
            jax.ShapeDtypeStruct((B, QA_W, S), BF16),
            jax.ShapeDtypeStruct((B, QI_W, S), BF16),
            jax.ShapeDtypeStruct((B, S, KA_W), BF16),
            jax.ShapeDtypeStruct((B, nt, VA_W, tt), BF16),
            jax.ShapeDtypeStruct((B, S, KI_W), BF16),
            jax.ShapeDtypeStruct((B, IDX_HEADS, S), F32),
        ],
        compiler_params=pltpu.CompilerParams(dimension_semantics=("arbitrary", "arbitrary"),
                                             vmem_limit_bytes=VMEM_LIMIT),
        name="dsa_proj",
    )(x, g_mix, w_dsa, vbias, g_qa, g_ka, g_idx_k, cos_t, sin_t, bd)


def _key_of(f):
    bits = lax.bitcast_convert_type(f, jnp.int32)
    return bits ^ ((bits >> 31) & jnp.int32(0x7FFFFFFF))


def _pipelined_chunks(nchunks, last_chunk, matmul, finish):
    matmul(0, 0)

    def pair(i, carry):
        c0 = 2 * i
        matmul(jnp.minimum(c0 + 1, last_chunk), 1)
        finish(c0, 0)

        @pl.when(c0 + 1 < nchunks)
        def _():
            matmul(jnp.minimum(c0 + 2, last_chunk), 0)
            finish(c0 + 1, 1)

        return carry

    lax.fori_loop(0, (nchunks + 1) // 2, pair, 0)


def _dsa_attn_kernel(qat_ref, qit_ref, wit_ref, k_ref, vt_ref, ki_ref, o_ref,
                     qa_s, qi_s, keys_s, buf0, buf1, m_s, acc_s, *, tq, tk, topk, idx_bits):
    qt = pl.program_id(1)
    q0 = qt * tq
    nchunks = q0 // tk + 1
    last_chunk = nchunks - 1
    hpg = N_HEADS_A // N_KV_HEADS_A
    gw = hpg * tq
    half_pairs = hpg // 2
    bufs = (buf0, buf1)
    dim = lax.broadcasted_iota(jnp.int32, (LANES, tq), 0)
    lo_half = dim < HEAD_DIM_A

    for j in range(IDX_HEADS // 2):
        pair = qit_ref[0, j * LANES:(j + 1) * LANES, :]
        qi_s[:, (2 * j) * tq:(2 * j + 1) * tq] = jnp.where(lo_half, pair, jnp.zeros_like(pair))
        qi_s[:, (2 * j + 1) * tq:(2 * j + 2) * tq] = jnp.where(lo_half, jnp.zeros_like(pair), pair)
    for g in range(N_KV_HEADS_A):
        for jj in range(half_pairs):
            pair = qat_ref[0, (g * half_pairs + jj) * LANES:(g * half_pairs + jj + 1) * LANES, :]
            qa_s[:, g * gw + jj * tq:g * gw + (jj + 1) * tq] = jnp.where(lo_half, pair, jnp.zeros_like(pair))
            qa_s[:, g * gw + (half_pairs + jj) * tq:g * gw + (half_pairs + jj + 1) * tq] = (
                jnp.where(lo_half, jnp.zeros_like(pair), pair))

    key_in_chunk = lax.broadcasted_iota(jnp.int32, (tk, tq), 0)
    q_pos = q0 + lax.broadcasted_iota(jnp.int32, (tk, tq), 1)

    def score_chunk(c):
        r0 = pl.multiple_of(c * tk, tk)
        logits = _dot(ki_ref[0, pl.ds(r0, tk), :], qi_s[...])
        score = jnp.zeros((tk, tq), F32)
        for h in range(IDX_HEADS):
            score = score + wit_ref[0, h:h + 1, :] * jnp.maximum(logits[:, h * tq:(h + 1) * tq], 0.0)
        keys_s[c] = jnp.where(key_in_chunk + c * tk <= q_pos, _key_of(score), jnp.int32(INT_MIN))

    def score_pair(i, carry):
        score_chunk(2 * i)
        score_chunk(2 * i + 1)
        return carry

    lax.fori_loop(0, nchunks // 2, score_pair, 0)

    @pl.when(nchunks % 2 == 1)
    def _():
        score_chunk(last_chunk)

    def count_keys(pred_fn):
        def one(c, acc):
            hit = pred_fn(keys_s[c], c * tk).astype(jnp.int32)
            return acc + jnp.sum(hit.reshape(tk // 8, 8, tq), axis=0)

        def two(i, acc):
            return one(2 * i + 1, one(2 * i, acc))

        part = lax.fori_loop(0, nchunks // 2, two, jnp.zeros((8, tq), jnp.int32))
        part = lax.cond(nchunks % 2 == 1, lambda acc: one(last_chunk, acc), lambda acc: acc, part)
        return jnp.sum(part, axis=0, keepdims=True)

    total = q0 + lax.broadcasted_iota(jnp.int32, (1, tq), 1) + 1
    has_tau = total >= topk

    def unsettled(cnt):
        return jnp.max(jnp.where(has_tau & (cnt != topk), 1, 0))

    def radix_step(i, state):
        tau, cnt = state
        cand = tau + (jnp.int32(1) << (31 - i))
        c = count_keys(lambda k, _: k >= cand)
        up = c >= topk
        return jnp.where(up, cand, tau), jnp.where(up, c, cnt)

    state = lax.fori_loop(0, RADIX_CHECK_BIT, radix_step, (jnp.full((1, tq), INT_MIN, jnp.int32), total))
    tau, cnt = lax.cond(unsettled(state[1]) > 0,
                        lambda s: lax.fori_loop(RADIX_CHECK_BIT, 32, radix_step, s), lambda s: s, state)
    tau = jnp.where(has_tau, jnp.maximum(tau, jnp.int32(INT_MIN + 1)), jnp.int32(INT_MIN))

    tied = has_tau & (cnt > topk)

    def tie_cut():
        need = topk - count_keys(lambda k, _: k > tau)

        def idx_step(i, j):
            bit = jnp.int32(1) << (idx_bits - 1 - i)
            test = j + bit - 1
            f = count_keys(lambda k, base: (k == tau) & (base + key_in_chunk <= test))
            return jnp.where(f < need, j + bit, j)
        return lax.fori_loop(0, idx_bits, idx_step, jnp.zeros((1, tq), jnp.int32))

    jcut = lax.cond(jnp.max(jnp.where(tied, 1, 0)) > 0, tie_cut, lambda: jnp.zeros((1, tq), jnp.int32))
    jcut = jnp.where(tied, jcut, jnp.where(has_tau, jnp.int32(2 ** 30), jnp.int32(-1)))

    m_s[...] = jnp.full(m_s.shape, NEG_BIG, F32)
    acc_s[...] = jnp.zeros(acc_s.shape, F32)

    def attn_matmul(c, slot):
        r0 = pl.multiple_of(c * tk, tk)
        for g in range(N_KV_HEADS_A):
            bufs[slot][:, g * gw:(g + 1) * gw] = _dot(k_ref[0, pl.ds(r0, tk), g * LANES:(g + 1) * LANES],
                                                      qa_s[:, g * gw:(g + 1) * gw])

    def attn_finish(c, slot):
        k = keys_s[c]
        kpos = key_in_chunk + c * tk
        bias = jnp.where(k > tau, 0.0, jnp.where(k == tau, jnp.where(kpos <= jcut, 0.0, NEG_BIG), NEG_BIG))
        bias4 = jnp.concatenate([bias] * hpg, axis=1)
        for g in range(N_KV_HEADS_A):
            s = bufs[slot][:, g * gw:(g + 1) * gw] + bias4
            m_old = m_s[g]
            m_new = jnp.maximum(m_old, jnp.max(s, axis=0, keepdims=True))
            p = jnp.exp2(s - m_new)
            alpha = jnp.exp2(m_old - m_new)
            pv = _dot(vt_ref[0, c, g * LANES:(g + 1) * LANES, :], p.astype(BF16))
            acc_s[g] = alpha * acc_s[g] + pv
            m_s[g] = m_new

    _pipelined_chunks(nchunks, last_chunk, attn_matmul, attn_finish)

    for g in range(N_KV_HEADS_A):
        for jj in range(half_pairs):
            a_e = acc_s[g, :, jj * tq:(jj + 1) * tq]
            a_o = acc_s[g, :, (half_pairs + jj) * tq:(half_pairs + jj + 1) * tq]
            o_pair = jnp.concatenate([a_e[:HEAD_DIM_A] / a_e[HEAD_DIM_A:], a_o[:HEAD_DIM_A] / a_o[HEAD_DIM_A:]], axis=0)
            j = g * half_pairs + jj
            o_ref[0, :, j * LANES:(j + 1) * LANES] = o_pair.T.astype(BF16)


def _dsa_attn(qat, qit, wit, k, vt, ki, tq, topk):
    B, _, S = qat.shape
    nc, tk = vt.shape[1], vt.shape[3]
    hpg = N_HEADS_A // N_KV_HEADS_A
    idx_bits = max(1, int(np.ceil(np.log2(S))))
    assert IDX_HEADS == N_HEADS_A
    kernel = functools.partial(_dsa_attn_kernel, tq=tq, tk=tk, topk=topk, idx_bits=idx_bits)
    return pl.pallas_call(
        kernel,
        grid=(B, S // tq),
        in_specs=[
            pl.BlockSpec((1, QA_W, tq), lambda b, q: (b, 0, q)),
            pl.BlockSpec((1, QI_W, tq), lambda b, q: (b, 0, q)),
            pl.BlockSpec((1, IDX_HEADS, tq), lambda b, q: (b, 0, q)),
            pl.BlockSpec((1, S, KA_W), lambda b, q: (b, 0, 0)),
            pl.BlockSpec((1, nc, VA_W, tk), lambda b, q: (b, 0, 0, 0)),
            pl.BlockSpec((1, S, KI_W), lambda b, q: (b, 0, 0)),
        ],
        out_specs=pl.BlockSpec((1, tq, QA_W), lambda b, q: (b, q, 0)),
        out_shape=jax.ShapeDtypeStruct((B, S, QA_W), BF16),
        scratch_shapes=[
            pltpu.VMEM((LANES, N_HEADS_A * tq), BF16),
            pltpu.VMEM((LANES, IDX_HEADS * tq), BF16),
            pltpu.VMEM((nc, tk, tq), jnp.int32),
            pltpu.VMEM((tk, IDX_HEADS * tq), F32),
            pltpu.VMEM((tk, IDX_HEADS * tq), F32),
            pltpu.VMEM((N_KV_HEADS_A, 1, hpg * tq), F32),
            pltpu.VMEM((N_KV_HEADS_A, LANES, hpg * tq), F32),
        ],
        compiler_params=pltpu.CompilerParams(dimension_semantics=("arbitrary", "arbitrary"),
                                             vmem_limit_bytes=VMEM_LIMIT),
        name="dsa_attn",
    )(qat, qit, wit, k, vt, ki)


CONV_IN_W = 2 * CONV_CH
QM_W = MEM_HEADS * MEM_HEAD_DIM


def _mixer_kernel(x_ref, oa_ref, g_mix_ref, wc_ref, wq_ref, wg_ref, bg_ref, cw_ref, cb_ref, lng_ref, lnb_ref,
                  gqm_ref, kmt_ref, vm_ref, wa_ref, wb_ref, wm_ref, wo_ref, gffn_ref, wrh_ref, wrl_ref, br_ref,
                  x1_ref, h2_ref, comb_ref, ubuf, wbuf, *, tt):
    D = x_ref.shape[-1]
    x = x_ref[0]
    h = (x * lax.rsqrt(jnp.mean(x * x, axis=-1, keepdims=True) + EPS) * g_mix_ref[...]).astype(BF16)

    cin = _dot(h, wc_ref[...])
    u = cin[:, :CONV_CH] * _sigmoid(cin[:, CONV_CH:])

    @pl.when(pl.program_id(1) == 0)
    def _():
        ubuf[0:CONV_HALO, :] = jnp.zeros((CONV_HALO, CONV_CH), F32)

    ubuf[CONV_HALO:CONV_HALO + tt, :] = u
    y = jnp.zeros((tt, CONV_CH), F32) + cb_ref[...]
    first = CONV_HALO - (CONV_WIDTH - 1)
    for b in range(SUBLANES):
        taps = [j for j in range(CONV_WIDTH) if (first + j) % SUBLANES == b]
        if not taps:
            continue
        base = first + taps[0]
        if b == 0:
            window, w0 = ubuf, base
        else:
            span = taps[-1] - taps[0] + tt
            wbuf[0:span, :] = ubuf[base:base + span, :]
            window, w0 = wbuf, 0
        for j in taps:
            off = w0 + j - taps[0]
            y = y + cw_ref[j:j + 1, :] * window[off:off + tt, :]
    halo = ubuf[tt:tt + CONV_HALO, :]
    ubuf[0:CONV_HALO, :] = halo
    mu = jnp.mean(y, axis=-1, keepdims=True)
    yc = y - mu
    var = jnp.mean(yc * yc, axis=-1, keepdims=True)
    yn = yc * lax.rsqrt(var + EPS) * lng_ref[...] + lnb_ref[...]
    o_b = (yn * _sigmoid(yn)).astype(BF16)

    qm = _dot(h, wq_ref[...])
    o_m = []
    for hh in range(MEM_HEADS):
        q = qm[:, hh * MEM_HEAD_DIM:(hh + 1) * MEM_HEAD_DIM]
        q = q * lax.rsqrt(jnp.mean(q * q, axis=-1, keepdims=True) + EPS) * gqm_ref[...] * (MEM_HEAD_DIM ** -0.5)
        s = _dot(q.astype(BF16), kmt_ref[0, hh])
        p = jnp.exp(s - jnp.max(s, axis=-1, keepdims=True))
        l = jnp.sum(p, axis=-1, keepdims=True)
        o_m.append((_dot(p.astype(BF16), vm_ref[0, hh]) / l).astype(BF16))
    o_m = jnp.concatenate(o_m, axis=-1)

    gates = _sigmoid(_dot(h, wg_ref[...]) + bg_ref[...])
    merged = (gates[:, :D] * _dot(oa_ref[0], wa_ref[...])
              + gates[:, D:2 * D] * _dot(o_b, wb_ref[...])
              + gates[:, 2 * D:] * _dot(o_m, wm_ref[...]))
    x1 = x + _dot(merged.astype(BF16), wo_ref[...])
    x1_ref[0] = x1

    h2 = x1 * lax.rsqrt(jnp.mean(x1 * x1, axis=-1, keepdims=True) + EPS) * gffn_ref[...]
    h2_ref[0] = h2.astype(BF16)
    hi, lo = _split_bf16(h2)
    logits = _dot(hi, wrh_ref[...]) + _dot(hi, wrl_ref[...]) + _dot(lo, wrh_ref[...]) + br_ref[...]
    lane = lax.broadcasted_iota(jnp.int32, logits.shape, 1)
    ninf = jnp.float32(-jnp.inf)
    big = jnp.int32(1 << 20)
    is_g = (lane >= N_EXPERTS) & (lane < N_EXPERTS + N_GROUPS)
    gl = jnp.where(is_g, logits, ninf)
    gmax = jnp.max(gl, axis=-1, keepdims=True)
    gsel = jnp.min(jnp.where(gl == gmax, lane, big), axis=-1, keepdims=True) - N_EXPERTS
    p_sel = 1.0 / jnp.sum(jnp.where(is_g, jnp.exp(gl - gmax), 0.0), axis=-1, keepdims=True)
    in_grp = (lane < N_EXPERTS) & ((lane >> 3) == gsel)
    el = jnp.where(in_grp, logits, ninf)
    v1 = jnp.max(el, axis=-1, keepdims=True)
    i1 = jnp.min(jnp.where(el == v1, lane, big), axis=-1, keepdims=True)
    el2 = jnp.where(lane == i1, ninf, el)
    v2 = jnp.max(el2, axis=-1, keepdims=True)
    i2 = jnp.min(jnp.where(el2 == v2, lane, big), axis=-1, keepdims=True)
    e2 = jnp.exp(v2 - v1)
    den = 1.0 + e2
    comb_ref[0] = jnp.where(lane == i1, (1.0 / den) * p_sel,
                            jnp.where(lane == i2, (e2 / den) * p_sel,
                                      jnp.where(lane == N_EXPERTS, gsel.astype(F32), 0.0)))


def _mixer(x, oa, g_mix, wc, wq, wg, bg, cw, cb, lng, lnb, gqm, kmt, vm, wa, wb, wm, wo, gffn, wrh, wrl, br, tt):
    B, S, D = x.shape
    M = kmt.shape[-1]
    const = lambda b, s: (0, 0)
    tile = lambda w: pl.BlockSpec((1, tt, w), lambda b, s: (b, s, 0))
    full = lambda a: pl.BlockSpec(a.shape, const)
    return pl.pallas_call(
        functools.partial(_mixer_kernel, tt=tt),
        grid=(B, S // tt),
        in_specs=[
            tile(D), tile(QA_W), full(g_mix), full(wc), full(wq), full(wg), full(bg), full(cw), full(cb),
            full(lng), full(lnb), full(gqm),
            pl.BlockSpec((1, MEM_HEADS, MEM_HEAD_DIM, M), lambda b, s: (b, 0, 0, 0)),
            pl.BlockSpec((1, MEM_HEADS, M, MEM_HEAD_DIM), lambda b, s: (b, 0, 0, 0)),
            full(wa), full(wb), full(wm), full(wo), full(gffn), full(wrh), full(wrl), full(br),
        ],
        out_specs=[tile(D), tile(D), tile(LANES)],
        out_shape=[
            jax.ShapeDtypeStruct((B, S, D), F32),
            jax.ShapeDtypeStruct((B, S, D), BF16),
            jax.ShapeDtypeStruct((B, S, LANES), F32),
        ],
        scratch_shapes=[pltpu.VMEM((CONV_HALO + tt, CONV_CH), F32),
                        pltpu.VMEM((CONV_HALO + tt, CONV_CH), F32)],
        compiler_params=pltpu.CompilerParams(dimension_semantics=("arbitrary", "arbitrary"),
                                             vmem_limit_bytes=VMEM_LIMIT),
        name="mixer",
    )(x, oa, g_mix, wc, wq, wg, bg, cw, cb, lng, lnb, gqm, kmt, vm, wa, wb, wm, wo, gffn, wrh, wrl, br)


MOE_BLK = 256
MOE_EXPERTS_PER_STEP = 4


def _moe_kernel(h2_ref, x1_ref, comb_ref, ltri_ref, wup_ref, wdn_ref, o_ref,
                hs_s, combs_s, ys_s, pt_s, seg_s, *, tm, rows):
    n = pl.program_id(1)
    lane = lax.broadcasted_iota(jnp.int32, (tm, LANES), 1)

    @pl.when(n == 0)
    def _():
        comb = comb_ref[...]
        gid = comb[:, N_EXPERTS:N_EXPERTS + 1].astype(jnp.int32)
        member = lane == gid
        onehot = jnp.where(member, 1.0, 0.0)
        before = _dot(ltri_ref[...], onehot.astype(BF16))
        rank = jnp.sum(jnp.where(member, before, 0.0), axis=-1, keepdims=True)
        count = jnp.sum(onehot, axis=0, keepdims=True)
        lane1 = lax.broadcasted_iota(jnp.int32, (1, LANES), 1)
        start_blk = jnp.int32(0)
        start_row = jnp.zeros((1, LANES), F32)
        for g in range(N_GROUPS):
            cnt_g = jnp.sum(jnp.where(lane1 == g, count, 0.0)).astype(jnp.int32)
            nblk_g = (cnt_g + (MOE_BLK - 1)) // MOE_BLK
            seg_s[g] = start_blk
            seg_s[N_GROUPS + g] = nblk_g
            start_row = jnp.where(lane1 == g, (start_blk * MOE_BLK).astype(F32), start_row)
            start_blk = start_blk + nblk_g
        dest = jnp.sum(jnp.where(member, start_row, 0.0), axis=-1, keepdims=True) + rank
        seg_s[2 * N_GROUPS] = start_blk
        dest_i = dest.astype(jnp.int32)
        dest_row = jnp.broadcast_to(dest, (tm, LANES)).T[0:1, :].astype(jnp.int32)
        c_hi, c_lo = _split_bf16(comb)
        col = lax.broadcasted_iota(jnp.int32, (tm, MOE_BLK), 1)
        row = lax.broadcasted_iota(jnp.int32, (MOE_BLK, tm), 0)

        def permute_block(b, carry):
            r0 = pl.multiple_of(b * MOE_BLK, MOE_BLK)
            pt_s[b] = jnp.where(col + r0 == dest_i, 1.0, 0.0).astype(BF16)
            perm = jnp.where(row + r0 == dest_row, 1.0, 0.0).astype(BF16)
            hs_s[pl.ds(r0, MOE_BLK), :] = _dot(perm, h2_ref[...]).astype(BF16)
            combs_s[pl.ds(r0, MOE_BLK), :] = _dot(perm, c_hi) + _dot(perm, c_lo)
            ys_s[pl.ds(r0, MOE_BLK), :] = jnp.zeros((MOE_BLK, ys_s.shape[1]), F32)
            return carry

        lax.fori_loop(0, start_blk, permute_block, 0)

    g = (n * MOE_EXPERTS_PER_STEP) // EXPERTS_PER_GROUP
    first_blk = seg_s[g]
    lane_b = lax.broadcasted_iota(jnp.int32, (MOE_BLK, LANES), 1)

    def expert_block(b, carry):
        r0 = pl.multiple_of((first_blk + b) * MOE_BLK, MOE_BLK)
        hb = hs_s[pl.ds(r0, MOE_BLK), :]
        cb = combs_s[pl.ds(r0, MOE_BLK), :]
        acts = []
        for i in range(MOE_EXPERTS_PER_STEP):
            up = _dot(hb, wup_ref[i])
            a = up[:, :D_FF_EXPERT]
            c = jnp.sum(jnp.where(lane_b == n * MOE_EXPERTS_PER_STEP + i, cb, 0.0), axis=-1, keepdims=True)
            acts.append((a * _sigmoid(a) * up[:, D_FF_EXPERT:] * c).astype(BF16))
        w_dn = wdn_ref[...].reshape(MOE_EXPERTS_PER_STEP * D_FF_EXPERT, wdn_ref.shape[-1])
        ys_s[pl.ds(r0, MOE_BLK), :] += _dot(jnp.concatenate(acts, axis=-1), w_dn)
        return carry

    lax.fori_loop(0, seg_s[N_GROUPS + g], expert_block, 0)

    @pl.when(n == N_EXPERTS // MOE_EXPERTS_PER_STEP - 1)
    def _():
        o_ref[...] = x1_ref[...]

        def unpermute_block(b, carry):
            r0 = pl.multiple_of(b * MOE_BLK, MOE_BLK)
            o_ref[...] += _dot(pt_s[b], ys_s[pl.ds(r0, MOE_BLK), :].astype(BF16))
            return carry

        lax.fori_loop(0, seg_s[2 * N_GROUPS], unpermute_block, 0)


def _moe(h2, x1, comb, w_up, w_down, tm):
    T, D = h2.shape
    rows = tm + N_GROUPS * MOE_BLK
    ltri = jnp.asarray(np.tril(np.ones((tm, tm), np.float32), -1), BF16)
    return pl.pallas_call(
        functools.partial(_moe_kernel, tm=tm, rows=rows),
        grid=(T // tm, N_EXPERTS // MOE_EXPERTS_PER_STEP),
        in_specs=[
            pl.BlockSpec((tm, D), lambda t, n: (t, 0)),
            pl.BlockSpec((tm, D), lambda t, n: (t, 0)),
            pl.BlockSpec((tm, LANES), lambda t, n: (t, 0)),
            pl.BlockSpec((tm, tm), lambda t, n: (0, 0)),
            pl.BlockSpec((MOE_EXPERTS_PER_STEP, D, 2 * D_FF_EXPERT), lambda t, n: (n, 0, 0)),
            pl.BlockSpec((MOE_EXPERTS_PER_STEP, D_FF_EXPERT, D), lambda t, n: (n, 0, 0)),
        ],
        out_specs=pl.BlockSpec((tm, D), lambda t, n: (t, 0)),
        out_shape=jax.ShapeDtypeStruct((T, D), F32),
        scratch_shapes=[
            pltpu.VMEM((rows, D), BF16),
            pltpu.VMEM((rows, LANES), F32),
            pltpu.VMEM((rows, D), F32),
            pltpu.VMEM((rows // MOE_BLK, tm, MOE_BLK), BF16),
            pltpu.SMEM((2 * N_GROUPS + 1,), jnp.int32),
        ],
        compiler_params=pltpu.CompilerParams(dimension_semantics=("arbitrary", "arbitrary"),
                                             vmem_limit_bytes=VMEM_LIMIT),
        name="moe",
    )(h2, x1, comb, ltri, w_up, w_down)


def _pick_tile(n, pref):
    t = min(n, pref)
    assert n % t == 0, (n, t)
    return t


def _rope_tables(S):
    half = HEAD_DIM_A // 2
    inv = ROPE_THETA ** (-jnp.arange(half, dtype=F32) / half)
    ang = jnp.arange(S, dtype=jnp.int32).astype(F32)[:, None] * inv[None, :]
    cos, sin = jnp.cos(ang), jnp.sin(ang)
    cos_t = jnp.concatenate([cos, cos, cos, cos], axis=-1)
    sin_t = jnp.concatenate([-sin, sin, -sin, sin], axis=-1)
    return cos_t, sin_t


def _layer(x, mem, g_mix, w_in, b_gate, g_qa, g_ka, g_idx_k, conv_w, conv_b, ln_g, ln_b,
           g_mem, w_mem_kv, g_qm, g_km, w_br_a, w_br_b, w_br_m, w_o, g_ffn,
           w_rg, b_rg, w_re, b_re, w_up, w_down):
    B, S, D = x.shape
    topk = min(TOPK_MAX, S // 4)
    tk = _pick_tile(S, 512)
    tq = _pick_tile(tk, 128)
    tt = _pick_tile(S, 512)
    tm = _pick_tile(B * S, 1024)

    sizes = (QA_W, N_KV_HEADS_A * HEAD_DIM_A, N_KV_HEADS_A * HEAD_DIM_A, QI_W, IDX_DIM, IDX_HEADS,
             CONV_IN_W, QM_W, N_BRANCHES * D)
    offs = np.concatenate([[0], np.cumsum(sizes)])
    col = lambda i: w_in[:, offs[i]:offs[i + 1]]
    w_qa, w_ka, w_va, w_qi, w_ki, w_wi, w_conv, w_qm, w_gate = (col(i) for i in range(9))
    hd = HEAD_DIM_A
    zeros_hd = jnp.zeros((D, hd), F32)
    ka_dup = [w_ka[:, g * hd:(g + 1) * hd] for g in range(N_KV_HEADS_A) for _ in range(2)]
    va_aug = [w for g in range(N_KV_HEADS_A) for w in (w_va[:, g * hd:(g + 1) * hd], zeros_hd)]
    w_dsa = jnp.concatenate(
        [w_qa, w_qi] + ka_dup + va_aug + [w_ki, w_ki, w_wi, jnp.zeros((D, WI_W - IDX_HEADS), F32)], axis=1).astype(BF16)
    vbias = jnp.tile(jnp.concatenate([jnp.zeros((hd,), F32), jnp.ones((hd,), F32)]), N_KV_HEADS_A).reshape(1, VA_W)
    seg = np.arange(QA_W) // hd
    bd = jnp.asarray(seg[:, None] == seg[None, :], BF16)
    cos_t, sin_t = _rope_tables(S)

    kmt, vm = _mem_kv(mem, g_mem, w_mem_kv, g_km)
    qat, qit, k, vt, ki, wit = _dsa_proj(
        x, g_mix.reshape(1, D), w_dsa, vbias,
        jnp.tile(g_qa, N_HEADS_A).reshape(1, QA_W), jnp.tile(g_ka, KA_W // hd).reshape(1, KA_W),
        jnp.tile(g_idx_k, KI_W // IDX_DIM).reshape(1, KI_W), cos_t, sin_t, bd, tk)
    o_a = _dsa_attn(qat, qit, wit, k, vt, ki, tq, topk)

    w_r = jnp.concatenate([w_re, w_rg, jnp.zeros((D, LANES - N_EXPERTS - N_GROUPS), F32)], axis=1)
    b_r = jnp.concatenate([b_re, b_rg, jnp.zeros((LANES - N_EXPERTS - N_GROUPS,), F32)]).reshape(1, LANES)
    w_r_hi = w_r.astype(BF16)
    w_r_lo = (w_r - w_r_hi.astype(F32)).astype(BF16)
    x1, h2, comb = _mixer(
        x, o_a, g_mix.reshape(1, D), w_conv.astype(BF16), w_qm.astype(BF16), w_gate.astype(BF16),
        b_gate.reshape(1, -1), conv_w.reshape(CONV_WIDTH, CONV_CH), conv_b.reshape(1, CONV_CH),
        ln_g.reshape(1, CONV_CH), ln_b.reshape(1, CONV_CH), g_qm.reshape(1, MEM_HEAD_DIM), kmt, vm,
        w_br_a.astype(BF16), w_br_b.astype(BF16), w_br_m.astype(BF16), w_o.astype(BF16),
        g_ffn.reshape(1, D), w_r_hi, w_r_lo, b_r, tt)

    out = _moe(h2.reshape(B * S, D), x1.reshape(B * S, D), comb.reshape(B * S, LANES),
               w_up.astype(BF16), w_down.astype(BF16), tm)
    return out.reshape(B, S, D)


def kernel(x, mem, g_mix, w_in, b_gate, g_qa, g_ka, g_idx_k, conv_w, conv_b, ln_g, ln_b, g_mem, w_mem_kv, g_qm, g_km,
           w_br_a, w_br_b, w_br_m, w_o, g_ffn, w_rg, b_rg, w_re, b_re, w_up, w_down):
    params = (g_mix, w_in, b_gate, g_qa, g_ka, g_idx_k, conv_w, conv_b, ln_g, ln_b, g_mem, w_mem_kv, g_qm, g_km,
              w_br_a, w_br_b, w_br_m, w_o, g_ffn, w_rg, b_rg, w_re, b_re, w_up, w_down)
    for l in range(g_mix.shape[0]):
        x = _layer(x, mem, *(p[l] for p in params))
    return x
```

```python
import functools

import jax
import jax.numpy as jnp
import numpy as np
from jax import lax
from jax.experimental import pallas as pl
from jax.experimental.pallas import tpu as pltpu

N_HEADS_A = 8
N_KV_HEADS_A = 2
HEAD_DIM_A = 64
IDX_HEADS = 8
IDX_DIM = 64
TOPK_MAX = 256
CONV_CH = 512
CONV_WIDTH = 31
MEM_HEADS = 4
MEM_HEAD_DIM = 128
N_BRANCHES = 3
N_GROUPS = 4
EXPERTS_PER_GROUP = 8
N_EXPERTS = N_GROUPS * EXPERTS_PER_GROUP
D_FF_EXPERT = 256
ROPE_THETA = 10000.0
EPS = 1e-6

LANES = 128
SUBLANES = 8
VMEM_LIMIT = 56 * 1024 * 1024

INT_MIN = -2 ** 31
LOG2_E = 1.4426950408889634
HALF_BIAS = 2 ** 15
NEG_BIG = -1e30
CONV_HALO = 32

BF16 = jnp.bfloat16
F32 = jnp.float32


def _dot(a, b):
    return jnp.dot(a, b, preferred_element_type=F32)


def _sigmoid(x):
    return 0.5 * jnp.tanh(0.5 * x) + 0.5


def _split_bf16(x):
    hi = x.astype(BF16)
    lo = (x - hi.astype(F32)).astype(BF16)
    return hi, lo


def _seg_sum(xsq, ones_bd):
    hi, lo = _split_bf16(xsq)
    return _dot(hi, ones_bd) + _dot(lo, ones_bd)


def _rot_half(x, head_dim):
    n = x.shape[-1]
    half = head_dim // 2
    lane = lax.broadcasted_iota(jnp.int32, x.shape, x.ndim - 1)
    is_lo = (lane & (head_dim - 1)) < half
    return jnp.where(is_lo, pltpu.roll(x, n - half, x.ndim - 1), pltpu.roll(x, half, x.ndim - 1))


def _tile_lanes(x, reps):
    return jnp.concatenate([x] * reps, axis=-1) if reps > 1 else x


def _mem_kv_kernel(mem_ref, g_mem_ref, w_ref, g_km_ref, kt_ref, v_ref):
    m = mem_ref[0]
    hm = m * lax.rsqrt(jnp.mean(m * m, axis=-1, keepdims=True) + EPS) * g_mem_ref[...]
    kv = _dot(hm.astype(BF16), w_ref[...])
    hd = MEM_HEADS * MEM_HEAD_DIM
    for h in range(MEM_HEADS):
        k = kv[:, h * MEM_HEAD_DIM:(h + 1) * MEM_HEAD_DIM]
        k = k * lax.rsqrt(jnp.mean(k * k, axis=-1, keepdims=True) + EPS) * g_km_ref[...]
        kt_ref[0, h] = k.T.astype(BF16)
        v_ref[0, h] = kv[:, hd + h * MEM_HEAD_DIM: hd + (h + 1) * MEM_HEAD_DIM].astype(BF16)


def _mem_kv(mem, g_mem, w_mem_kv, g_km):
    B, M, D = mem.shape
    hd2 = 2 * MEM_HEADS * MEM_HEAD_DIM
    return pl.pallas_call(
        _mem_kv_kernel,
        grid=(B,),
        in_specs=[
            pl.BlockSpec((1, M, D), lambda b: (b, 0, 0)),
            pl.BlockSpec((1, D), lambda b: (0, 0)),
            pl.BlockSpec((D, hd2), lambda b: (0, 0)),
            pl.BlockSpec((1, MEM_HEAD_DIM), lambda b: (0, 0)),
        ],
        out_specs=[
            pl.BlockSpec((1, MEM_HEADS, MEM_HEAD_DIM, M), lambda b: (b, 0, 0, 0)),
            pl.BlockSpec((1, MEM_HEADS, M, MEM_HEAD_DIM), lambda b: (b, 0, 0, 0)),
        ],
        out_shape=[
            jax.ShapeDtypeStruct((B, MEM_HEADS, MEM_HEAD_DIM, M), BF16),
            jax.ShapeDtypeStruct((B, MEM_HEADS, M, MEM_HEAD_DIM), BF16),
        ],
        compiler_params=pltpu.CompilerParams(dimension_semantics=("arbitrary",)),
        name="mem_kv",
    )(mem, g_mem.reshape(1, D), w_mem_kv.astype(BF16), g_km.reshape(1, MEM_HEAD_DIM))


QA_W = N_HEADS_A * HEAD_DIM_A
QI_W = IDX_HEADS * IDX_DIM
KA_W = 2 * N_KV_HEADS_A * HEAD_DIM_A
VA_W = 2 * N_KV_HEADS_A * HEAD_DIM_A
KI_W = 2 * IDX_DIM
WI_W = LANES
DSA_COLS = QA_W + QI_W + KA_W + VA_W + KI_W + WI_W


def _dsa_proj_kernel(x_ref, g_mix_ref, w_ref, vbias_ref, gq_ref, gk_ref, gki_ref, cos_ref, sin_ref, bd_ref,
                     qat_ref, qit_ref, k_ref, vt_ref, ki_ref, wit_ref):
    x = x_ref[0]
    h = x * lax.rsqrt(jnp.mean(x * x, axis=-1, keepdims=True) + EPS) * g_mix_ref[...]
    p = _dot(h.astype(BF16), w_ref[...])
    cos1 = cos_ref[...]
    sin1 = sin_ref[...]
    cos4, sin4 = _tile_lanes(cos1, 4), _tile_lanes(sin1, 4)
    cos2, sin2 = _tile_lanes(cos1, 2), _tile_lanes(sin1, 2)
    bd = bd_ref[...]
    inv_hd = 1.0 / HEAD_DIM_A

    def rope(v, c, s):
        return v * c + _rot_half(v, HEAD_DIM_A) * s

    o = 0
    qa = p[:, o:o + QA_W]; o += QA_W
    qa = qa * lax.rsqrt(_seg_sum(qa * qa, bd) * inv_hd + EPS) * gq_ref[...]
    qat_ref[0] = (rope(qa, cos4, sin4) * (HEAD_DIM_A ** -0.5 * LOG2_E)).T.astype(BF16)

    qi = p[:, o:o + QI_W]; o += QI_W
    qit_ref[0] = (rope(qi, cos4, sin4) * (IDX_DIM ** -0.5)).T.astype(BF16)

    ka = p[:, o:o + KA_W]; o += KA_W
    ka = ka * lax.rsqrt(_seg_sum(ka * ka, bd[:KA_W, :KA_W]) * inv_hd + EPS) * gk_ref[...]
    k_ref[0] = rope(ka, cos2, sin2).astype(BF16)

    vt_ref[0, 0] = (p[:, o:o + VA_W] + vbias_ref[...]).T.astype(BF16); o += VA_W

    ki = p[:, o:o + KI_W]; o += KI_W
    ki = ki * lax.rsqrt(_seg_sum(ki * ki, bd[:KI_W, :KI_W]) * (1.0 / IDX_DIM) + EPS) * gki_ref[...]
    ki_ref[0] = rope(ki, cos1, sin1).astype(BF16)

    wit_ref[0] = (p[:, o:o + WI_W] * (IDX_HEADS ** -0.5)).T[:IDX_HEADS, :]


def _dsa_proj(x, g_mix, w_dsa, vbias, g_qa, g_ka, g_idx_k, cos_t, sin_t, bd, tt):
    B, S, D = x.shape
    nt = S // tt
    const = lambda b, s: (0, 0)
    return pl.pallas_call(
        _dsa_proj_kernel,
        grid=(B, nt),
        in_specs=[
            pl.BlockSpec((1, tt, D), lambda b, s: (b, s, 0)),
            pl.BlockSpec((1, D), const),
            pl.BlockSpec((D, DSA_COLS), const),
            pl.BlockSpec((1, VA_W), const),
            pl.BlockSpec((1, QA_W), const),
            pl.BlockSpec((1, KA_W), const),
            pl.BlockSpec((1, KI_W), const),
            pl.BlockSpec((tt, LANES), lambda b, s: (s, 0)),
            pl.BlockSpec((tt, LANES), lambda b, s: (s, 0)),
            pl.BlockSpec((QA_W, QA_W), const),
        ],
        out_specs=[
            pl.BlockSpec((1, QA_W, tt), lambda b, s: (b, 0, s)),
            pl.BlockSpec((1, QI_W, tt), lambda b, s: (b, 0, s)),
            pl.BlockSpec((1, tt, KA_W), lambda b, s: (b, s, 0)),
            pl.BlockSpec((1, 1, VA_W, tt), lambda b, s: (b, s, 0, 0)),
            pl.BlockSpec((1, tt, KI_W), lambda b, s: (b, s, 0)),
            pl.BlockSpec((1, IDX_HEADS, tt), lambda b, s: (b, 0, s)),
        ],
        out_shape=[
            jax.ShapeDtypeStruct((B, QA_W, S), BF16),
            jax.ShapeDtypeStruct((B, QI_W, S), BF16),
            jax.ShapeDtypeStruct((B, S, KA_W), BF16),
            jax.ShapeDtypeStruct((B, nt, VA_W, tt), BF16),
            jax.ShapeDtypeStruct((B, S, KI_W), BF16),
            jax.ShapeDtypeStruct((B, IDX_HEADS, S), F32),
        ],
        compiler_params=pltpu.CompilerParams(dimension_semantics=("arbitrary", "arbitrary"),
                                             vmem_limit_bytes=VMEM_LIMIT),
        name="dsa_proj",
    )(x, g_mix, w_dsa, vbias, g_qa, g_ka, g_idx_k, cos_t, sin_t, bd)


def _key_of(f):
    bits = lax.bitcast_convert_type(f, jnp.int32)
    return bits ^ ((bits >> 31) & jnp.int32(0x7FFFFFFF))


def _pack_halves(x):
    m = x.shape[0] // 2
    return (x[:m] & 0xFFFF) | (x[m:] << 16)


def _halves_of(words):
    return pltpu.bitcast(words, jnp.int16)


def _pipelined_chunks(nchunks, last_chunk, matmul, finish):
    matmul(0, 0)

    def pair(i, carry):
        c0 = 2 * i
        matmul(jnp.minimum(c0 + 1, last_chunk), 1)
        finish(c0, 0)

        @pl.when(c0 + 1 < nchunks)
        def _():
            matmul(jnp.minimum(c0 + 2, last_chunk), 0)
            finish(c0 + 1, 1)

        return carry

    lax.fori_loop(0, (nchunks + 1) // 2, pair, 0)


def _dsa_attn_kernel(qat_ref, qit_ref, wit_ref, k_ref, vt_ref, ki_ref, o_ref,
                     qa_s, qi_s, keys_s, khi_s, klo_s, zlo_s, buf0, buf1, m_s, acc_s, *, tq, tk, topk, idx_bits):
    qt = pl.program_id(1)
    q0 = qt * tq
    nchunks = q0 // tk + 1
    last_chunk = nchunks - 1
    hpg = N_HEADS_A // N_KV_HEADS_A
    gw = hpg * tq
    half_pairs = hpg // 2
    bufs = (buf0, buf1)
    dim = lax.broadcasted_iota(jnp.int32, (LANES, tq), 0)
    lo_half = dim < HEAD_DIM_A

    for j in range(IDX_HEADS // 2):
        pair = qit_ref[0, j * LANES:(j + 1) * LANES, :]
        qi_s[:, (2 * j) * tq:(2 * j + 1) * tq] = jnp.where(lo_half, pair, jnp.zeros_like(pair))
        qi_s[:, (2 * j + 1) * tq:(2 * j + 2) * tq] = jnp.where(lo_half, jnp.zeros_like(pair), pair)
    for g in range(N_KV_HEADS_A):
        for jj in range(half_pairs):
            pair = qat_ref[0, (g * half_pairs + jj) * LANES:(g * half_pairs + jj + 1) * LANES, :]
            qa_s[:, g * gw + jj * tq:g * gw + (jj + 1) * tq] = jnp.where(lo_half, pair, jnp.zeros_like(pair))
            qa_s[:, g * gw + (half_pairs + jj) * tq:g * gw + (half_pairs + jj + 1) * tq] = (
                jnp.where(lo_half, jnp.zeros_like(pair), pair))

    key_in_chunk = lax.broadcasted_iota(jnp.int32, (tk, tq), 0)
    q_pos = q0 + lax.broadcasted_iota(jnp.int32, (tk, tq), 1)

    def score_chunk(c):
        r0 = pl.multiple_of(c * tk, tk)
        logits = _dot(ki_ref[0, pl.ds(r0, tk), :], qi_s[...])
        score = jnp.zeros((tk, tq), F32)
        for h in range(IDX_HEADS):
            score = score + wit_ref[0, h:h + 1, :] * jnp.maximum(logits[:, h * tq:(h + 1) * tq], 0.0)
        key = jnp.where(key_in_chunk + c * tk <= q_pos, _key_of(score), jnp.int32(INT_MIN))
        keys_s[c] = key
        khi_s[c] = _pack_halves(key >> 16)
        klo_s[c] = _pack_halves((key & 0xFFFF) - HALF_BIAS)

    def score_pair(i, carry):
        score_chunk(2 * i)
        score_chunk(2 * i + 1)
        return carry

    lax.fori_loop(0, nchunks // 2, score_pair, 0)

    @pl.when(nchunks % 2 == 1)
    def _():
        score_chunk(last_chunk)

    def count_keys(pred_fn):
        def one(c, acc):
            hit = pred_fn(keys_s[c], c * tk).astype(jnp.int32)
            return acc + jnp.sum(hit.reshape(tk // 8, 8, tq), axis=0)

        def two(i, acc):
            return one(2 * i + 1, one(2 * i, acc))

        part = lax.fori_loop(0, nchunks // 2, two, jnp.zeros((8, tq), jnp.int32))
        part = lax.cond(nchunks % 2 == 1, lambda acc: one(last_chunk, acc), lambda acc: acc, part)
        return jnp.sum(part, axis=0, keepdims=True)

    total = q0 + lax.broadcasted_iota(jnp.int32, (1, tq), 1) + 1
    has_tau = total >= topk

    def count_halves(src_s, cand):
        cand16 = _halves_of(jnp.broadcast_to((cand & 0xFFFF) | (cand << 16), (SUBLANES, tq)))

        def one(c, acc):
            half = _halves_of(src_s[c])
            hits = [jnp.where(half[j * 2 * SUBLANES:(j + 1) * 2 * SUBLANES, :] >= cand16, jnp.int16(1), jnp.int16(0))
                    for j in range(tk // (2 * SUBLANES))]
            while len(hits) > 1:
                hits = [a + b for a, b in zip(hits[::2], hits[1::2])]
            return acc + hits[0]

        def two(i, acc):
            return one(2 * i + 1, one(2 * i, acc))

        acc = lax.fori_loop(0, nchunks // 2, two, jnp.zeros((2 * SUBLANES, tq), jnp.int16))
        acc = lax.cond(nchunks % 2 == 1, lambda a: one(last_chunk, a), lambda a: a, acc)
        words = pltpu.bitcast(acc, jnp.int32)
        return jnp.sum((words & 0xFFFF) + (words >> 16), axis=0, keepdims=True)

    def radix16(src_s, want):
        def step(i, state):
            v, cnt = state
            cand = v + (jnp.int32(1) << (15 - i))
            c = count_halves(src_s, cand)
            up = c >= want
            return jnp.where(up, cand, v), jnp.where(up, c, cnt)
        return lax.fori_loop(0, 16, step, (jnp.full((1, tq), -HALF_BIAS, jnp.int32), jnp.zeros((1, tq), jnp.int32)))

    tau_hi, cnt_hi_ge = radix16(khi_s, topk)
    cnt_hi_gt = jnp.where(tau_hi == HALF_BIAS - 1, 0, count_halves(khi_s, jnp.minimum(tau_hi + 1, HALF_BIAS - 1)))
    tau_hi16 = _halves_of(jnp.broadcast_to((tau_hi & 0xFFFF) | (tau_hi << 16), (SUBLANES, tq)))

    def low_of_bucket(c, carry):
        hi, lo = _halves_of(khi_s[c]), _halves_of(klo_s[c])
        for j in range(tk // (2 * SUBLANES)):
            rows = slice(j * 2 * SUBLANES, (j + 1) * 2 * SUBLANES)
            zlo_s[c, j * SUBLANES:(j + 1) * SUBLANES, :] = pltpu.bitcast(
                jnp.where(hi[rows, :] == tau_hi16, lo[rows, :], jnp.int16(-HALF_BIAS)), jnp.int32)
        return carry

    lax.fori_loop(0, nchunks, low_of_bucket, 0)
    tau_lo, cnt_lo_ge = radix16(zlo_s, topk - cnt_hi_gt)
    tau = (tau_hi << 16) | ((tau_lo + HALF_BIAS) & 0xFFFF)
    cnt = cnt_hi_gt + jnp.where(tau_lo == -HALF_BIAS, cnt_hi_ge - cnt_hi_gt, cnt_lo_ge)
    tau = jnp.where(has_tau, jnp.maximum(tau, jnp.int32(INT_MIN + 1)), jnp.int32(INT_MIN))

    tied = has_tau & (cnt > topk)

    def tie_cut():
        need = topk - count_keys(lambda k, _: k > tau)

        def idx_step(i, j):
            bit = jnp.int32(1) << (idx_bits - 1 - i)
            test = j + bit - 1
            f = count_keys(lambda k, base: (k == tau) & (base + key_in_chunk <= test))
            return jnp.where(f < need, j + bit, j)
        return lax.fori_loop(0, idx_bits, idx_step, jnp.zeros((1, tq), jnp.int32))

    jcut = lax.cond(jnp.max(jnp.where(tied, 1, 0)) > 0, tie_cut, lambda: jnp.zeros((1, tq), jnp.int32))
    jcut = jnp.where(tied, jcut, jnp.where(has_tau, jnp.int32(2 ** 30), jnp.int32(-1)))

    m_s[...] = jnp.full(m_s.shape, NEG_BIG, F32)
    acc_s[...] = jnp.zeros(acc_s.shape, F32)

    def attn_matmul(c, slot):
        r0 = pl.multiple_of(c * tk, tk)
        for g in range(N_KV_HEADS_A):
            bufs[slot][:, g * gw:(g + 1) * gw] = _dot(k_ref[0, pl.ds(r0, tk), g * LANES:(g + 1) * LANES],
                                                      qa_s[:, g * gw:(g + 1) * gw])

    def attn_finish(c, slot):
        k = keys_s[c]
        kpos = key_in_chunk + c * tk
        bias = jnp.where(k > tau, 0.0, jnp.where(k == tau, jnp.where(kpos <= jcut, 0.0, NEG_BIG), NEG_BIG))
        bias4 = jnp.concatenate([bias] * hpg, axis=1)
        for g in range(N_KV_HEADS_A):
            s = bufs[slot][:, g * gw:(g + 1) * gw] + bias4
            m_old = m_s[g]
            m_new = jnp.maximum(m_old, jnp.max(s, axis=0, keepdims=True))
            p = jnp.exp2(s - m_new)
            alpha = jnp.exp2(m_old - m_new)
            pv = _dot(vt_ref[0, c, g * LANES:(g + 1) * LANES, :], p.astype(BF16))
            acc_s[g] = alpha * acc_s[g] + pv
            m_s[g] = m_new

    _pipelined_chunks(nchunks, last_chunk, attn_matmul, attn_finish)

    for g in range(N_KV_HEADS_A):
        for jj in range(half_pairs):
            a_e = acc_s[g, :, jj * tq:(jj + 1) * tq]
            a_o = acc_s[g, :, (half_pairs + jj) * tq:(half_pairs + jj + 1) * tq]
            o_pair = jnp.concatenate([a_e[:HEAD_DIM_A] / a_e[HEAD_DIM_A:], a_o[:HEAD_DIM_A] / a_o[HEAD_DIM_A:]], axis=0)
            j = g * half_pairs + jj
            o_ref[0, :, j * LANES:(j + 1) * LANES] = o_pair.T.astype(BF16)


def _dsa_attn(qat, qit, wit, k, vt, ki, tq, topk):
    B, _, S = qat.shape
    nc, tk = vt.shape[1], vt.shape[3]
    hpg = N_HEADS_A // N_KV_HEADS_A
    idx_bits = max(1, int(np.ceil(np.log2(S))))
    assert IDX_HEADS == N_HEADS_A
    kernel = functools.partial(_dsa_attn_kernel, tq=tq, tk=tk, topk=topk, idx_bits=idx_bits)
    return pl.pallas_call(
        kernel,
        grid=(B, S // tq),
        in_specs=[
            pl.BlockSpec((1, QA_W, tq), lambda b, q: (b, 0, q)),
            pl.BlockSpec((1, QI_W, tq), lambda b, q: (b, 0, q)),
            pl.BlockSpec((1, IDX_HEADS, tq), lambda b, q: (b, 0, q)),
            pl.BlockSpec((1, S, KA_W), lambda b, q: (b, 0, 0)),
            pl.BlockSpec((1, nc, VA_W, tk), lambda b, q: (b, 0, 0, 0)),
            pl.BlockSpec((1, S, KI_W), lambda b, q: (b, 0, 0)),
        ],
        out_specs=pl.BlockSpec((1, tq, QA_W), lambda b, q: (b, q, 0)),
        out_shape=jax.ShapeDtypeStruct((B, S, QA_W), BF16),
        scratch_shapes=[
            pltpu.VMEM((LANES, N_HEADS_A * tq), BF16),
            pltpu.VMEM((LANES, IDX_HEADS * tq), BF16),
            pltpu.VMEM((nc, tk, tq), jnp.int32),
            pltpu.VMEM((nc, tk // 2, tq), jnp.int32),
            pltpu.VMEM((nc, tk // 2, tq), jnp.int32),
            pltpu.VMEM((nc, tk // 2, tq), jnp.int32),
            pltpu.VMEM((tk, IDX_HEADS * tq), F32),
            pltpu.VMEM((tk, IDX_HEADS * tq), F32),
            pltpu.VMEM((N_KV_HEADS_A, 1, hpg * tq), F32),
            pltpu.VMEM((N_KV_HEADS_A, LANES, hpg * tq), F32),
        ],
        compiler_params=pltpu.CompilerParams(dimension_semantics=("arbitrary", "arbitrary"),
                                             vmem_limit_bytes=VMEM_LIMIT),
        name="dsa_attn",
    )(qat, qit, wit, k, vt, ki)


CONV_IN_W = 2 * CONV_CH
QM_W = MEM_HEADS * MEM_HEAD_DIM


def _mixer_kernel(x_ref, oa_ref, g_mix_ref, wc_ref, wq_ref, wg_ref, bg_ref, cw_ref, cb_ref, lng_ref, lnb_ref,
                  gqm_ref, kmt_ref, vm_ref, wa_ref, wb_ref, wm_ref, wo_ref, gffn_ref, wrh_ref, wrl_ref, br_ref,
                  x1_ref, h2_ref, comb_ref, ubuf, wbuf, *, tt):
    D = x_ref.shape[-1]
    x = x_ref[0]
    h = (x * lax.rsqrt(jnp.mean(x * x, axis=-1, keepdims=True) + EPS) * g_mix_ref[...]).astype(BF16)

    cin = _dot(h, wc_ref[...])
    u = cin[:, :CONV_CH] * _sigmoid(cin[:, CONV_CH:])

    @pl.when(pl.program_id(1) == 0)
    def _():
        ubuf[0:CONV_HALO, :] = jnp.zeros((CONV_HALO, CONV_CH), F32)

    ubuf[CONV_HALO:CONV_HALO + tt, :] = u
    y = jnp.zeros((tt, CONV_CH), F32) + cb_ref[...]
    first = CONV_HALO - (CONV_WIDTH - 1)
    for b in range(SUBLANES):
        taps = [j for j in range(CONV_WIDTH) if (first + j) % SUBLANES == b]
        if not taps:
            continue
        base = first + taps[0]
        if b == 0:
            window, w0 = ubuf, base
        else:
            span = taps[-1] - taps[0] + tt
            wbuf[0:span, :] = ubuf[base:base + span, :]
            window, w0 = wbuf, 0
        for j in taps:
            off = w0 + j - taps[0]
            y = y + cw_ref[j:j + 1, :] * window[off:off + tt, :]
    halo = ubuf[tt:tt + CONV_HALO, :]
    ubuf[0:CONV_HALO, :] = halo
    mu = jnp.mean(y, axis=-1, keepdims=True)
    yc = y - mu
    var = jnp.mean(yc * yc, axis=-1, keepdims=True)
    yn = yc * lax.rsqrt(var + EPS) * lng_ref[...] + lnb_ref[...]
    o_b = (yn * _sigmoid(yn)).astype(BF16)

    qm = _dot(h, wq_ref[...])
    o_m = []
    for hh in range(MEM_HEADS):
        q = qm[:, hh * MEM_HEAD_DIM:(hh + 1) * MEM_HEAD_DIM]
        q = q * lax.rsqrt(jnp.mean(q * q, axis=-1, keepdims=True) + EPS) * gqm_ref[...] * (MEM_HEAD_DIM ** -0.5)
        s = _dot(q.astype(BF16), kmt_ref[0, hh])
        p = jnp.exp(s - jnp.max(s, axis=-1, keepdims=True))
        l = jnp.sum(p, axis=-1, keepdims=True)
        o_m.append((_dot(p.astype(BF16), vm_ref[0, hh]) / l).astype(BF16))
    o_m = jnp.concatenate(o_m, axis=-1)

    gates = _sigmoid(_dot(h, wg_ref[...]) + bg_ref[...])
    merged = (gates[:, :D] * _dot(oa_ref[0], wa_ref[...])
              + gates[:, D:2 * D] * _dot(o_b, wb_ref[...])
              + gates[:, 2 * D:] * _dot(o_m, wm_ref[...]))
    x1 = x + _dot(merged.astype(BF16), wo_ref[...])
    x1_ref[0] = x1

    h2 = x1 * lax.rsqrt(jnp.mean(x1 * x1, axis=-1, keepdims=True) + EPS) * gffn_ref[...]
    h2_ref[0] = h2.astype(BF16)
    hi, lo = _split_bf16(h2)
    logits = _dot(hi, wrh_ref[...]) + _dot(hi, wrl_ref[...]) + _dot(lo, wrh_ref[...]) + br_ref[...]
    lane = lax.broadcasted_iota(jnp.int32, logits.shape, 1)
    ninf = jnp.float32(-jnp.inf)
    big = jnp.int32(1 << 20)
    is_g = (lane >= N_EXPERTS) & (lane < N_EXPERTS + N_GROUPS)
    gl = jnp.where(is_g, logits, ninf)
    gmax = jnp.max(gl, axis=-1, keepdims=True)
    gsel = jnp.min(jnp.where(gl == gmax, lane, big), axis=-1, keepdims=True) - N_EXPERTS
    p_sel = 1.0 / jnp.sum(jnp.where(is_g, jnp.exp(gl - gmax), 0.0), axis=-1, keepdims=True)
    in_grp = (lane < N_EXPERTS) & ((lane >> 3) == gsel)
    el = jnp.where(in_grp, logits, ninf)
    v1 = jnp.max(el, axis=-1, keepdims=True)
    i1 = jnp.min(jnp.where(el == v1, lane, big), axis=-1, keepdims=True)
    el2 = jnp.where(lane == i1, ninf, el)
    v2 = jnp.max(el2, axis=-1, keepdims=True)
    i2 = jnp.min(jnp.where(el2 == v2, lane, big), axis=-1, keepdims=True)
    e2 = jnp.exp(v2 - v1)
    den = 1.0 + e2
    comb_ref[0] = jnp.where(lane == i1, (1.0 / den) * p_sel,
                            jnp.where(lane == i2, (e2 / den) * p_sel,
                                      jnp.where(lane == N_EXPERTS, gsel.astype(F32), 0.0)))


def _mixer(x, oa, g_mix, wc, wq, wg, bg, cw, cb, lng, lnb, gqm, kmt, vm, wa, wb, wm, wo, gffn, wrh, wrl, br, tt):
    B, S, D = x.shape
    M = kmt.shape[-1]
    const = lambda b, s: (0, 0)
    tile = lambda w: pl.BlockSpec((1, tt, w), lambda b, s: (b, s, 0))
    full = lambda a: pl.BlockSpec(a.shape, const)
    return pl.pallas_call(
        functools.partial(_mixer_kernel, tt=tt),
        grid=(B, S // tt),
        in_specs=[
            tile(D), tile(QA_W), full(g_mix), full(wc), full(wq), full(wg), full(bg), full(cw), full(cb),
            full(lng), full(lnb), full(gqm),
            pl.BlockSpec((1, MEM_HEADS, MEM_HEAD_DIM, M), lambda b, s: (b, 0, 0, 0)),
            pl.BlockSpec((1, MEM_HEADS, M, MEM_HEAD_DIM), lambda b, s: (b, 0, 0, 0)),
            full(wa), full(wb), full(wm), full(wo), full(gffn), full(wrh), full(wrl), full(br),
        ],
        out_specs=[tile(D), tile(D), tile(LANES)],
        out_shape=[
            jax.ShapeDtypeStruct((B, S, D), F32),
            jax.ShapeDtypeStruct((B, S, D), BF16),
            jax.ShapeDtypeStruct((B, S, LANES), F32),
        ],
        scratch_shapes=[pltpu.VMEM((CONV_HALO + tt, CONV_CH), F32),
                        pltpu.VMEM((CONV_HALO + tt, CONV_CH), F32)],
        compiler_params=pltpu.CompilerParams(dimension_semantics=("arbitrary", "arbitrary"),
                                             vmem_limit_bytes=VMEM_LIMIT),
        name="mixer",
    )(x, oa, g_mix, wc, wq, wg, bg, cw, cb, lng, lnb, gqm, kmt, vm, wa, wb, wm, wo, gffn, wrh, wrl, br)


MOE_BLK = 256
MOE_EXPERTS_PER_STEP = 4


def _moe_kernel(h2_ref, x1_ref, comb_ref, ltri_ref, wup_ref, wdn_ref, o_ref,
                hs_s, combs_s, ys_s, pt_s, seg_s, *, tm, rows):
    n = pl.program_id(1)
    lane = lax.broadcasted_iota(jnp.int32, (tm, LANES), 1)

    @pl.when(n == 0)
    def _():
        comb = comb_ref[...]
        gid = comb[:, N_EXPERTS:N_EXPERTS + 1].astype(jnp.int32)
        member = lane == gid
        onehot = jnp.where(member, 1.0, 0.0)
        before = _dot(ltri_ref[...], onehot.astype(BF16))
        rank = jnp.sum(jnp.where(member, before, 0.0), axis=-1, keepdims=True)
        count = jnp.sum(onehot, axis=0, keepdims=True)
        lane1 = lax.broadcasted_iota(jnp.int32, (1, LANES), 1)
        start_blk = jnp.int32(0)
        start_row = jnp.zeros((1, LANES), F32)
        for g in range(N_GROUPS):
            cnt_g = jnp.sum(jnp.where(lane1 == g, count, 0.0)).astype(jnp.int32)
            nblk_g = (cnt_g + (MOE_BLK - 1)) // MOE_BLK
            seg_s[g] = start_blk
            seg_s[N_GROUPS + g] = nblk_g
            start_row = jnp.where(lane1 == g, (start_blk * MOE_BLK).astype(F32), start_row)
            start_blk = start_blk + nblk_g
        dest = jnp.sum(jnp.where(member, start_row, 0.0), axis=-1, keepdims=True) + rank
        seg_s[2 * N_GROUPS] = start_blk
        dest_i = dest.astype(jnp.int32)
        dest_row = jnp.broadcast_to(dest, (tm, LANES)).T[0:1, :].astype(jnp.int32)
        c_hi, c_lo = _split_bf16(comb)
        col = lax.broadcasted_iota(jnp.int32, (tm, MOE_BLK), 1)
        row = lax.broadcasted_iota(jnp.int32, (MOE_BLK, tm), 0)

        def permute_block(b, carry):
            r0 = pl.multiple_of(b * MOE_BLK, MOE_BLK)
            pt_s[b] = jnp.where(col + r0 == dest_i, 1.0, 0.0).astype(BF16)
            perm = jnp.where(row + r0 == dest_row, 1.0, 0.0).astype(BF16)
            hs_s[pl.ds(r0, MOE_BLK), :] = _dot(perm, h2_ref[...]).astype(BF16)
            combs_s[pl.ds(r0, MOE_BLK), :] = _dot(perm, c_hi) + _dot(perm, c_lo)
            ys_s[pl.ds(r0, MOE_BLK), :] = jnp.zeros((MOE_BLK, ys_s.shape[1]), F32)
            return carry

        lax.fori_loop(0, start_blk, permute_block, 0)

    g = (n * MOE_EXPERTS_PER_STEP) // EXPERTS_PER_GROUP
    first_blk = seg_s[g]
    lane_b = lax.broadcasted_iota(jnp.int32, (MOE_BLK, LANES), 1)

    def expert_block(b, carry):
        r0 = pl.multiple_of((first_blk + b) * MOE_BLK, MOE_BLK)
        hb = hs_s[pl.ds(r0, MOE_BLK), :]
        cb = combs_s[pl.ds(r0, MOE_BLK), :]
        acts = []
        for i in range(MOE_EXPERTS_PER_STEP):
            up = _dot(hb, wup_ref[i])
            a = up[:, :D_FF_EXPERT]
            c = jnp.sum(jnp.where(lane_b == n * MOE_EXPERTS_PER_STEP + i, cb, 0.0), axis=-1, keepdims=True)
            acts.append((a * _sigmoid(a) * up[:, D_FF_EXPERT:] * c).astype(BF16))
        w_dn = wdn_ref[...].reshape(MOE_EXPERTS_PER_STEP * D_FF_EXPERT, wdn_ref.shape[-1])
        ys_s[pl.ds(r0, MOE_BLK), :] += _dot(jnp.concatenate(acts, axis=-1), w_dn)
        return carry

    lax.fori_loop(0, seg_s[N_GROUPS + g], expert_block, 0)

    @pl.when(n == N_EXPERTS // MOE_EXPERTS_PER_STEP - 1)
    def _():
        o_ref[...] = x1_ref[...]

        def unpermute_block(b, carry):
            r0 = pl.multiple_of(b * MOE_BLK, MOE_BLK)
            o_ref[...] += _dot(pt_s[b], ys_s[pl.ds(r0, MOE_BLK), :].astype(BF16))
            return carry

        lax.fori_loop(0, seg_s[2 * N_GROUPS], unpermute_block, 0)


def _moe(h2, x1, comb, w_up, w_down, tm):
    T, D = h2.shape
    rows = tm + N_GROUPS * MOE_BLK
    ltri = jnp.asarray(np.tril(np.ones((tm, tm), np.float32), -1), BF16)
    return pl.pallas_call(
        functools.partial(_moe_kernel, tm=tm, rows=rows),
        grid=(T // tm, N_EXPERTS // MOE_EXPERTS_PER_STEP),
        in_specs=[
            pl.BlockSpec((tm, D), lambda t, n: (t, 0)),
            pl.BlockSpec((tm, D), lambda t, n: (t, 0)),
            pl.BlockSpec((tm, LANES), lambda t, n: (t, 0)),
            pl.BlockSpec((tm, tm), lambda t, n: (0, 0)),
            pl.BlockSpec((MOE_EXPERTS_PER_STEP, D, 2 * D_FF_EXPERT), lambda t, n: (n, 0, 0)),
            pl.BlockSpec((MOE_EXPERTS_PER_STEP, D_FF_EXPERT, D), lambda t, n: (n, 0, 0)),
        ],
        out_specs=pl.BlockSpec((tm, D), lambda t, n: (t, 0)),
        out_shape=jax.ShapeDtypeStruct((T, D), F32),
        scratch_shapes=[
            pltpu.VMEM((rows, D), BF16),
            pltpu.VMEM((rows, LANES), F32),
            pltpu.VMEM((rows, D), F32),
            pltpu.VMEM((rows // MOE_BLK, tm, MOE_BLK), BF16),
            pltpu.SMEM((2 * N_GROUPS + 1,), jnp.int32),
        ],
        compiler_params=pltpu.CompilerParams(dimension_semantics=("arbitrary", "arbitrary"),
                                             vmem_limit_bytes=VMEM_LIMIT),
        name="moe",
    )(h2, x1, comb, ltri, w_up, w_down)


def _pick_tile(n, pref):
    t = min(n, pref)
    assert n % t == 0, (n, t)
    return t


def _rope_tables(S):
    half = HEAD_DIM_A // 2
    inv = ROPE_THETA ** (-jnp.arange(half, dtype=F32) / half)
    ang = jnp.arange(S, dtype=jnp.int32).astype(F32)[:, None] * inv[None, :]
    cos, sin = jnp.cos(ang), jnp.sin(ang)
    cos_t = jnp.concatenate([cos, cos, cos, cos], axis=-1)
    sin_t = jnp.concatenate([-sin, sin, -sin, sin], axis=-1)
    return cos_t, sin_t


def _layer(x, mem, g_mix, w_in, b_gate, g_qa, g_ka, g_idx_k, conv_w, conv_b, ln_g, ln_b,
           g_mem, w_mem_kv, g_qm, g_km, w_br_a, w_br_b, w_br_m, w_o, g_ffn,
           w_rg, b_rg, w_re, b_re, w_up, w_down):
    B, S, D = x.shape
    topk = min(TOPK_MAX, S // 4)
    tk = _pick_tile(S, 512)
    tq = _pick_tile(tk, 128)
    tt = _pick_tile(S, 512)
    tm = _pick_tile(B * S, 1024)

    sizes = (QA_W, N_KV_HEADS_A * HEAD_DIM_A, N_KV_HEADS_A * HEAD_DIM_A, QI_W, IDX_DIM, IDX_HEADS,
             CONV_IN_W, QM_W, N_BRANCHES * D)
    offs = np.concatenate([[0], np.cumsum(sizes)])
    col = lambda i: w_in[:, offs[i]:offs[i + 1]]
    w_qa, w_ka, w_va, w_qi, w_ki, w_wi, w_conv, w_qm, w_gate = (col(i) for i in range(9))
    hd = HEAD_DIM_A
    zeros_hd = jnp.zeros((D, hd), F32)
    ka_dup = [w_ka[:, g * hd:(g + 1) * hd] for g in range(N_KV_HEADS_A) for _ in range(2)]
    va_aug = [w for g in range(N_KV_HEADS_A) for w in (w_va[:, g * hd:(g + 1) * hd], zeros_hd)]
    w_dsa = jnp.concatenate(
        [w_qa, w_qi] + ka_dup + va_aug + [w_ki, w_ki, w_wi, jnp.zeros((D, WI_W - IDX_HEADS), F32)], axis=1).astype(BF16)
    vbias = jnp.tile(jnp.concatenate([jnp.zeros((hd,), F32), jnp.ones((hd,), F32)]), N_KV_HEADS_A).reshape(1, VA_W)
    seg = np.arange(QA_W) // hd
    bd = jnp.asarray(seg[:, None] == seg[None, :], BF16)
    cos_t, sin_t = _rope_tables(S)

    kmt, vm = _mem_kv(mem, g_mem, w_mem_kv, g_km)
    qat, qit, k, vt, ki, wit = _dsa_proj(
        x, g_mix.reshape(1, D), w_dsa, vbias,
        jnp.tile(g_qa, N_HEADS_A).reshape(1, QA_W), jnp.tile(g_ka, KA_W // hd).reshape(1, KA_W),
        jnp.tile(g_idx_k, KI_W // IDX_DIM).reshape(1, KI_W), cos_t, sin_t, bd, tk)
    o_a = _dsa_attn(qat, qit, wit, k, vt, ki, tq, topk)

    w_r = jnp.concatenate([w_re, w_rg, jnp.zeros((D, LANES - N_EXPERTS - N_GROUPS), F32)], axis=1)
    b_r = jnp.concatenate([b_re, b_rg, jnp.zeros((LANES - N_EXPERTS - N_GROUPS,), F32)]).reshape(1, LANES)
    w_r_hi = w_r.astype(BF16)
    w_r_lo = (w_r - w_r_hi.astype(F32)).astype(BF16)
    x1, h2, comb = _mixer(
        x, o_a, g_mix.reshape(1, D), w_conv.astype(BF16), w_qm.astype(BF16), w_gate.astype(BF16),
        b_gate.reshape(1, -1), conv_w.reshape(CONV_WIDTH, CONV_CH), conv_b.reshape(1, CONV_CH),
        ln_g.reshape(1, CONV_CH), ln_b.reshape(1, CONV_CH), g_qm.reshape(1, MEM_HEAD_DIM), kmt, vm,
        w_br_a.astype(BF16), w_br_b.astype(BF16), w_br_m.astype(BF16), w_o.astype(BF16),
        g_ffn.reshape(1, D), w_r_hi, w_r_lo, b_r, tt)

    out = _moe(h2.reshape(B * S, D), x1.reshape(B * S, D), comb.reshape(B * S, LANES),
               w_up.astype(BF16), w_down.astype(BF16), tm)
    return out.reshape(B, S, D)


def kernel(x, mem, g_mix, w_in, b_gate, g_qa, g_ka, g_idx_k, conv_w, conv_b, ln_g, ln_b, g_mem, w_mem_kv, g_qm, g_km,
           w_br_a, w_br_b, w_br_m, w_o, g_ffn, w_rg, b_rg, w_re, b_re, w_up, w_down):
    params = (g_mix, w_in, b_gate, g_qa, g_ka, g_idx_k, conv_w, conv_b, ln_g, ln_b, g_mem, w_mem_kv, g_qm, g_km,
              w_br_a, w_br_b, w_br_m, w_o, g_ffn, w_rg, b_rg, w_re, b_re, w_up, w_down)
    for l in range(g_mix.shape[0]):
        x = _layer(x, mem, *(p[l] for p in params))
    return x
```

```python
import functools

import jax
import jax.numpy as jnp
import numpy as np
from jax import lax
from jax.experimental import pallas as pl
from jax.experimental.pallas import tpu as pltpu

N_HEADS_A = 8
N_KV_HEADS_A = 2
HEAD_DIM_A = 64
IDX_HEADS = 8
IDX_DIM = 64
TOPK_MAX = 256
CONV_CH = 512
CONV_WIDTH = 31
MEM_HEADS = 4
MEM_HEAD_DIM = 128
N_BRANCHES = 3
N_GROUPS = 4
EXPERTS_PER_GROUP = 8
N_EXPERTS = N_GROUPS * EXPERTS_PER_GROUP
D_FF_EXPERT = 256
ROPE_THETA = 10000.0
EPS = 1e-6

LANES = 128
SUBLANES = 8
VMEM_LIMIT = 56 * 1024 * 1024

INT_MIN = -2 ** 31
LOG2_E = 1.4426950408889634
HALF_BIAS = 2 ** 15
NEG_BIG = -1e30
CONV_HALO = 32

BF16 = jnp.bfloat16
F32 = jnp.float32


def _dot(a, b):
    return jnp.dot(a, b, preferred_element_type=F32)


def _sigmoid(x):
    return 0.5 * jnp.tanh(0.5 * x) + 0.5


def _split_bf16(x):
    hi = x.astype(BF16)
    lo = (x - hi.astype(F32)).astype(BF16)
    return hi, lo


def _seg_sum(xsq, ones_bd):
    hi, lo = _split_bf16(xsq)
    return _dot(hi, ones_bd) + _dot(lo, ones_bd)


def _rot_half(x, head_dim):
    n = x.shape[-1]
    half = head_dim // 2
    lane = lax.broadcasted_iota(jnp.int32, x.shape, x.ndim - 1)
    is_lo = (lane & (head_dim - 1)) < half
    return jnp.where(is_lo, pltpu.roll(x, n - half, x.ndim - 1), pltpu.roll(x, half, x.ndim - 1))


def _tile_lanes(x, reps):
    return jnp.concatenate([x] * reps, axis=-1) if reps > 1 else x


def _mem_kv_kernel(mem_ref, g_mem_ref, w_ref, g_km_ref, kt_ref, v_ref):
    m = mem_ref[0]
    hm = m * lax.rsqrt(jnp.mean(m * m, axis=-1, keepdims=True) + EPS) * g_mem_ref[...]
    kv = _dot(hm.astype(BF16), w_ref[...])
    hd = MEM_HEADS * MEM_HEAD_DIM
    for h in range(MEM_HEADS):
        k = kv[:, h * MEM_HEAD_DIM:(h + 1) * MEM_HEAD_DIM]
        k = k * lax.rsqrt(jnp.mean(k * k, axis=-1, keepdims=True) + EPS) * g_km_ref[...]
        kt_ref[0, h] = k.T.astype(BF16)
        v_ref[0, h] = kv[:, hd + h * MEM_HEAD_DIM: hd + (h + 1) * MEM_HEAD_DIM].astype(BF16)


def _mem_kv(mem, g_mem, w_mem_kv, g_km):
    B, M, D = mem.shape
    hd2 = 2 * MEM_HEADS * MEM_HEAD_DIM
    return pl.pallas_call(
        _mem_kv_kernel,
        grid=(B,),
        in_specs=[
            pl.BlockSpec((1, M, D), lambda b: (b, 0, 0)),
            pl.BlockSpec((1, D), lambda b: (0, 0)),
            pl.BlockSpec((D, hd2), lambda b: (0, 0)),
            pl.BlockSpec((1, MEM_HEAD_DIM), lambda b: (0, 0)),
        ],
        out_specs=[
            pl.BlockSpec((1, MEM_HEADS, MEM_HEAD_DIM, M), lambda b: (b, 0, 0, 0)),
            pl.BlockSpec((1, MEM_HEADS, M, MEM_HEAD_DIM), lambda b: (b, 0, 0, 0)),
        ],
        out_shape=[
            jax.ShapeDtypeStruct((B, MEM_HEADS, MEM_HEAD_DIM, M), BF16),
            jax.ShapeDtypeStruct((B, MEM_HEADS, M, MEM_HEAD_DIM), BF16),
        ],
        compiler_params=pltpu.CompilerParams(dimension_semantics=("arbitrary",)),
        name="mem_kv",
    )(mem, g_mem.reshape(1, D), w_mem_kv.astype(BF16), g_km.reshape(1, MEM_HEAD_DIM))


QA_W = N_HEADS_A * HEAD_DIM_A
QI_W = IDX_HEADS * IDX_DIM
KA_W = 2 * N_KV_HEADS_A * HEAD_DIM_A
VA_W = 2 * N_KV_HEADS_A * HEAD_DIM_A
KI_W = 2 * IDX_DIM
WI_W = LANES
DSA_COLS = QA_W + QI_W + KA_W + VA_W + KI_W + WI_W


def _dsa_proj_kernel(x_ref, g_mix_ref, w_ref, vbias_ref, gq_ref, gk_ref, gki_ref, cos_ref, sin_ref, bd_ref,
                     qat_ref, qit_ref, k_ref, vt_ref, ki_ref, wit_ref):
    x = x_ref[0]
    h = x * lax.rsqrt(jnp.mean(x * x, axis=-1, keepdims=True) + EPS) * g_mix_ref[...]
    p = _dot(h.astype(BF16), w_ref[...])
    cos1 = cos_ref[...]
    sin1 = sin_ref[...]
    cos4, sin4 = _tile_lanes(cos1, 4), _tile_lanes(sin1, 4)
    cos2, sin2 = _tile_lanes(cos1, 2), _tile_lanes(sin1, 2)
    bd = bd_ref[...]
    inv_hd = 1.0 / HEAD_DIM_A

    def rope(v, c, s):
        return v * c + _rot_half(v, HEAD_DIM_A) * s

    o = 0
    qa = p[:, o:o + QA_W]; o += QA_W
    qa = qa * lax.rsqrt(_seg_sum(qa * qa, bd) * inv_hd + EPS) * gq_ref[...]
    qat_ref[0] = (rope(qa, cos4, sin4) * (HEAD_DIM_A ** -0.5 * LOG2_E)).T.astype(BF16)

    qi = p[:, o:o + QI_W]; o += QI_W
    qit_ref[0] = (rope(qi, cos4, sin4) * (IDX_DIM ** -0.5)).T.astype(BF16)

    ka = p[:, o:o + KA_W]; o += KA_W
    ka = ka * lax.rsqrt(_seg_sum(ka * ka, bd[:KA_W, :KA_W]) * inv_hd + EPS) * gk_ref[...]
    k_ref[0] = rope(ka, cos2, sin2).astype(BF16)

    vt_ref[0, 0] = (p[:, o:o + VA_W] + vbias_ref[...]).T.astype(BF16); o += VA_W

    ki = p[:, o:o + KI_W]; o += KI_W
    ki = ki * lax.rsqrt(_seg_sum(ki * ki, bd[:KI_W, :KI_W]) * (1.0 / IDX_DIM) + EPS) * gki_ref[...]
    ki_ref[0] = rope(ki, cos1, sin1).astype(BF16)

    wit_ref[0] = (p[:, o:o + WI_W] * (IDX_HEADS ** -0.5)).T[:IDX_HEADS, :]


def _dsa_proj(x, g_mix, w_dsa, vbias, g_qa, g_ka, g_idx_k, cos_t, sin_t, bd, tt):
    B, S, D = x.shape
    nt = S // tt
    const = lambda b, s: (0, 0)
    return pl.pallas_call(
        _dsa_proj_kernel,
        grid=(B, nt),
        in_specs=[
            pl.BlockSpec((1, tt, D), lambda b, s: (b, s, 0)),
            pl.BlockSpec((1, D), const),
            pl.BlockSpec((D, DSA_COLS), const),
            pl.BlockSpec((1, VA_W), const),
            pl.BlockSpec((1, QA_W), const),
            pl.BlockSpec((1, KA_W), const),
            pl.BlockSpec((1, KI_W), const),
            pl.BlockSpec((tt, LANES), lambda b, s: (s, 0)),
            pl.BlockSpec((tt, LANES), lambda b, s: (s, 0)),
            pl.BlockSpec((QA_W, QA_W), const),
        ],
        out_specs=[
            pl.BlockSpec((1, QA_W, tt), lambda b, s: (b, 0, s)),
            pl.BlockSpec((1, QI_W, tt), lambda b, s: (b, 0, s)),
            pl.BlockSpec((1, tt, KA_W), lambda b, s: (b, s, 0)),
            pl.BlockSpec((1, 1, VA_W, tt), lambda b, s: (b, s, 0, 0)),
            pl.BlockSpec((1, tt, KI_W), lambda b, s: (b, s, 0)),
            pl.BlockSpec((1, IDX_HEADS, tt), lambda b, s: (b, 0, s)),
        ],
        out_shape=[
            jax.ShapeDtypeStruct((B, QA_W, S), BF16),
            jax.ShapeDtypeStruct((B, QI_W, S), BF16),
            jax.ShapeDtypeStruct((B, S, KA_W), BF16),
            jax.ShapeDtypeStruct((B, nt, VA_W, tt), BF16),
            jax.ShapeDtypeStruct((B, S, KI_W), BF16),
            jax.ShapeDtypeStruct((B, IDX_HEADS, S), F32),
        ],
        compiler_params=pltpu.CompilerParams(dimension_semantics=("arbitrary", "arbitrary"),
                                             vmem_limit_bytes=VMEM_LIMIT),
        name="dsa_proj",
    )(x, g_mix, w_dsa, vbias, g_qa, g_ka, g_idx_k, cos_t, sin_t, bd)


def _key_of(f):
    bits = lax.bitcast_convert_type(f, jnp.int32)
    return bits ^ ((bits >> 31) & jnp.int32(0x7FFFFFFF))


def _pack_halves(x):
    m = x.shape[0] // 2
    return (x[:m] & 0xFFFF) | (x[m:] << 16)


def _halves_of(words):
    return pltpu.bitcast(words, jnp.int16)


def _pipelined_chunks(nchunks, last_chunk, matmul, finish):
    matmul(0, 0)

    def pair(i, carry):
        c0 = 2 * i
        matmul(jnp.minimum(c0 + 1, last_chunk), 1)
        finish(c0, 0)

        @pl.when(c0 + 1 < nchunks)
        def _():
            matmul(jnp.minimum(c0 + 2, last_chunk), 0)
            finish(c0 + 1, 1)

        return carry

    lax.fori_loop(0, (nchunks + 1) // 2, pair, 0)


def _dsa_attn_kernel(qat_ref, qit_ref, wit_ref, k_ref, vt_ref, ki_ref, o_ref,
                     qa_s, qi_s, keys_s, khi_s, klo_s, zlo_s, buf0, buf1, m_s, acc_s, *, tq, tk, topk, idx_bits):
    qt = pl.program_id(1)
    q0 = qt * tq
    nchunks = q0 // tk + 1
    last_chunk = nchunks - 1
    hpg = N_HEADS_A // N_KV_HEADS_A
    gw = hpg * tq
    half_pairs = hpg // 2
    bufs = (buf0, buf1)
    dim = lax.broadcasted_iota(jnp.int32, (LANES, tq), 0)
    lo_half = dim < HEAD_DIM_A

    for j in range(IDX_HEADS // 2):
        pair = qit_ref[0, j * LANES:(j + 1) * LANES, :]
        qi_s[:, (2 * j) * tq:(2 * j + 1) * tq] = jnp.where(lo_half, pair, jnp.zeros_like(pair))
        qi_s[:, (2 * j + 1) * tq:(2 * j + 2) * tq] = jnp.where(lo_half, jnp.zeros_like(pair), pair)
    for g in range(N_KV_HEADS_A):
        for jj in range(half_pairs):
            pair = qat_ref[0, (g * half_pairs + jj) * LANES:(g * half_pairs + jj + 1) * LANES, :]
            qa_s[:, g * gw + jj * tq:g * gw + (jj + 1) * tq] = jnp.where(lo_half, pair, jnp.zeros_like(pair))
            qa_s[:, g * gw + (half_pairs + jj) * tq:g * gw + (half_pairs + jj + 1) * tq] = (
                jnp.where(lo_half, jnp.zeros_like(pair), pair))

    key_in_chunk = lax.broadcasted_iota(jnp.int32, (tk, tq), 0)
    q_pos = q0 + lax.broadcasted_iota(jnp.int32, (tk, tq), 1)

    def score_chunk(c, diagonal=False):
        r0 = pl.multiple_of(c * tk, tk)
        logits = _dot(ki_ref[0, pl.ds(r0, tk), :], qi_s[...])
        score = jnp.zeros((tk, tq), F32)
        for h in range(IDX_HEADS):
            score = score + wit_ref[0, h:h + 1, :] * jnp.maximum(logits[:, h * tq:(h + 1) * tq], 0.0)
        key = _key_of(score)
        if diagonal:
            key = jnp.where(key_in_chunk + c * tk <= q_pos, key, jnp.int32(INT_MIN))
        keys_s[c] = key
        khi_s[c] = _pack_halves(key >> 16)
        klo_s[c] = _pack_halves((key & 0xFFFF) - HALF_BIAS)

    def score_pair(i, carry):
        score_chunk(2 * i)
        score_chunk(2 * i + 1)
        return carry

    lax.fori_loop(0, last_chunk // 2, score_pair, 0)

    @pl.when(last_chunk % 2 == 1)
    def _():
        score_chunk(last_chunk - 1)

    score_chunk(last_chunk, diagonal=True)

    def count_keys(pred_fn):
        def one(c, acc):
            hit = pred_fn(keys_s[c], c * tk).astype(jnp.int32)
            return acc + jnp.sum(hit.reshape(tk // 8, 8, tq), axis=0)

        def two(i, acc):
            return one(2 * i + 1, one(2 * i, acc))

        part = lax.fori_loop(0, nchunks // 2, two, jnp.zeros((8, tq), jnp.int32))
        part = lax.cond(nchunks % 2 == 1, lambda acc: one(last_chunk, acc), lambda acc: acc, part)
        return jnp.sum(part, axis=0, keepdims=True)

    total = q0 + lax.broadcasted_iota(jnp.int32, (1, tq), 1) + 1
    has_tau = total >= topk

    def count_halves(src_s, cand):
        cand16 = _halves_of(jnp.broadcast_to((cand & 0xFFFF) | (cand << 16), (SUBLANES, tq)))

        def one(c, acc):
            half = _halves_of(src_s[c])
            hits = [jnp.where(half[j * 2 * SUBLANES:(j + 1) * 2 * SUBLANES, :] >= cand16, jnp.int16(1), jnp.int16(0))
                    for j in range(tk // (2 * SUBLANES))]
            while len(hits) > 1:
                hits = [a + b for a, b in zip(hits[::2], hits[1::2])]
            return acc + hits[0]

        def two(i, acc):
            return one(2 * i + 1, one(2 * i, acc))

        acc = lax.fori_loop(0, nchunks // 2, two, jnp.zeros((2 * SUBLANES, tq), jnp.int16))
        acc = lax.cond(nchunks % 2 == 1, lambda a: one(last_chunk, a), lambda a: a, acc)
        words = pltpu.bitcast(acc, jnp.int32)
        return jnp.sum((words & 0xFFFF) + (words >> 16), axis=0, keepdims=True)

    def radix16(src_s, want):
        def step(i, state):
            v, cnt = state
            cand = v + (jnp.int32(1) << (15 - i))
            c = count_halves(src_s, cand)
            up = c >= want
            return jnp.where(up, cand, v), jnp.where(up, c, cnt)
        return lax.fori_loop(0, 16, step, (jnp.full((1, tq), -HALF_BIAS, jnp.int32), jnp.zeros((1, tq), jnp.int32)))

    tau_hi, cnt_hi_ge = radix16(khi_s, topk)
    cnt_hi_gt = jnp.where(tau_hi == HALF_BIAS - 1, 0, count_halves(khi_s, jnp.minimum(tau_hi + 1, HALF_BIAS - 1)))
    tau_hi16 = _halves_of(jnp.broadcast_to((tau_hi & 0xFFFF) | (tau_hi << 16), (SUBLANES, tq)))

    def low_of_bucket(c, carry):
        hi, lo = _halves_of(khi_s[c]), _halves_of(klo_s[c])
        for j in range(tk // (2 * SUBLANES)):
            rows = slice(j * 2 * SUBLANES, (j + 1) * 2 * SUBLANES)
            zlo_s[c, j * SUBLANES:(j + 1) * SUBLANES, :] = pltpu.bitcast(
                jnp.where(hi[rows, :] == tau_hi16, lo[rows, :], jnp.int16(-HALF_BIAS)), jnp.int32)
        return carry

    lax.fori_loop(0, nchunks, low_of_bucket, 0)
    tau_lo, cnt_lo_ge = radix16(zlo_s, topk - cnt_hi_gt)
    tau = (tau_hi << 16) | ((tau_lo + HALF_BIAS) & 0xFFFF)
    cnt = cnt_hi_gt + jnp.where(tau_lo == -HALF_BIAS, cnt_hi_ge - cnt_hi_gt, cnt_lo_ge)
    tau = jnp.where(has_tau, jnp.maximum(tau, jnp.int32(INT_MIN + 1)), jnp.int32(INT_MIN))

    tied = has_tau & (cnt > topk)

    def tie_cut():
        need = topk - count_keys(lambda k, _: k > tau)

        def idx_step(i, j):
            bit = jnp.int32(1) << (idx_bits - 1 - i)
            test = j + bit - 1
            f = count_keys(lambda k, base: (k == tau) & (base + key_in_chunk <= test))
            return jnp.where(f < need, j + bit, j)
        return lax.fori_loop(0, idx_bits, idx_step, jnp.zeros((1, tq), jnp.int32))

    jcut = lax.cond(jnp.max(jnp.where(tied, 1, 0)) > 0, tie_cut, lambda: jnp.zeros((1, tq), jnp.int32))
    jcut = jnp.where(tied, jcut, jnp.where(has_tau, jnp.int32(2 ** 30), jnp.int32(-1)))

    m_s[...] = jnp.full(m_s.shape, NEG_BIG, F32)
    acc_s[...] = jnp.zeros(acc_s.shape, F32)

    def attn_matmul(c, slot):
        r0 = pl.multiple_of(c * tk, tk)
        k = keys_s[c]
        kpos = key_in_chunk + c * tk
        bias = jnp.where(k > tau, 0.0, jnp.where(k == tau, jnp.where(kpos <= jcut, 0.0, NEG_BIG), NEG_BIG))
        bias4 = jnp.concatenate([bias] * hpg, axis=1)
        for g in range(N_KV_HEADS_A):
            bufs[slot][:, g * gw:(g + 1) * gw] = _dot(k_ref[0, pl.ds(r0, tk), g * LANES:(g + 1) * LANES],
                                                      qa_s[:, g * gw:(g + 1) * gw]) + bias4

    def attn_finish(c, slot):
        for g in range(N_KV_HEADS_A):
            m_old = m_s[g]
            m_new = jnp.maximum(m_old, jnp.max(bufs[slot][:, g * gw:(g + 1) * gw], axis=0, keepdims=True))
            p = jnp.exp2(bufs[slot][:, g * gw:(g + 1) * gw] - m_new)
            alpha = jnp.exp2(m_old - m_new)
            pv = _dot(vt_ref[0, c, g * LANES:(g + 1) * LANES, :], p.astype(BF16))
            acc_s[g] = alpha * acc_s[g] + pv
            m_s[g] = m_new

    _pipelined_chunks(nchunks, last_chunk, attn_matmul, attn_finish)

    for g in range(N_KV_HEADS_A):
        for jj in range(half_pairs):
            a_e = acc_s[g, :, jj * tq:(jj + 1) * tq]
            a_o = acc_s[g, :, (half_pairs + jj) * tq:(half_pairs + jj + 1) * tq]
            o_pair = jnp.concatenate([a_e[:HEAD_DIM_A] / a_e[HEAD_DIM_A:], a_o[:HEAD_DIM_A] / a_o[HEAD_DIM_A:]], axis=0)
            j = g * half_pairs + jj
            o_ref[0, :, j * LANES:(j + 1) * LANES] = o_pair.T.astype(BF16)


def _dsa_attn(qat, qit, wit, k, vt, ki, tq, topk):
    B, _, S = qat.shape
    nc, tk = vt.shape[1], vt.shape[3]
    hpg = N_HEADS_A // N_KV_HEADS_A
    idx_bits = max(1, int(np.ceil(np.log2(S))))
    assert IDX_HEADS == N_HEADS_A
    kernel = functools.partial(_dsa_attn_kernel, tq=tq, tk=tk, topk=topk, idx_bits=idx_bits)
    return pl.pallas_call(
        kernel,
        grid=(B, S // tq),
        in_specs=[
            pl.BlockSpec((1, QA_W, tq), lambda b, q: (b, 0, q)),
            pl.BlockSpec((1, QI_W, tq), lambda b, q: (b, 0, q)),
            pl.BlockSpec((1, IDX_HEADS, tq), lambda b, q: (b, 0, q)),
            pl.BlockSpec((1, S, KA_W), lambda b, q: (b, 0, 0)),
            pl.BlockSpec((1, nc, VA_W, tk), lambda b, q: (b, 0, 0, 0)),
            pl.BlockSpec((1, S, KI_W), lambda b, q: (b, 0, 0)),
        ],
        out_specs=pl.BlockSpec((1, tq, QA_W), lambda b, q: (b, q, 0)),
        out_shape=jax.ShapeDtypeStruct((B, S, QA_W), BF16),
        scratch_shapes=[
            pltpu.VMEM((LANES, N_HEADS_A * tq), BF16),
            pltpu.VMEM((LANES, IDX_HEADS * tq), BF16),
            pltpu.VMEM((nc, tk, tq), jnp.int32),
            pltpu.VMEM((nc, tk // 2, tq), jnp.int32),
            pltpu.VMEM((nc, tk // 2, tq), jnp.int32),
            pltpu.VMEM((nc, tk // 2, tq), jnp.int32),
            pltpu.VMEM((tk, IDX_HEADS * tq), F32),
            pltpu.VMEM((tk, IDX_HEADS * tq), F32),
            pltpu.VMEM((N_KV_HEADS_A, 1, hpg * tq), F32),
            pltpu.VMEM((N_KV_HEADS_A, LANES, hpg * tq), F32),
        ],
        compiler_params=pltpu.CompilerParams(dimension_semantics=("arbitrary", "arbitrary"),
                                             vmem_limit_bytes=VMEM_LIMIT),
        name="dsa_attn",
    )(qat, qit, wit, k, vt, ki)


CONV_IN_W = 2 * CONV_CH
QM_W = MEM_HEADS * MEM_HEAD_DIM


def _mixer_kernel(x_ref, oa_ref, g_mix_ref, wc_ref, wq_ref, wg_ref, bg_ref, cw_ref, cb_ref, lng_ref, lnb_ref,
                  gqm_ref, kmt_ref, vm_ref, wa_ref, wb_ref, wm_ref, wo_ref, gffn_ref, wrh_ref, wrl_ref, br_ref,
                  x1_ref, h2_ref, comb_ref, ubuf, wbuf, *, tt):
    D = x_ref.shape[-1]
    x = x_ref[0]
    h = (x * lax.rsqrt(jnp.mean(x * x, axis=-1, keepdims=True) + EPS) * g_mix_ref[...]).astype(BF16)

    cin = _dot(h, wc_ref[...])
    u = cin[:, :CONV_CH] * _sigmoid(cin[:, CONV_CH:])

    @pl.when(pl.program_id(1) == 0)
    def _():
        ubuf[0:CONV_HALO, :] = jnp.zeros((CONV_HALO, CONV_CH), F32)

    ubuf[CONV_HALO:CONV_HALO + tt, :] = u
    y = jnp.zeros((tt, CONV_CH), F32) + cb_ref[...]
    first = CONV_HALO - (CONV_WIDTH - 1)
    for b in range(SUBLANES):
        taps = [j for j in range(CONV_WIDTH) if (first + j) % SUBLANES == b]
        if not taps:
            continue
        base = first + taps[0]
        if b == 0:
            window, w0 = ubuf, base
        else:
            span = taps[-1] - taps[0] + tt
            wbuf[0:span, :] = ubuf[base:base + span, :]
            window, w0 = wbuf, 0
        for j in taps:
            off = w0 + j - taps[0]
            y = y + cw_ref[j:j + 1, :] * window[off:off + tt, :]
    halo = ubuf[tt:tt + CONV_HALO, :]
    ubuf[0:CONV_HALO, :] = halo
    mu = jnp.mean(y, axis=-1, keepdims=True)
    yc = y - mu
    var = jnp.mean(yc * yc, axis=-1, keepdims=True)
    yn = yc * lax.rsqrt(var + EPS) * lng_ref[...] + lnb_ref[...]
    o_b = (yn * _sigmoid(yn)).astype(BF16)

    qm = _dot(h, wq_ref[...])
    o_m = []
    for hh in range(MEM_HEADS):
        q = qm[:, hh * MEM_HEAD_DIM:(hh + 1) * MEM_HEAD_DIM]
        q = q * lax.rsqrt(jnp.mean(q * q, axis=-1, keepdims=True) + EPS) * gqm_ref[...] * (MEM_HEAD_DIM ** -0.5)
        s = _dot(q.astype(BF16), kmt_ref[0, hh])
        p = jnp.exp(s - jnp.max(s, axis=-1, keepdims=True))
        l = jnp.sum(p, axis=-1, keepdims=True)
        o_m.append((_dot(p.astype(BF16), vm_ref[0, hh]) / l).astype(BF16))
    o_m = jnp.concatenate(o_m, axis=-1)

    gates = _sigmoid(_dot(h, wg_ref[...]) + bg_ref[...])
    merged = (gates[:, :D] * _dot(oa_ref[0], wa_ref[...])
              + gates[:, D:2 * D] * _dot(o_b, wb_ref[...])
              + gates[:, 2 * D:] * _dot(o_m, wm_ref[...]))
    x1 = x + _dot(merged.astype(BF16), wo_ref[...])
    x1_ref[0] = x1

    h2 = x1 * lax.rsqrt(jnp.mean(x1 * x1, axis=-1, keepdims=True) + EPS) * gffn_ref[...]
    h2_ref[0] = h2.astype(BF16)
    hi, lo = _split_bf16(h2)
    logits = _dot(hi, wrh_ref[...]) + _dot(hi, wrl_ref[...]) + _dot(lo, wrh_ref[...]) + br_ref[...]
    lane = lax.broadcasted_iota(jnp.int32, logits.shape, 1)
    ninf = jnp.float32(-jnp.inf)
    big = jnp.int32(1 << 20)
    is_g = (lane >= N_EXPERTS) & (lane < N_EXPERTS + N_GROUPS)
    gl = jnp.where(is_g, logits, ninf)
    gmax = jnp.max(gl, axis=-1, keepdims=True)
    gsel = jnp.min(jnp.where(gl == gmax, lane, big), axis=-1, keepdims=True) - N_EXPERTS
    p_sel = 1.0 / jnp.sum(jnp.where(is_g, jnp.exp(gl - gmax), 0.0), axis=-1, keepdims=True)
    in_grp = (lane < N_EXPERTS) & ((lane >> 3) == gsel)
    el = jnp.where(in_grp, logits, ninf)
    v1 = jnp.max(el, axis=-1, keepdims=True)
    i1 = jnp.min(jnp.where(el == v1, lane, big), axis=-1, keepdims=True)
    el2 = jnp.where(lane == i1, ninf, el)
    v2 = jnp.max(el2, axis=-1, keepdims=True)
    i2 = jnp.min(jnp.where(el2 == v2, lane, big), axis=-1, keepdims=True)
    e2 = jnp.exp(v2 - v1)
    den = 1.0 + e2
    comb_ref[0] = jnp.where(lane == i1, (1.0 / den) * p_sel,
                            jnp.where(lane == i2, (e2 / den) * p_sel,
                                      jnp.where(lane == N_EXPERTS, gsel.astype(F32), 0.0)))


def _mixer(x, oa, g_mix, wc, wq, wg, bg, cw, cb, lng, lnb, gqm, kmt, vm, wa, wb, wm, wo, gffn, wrh, wrl, br, tt):
    B, S, D = x.shape
    M = kmt.shape[-1]
    const = lambda b, s: (0, 0)
    tile = lambda w: pl.BlockSpec((1, tt, w), lambda b, s: (b, s, 0))
    full = lambda a: pl.BlockSpec(a.shape, const)
    return pl.pallas_call(
        functools.partial(_mixer_kernel, tt=tt),
        grid=(B, S // tt),
        in_specs=[
            tile(D), tile(QA_W), full(g_mix), full(wc), full(wq), full(wg), full(bg), full(cw), full(cb),
            full(lng), full(lnb), full(gqm),
            pl.BlockSpec((1, MEM_HEADS, MEM_HEAD_DIM, M), lambda b, s: (b, 0, 0, 0)),
            pl.BlockSpec((1, MEM_HEADS, M, MEM_HEAD_DIM), lambda b, s: (b, 0, 0, 0)),
            full(wa), full(wb), full(wm), full(wo), full(gffn), full(wrh), full(wrl), full(br),
        ],
        out_specs=[tile(D), tile(D), tile(LANES)],
        out_shape=[
            jax.ShapeDtypeStruct((B, S, D), F32),
            jax.ShapeDtypeStruct((B, S, D), BF16),
            jax.ShapeDtypeStruct((B, S, LANES), F32),
        ],
        scratch_shapes=[pltpu.VMEM((CONV_HALO + tt, CONV_CH), F32),
                        pltpu.VMEM((CONV_HALO + tt, CONV_CH), F32)],
        compiler_params=pltpu.CompilerParams(dimension_semantics=("arbitrary", "arbitrary"),
                                             vmem_limit_bytes=VMEM_LIMIT),
        name="mixer",
    )(x, oa, g_mix, wc, wq, wg, bg, cw, cb, lng, lnb, gqm, kmt, vm, wa, wb, wm, wo, gffn, wrh, wrl, br)


MOE_BLK = 256
MOE_EXPERTS_PER_STEP = 4


def _moe_kernel(h2_ref, x1_ref, comb_ref, ltri_ref, wup_ref, wdn_ref, o_ref,
                hs_s, combs_s, ys_s, pt_s, seg_s, *, tm, rows):
    n = pl.program_id(1)
    lane = lax.broadcasted_iota(jnp.int32, (tm, LANES), 1)

    @pl.when(n == 0)
    def _():
        comb = comb_ref[...]
        gid = comb[:, N_EXPERTS:N_EXPERTS + 1].astype(jnp.int32)
        member = lane == gid
        onehot = jnp.where(member, 1.0, 0.0)
        before = _dot(ltri_ref[...], onehot.astype(BF16))
        rank = jnp.sum(jnp.where(member, before, 0.0), axis=-1, keepdims=True)
        count = jnp.sum(onehot, axis=0, keepdims=True)
        lane1 = lax.broadcasted_iota(jnp.int32, (1, LANES), 1)
        start_blk = jnp.int32(0)
        start_row = jnp.zeros((1, LANES), F32)
        for g in range(N_GROUPS):
            cnt_g = jnp.sum(jnp.where(lane1 == g, count, 0.0)).astype(jnp.int32)
            nblk_g = (cnt_g + (MOE_BLK - 1)) // MOE_BLK
            seg_s[g] = start_blk
            seg_s[N_GROUPS + g] = nblk_g
            start_row = jnp.where(lane1 == g, (start_blk * MOE_BLK).astype(F32), start_row)
            start_blk = start_blk + nblk_g
        dest = jnp.sum(jnp.where(member, start_row, 0.0), axis=-1, keepdims=True) + rank
        seg_s[2 * N_GROUPS] = start_blk
        dest_i = dest.astype(jnp.int32)
        dest_row = jnp.broadcast_to(dest, (tm, LANES)).T[0:1, :].astype(jnp.int32)
        c_hi, c_lo = _split_bf16(comb)
        col = lax.broadcasted_iota(jnp.int32, (tm, MOE_BLK), 1)
        row = lax.broadcasted_iota(jnp.int32, (MOE_BLK, tm), 0)

        def permute_block(b, carry):
            r0 = pl.multiple_of(b * MOE_BLK, MOE_BLK)
            pt_s[b] = jnp.where(col + r0 == dest_i, 1.0, 0.0).astype(BF16)
            perm = jnp.where(row + r0 == dest_row, 1.0, 0.0).astype(BF16)
            hs_s[pl.ds(r0, MOE_BLK), :] = _dot(perm, h2_ref[...]).astype(BF16)
            combs_s[pl.ds(r0, MOE_BLK), :] = _dot(perm, c_hi) + _dot(perm, c_lo)
            ys_s[pl.ds(r0, MOE_BLK), :] = jnp.zeros((MOE_BLK, ys_s.shape[1]), F32)
            return carry

        lax.fori_loop(0, start_blk, permute_block, 0)

    g = (n * MOE_EXPERTS_PER_STEP) // EXPERTS_PER_GROUP
    first_blk = seg_s[g]
    lane_b = lax.broadcasted_iota(jnp.int32, (MOE_BLK, LANES), 1)

    def expert_block(b, carry):
        r0 = pl.multiple_of((first_blk + b) * MOE_BLK, MOE_BLK)
        hb = hs_s[pl.ds(r0, MOE_BLK), :]
        cb = combs_s[pl.ds(r0, MOE_BLK), :]
        acts = []
        for i in range(MOE_EXPERTS_PER_STEP):
            up = _dot(hb, wup_ref[i])
            a = up[:, :D_FF_EXPERT]
            c = jnp.sum(jnp.where(lane_b == n * MOE_EXPERTS_PER_STEP + i, cb, 0.0), axis=-1, keepdims=True)
            acts.append((a * _sigmoid(a) * up[:, D_FF_EXPERT:] * c).astype(BF16))
        w_dn = wdn_ref[...].reshape(MOE_EXPERTS_PER_STEP * D_FF_EXPERT, wdn_ref.shape[-1])
        ys_s[pl.ds(r0, MOE_BLK), :] += _dot(jnp.concatenate(acts, axis=-1), w_dn)
        return carry

    lax.fori_loop(0, seg_s[N_GROUPS + g], expert_block, 0)

    @pl.when(n == N_EXPERTS // MOE_EXPERTS_PER_STEP - 1)
    def _():
        o_ref[...] = x1_ref[...]

        def unpermute_block(b, carry):
            r0 = pl.multiple_of(b * MOE_BLK, MOE_BLK)
            o_ref[...] += _dot(pt_s[b], ys_s[pl.ds(r0, MOE_BLK), :].astype(BF16))
            return carry

        lax.fori_loop(0, seg_s[2 * N_GROUPS], unpermute_block, 0)


def _moe(h2, x1, comb, w_up, w_down, tm):
    T, D = h2.shape
    rows = tm + N_GROUPS * MOE_BLK
    ltri = jnp.asarray(np.tril(np.ones((tm, tm), np.float32), -1), BF16)
    return pl.pallas_call(
        functools.partial(_moe_kernel, tm=tm, rows=rows),
        grid=(T // tm, N_EXPERTS // MOE_EXPERTS_PER_STEP),
        in_specs=[
            pl.BlockSpec((tm, D), lambda t, n: (t, 0)),
            pl.BlockSpec((tm, D), lambda t, n: (t, 0)),
            pl.BlockSpec((tm, LANES), lambda t, n: (t, 0)),
            pl.BlockSpec((tm, tm), lambda t, n: (0, 0)),
            pl.BlockSpec((MOE_EXPERTS_PER_STEP, D, 2 * D_FF_EXPERT), lambda t, n: (n, 0, 0)),
            pl.BlockSpec((MOE_EXPERTS_PER_STEP, D_FF_EXPERT, D), lambda t, n: (n, 0, 0)),
        ],
        out_specs=pl.BlockSpec((tm, D), lambda t, n: (t, 0)),
        out_shape=jax.ShapeDtypeStruct((T, D), F32),
        scratch_shapes=[
            pltpu.VMEM((rows, D), BF16),
            pltpu.VMEM((rows, LANES), F32),
            pltpu.VMEM((rows, D), F32),
            pltpu.VMEM((rows // MOE_BLK, tm, MOE_BLK), BF16),
            pltpu.SMEM((2 * N_GROUPS + 1,), jnp.int32),
        ],
        compiler_params=pltpu.CompilerParams(dimension_semantics=("arbitrary", "arbitrary"),
                                             vmem_limit_bytes=VMEM_LIMIT),
        name="moe",
    )(h2, x1, comb, ltri, w_up, w_down)


def _pick_tile(n, pref):
    t = min(n, pref)
    assert n % t == 0, (n, t)
    return t


def _rope_tables(S):
    half = HEAD_DIM_A // 2
    inv = ROPE_THETA ** (-jnp.arange(half, dtype=F32) / half)
    ang = jnp.arange(S, dtype=jnp.int32).astype(F32)[:, None] * inv[None, :]
    cos, sin = jnp.cos(ang), jnp.sin(ang)
    cos_t = jnp.concatenate([cos, cos, cos, cos], axis=-1)
    sin_t = jnp.concatenate([-sin, sin, -sin, sin], axis=-1)
    return cos_t, sin_t


def _layer(x, mem, g_mix, w_in, b_gate, g_qa, g_ka, g_idx_k, conv_w, conv_b, ln_g, ln_b,
           g_mem, w_mem_kv, g_qm, g_km, w_br_a, w_br_b, w_br_m, w_o, g_ffn,
           w_rg, b_rg, w_re, b_re, w_up, w_down):
    B, S, D = x.shape
    topk = min(TOPK_MAX, S // 4)
    tk = _pick_tile(S, 512)
    tq = _pick_tile(tk, 128)
    tt = _pick_tile(S, 512)
    tm = _pick_tile(B * S, 1024)

    sizes = (QA_W, N_KV_HEADS_A * HEAD_DIM_A, N_KV_HEADS_A * HEAD_DIM_A, QI_W, IDX_DIM, IDX_HEADS,
             CONV_IN_W, QM_W, N_BRANCHES * D)
    offs = np.concatenate([[0], np.cumsum(sizes)])
    col = lambda i: w_in[:, offs[i]:offs[i + 1]]
    w_qa, w_ka, w_va, w_qi, w_ki, w_wi, w_conv, w_qm, w_gate = (col(i) for i in range(9))
    hd = HEAD_DIM_A
    zeros_hd = jnp.zeros((D, hd), F32)
    ka_dup = [w_ka[:, g * hd:(g + 1) * hd] for g in range(N_KV_HEADS_A) for _ in range(2)]
    va_aug = [w for g in range(N_KV_HEADS_A) for w in (w_va[:, g * hd:(g + 1) * hd], zeros_hd)]
    w_dsa = jnp.concatenate(
        [w_qa, w_qi] + ka_dup + va_aug + [w_ki, w_ki, w_wi, jnp.zeros((D, WI_W - IDX_HEADS), F32)], axis=1).astype(BF16)
    vbias = jnp.tile(jnp.concatenate([jnp.zeros((hd,), F32), jnp.ones((hd,), F32)]), N_KV_HEADS_A).reshape(1, VA_W)
    seg = np.arange(QA_W) // hd
    bd = jnp.asarray(seg[:, None] == seg[None, :], BF16)
    cos_t, sin_t = _rope_tables(S)

    kmt, vm = _mem_kv(mem, g_mem, w_mem_kv, g_km)
    qat, qit, k, vt, ki, wit = _dsa_proj(
        x, g_mix.reshape(1, D), w_dsa, vbias,
        jnp.tile(g_qa, N_HEADS_A).reshape(1, QA_W), jnp.tile(g_ka, KA_W // hd).reshape(1, KA_W),
        jnp.tile(g_idx_k, KI_W // IDX_DIM).reshape(1, KI_W), cos_t, sin_t, bd, tk)
    o_a = _dsa_attn(qat, qit, wit, k, vt, ki, tq, topk)

    w_r = jnp.concatenate([w_re, w_rg, jnp.zeros((D, LANES - N_EXPERTS - N_GROUPS), F32)], axis=1)
    b_r = jnp.concatenate([b_re, b_rg, jnp.zeros((LANES - N_EXPERTS - N_GROUPS,), F32)]).reshape(1, LANES)
    w_r_hi = w_r.astype(BF16)
    w_r_lo = (w_r - w_r_hi.astype(F32)).astype(BF16)
    x1, h2, comb = _mixer(
        x, o_a, g_mix.reshape(1, D), w_conv.astype(BF16), w_qm.astype(BF16), w_gate.astype(BF16),
        b_gate.reshape(1, -1), conv_w.reshape(CONV_WIDTH, CONV_CH), conv_b.reshape(1, CONV_CH),
        ln_g.reshape(1, CONV_CH), ln_b.reshape(1, CONV_CH), g_qm.reshape(1, MEM_HEAD_DIM), kmt, vm,
        w_br_a.astype(BF16), w_br_b.astype(BF16), w_br_m.astype(BF16), w_o.astype(BF16),
        g_ffn.reshape(1, D), w_r_hi, w_r_lo, b_r, tt)

    out = _moe(h2.reshape(B * S, D), x1.reshape(B * S, D), comb.reshape(B * S, LANES),
               w_up.astype(BF16), w_down.astype(BF16), tm)
    return out.reshape(B, S, D)


def kernel(x, mem, g_mix, w_in, b_gate, g_qa, g_ka, g_idx_k, conv_w, conv_b, ln_g, ln_b, g_mem, w_mem_kv, g_qm, g_km,
           w_br_a, w_br_b, w_br_m, w_o, g_ffn, w_rg, b_rg, w_re, b_re, w_up, w_down):
    params = (g_mix, w_in, b_gate, g_qa, g_ka, g_idx_k, conv_w, conv_b, ln_g, ln_b, g_mem, w_mem_kv, g_qm, g_km,
              w_br_a, w_br_b, w_br_m, w_o, g_ffn, w_rg, b_rg, w_re, b_re, w_up, w_down)
    for l in range(g_mix.shape[0]):
        x = _layer(x, mem, *(p[l] for p in params))
    return x
```

```python
import functools

import jax
import jax.numpy as jnp
import numpy as np
from jax import lax
from jax.experimental import pallas as pl
from jax.experimental.pallas import tpu as pltpu

N_HEADS_A = 8
N_KV_HEADS_A = 2
HEAD_DIM_A = 64
IDX_HEADS = 8
IDX_DIM = 64
TOPK_MAX = 256
CONV_CH = 512
CONV_WIDTH = 31
MEM_HEADS = 4
MEM_HEAD_DIM = 128
N_BRANCHES = 3
N_GROUPS = 4
EXPERTS_PER_GROUP = 8
N_EXPERTS = N_GROUPS * EXPERTS_PER_GROUP
D_FF_EXPERT = 256
ROPE_THETA = 10000.0
EPS = 1e-6

LANES = 128
SUBLANES = 8
VMEM_LIMIT = 56 * 1024 * 1024

INT_MIN = -2 ** 31
LOG2_E = 1.4426950408889634
HALF_BIAS = 2 ** 15
NEG_BIG = -1e30
CONV_HALO = 32

BF16 = jnp.bfloat16
F32 = jnp.float32


def _dot(a, b):
    return jnp.dot(a, b, preferred_element_type=F32)


def _sigmoid(x):
    return 0.5 * jnp.tanh(0.5 * x) + 0.5


def _split_bf16(x):
    hi = x.astype(BF16)
    lo = (x - hi.astype(F32)).astype(BF16)
    return hi, lo


def _seg_sum(xsq, ones_bd):
    hi, lo = _split_bf16(xsq)
    return _dot(hi, ones_bd) + _dot(lo, ones_bd)


def _rot_half(x, head_dim):
    n = x.shape[-1]
    half = head_dim // 2
    lane = lax.broadcasted_iota(jnp.int32, x.shape, x.ndim - 1)
    is_lo = (lane & (head_dim - 1)) < half
    return jnp.where(is_lo, pltpu.roll(x, n - half, x.ndim - 1), pltpu.roll(x, half, x.ndim - 1))


def _tile_lanes(x, reps):
    return jnp.concatenate([x] * reps, axis=-1) if reps > 1 else x


def _mem_kv_kernel(mem_ref, g_mem_ref, w_ref, g_km_ref, kt_ref, v_ref):
    m = mem_ref[0]
    hm = m * lax.rsqrt(jnp.mean(m * m, axis=-1, keepdims=True) + EPS) * g_mem_ref[...]
    kv = _dot(hm.astype(BF16), w_ref[...])
    hd = MEM_HEADS * MEM_HEAD_DIM
    for h in range(MEM_HEADS):
        k = kv[:, h * MEM_HEAD_DIM:(h + 1) * MEM_HEAD_DIM]
        k = k * lax.rsqrt(jnp.mean(k * k, axis=-1, keepdims=True) + EPS) * g_km_ref[...]
        kt_ref[0, h] = k.T.astype(BF16)
        v_ref[0, h] = kv[:, hd + h * MEM_HEAD_DIM: hd + (h + 1) * MEM_HEAD_DIM].astype(BF16)


def _mem_kv(mem, g_mem, w_mem_kv, g_km):
    B, M, D = mem.shape
    hd2 = 2 * MEM_HEADS * MEM_HEAD_DIM
    return pl.pallas_call(
        _mem_kv_kernel,
        grid=(B,),
        in_specs=[
            pl.BlockSpec((1, M, D), lambda b: (b, 0, 0)),
            pl.BlockSpec((1, D), lambda b: (0, 0)),
            pl.BlockSpec((D, hd2), lambda b: (0, 0)),
            pl.BlockSpec((1, MEM_HEAD_DIM), lambda b: (0, 0)),
        ],
        out_specs=[
            pl.BlockSpec((1, MEM_HEADS, MEM_HEAD_DIM, M), lambda b: (b, 0, 0, 0)),
            pl.BlockSpec((1, MEM_HEADS, M, MEM_HEAD_DIM), lambda b: (b, 0, 0, 0)),
        ],
        out_shape=[
            jax.ShapeDtypeStruct((B, MEM_HEADS, MEM_HEAD_DIM, M), BF16),
            jax.ShapeDtypeStruct((B, MEM_HEADS, M, MEM_HEAD_DIM), BF16),
        ],
        compiler_params=pltpu.CompilerParams(dimension_semantics=("arbitrary",)),
        name="mem_kv",
    )(mem, g_mem.reshape(1, D), w_mem_kv.astype(BF16), g_km.reshape(1, MEM_HEAD_DIM))


QA_W = N_HEADS_A * HEAD_DIM_A
QI_W = IDX_HEADS * IDX_DIM
KA_W = 2 * N_KV_HEADS_A * HEAD_DIM_A
VA_W = 2 * N_KV_HEADS_A * HEAD_DIM_A
KI_W = 2 * IDX_DIM
WI_W = LANES
DSA_COLS = QA_W + QI_W + KA_W + VA_W + KI_W + WI_W


def _dsa_proj_kernel(x_ref, g_mix_ref, w_ref, vbias_ref, gq_ref, gk_ref, gki_ref, cos_ref, sin_ref, bd_ref,
                     qat_ref, qit_ref, k_ref, vt_ref, ki_ref, wit_ref):
    x = x_ref[0]
    h = x * lax.rsqrt(jnp.mean(x * x, axis=-1, keepdims=True) + EPS) * g_mix_ref[...]
    p = _dot(h.astype(BF16), w_ref[...])
    cos1 = cos_ref[...]
    sin1 = sin_ref[...]
    cos4, sin4 = _tile_lanes(cos1, 4), _tile_lanes(sin1, 4)
    cos2, sin2 = _tile_lanes(cos1, 2), _tile_lanes(sin1, 2)
    bd = bd_ref[...]
    inv_hd = 1.0 / HEAD_DIM_A

    def rope(v, c, s):
        return v * c + _rot_half(v, HEAD_DIM_A) * s

    o = 0
    qa = p[:, o:o + QA_W]; o += QA_W
    qa = qa * lax.rsqrt(_seg_sum(qa * qa, bd) * inv_hd + EPS) * gq_ref[...]
    qat_ref[0] = (rope(qa, cos4, sin4) * (HEAD_DIM_A ** -0.5 * LOG2_E)).T.astype(BF16)

    qi = p[:, o:o + QI_W]; o += QI_W
    qit_ref[0] = (rope(qi, cos4, sin4) * (IDX_DIM ** -0.5)).T.astype(BF16)

    ka = p[:, o:o + KA_W]; o += KA_W
    ka = ka * lax.rsqrt(_seg_sum(ka * ka, bd[:KA_W, :KA_W]) * inv_hd + EPS) * gk_ref[...]
    k_ref[0] = rope(ka, cos2, sin2).astype(BF16)

    vt_ref[0, 0] = (p[:, o:o + VA_W] + vbias_ref[...]).T.astype(BF16); o += VA_W

    ki = p[:, o:o + KI_W]; o += KI_W
    ki = ki * lax.rsqrt(_seg_sum(ki * ki, bd[:KI_W, :KI_W]) * (1.0 / IDX_DIM) + EPS) * gki_ref[...]
    ki_ref[0] = rope(ki, cos1, sin1).astype(BF16)

    wit_ref[0] = (p[:, o:o + WI_W] * (IDX_HEADS ** -0.5)).T[:IDX_HEADS, :]


def _dsa_proj(x, g_mix, w_dsa, vbias, g_qa, g_ka, g_idx_k, cos_t, sin_t, bd, tt):
    B, S, D = x.shape
    nt = S // tt
    const = lambda b, s: (0, 0)
    return pl.pallas_call(
        _dsa_proj_kernel,
        grid=(B, nt),
        in_specs=[
            pl.BlockSpec((1, tt, D), lambda b, s: (b, s, 0)),
            pl.BlockSpec((1, D), const),
            pl.BlockSpec((D, DSA_COLS), const),
            pl.BlockSpec((1, VA_W), const),
            pl.BlockSpec((1, QA_W), const),
            pl.BlockSpec((1, KA_W), const),
            pl.BlockSpec((1, KI_W), const),
            pl.BlockSpec((tt, LANES), lambda b, s: (s, 0)),
            pl.BlockSpec((tt, LANES), lambda b, s: (s, 0)),
            pl.BlockSpec((QA_W, QA_W), const),
        ],
        out_specs=[
            pl.BlockSpec((1, QA_W, tt), lambda b, s: (b, 0, s)),
            pl.BlockSpec((1, QI_W, tt), lambda b, s: (b, 0, s)),
            pl.BlockSpec((1, tt, KA_W), lambda b, s: (b, s, 0)),
            pl.BlockSpec((1, 1, VA_W, tt), lambda b, s: (b, s, 0, 0)),
            pl.BlockSpec((1, tt, KI_W), lambda b, s: (b, s, 0)),
            pl.BlockSpec((1, IDX_HEADS, tt), lambda b, s: (b, 0, s)),
        ],
        out_shape=[
            jax.ShapeDtypeStruct((B, QA_W, S), BF16),
            jax.ShapeDtypeStruct((B, QI_W, S), BF16),
            jax.ShapeDtypeStruct((B, S, KA_W), BF16),
            jax.ShapeDtypeStruct((B, nt, VA_W, tt), BF16),
            jax.ShapeDtypeStruct((B, S, KI_W), BF16),
            jax.ShapeDtypeStruct((B, IDX_HEADS, S), F32),
        ],
        compiler_params=pltpu.CompilerParams(dimension_semantics=("arbitrary", "arbitrary"),
                                             vmem_limit_bytes=VMEM_LIMIT),
        name="dsa_proj",
    )(x, g_mix, w_dsa, vbias, g_qa, g_ka, g_idx_k, cos_t, sin_t, bd)


def _key_of(f):
    bits = lax.bitcast_convert_type(f, jnp.int32)
    return bits ^ ((bits >> 31) & jnp.int32(0x7FFFFFFF))


def _pack_halves(x):
    m = x.shape[0] // 2
    return (x[:m] & 0xFFFF) | (x[m:] << 16)


def _halves_of(words):
    return pltpu.bitcast(words, jnp.int16)


def _pipelined_chunks(nchunks, last_chunk, matmul, finish):
    matmul(0, 0)

    def pair(i, carry):
        c0 = 2 * i
        matmul(jnp.minimum(c0 + 1, last_chunk), 1)
        finish(c0, 0)

        @pl.when(c0 + 1 < nchunks)
        def _():
            matmul(jnp.minimum(c0 + 2, last_chunk), 0)
            finish(c0 + 1, 1)

        return carry

    lax.fori_loop(0, (nchunks + 1) // 2, pair, 0)


def _dsa_attn_kernel(qat_ref, qit_ref, wit_ref, k_ref, vt_ref, ki_ref, o_ref,
                     qa_s, qi_s, keys_s, khi_s, klo_s, zlo_s, buf0, buf1, m_s, acc_s, *, tq, tk, topk, idx_bits):
    qt = pl.program_id(1)
    q0 = qt * tq
    nchunks = q0 // tk + 1
    last_chunk = nchunks - 1
    hpg = N_HEADS_A // N_KV_HEADS_A
    gw = hpg * tq
    half_pairs = hpg // 2
    bufs = (buf0, buf1)
    dim = lax.broadcasted_iota(jnp.int32, (LANES, tq), 0)
    lo_half = dim < HEAD_DIM_A

    for j in range(IDX_HEADS // 2):
        pair = qit_ref[0, j * LANES:(j + 1) * LANES, :]
        qi_s[:, (2 * j) * tq:(2 * j + 1) * tq] = jnp.where(lo_half, pair, jnp.zeros_like(pair))
        qi_s[:, (2 * j + 1) * tq:(2 * j + 2) * tq] = jnp.where(lo_half, jnp.zeros_like(pair), pair)
    for g in range(N_KV_HEADS_A):
        for jj in range(half_pairs):
            pair = qat_ref[0, (g * half_pairs + jj) * LANES:(g * half_pairs + jj + 1) * LANES, :]
            qa_s[:, g * gw + jj * tq:g * gw + (jj + 1) * tq] = jnp.where(lo_half, pair, jnp.zeros_like(pair))
            qa_s[:, g * gw + (half_pairs + jj) * tq:g * gw + (half_pairs + jj + 1) * tq] = (
                jnp.where(lo_half, jnp.zeros_like(pair), pair))

    key_in_chunk = lax.broadcasted_iota(jnp.int32, (tk, tq), 0)
    q_pos = q0 + lax.broadcasted_iota(jnp.int32, (tk, tq), 1)

    def score_chunk(c, diagonal=False):
        r0 = pl.multiple_of(c * tk, tk)
        logits = _dot(ki_ref[0, pl.ds(r0, tk), :], qi_s[...])
        score = jnp.zeros((tk, tq), F32)
        for h in range(IDX_HEADS):
            score = score + wit_ref[0, h:h + 1, :] * jnp.maximum(logits[:, h * tq:(h + 1) * tq], 0.0)
        key = _key_of(score)
        if diagonal:
            key = jnp.where(key_in_chunk + c * tk <= q_pos, key, jnp.int32(INT_MIN))
        keys_s[c] = key
        khi_s[c] = _pack_halves(key >> 16)
        klo_s[c] = _pack_halves((key & 0xFFFF) - HALF_BIAS)

    def score_pair(i, carry):
        score_chunk(2 * i)
        score_chunk(2 * i + 1)
        return carry

    lax.fori_loop(0, last_chunk // 2, score_pair, 0)

    @pl.when(last_chunk % 2 == 1)
    def _():
        score_chunk(last_chunk - 1)

    score_chunk(last_chunk, diagonal=True)

    npairs = (nchunks + 1) // 2
    filler = jnp.full((tk // 2, tq), (-HALF_BIAS << 16) | HALF_BIAS, jnp.int32)

    @pl.when(nchunks % 2 == 1)
    def _():
        khi_s[nchunks] = filler
        klo_s[nchunks] = filler

    def count_keys(pred_fn):
        def one(c, acc):
            hit = pred_fn(keys_s[c], c * tk).astype(jnp.int32)
            return acc + jnp.sum(hit.reshape(tk // 8, 8, tq), axis=0)

        def two(i, acc):
            return one(2 * i + 1, one(2 * i, acc))

        part = lax.fori_loop(0, nchunks // 2, two, jnp.zeros((8, tq), jnp.int32))
        part = lax.cond(nchunks % 2 == 1, lambda acc: one(last_chunk, acc), lambda acc: acc, part)
        return jnp.sum(part, axis=0, keepdims=True)

    total = q0 + lax.broadcasted_iota(jnp.int32, (1, tq), 1) + 1
    has_tau = total >= topk

    def count_halves(src_s, cand):
        cand16 = _halves_of(jnp.broadcast_to((cand & 0xFFFF) | (cand << 16), (SUBLANES, tq)))

        def one(c, acc):
            half = _halves_of(src_s[c])
            hits = [jnp.where(half[j * 2 * SUBLANES:(j + 1) * 2 * SUBLANES, :] >= cand16, jnp.int16(1), jnp.int16(0))
                    for j in range(tk // (2 * SUBLANES))]
            while len(hits) > 1:
                hits = [a + b for a, b in zip(hits[::2], hits[1::2])]
            return acc + hits[0]

        def two(i, acc):
            return one(2 * i + 1, one(2 * i, acc))

        acc = lax.fori_loop(0, npairs, two, jnp.zeros((2 * SUBLANES, tq), jnp.int16))
        words = pltpu.bitcast(acc, jnp.int32)
        return jnp.sum((words & 0xFFFF) + (words >> 16), axis=0, keepdims=True)

    def radix16(src_s, want):
        def step(i, state):
            v, cnt = state
            cand = v + (jnp.int32(1) << (15 - i))
            c = count_halves(src_s, cand)
            up = c >= want
            return jnp.where(up, cand, v), jnp.where(up, c, cnt)
        return lax.fori_loop(0, 16, step, (jnp.full((1, tq), -HALF_BIAS, jnp.int32), jnp.zeros((1, tq), jnp.int32)))

    tau_hi, cnt_hi_ge = radix16(khi_s, topk)
    cnt_hi_gt = jnp.where(tau_hi == HALF_BIAS - 1, 0, count_halves(khi_s, jnp.minimum(tau_hi + 1, HALF_BIAS - 1)))
    tau_hi16 = _halves_of(jnp.broadcast_to((tau_hi & 0xFFFF) | (tau_hi << 16), (SUBLANES, tq)))

    def low_of_bucket(c, carry):
        hi, lo = _halves_of(khi_s[c]), _halves_of(klo_s[c])
        for j in range(tk // (2 * SUBLANES)):
            rows = slice(j * 2 * SUBLANES, (j + 1) * 2 * SUBLANES)
            zlo_s[c, j * SUBLANES:(j + 1) * SUBLANES, :] = pltpu.bitcast(
                jnp.where(hi[rows, :] == tau_hi16, lo[rows, :], jnp.int16(-HALF_BIAS)), jnp.int32)
        return carry

    lax.fori_loop(0, 2 * npairs, low_of_bucket, 0)
    tau_lo, cnt_lo_ge = radix16(zlo_s, topk - cnt_hi_gt)
    tau = (tau_hi << 16) | ((tau_lo + HALF_BIAS) & 0xFFFF)
    cnt = cnt_hi_gt + jnp.where(tau_lo == -HALF_BIAS, cnt_hi_ge - cnt_hi_gt, cnt_lo_ge)
    tau = jnp.where(has_tau, jnp.maximum(tau, jnp.int32(INT_MIN + 1)), jnp.int32(INT_MIN))

    tied = has_tau & (cnt > topk)

    def tie_cut():
        need = topk - count_keys(lambda k, _: k > tau)

        def idx_step(i, j):
            bit = jnp.int32(1) << (idx_bits - 1 - i)
            test = j + bit - 1
            f = count_keys(lambda k, base: (k == tau) & (base + key_in_chunk <= test))
            return jnp.where(f < need, j + bit, j)
        return lax.fori_loop(0, idx_bits, idx_step, jnp.zeros((1, tq), jnp.int32))

    jcut = lax.cond(jnp.max(jnp.where(tied, 1, 0)) > 0, tie_cut, lambda: jnp.zeros((1, tq), jnp.int32))
    jcut = jnp.where(tied, jcut, jnp.where(has_tau, jnp.int32(2 ** 30), jnp.int32(-1)))

    m_s[...] = jnp.full(m_s.shape, NEG_BIG, F32)
    acc_s[...] = jnp.zeros(acc_s.shape, F32)

    def attn_matmul(c, slot):
        r0 = pl.multiple_of(c * tk, tk)
        k = keys_s[c]
        kpos = key_in_chunk + c * tk
        bias = jnp.where(k > tau, 0.0, jnp.where(k == tau, jnp.where(kpos <= jcut, 0.0, NEG_BIG), NEG_BIG))
        bias4 = jnp.concatenate([bias] * hpg, axis=1)
        for g in range(N_KV_HEADS_A):
            bufs[slot][:, g * gw:(g + 1) * gw] = _dot(k_ref[0, pl.ds(r0, tk), g * LANES:(g + 1) * LANES],
                                                      qa_s[:, g * gw:(g + 1) * gw]) + bias4

    def attn_finish(c, slot):
        for g in range(N_KV_HEADS_A):
            m_old = m_s[g]
            m_new = jnp.maximum(m_old, jnp.max(bufs[slot][:, g * gw:(g + 1) * gw], axis=0, keepdims=True))
            p = jnp.exp2(bufs[slot][:, g * gw:(g + 1) * gw] - m_new)
            alpha = jnp.exp2(m_old - m_new)
            pv = _dot(vt_ref[0, c, g * LANES:(g + 1) * LANES, :], p.astype(BF16))
            acc_s[g] = alpha * acc_s[g] + pv
            m_s[g] = m_new

    _pipelined_chunks(nchunks, last_chunk, attn_matmul, attn_finish)

    for g in range(N_KV_HEADS_A):
        for jj in range(half_pairs):
            a_e = acc_s[g, :, jj * tq:(jj + 1) * tq]
            a_o = acc_s[g, :, (half_pairs + jj) * tq:(half_pairs + jj + 1) * tq]
            o_pair = jnp.concatenate([a_e[:HEAD_DIM_A] / a_e[HEAD_DIM_A:], a_o[:HEAD_DIM_A] / a_o[HEAD_DIM_A:]], axis=0)
            j = g * half_pairs + jj
            o_ref[0, :, j * LANES:(j + 1) * LANES] = o_pair.T.astype(BF16)


def _dsa_attn(qat, qit, wit, k, vt, ki, tq, topk):
    B, _, S = qat.shape
    nc, tk = vt.shape[1], vt.shape[3]
    hpg = N_HEADS_A // N_KV_HEADS_A
    idx_bits = max(1, int(np.ceil(np.log2(S))))
    assert IDX_HEADS == N_HEADS_A
    kernel = functools.partial(_dsa_attn_kernel, tq=tq, tk=tk, topk=topk, idx_bits=idx_bits)
    return pl.pallas_call(
        kernel,
        grid=(B, S // tq),
        in_specs=[
            pl.BlockSpec((1, QA_W, tq), lambda b, q: (b, 0, q)),
            pl.BlockSpec((1, QI_W, tq), lambda b, q: (b, 0, q)),
            pl.BlockSpec((1, IDX_HEADS, tq), lambda b, q: (b, 0, q)),
            pl.BlockSpec((1, S, KA_W), lambda b, q: (b, 0, 0)),
            pl.BlockSpec((1, nc, VA_W, tk), lambda b, q: (b, 0, 0, 0)),
            pl.BlockSpec((1, S, KI_W), lambda b, q: (b, 0, 0)),
        ],
        out_specs=pl.BlockSpec((1, tq, QA_W), lambda b, q: (b, q, 0)),
        out_shape=jax.ShapeDtypeStruct((B, S, QA_W), BF16),
        scratch_shapes=[
            pltpu.VMEM((LANES, N_HEADS_A * tq), BF16),
            pltpu.VMEM((LANES, IDX_HEADS * tq), BF16),
            pltpu.VMEM((nc, tk, tq), jnp.int32),
            pltpu.VMEM((nc + nc % 2, tk // 2, tq), jnp.int32),
            pltpu.VMEM((nc + nc % 2, tk // 2, tq), jnp.int32),
            pltpu.VMEM((nc + nc % 2, tk // 2, tq), jnp.int32),
            pltpu.VMEM((tk, IDX_HEADS * tq), F32),
            pltpu.VMEM((tk, IDX_HEADS * tq), F32),
            pltpu.VMEM((N_KV_HEADS_A, 1, hpg * tq), F32),
            pltpu.VMEM((N_KV_HEADS_A, LANES, hpg * tq), F32),
        ],
        compiler_params=pltpu.CompilerParams(dimension_semantics=("arbitrary", "arbitrary"),
                                             vmem_limit_bytes=VMEM_LIMIT),
        name="dsa_attn",
    )(qat, qit, wit, k, vt, ki)


CONV_IN_W = 2 * CONV_CH
QM_W = MEM_HEADS * MEM_HEAD_DIM


def _mixer_kernel(x_ref, oa_ref, g_mix_ref, wc_ref, wq_ref, wg_ref, bg_ref, cw_ref, cb_ref, lng_ref, lnb_ref,
                  gqm_ref, kmt_ref, vm_ref, wa_ref, wb_ref, wm_ref, wo_ref, gffn_ref, wrh_ref, wrl_ref, br_ref,
                  x1_ref, h2_ref, comb_ref, ubuf, wbuf, *, tt):
    D = x_ref.shape[-1]
    x = x_ref[0]
    h = (x * lax.rsqrt(jnp.mean(x * x, axis=-1, keepdims=True) + EPS) * g_mix_ref[...]).astype(BF16)

    cin = _dot(h, wc_ref[...])
    u = cin[:, :CONV_CH] * _sigmoid(cin[:, CONV_CH:])

    @pl.when(pl.program_id(1) == 0)
    def _():
        ubuf[0:CONV_HALO, :] = jnp.zeros((CONV_HALO, CONV_CH), F32)

    ubuf[CONV_HALO:CONV_HALO + tt, :] = u
    y = jnp.zeros((tt, CONV_CH), F32) + cb_ref[...]
    first = CONV_HALO - (CONV_WIDTH - 1)
    for b in range(SUBLANES):
        taps = [j for j in range(CONV_WIDTH) if (first + j) % SUBLANES == b]
        if not taps:
            continue
        base = first + taps[0]
        if b == 0:
            window, w0 = ubuf, base
        else:
            span = taps[-1] - taps[0] + tt
            wbuf[0:span, :] = ubuf[base:base + span, :]
            window, w0 = wbuf, 0
        for j in taps:
            off = w0 + j - taps[0]
            y = y + cw_ref[j:j + 1, :] * window[off:off + tt, :]
    halo = ubuf[tt:tt + CONV_HALO, :]
    ubuf[0:CONV_HALO, :] = halo
    mu = jnp.mean(y, axis=-1, keepdims=True)
    yc = y - mu
    var = jnp.mean(yc * yc, axis=-1, keepdims=True)
    yn = yc * lax.rsqrt(var + EPS) * lng_ref[...] + lnb_ref[...]
    o_b = (yn * _sigmoid(yn)).astype(BF16)

    qm = _dot(h, wq_ref[...])
    o_m = []
    for hh in range(MEM_HEADS):
        q = qm[:, hh * MEM_HEAD_DIM:(hh + 1) * MEM_HEAD_DIM]
        q = q * lax.rsqrt(jnp.mean(q * q, axis=-1, keepdims=True) + EPS) * gqm_ref[...] * (MEM_HEAD_DIM ** -0.5)
        s = _dot(q.astype(BF16), kmt_ref[0, hh])
        p = jnp.exp(s - jnp.max(s, axis=-1, keepdims=True))
        l = jnp.sum(p, axis=-1, keepdims=True)
        o_m.append((_dot(p.astype(BF16), vm_ref[0, hh]) / l).astype(BF16))
    o_m = jnp.concatenate(o_m, axis=-1)

    gates = _sigmoid(_dot(h, wg_ref[...]) + bg_ref[...])
    merged = (gates[:, :D] * _dot(oa_ref[0], wa_ref[...])
              + gates[:, D:2 * D] * _dot(o_b, wb_ref[...])
              + gates[:, 2 * D:] * _dot(o_m, wm_ref[...]))
    x1 = x + _dot(merged.astype(BF16), wo_ref[...])
    x1_ref[0] = x1

    h2 = x1 * lax.rsqrt(jnp.mean(x1 * x1, axis=-1, keepdims=True) + EPS) * gffn_ref[...]
    h2_ref[0] = h2.astype(BF16)
    hi, lo = _split_bf16(h2)
    logits = _dot(hi, wrh_ref[...]) + _dot(hi, wrl_ref[...]) + _dot(lo, wrh_ref[...]) + br_ref[...]
    lane = lax.broadcasted_iota(jnp.int32, logits.shape, 1)
    ninf = jnp.float32(-jnp.inf)
    big = jnp.int32(1 << 20)
    is_g = (lane >= N_EXPERTS) & (lane < N_EXPERTS + N_GROUPS)
    gl = jnp.where(is_g, logits, ninf)
    gmax = jnp.max(gl, axis=-1, keepdims=True)
    gsel = jnp.min(jnp.where(gl == gmax, lane, big), axis=-1, keepdims=True) - N_EXPERTS
    p_sel = 1.0 / jnp.sum(jnp.where(is_g, jnp.exp(gl - gmax), 0.0), axis=-1, keepdims=True)
    in_grp = (lane < N_EXPERTS) & ((lane >> 3) == gsel)
    el = jnp.where(in_grp, logits, ninf)
    v1 = jnp.max(el, axis=-1, keepdims=True)
    i1 = jnp.min(jnp.where(el == v1, lane, big), axis=-1, keepdims=True)
    el2 = jnp.where(lane == i1, ninf, el)
    v2 = jnp.max(el2, axis=-1, keepdims=True)
    i2 = jnp.min(jnp.where(el2 == v2, lane, big), axis=-1, keepdims=True)
    e2 = jnp.exp(v2 - v1)
    den = 1.0 + e2
    comb_ref[0] = jnp.where(lane == i1, (1.0 / den) * p_sel,
                            jnp.where(lane == i2, (e2 / den) * p_sel,
                                      jnp.where(lane == N_EXPERTS, gsel.astype(F32), 0.0)))


def _mixer(x, oa, g_mix, wc, wq, wg, bg, cw, cb, lng, lnb, gqm, kmt, vm, wa, wb, wm, wo, gffn, wrh, wrl, br, tt):
    B, S, D = x.shape
    M = kmt.shape[-1]
    const = lambda b, s: (0, 0)
    tile = lambda w: pl.BlockSpec((1, tt, w), lambda b, s: (b, s, 0))
    full = lambda a: pl.BlockSpec(a.shape, const)
    return pl.pallas_call(
        functools.partial(_mixer_kernel, tt=tt),
        grid=(B, S // tt),
        in_specs=[
            tile(D), tile(QA_W), full(g_mix), full(wc), full(wq), full(wg), full(bg), full(cw), full(cb),
            full(lng), full(lnb), full(gqm),
            pl.BlockSpec((1, MEM_HEADS, MEM_HEAD_DIM, M), lambda b, s: (b, 0, 0, 0)),
            pl.BlockSpec((1, MEM_HEADS, M, MEM_HEAD_DIM), lambda b, s: (b, 0, 0, 0)),
            full(wa), full(wb), full(wm), full(wo), full(gffn), full(wrh), full(wrl), full(br),
        ],
        out_specs=[tile(D), tile(D), tile(LANES)],
        out_shape=[
            jax.ShapeDtypeStruct((B, S, D), F32),
            jax.ShapeDtypeStruct((B, S, D), BF16),
            jax.ShapeDtypeStruct((B, S, LANES), F32),
        ],
        scratch_shapes=[pltpu.VMEM((CONV_HALO + tt, CONV_CH), F32),
                        pltpu.VMEM((CONV_HALO + tt, CONV_CH), F32)],
        compiler_params=pltpu.CompilerParams(dimension_semantics=("arbitrary", "arbitrary"),
                                             vmem_limit_bytes=VMEM_LIMIT),
        name="mixer",
    )(x, oa, g_mix, wc, wq, wg, bg, cw, cb, lng, lnb, gqm, kmt, vm, wa, wb, wm, wo, gffn, wrh, wrl, br)


MOE_BLK = 288
MOE_EXPERTS_PER_STEP = 4


def _moe_kernel(h2_ref, x1_ref, comb_ref, ltri_ref, wup_ref, wdn_ref, o_ref,
                hs_s, combs_s, ys_s, pt_s, seg_s, *, tm, rows):
    n = pl.program_id(1)
    lane = lax.broadcasted_iota(jnp.int32, (tm, LANES), 1)

    @pl.when(n == 0)
    def _():
        comb = comb_ref[...]
        gid = comb[:, N_EXPERTS:N_EXPERTS + 1].astype(jnp.int32)
        member = lane == gid
        onehot = jnp.where(member, 1.0, 0.0)
        before = _dot(ltri_ref[...], onehot.astype(BF16))
        rank = jnp.sum(jnp.where(member, before, 0.0), axis=-1, keepdims=True)
        count = jnp.sum(onehot, axis=0, keepdims=True)
        lane1 = lax.broadcasted_iota(jnp.int32, (1, LANES), 1)
        start_blk = jnp.int32(0)
        start_row = jnp.zeros((1, LANES), F32)
        for g in range(N_GROUPS):
            cnt_g = jnp.sum(jnp.where(lane1 == g, count, 0.0)).astype(jnp.int32)
            nblk_g = (cnt_g + (MOE_BLK - 1)) // MOE_BLK
            seg_s[g] = start_blk
            seg_s[N_GROUPS + g] = nblk_g
            start_row = jnp.where(lane1 == g, (start_blk * MOE_BLK).astype(F32), start_row)
            start_blk = start_blk + nblk_g
        dest = jnp.sum(jnp.where(member, start_row, 0.0), axis=-1, keepdims=True) + rank
        seg_s[2 * N_GROUPS] = start_blk
        dest_i = dest.astype(jnp.int32)
        dest_row = jnp.broadcast_to(dest, (tm, LANES)).T[0:1, :].astype(jnp.int32)
        c_hi, c_lo = _split_bf16(comb)
        col = lax.broadcasted_iota(jnp.int32, (tm, MOE_BLK), 1)
        row = lax.broadcasted_iota(jnp.int32, (MOE_BLK, tm), 0)

        def permute_block(b, carry):
            r0 = pl.multiple_of(b * MOE_BLK, MOE_BLK)
            pt_s[b] = jnp.where(col + r0 == dest_i, 1.0, 0.0).astype(BF16)
            perm = jnp.where(row + r0 == dest_row, 1.0, 0.0).astype(BF16)
            hs_s[pl.ds(r0, MOE_BLK), :] = _dot(perm, h2_ref[...]).astype(BF16)
            combs_s[pl.ds(r0, MOE_BLK), :] = _dot(perm, c_hi) + _dot(perm, c_lo)
            ys_s[pl.ds(r0, MOE_BLK), :] = jnp.zeros((MOE_BLK, ys_s.shape[1]), F32)
            return carry

        lax.fori_loop(0, start_blk, permute_block, 0)

    g = (n * MOE_EXPERTS_PER_STEP) // EXPERTS_PER_GROUP
    first_blk = seg_s[g]
    lane_b = lax.broadcasted_iota(jnp.int32, (MOE_BLK, LANES), 1)

    def expert_block(b, carry):
        r0 = pl.multiple_of((first_blk + b) * MOE_BLK, MOE_BLK)
        hb = hs_s[pl.ds(r0, MOE_BLK), :]
        cb = combs_s[pl.ds(r0, MOE_BLK), :]
        acts = []
        for i in range(MOE_EXPERTS_PER_STEP):
            up = _dot(hb, wup_ref[i])
            a = up[:, :D_FF_EXPERT]
            c = jnp.sum(jnp.where(lane_b == n * MOE_EXPERTS_PER_STEP + i, cb, 0.0), axis=-1, keepdims=True)
            acts.append((a * _sigmoid(a) * up[:, D_FF_EXPERT:] * c).astype(BF16))
        w_dn = wdn_ref[...].reshape(MOE_EXPERTS_PER_STEP * D_FF_EXPERT, wdn_ref.shape[-1])
        ys_s[pl.ds(r0, MOE_BLK), :] += _dot(jnp.concatenate(acts, axis=-1), w_dn)
        return carry

    lax.fori_loop(0, seg_s[N_GROUPS + g], expert_block, 0)

    @pl.when(n == N_EXPERTS // MOE_EXPERTS_PER_STEP - 1)
    def _():
        o_ref[...] = x1_ref[...]

        def unpermute_block(b, carry):
            r0 = pl.multiple_of(b * MOE_BLK, MOE_BLK)
            o_ref[...] += _dot(pt_s[b], ys_s[pl.ds(r0, MOE_BLK), :].astype(BF16))
            return carry

        lax.fori_loop(0, seg_s[2 * N_GROUPS], unpermute_block, 0)


def _moe(h2, x1, comb, w_up, w_down, tm):
    T, D = h2.shape
    rows = tm + N_GROUPS * MOE_BLK
    ltri = jnp.asarray(np.tril(np.ones((tm, tm), np.float32), -1), BF16)
    return pl.pallas_call(
        functools.partial(_moe_kernel, tm=tm, rows=rows),
        grid=(T // tm, N_EXPERTS // MOE_EXPERTS_PER_STEP),
        in_specs=[
            pl.BlockSpec((tm, D), lambda t, n: (t, 0)),
            pl.BlockSpec((tm, D), lambda t, n: (t, 0), pipeline_mode=pl.Buffered(1)),
            pl.BlockSpec((tm, LANES), lambda t, n: (t, 0)),
            pl.BlockSpec((tm, tm), lambda t, n: (0, 0), pipeline_mode=pl.Buffered(1)),
            pl.BlockSpec((MOE_EXPERTS_PER_STEP, D, 2 * D_FF_EXPERT), lambda t, n: (n, 0, 0)),
            pl.BlockSpec((MOE_EXPERTS_PER_STEP, D_FF_EXPERT, D), lambda t, n: (n, 0, 0)),
        ],
        out_specs=pl.BlockSpec((tm, D), lambda t, n: (t, 0)),
        out_shape=jax.ShapeDtypeStruct((T, D), F32),
        scratch_shapes=[
            pltpu.VMEM((rows, D), BF16),
            pltpu.VMEM((rows, LANES), F32),
            pltpu.VMEM((rows, D), F32),
            pltpu.VMEM((rows // MOE_BLK, tm, MOE_BLK), BF16),
            pltpu.SMEM((2 * N_GROUPS + 1,), jnp.int32),
        ],
        compiler_params=pltpu.CompilerParams(dimension_semantics=("arbitrary", "arbitrary"),
                                             vmem_limit_bytes=VMEM_LIMIT),
        name="moe",
    )(h2, x1, comb, ltri, w_up, w_down)


def _pick_tile(n, pref):
    t = min(n, pref)
    assert n % t == 0, (n, t)
    return t


def _rope_tables(S):
    half = HEAD_DIM_A // 2
    inv = ROPE_THETA ** (-jnp.arange(half, dtype=F32) / half)
    ang = jnp.arange(S, dtype=jnp.int32).astype(F32)[:, None] * inv[None, :]
    cos, sin = jnp.cos(ang), jnp.sin(ang)
    cos_t = jnp.concatenate([cos, cos, cos, cos], axis=-1)
    sin_t = jnp.concatenate([-sin, sin, -sin, sin], axis=-1)
    return cos_t, sin_t


def _layer(x, mem, g_mix, w_in, b_gate, g_qa, g_ka, g_idx_k, conv_w, conv_b, ln_g, ln_b,
           g_mem, w_mem_kv, g_qm, g_km, w_br_a, w_br_b, w_br_m, w_o, g_ffn,
           w_rg, b_rg, w_re, b_re, w_up, w_down):
    B, S, D = x.shape
    topk = min(TOPK_MAX, S // 4)
    tk = _pick_tile(S, 512)
    tq = _pick_tile(tk, 128)
    tt = _pick_tile(S, 512)
    tm = _pick_tile(B * S, 1024)

    sizes = (QA_W, N_KV_HEADS_A * HEAD_DIM_A, N_KV_HEADS_A * HEAD_DIM_A, QI_W, IDX_DIM, IDX_HEADS,
             CONV_IN_W, QM_W, N_BRANCHES * D)
    offs = np.concatenate([[0], np.cumsum(sizes)])
    col = lambda i: w_in[:, offs[i]:offs[i + 1]]
    w_qa, w_ka, w_va, w_qi, w_ki, w_wi, w_conv, w_qm, w_gate = (col(i) for i in range(9))
    hd = HEAD_DIM_A
    zeros_hd = jnp.zeros((D, hd), F32)
    ka_dup = [w_ka[:, g * hd:(g + 1) * hd] for g in range(N_KV_HEADS_A) for _ in range(2)]
    va_aug = [w for g in range(N_KV_HEADS_A) for w in (w_va[:, g * hd:(g + 1) * hd], zeros_hd)]
    w_dsa = jnp.concatenate(
        [w_qa, w_qi] + ka_dup + va_aug + [w_ki, w_ki, w_wi, jnp.zeros((D, WI_W - IDX_HEADS), F32)], axis=1).astype(BF16)
    vbias = jnp.tile(jnp.concatenate([jnp.zeros((hd,), F32), jnp.ones((hd,), F32)]), N_KV_HEADS_A).reshape(1, VA_W)
    seg = np.arange(QA_W) // hd
    bd = jnp.asarray(seg[:, None] == seg[None, :], BF16)
    cos_t, sin_t = _rope_tables(S)

    kmt, vm = _mem_kv(mem, g_mem, w_mem_kv, g_km)
    qat, qit, k, vt, ki, wit = _dsa_proj(
        x, g_mix.reshape(1, D), w_dsa, vbias,
        jnp.tile(g_qa, N_HEADS_A).reshape(1, QA_W), jnp.tile(g_ka, KA_W // hd).reshape(1, KA_W),
        jnp.tile(g_idx_k, KI_W // IDX_DIM).reshape(1, KI_W), cos_t, sin_t, bd, tk)
    o_a = _dsa_attn(qat, qit, wit, k, vt, ki, tq, topk)

    w_r = jnp.concatenate([w_re, w_rg, jnp.zeros((D, LANES - N_EXPERTS - N_GROUPS), F32)], axis=1)
    b_r = jnp.concatenate([b_re, b_rg, jnp.zeros((LANES - N_EXPERTS - N_GROUPS,), F32)]).reshape(1, LANES)
    w_r_hi = w_r.astype(BF16)
    w_r_lo = (w_r - w_r_hi.astype(F32)).astype(BF16)
    x1, h2, comb = _mixer(
        x, o_a, g_mix.reshape(1, D), w_conv.astype(BF16), w_qm.astype(BF16), w_gate.astype(BF16),
        b_gate.reshape(1, -1), conv_w.reshape(CONV_WIDTH, CONV_CH), conv_b.reshape(1, CONV_CH),
        ln_g.reshape(1, CONV_CH), ln_b.reshape(1, CONV_CH), g_qm.reshape(1, MEM_HEAD_DIM), kmt, vm,
        w_br_a.astype(BF16), w_br_b.astype(BF16), w_br_m.astype(BF16), w_o.astype(BF16),
        g_ffn.reshape(1, D), w_r_hi, w_r_lo, b_r, tt)

    out = _moe(h2.reshape(B * S, D), x1.reshape(B * S, D), comb.reshape(B * S, LANES),
               w_up.astype(BF16), w_down.astype(BF16), tm)
    return out.reshape(B, S, D)


def kernel(x, mem, g_mix, w_in, b_gate, g_qa, g_ka, g_idx_k, conv_w, conv_b, ln_g, ln_b, g_mem, w_mem_kv, g_qm, g_km,
           w_br_a, w_br_b, w_br_m, w_o, g_ffn, w_rg, b_rg, w_re, b_re, w_up, w_down):
    params = (g_mix, w_in, b_gate, g_qa, g_ka, g_idx_k, conv_w, conv_b, ln_g, ln_b, g_mem, w_mem_kv, g_qm, g_km,
              w_br_a, w_br_b, w_br_m, w_o, g_ffn, w_rg, b_rg, w_re, b_re, w_up, w_down)
    for l in range(g_mix.shape[0]):
        x = _layer(x, mem, *(p[l] for p in params))
    return x
```

```python
import functools

import jax
import jax.numpy as jnp
import numpy as np
from jax import lax
from jax.experimental import pallas as pl
from jax.experimental.pallas import tpu as pltpu

N_HEADS_A = 8
N_KV_HEADS_A = 2
HEAD_DIM_A = 64
IDX_HEADS = 8
IDX_DIM = 64
TOPK_MAX = 256
CONV_CH = 512
CONV_WIDTH = 31
MEM_HEADS = 4
MEM_HEAD_DIM = 128
N_BRANCHES = 3
N_GROUPS = 4
EXPERTS_PER_GROUP = 8
N_EXPERTS = N_GROUPS * EXPERTS_PER_GROUP
D_FF_EXPERT = 256
ROPE_THETA = 10000.0
EPS = 1e-6

LANES = 128
SUBLANES = 8
VMEM_LIMIT = 56 * 1024 * 1024

INT_MIN = -2 ** 31
LOG2_E = 1.4426950408889634
HALF_BIAS = 2 ** 15
HALF_WORD = 2 ** 16
NEG_BIG = -1e30
CONV_HALO = 32

BF16 = jnp.bfloat16
F32 = jnp.float32


def _dot(a, b):
    return jnp.dot(a, b, preferred_element_type=F32)


def _sigmoid(x):
    return 0.5 * jnp.tanh(0.5 * x) + 0.5


def _split_bf16(x):
    hi = x.astype(BF16)
    lo = (x - hi.astype(F32)).astype(BF16)
    return hi, lo


def _seg_sum(xsq, ones_bd):
    hi, lo = _split_bf16(xsq)
    return _dot(hi, ones_bd) + _dot(lo, ones_bd)


def _rot_half(x, head_dim):
    n = x.shape[-1]
    half = head_dim // 2
    lane = lax.broadcasted_iota(jnp.int32, x.shape, x.ndim - 1)
    is_lo = (lane & (head_dim - 1)) < half
    return jnp.where(is_lo, pltpu.roll(x, n - half, x.ndim - 1), pltpu.roll(x, half, x.ndim - 1))


def _tile_lanes(x, reps):
    return jnp.concatenate([x] * reps, axis=-1) if reps > 1 else x


def _mem_kv_kernel(mem_ref, g_mem_ref, w_ref, g_km_ref, kt_ref, v_ref):
    m = mem_ref[0]
    hm = m * lax.rsqrt(jnp.mean(m * m, axis=-1, keepdims=True) + EPS) * g_mem_ref[...]
    kv = _dot(hm.astype(BF16), w_ref[...])
    hd = MEM_HEADS * MEM_HEAD_DIM
    for h in range(MEM_HEADS):
        k = kv[:, h * MEM_HEAD_DIM:(h + 1) * MEM_HEAD_DIM]
        k = k * lax.rsqrt(jnp.mean(k * k, axis=-1, keepdims=True) + EPS) * g_km_ref[...]
        kt_ref[0, h] = k.T.astype(BF16)
        v_ref[0, h] = kv[:, hd + h * MEM_HEAD_DIM: hd + (h + 1) * MEM_HEAD_DIM].astype(BF16)


def _mem_kv(mem, g_mem, w_mem_kv, g_km):
    B, M, D = mem.shape
    hd2 = 2 * MEM_HEADS * MEM_HEAD_DIM
    return pl.pallas_call(
        _mem_kv_kernel,
        grid=(B,),
        in_specs=[
            pl.BlockSpec((1, M, D), lambda b: (b, 0, 0)),
            pl.BlockSpec((1, D), lambda b: (0, 0)),
            pl.BlockSpec((D, hd2), lambda b: (0, 0)),
            pl.BlockSpec((1, MEM_HEAD_DIM), lambda b: (0, 0)),
        ],
        out_specs=[
            pl.BlockSpec((1, MEM_HEADS, MEM_HEAD_DIM, M), lambda b: (b, 0, 0, 0)),
            pl.BlockSpec((1, MEM_HEADS, M, MEM_HEAD_DIM), lambda b: (b, 0, 0, 0)),
        ],
        out_shape=[
            jax.ShapeDtypeStruct((B, MEM_HEADS, MEM_HEAD_DIM, M), BF16),
            jax.ShapeDtypeStruct((B, MEM_HEADS, M, MEM_HEAD_DIM), BF16),
        ],
        compiler_params=pltpu.CompilerParams(dimension_semantics=("arbitrary",)),
        name="mem_kv",
    )(mem, g_mem.reshape(1, D), w_mem_kv.astype(BF16), g_km.reshape(1, MEM_HEAD_DIM))


QA_W = N_HEADS_A * HEAD_DIM_A
QI_W = IDX_HEADS * IDX_DIM
KA_W = 2 * N_KV_HEADS_A * HEAD_DIM_A
VA_W = 2 * N_KV_HEADS_A * HEAD_DIM_A
KI_W = 2 * IDX_DIM
WI_W = LANES
DSA_COLS = QA_W + QI_W + KA_W + VA_W + KI_W + WI_W


def _dsa_proj_kernel(x_ref, g_mix_ref, w_ref, vbias_ref, gq_ref, gk_ref, gki_ref, cos_ref, sin_ref, bd_ref,
                     qat_ref, qit_ref, k_ref, vt_ref, ki_ref, wit_ref):
    x = x_ref[0]
    h = x * lax.rsqrt(jnp.mean(x * x, axis=-1, keepdims=True) + EPS) * g_mix_ref[...]
    p = _dot(h.astype(BF16), w_ref[...])
    cos1 = cos_ref[...]
    sin1 = sin_ref[...]
    cos4, sin4 = _tile_lanes(cos1, 4), _tile_lanes(sin1, 4)
    cos2, sin2 = _tile_lanes(cos1, 2), _tile_lanes(sin1, 2)
    bd = bd_ref[...]
    inv_hd = 1.0 / HEAD_DIM_A

    def rope(v, c, s):
        return v * c + _rot_half(v, HEAD_DIM_A) * s

    o = 0
    qa = p[:, o:o + QA_W]; o += QA_W
    qa = qa * lax.rsqrt(_seg_sum(qa * qa, bd) * inv_hd + EPS) * gq_ref[...]
    qat_ref[0] = (rope(qa, cos4, sin4) * (HEAD_DIM_A ** -0.5 * LOG2_E)).T.astype(BF16)

    qi = p[:, o:o + QI_W]; o += QI_W
    qit_ref[0] = (rope(qi, cos4, sin4) * (IDX_DIM ** -0.5)).T.astype(BF16)

    ka = p[:, o:o + KA_W]; o += KA_W
    ka = ka * lax.rsqrt(_seg_sum(ka * ka, bd[:KA_W, :KA_W]) * inv_hd + EPS) * gk_ref[...]
    k_ref[0] = rope(ka, cos2, sin2).astype(BF16)

    vt_ref[0, 0] = (p[:, o:o + VA_W] + vbias_ref[...]).T.astype(BF16); o += VA_W

    ki = p[:, o:o + KI_W]; o += KI_W
    ki = ki * lax.rsqrt(_seg_sum(ki * ki, bd[:KI_W, :KI_W]) * (1.0 / IDX_DIM) + EPS) * gki_ref[...]
    ki_ref[0] = rope(ki, cos1, sin1).astype(BF16)

    wit_ref[0] = (p[:, o:o + WI_W] * (IDX_HEADS ** -0.5)).T[:IDX_HEADS, :]


def _dsa_proj(x, g_mix, w_dsa, vbias, g_qa, g_ka, g_idx_k, cos_t, sin_t, bd, tt):
    B, S, D = x.shape
    nt = S // tt
    const = lambda b, s: (0, 0)
    return pl.pallas_call(
        _dsa_proj_kernel,
        grid=(B, nt),
        in_specs=[
            pl.BlockSpec((1, tt, D), lambda b, s: (b, s, 0)),
            pl.BlockSpec((1, D), const),
            pl.BlockSpec((D, DSA_COLS), const),
            pl.BlockSpec((1, VA_W), const),
            pl.BlockSpec((1, QA_W), const),
            pl.BlockSpec((1, KA_W), const),
            pl.BlockSpec((1, KI_W), const),
            pl.BlockSpec((tt, LANES), lambda b, s: (s, 0)),
            pl.BlockSpec((tt, LANES), lambda b, s: (s, 0)),
            pl.BlockSpec((QA_W, QA_W), const),
        ],
        out_specs=[
            pl.BlockSpec((1, QA_W, tt), lambda b, s: (b, 0, s)),
            pl.BlockSpec((1, QI_W, tt), lambda b, s: (b, 0, s)),
            pl.BlockSpec((1, tt, KA_W), lambda b, s: (b, s, 0)),
            pl.BlockSpec((1, 1, VA_W, tt), lambda b, s: (b, s, 0, 0)),
            pl.BlockSpec((1, tt, KI_W), lambda b, s: (b, s, 0)),
            pl.BlockSpec((1, IDX_HEADS, tt), lambda b, s: (b, 0, s)),
        ],
        out_shape=[
            jax.ShapeDtypeStruct((B, QA_W, S), BF16),
            jax.ShapeDtypeStruct((B, QI_W, S), BF16),
            jax.ShapeDtypeStruct((B, S, KA_W), BF16),
            jax.ShapeDtypeStruct((B, nt, VA_W, tt), BF16),
            jax.ShapeDtypeStruct((B, S, KI_W), BF16),
            jax.ShapeDtypeStruct((B, IDX_HEADS, S), F32),
        ],
        compiler_params=pltpu.CompilerParams(dimension_semantics=("arbitrary", "arbitrary"),
                                             vmem_limit_bytes=VMEM_LIMIT),
        name="dsa_proj",
    )(x, g_mix, w_dsa, vbias, g_qa, g_ka, g_idx_k, cos_t, sin_t, bd)


def _key_of(f):
    bits = lax.bitcast_convert_type(f, jnp.int32)
    return bits ^ ((bits >> 31) & jnp.int32(0x7FFFFFFF))


def _pack_high_halves(key):
    m = key.shape[0] // 2
    return lax.shift_right_logical(key[:m], 16) | (key[m:] & jnp.int32(-HALF_WORD))


def _pack_low_halves(key):
    m = key.shape[0] // 2
    return ((key[:m] & 0xFFFF) | (key[m:] << 16)) ^ jnp.int32(-HALF_BIAS * HALF_WORD + HALF_BIAS)


def _halves_of(words):
    return pltpu.bitcast(words, jnp.int16)


def _pipelined_chunks(nchunks, last_chunk, matmul, finish):
    matmul(0, 0)

    def pair(i, carry):
        c0 = 2 * i
        matmul(jnp.minimum(c0 + 1, last_chunk), 1)
        finish(c0, 0)

        @pl.when(c0 + 1 < nchunks)
        def _():
            matmul(jnp.minimum(c0 + 2, last_chunk), 0)
            finish(c0 + 1, 1)

        return carry

    lax.fori_loop(0, (nchunks + 1) // 2, pair, 0)


def _dsa_attn_kernel(qat_ref, qit_ref, wit_ref, k_ref, vt_ref, ki_ref, o_ref,
                     qa_s, qi_s, keys_s, khi_s, klo_s, zlo_s, buf0, buf1, m_s, acc_s, *, tq, tk, topk, idx_bits):
    qt = pl.program_id(1)
    q0 = qt * tq
    nchunks = q0 // tk + 1
    last_chunk = nchunks - 1
    hpg = N_HEADS_A // N_KV_HEADS_A
    gw = hpg * tq
    half_pairs = hpg // 2
    bufs = (buf0, buf1)
    dim = lax.broadcasted_iota(jnp.int32, (LANES, tq), 0)
    lo_half = dim < HEAD_DIM_A

    for j in range(IDX_HEADS // 2):
        pair = qit_ref[0, j * LANES:(j + 1) * LANES, :]
        qi_s[:, (2 * j) * tq:(2 * j + 1) * tq] = jnp.where(lo_half, pair, jnp.zeros_like(pair))
        qi_s[:, (2 * j + 1) * tq:(2 * j + 2) * tq] = jnp.where(lo_half, jnp.zeros_like(pair), pair)
    for g in range(N_KV_HEADS_A):
        for jj in range(half_pairs):
            pair = qat_ref[0, (g * half_pairs + jj) * LANES:(g * half_pairs + jj + 1) * LANES, :]
            qa_s[:, g * gw + jj * tq:g * gw + (jj + 1) * tq] = jnp.where(lo_half, pair, jnp.zeros_like(pair))
            qa_s[:, g * gw + (half_pairs + jj) * tq:g * gw + (half_pairs + jj + 1) * tq] = (
                jnp.where(lo_half, jnp.zeros_like(pair), pair))

    key_in_chunk = lax.broadcasted_iota(jnp.int32, (tk, tq), 0)
    q_pos = q0 + lax.broadcasted_iota(jnp.int32, (tk, tq), 1)

    def score_chunk(c, diagonal=False):
        r0 = pl.multiple_of(c * tk, tk)
        logits = _dot(ki_ref[0, pl.ds(r0, tk), :], qi_s[...])
        score = jnp.zeros((tk, tq), F32)
        for h in range(IDX_HEADS):
            score = score + wit_ref[0, h:h + 1, :] * jnp.maximum(logits[:, h * tq:(h + 1) * tq], 0.0)
        key = _key_of(score)
        if diagonal:
            key = jnp.where(key_in_chunk + c * tk <= q_pos, key, jnp.int32(INT_MIN))
        keys_s[c] = key
        khi_s[c] = _pack_high_halves(key)
        klo_s[c] = _pack_low_halves(key)

    def score_pair(i, carry):
        score_chunk(2 * i)
        score_chunk(2 * i + 1)
        return carry

    lax.fori_loop(0, last_chunk // 2, score_pair, 0)

    @pl.when(last_chunk % 2 == 1)
    def _():
        score_chunk(last_chunk - 1)

    score_chunk(last_chunk, diagonal=True)

    def count_keys(pred_fn):
        def one(c, acc):
            hit = pred_fn(keys_s[c], c * tk).astype(jnp.int32)
            return acc + jnp.sum(hit.reshape(tk // 8, 8, tq), axis=0)

        def two(i, acc):
            return one(2 * i + 1, one(2 * i, acc))

        part = lax.fori_loop(0, nchunks // 2, two, jnp.zeros((8, tq), jnp.int32))
        part = lax.cond(nchunks % 2 == 1, lambda acc: one(last_chunk, acc), lambda acc: acc, part)
        return jnp.sum(part, axis=0, keepdims=True)

    total = q0 + lax.broadcasted_iota(jnp.int32, (1, tq), 1) + 1
    has_tau = total >= topk

    def count_halves(src_s, cand):
        cand16 = _halves_of(jnp.broadcast_to((cand & 0xFFFF) | (cand << 16), (SUBLANES, tq)))

        def one(c, acc):
            half = _halves_of(src_s[c])
            hits = [jnp.where(half[j * 2 * SUBLANES:(j + 1) * 2 * SUBLANES, :] >= cand16, jnp.int16(1), jnp.int16(0))
                    for j in range(tk // (2 * SUBLANES))]
            while len(hits) > 1:
                hits = [a + b for a, b in zip(hits[::2], hits[1::2])]
            return acc + hits[0]

        def two(i, acc):
            return one(2 * i + 1, one(2 * i, acc))

        acc = lax.fori_loop(0, nchunks // 2, two, jnp.zeros((2 * SUBLANES, tq), jnp.int16))
        acc = lax.cond(nchunks % 2 == 1, lambda a: one(last_chunk, a), lambda a: a, acc)
        words = pltpu.bitcast(acc, jnp.int32)
        return jnp.sum((words & 0xFFFF) + (words >> 16), axis=0, keepdims=True)

    def radix16(src_s, want):
        def step(i, state):
            v, cnt_ge, cnt_gt = state
            cand = v + (jnp.int32(1) << (15 - i))
            c = count_halves(src_s, cand)
            up = c >= want
            return jnp.where(up, cand, v), jnp.where(up, c, cnt_ge), jnp.where(up, cnt_gt, c)
        zero = jnp.zeros((1, tq), jnp.int32)
        return lax.fori_loop(0, 16, step, (jnp.full((1, tq), -HALF_BIAS, jnp.int32), zero, zero))

    tau_hi, cnt_hi_ge, cnt_hi_gt = radix16(khi_s, topk)
    tau_hi16 = _halves_of(jnp.broadcast_to((tau_hi & 0xFFFF) | (tau_hi << 16), (SUBLANES, tq)))

    def low_of_bucket(c, carry):
        hi, lo = _halves_of(khi_s[c]), _halves_of(klo_s[c])
        for j in range(tk // (2 * SUBLANES)):
            rows = slice(j * 2 * SUBLANES, (j + 1) * 2 * SUBLANES)
            zlo_s[c, j * SUBLANES:(j + 1) * SUBLANES, :] = pltpu.bitcast(
                jnp.where(hi[rows, :] == tau_hi16, lo[rows, :], jnp.int16(-HALF_BIAS)), jnp.int32)
        return carry

    lax.fori_loop(0, nchunks, low_of_bucket, 0)
    tau_lo, cnt_lo_ge, _ = radix16(zlo_s, topk - cnt_hi_gt)
    tau = (tau_hi << 16) | ((tau_lo + HALF_BIAS) & 0xFFFF)
    cnt = cnt_hi_gt + jnp.where(tau_lo == -HALF_BIAS, cnt_hi_ge - cnt_hi_gt, cnt_lo_ge)
    tau = jnp.where(has_tau, jnp.maximum(tau, jnp.int32(INT_MIN + 1)), jnp.int32(INT_MIN))

    tied = has_tau & (cnt > topk)

    def tie_cut():
        need = topk - count_keys(lambda k, _: k > tau)

        def idx_step(i, j):
            bit = jnp.int32(1) << (idx_bits - 1 - i)
            test = j + bit - 1
            f = count_keys(lambda k, base: (k == tau) & (base + key_in_chunk <= test))
            return jnp.where(f < need, j + bit, j)
        return lax.fori_loop(0, idx_bits, idx_step, jnp.zeros((1, tq), jnp.int32))

    jcut = lax.cond(jnp.max(jnp.where(tied, 1, 0)) > 0, tie_cut, lambda: jnp.zeros((1, tq), jnp.int32))
    jcut = jnp.where(tied, jcut, jnp.where(has_tau, jnp.int32(2 ** 30), jnp.int32(-1)))

    m_s[...] = jnp.full(m_s.shape, NEG_BIG, F32)
    acc_s[...] = jnp.zeros(acc_s.shape, F32)

    def attn_matmul(c, slot):
        r0 = pl.multiple_of(c * tk, tk)
        k = keys_s[c]
        kpos = key_in_chunk + c * tk
        bias = jnp.where(k > tau, 0.0, jnp.where(k == tau, jnp.where(kpos <= jcut, 0.0, NEG_BIG), NEG_BIG))
        bias4 = jnp.concatenate([bias] * hpg, axis=1)
        for g in range(N_KV_HEADS_A):
            bufs[slot][:, g * gw:(g + 1) * gw] = _dot(k_ref[0, pl.ds(r0, tk), g * LANES:(g + 1) * LANES],
                                                      qa_s[:, g * gw:(g + 1) * gw]) + bias4

    def attn_finish(c, slot):
        for g in range(N_KV_HEADS_A):
            m_old = m_s[g]
            m_new = jnp.maximum(m_old, jnp.max(bufs[slot][:, g * gw:(g + 1) * gw], axis=0, keepdims=True))
            p = jnp.exp2(bufs[slot][:, g * gw:(g + 1) * gw] - m_new)
            alpha = jnp.exp2(m_old - m_new)
            pv = _dot(vt_ref[0, c, g * LANES:(g + 1) * LANES, :], p.astype(BF16))
            acc_s[g] = alpha * acc_s[g] + pv
            m_s[g] = m_new

    _pipelined_chunks(nchunks, last_chunk, attn_matmul, attn_finish)

    for g in range(N_KV_HEADS_A):
        for jj in range(half_pairs):
            a_e = acc_s[g, :, jj * tq:(jj + 1) * tq]
            a_o = acc_s[g, :, (half_pairs + jj) * tq:(half_pairs + jj + 1) * tq]
            o_pair = jnp.concatenate([a_e[:HEAD_DIM_A] / a_e[HEAD_DIM_A:], a_o[:HEAD_DIM_A] / a_o[HEAD_DIM_A:]], axis=0)
            j = g * half_pairs + jj
            o_ref[0, :, j * LANES:(j + 1) * LANES] = o_pair.T.astype(BF16)


def _dsa_attn(qat, qit, wit, k, vt, ki, tq, topk):
    B, _, S = qat.shape
    nc, tk = vt.shape[1], vt.shape[3]
    hpg = N_HEADS_A // N_KV_HEADS_A
    idx_bits = max(1, int(np.ceil(np.log2(S))))
    assert IDX_HEADS == N_HEADS_A
    kernel = functools.partial(_dsa_attn_kernel, tq=tq, tk=tk, topk=topk, idx_bits=idx_bits)
    return pl.pallas_call(
        kernel,
        grid=(B, S // tq),
        in_specs=[
            pl.BlockSpec((1, QA_W, tq), lambda b, q: (b, 0, q)),
            pl.BlockSpec((1, QI_W, tq), lambda b, q: (b, 0, q)),
            pl.BlockSpec((1, IDX_HEADS, tq), lambda b, q: (b, 0, q)),
            pl.BlockSpec((1, S, KA_W), lambda b, q: (b, 0, 0)),
            pl.BlockSpec((1, nc, VA_W, tk), lambda b, q: (b, 0, 0, 0)),
            pl.BlockSpec((1, S, KI_W), lambda b, q: (b, 0, 0)),
        ],
        out_specs=pl.BlockSpec((1, tq, QA_W), lambda b, q: (b, q, 0)),
        out_shape=jax.ShapeDtypeStruct((B, S, QA_W), BF16),
        scratch_shapes=[
            pltpu.VMEM((LANES, N_HEADS_A * tq), BF16),
            pltpu.VMEM((LANES, IDX_HEADS * tq), BF16),
            pltpu.VMEM((nc, tk, tq), jnp.int32),
            pltpu.VMEM((nc, tk // 2, tq), jnp.int32),
            pltpu.VMEM((nc, tk // 2, tq), jnp.int32),
            pltpu.VMEM((nc, tk // 2, tq), jnp.int32),
            pltpu.VMEM((tk, IDX_HEADS * tq), F32),
            pltpu.VMEM((tk, IDX_HEADS * tq), F32),
            pltpu.VMEM((N_KV_HEADS_A, 1, hpg * tq), F32),
            pltpu.VMEM((N_KV_HEADS_A, LANES, hpg * tq), F32),
        ],
        compiler_params=pltpu.CompilerParams(dimension_semantics=("arbitrary", "arbitrary"),
                                             vmem_limit_bytes=VMEM_LIMIT),
        name="dsa_attn",
    )(qat, qit, wit, k, vt, ki)


CONV_IN_W = 2 * CONV_CH
QM_W = MEM_HEADS * MEM_HEAD_DIM


def _mixer_kernel(x_ref, oa_ref, g_mix_ref, wc_ref, wq_ref, wg_ref, bg_ref, cw_ref, cb_ref, lng_ref, lnb_ref,
                  gqm_ref, kmt_ref, vm_ref, wa_ref, wb_ref, wm_ref, wo_ref, gffn_ref, wrh_ref, wrl_ref, br_ref,
                  x1_ref, h2_ref, comb_ref, ubuf, wbuf, *, tt):
    D = x_ref.shape[-1]
    x = x_ref[0]
    h = (x * lax.rsqrt(jnp.mean(x * x, axis=-1, keepdims=True) + EPS) * g_mix_ref[...]).astype(BF16)

    cin = _dot(h, wc_ref[...])
    u = cin[:, :CONV_CH] * _sigmoid(cin[:, CONV_CH:])

    @pl.when(pl.program_id(1) == 0)
    def _():
        ubuf[0:CONV_HALO, :] = jnp.zeros((CONV_HALO, CONV_CH), F32)

    ubuf[CONV_HALO:CONV_HALO + tt, :] = u
    y = jnp.zeros((tt, CONV_CH), F32) + cb_ref[...]
    first = CONV_HALO - (CONV_WIDTH - 1)
    for b in range(SUBLANES):
        taps = [j for j in range(CONV_WIDTH) if (first + j) % SUBLANES == b]
        if not taps:
            continue
        base = first + taps[0]
        if b == 0:
            window, w0 = ubuf, base
        else:
            span = taps[-1] - taps[0] + tt
            wbuf[0:span, :] = ubuf[base:base + span, :]
            window, w0 = wbuf, 0
        for j in taps:
            off = w0 + j - taps[0]
            y = y + cw_ref[j:j + 1, :] * window[off:off + tt, :]
    halo = ubuf[tt:tt + CONV_HALO, :]
    ubuf[0:CONV_HALO, :] = halo
    mu = jnp.mean(y, axis=-1, keepdims=True)
    yc = y - mu
    var = jnp.mean(yc * yc, axis=-1, keepdims=True)
    yn = yc * lax.rsqrt(var + EPS) * lng_ref[...] + lnb_ref[...]
    o_b = (yn * _sigmoid(yn)).astype(BF16)

    qm = _dot(h, wq_ref[...])
    o_m = []
    for hh in range(MEM_HEADS):
        q = qm[:, hh * MEM_HEAD_DIM:(hh + 1) * MEM_HEAD_DIM]
        q = q * lax.rsqrt(jnp.mean(q * q, axis=-1, keepdims=True) + EPS) * gqm_ref[...] * (MEM_HEAD_DIM ** -0.5)
        s = _dot(q.astype(BF16), kmt_ref[0, hh])
        p = jnp.exp(s - jnp.max(s, axis=-1, keepdims=True))
        l = jnp.sum(p, axis=-1, keepdims=True)
        o_m.append((_dot(p.astype(BF16), vm_ref[0, hh]) / l).astype(BF16))
    o_m = jnp.concatenate(o_m, axis=-1)

    gates = _sigmoid(_dot(h, wg_ref[...]) + bg_ref[...])
    merged = (gates[:, :D] * _dot(oa_ref[0], wa_ref[...])
              + gates[:, D:2 * D] * _dot(o_b, wb_ref[...])
              + gates[:, 2 * D:] * _dot(o_m, wm_ref[...]))
    x1 = x + _dot(merged.astype(BF16), wo_ref[...])
    x1_ref[0] = x1

    h2 = x1 * lax.rsqrt(jnp.mean(x1 * x1, axis=-1, keepdims=True) + EPS) * gffn_ref[...]
    h2_ref[0] = h2.astype(BF16)
    hi, lo = _split_bf16(h2)
    logits = _dot(hi, wrh_ref[...]) + _dot(hi, wrl_ref[...]) + _dot(lo, wrh_ref[...]) + br_ref[...]
    lane = lax.broadcasted_iota(jnp.int32, logits.shape, 1)
    ninf = jnp.float32(-jnp.inf)
    big = jnp.int32(1 << 20)
    is_g = (lane >= N_EXPERTS) & (lane < N_EXPERTS + N_GROUPS)
    gl = jnp.where(is_g, logits, ninf)
    gmax = jnp.max(gl, axis=-1, keepdims=True)
    gsel = jnp.min(jnp.where(gl == gmax, lane, big), axis=-1, keepdims=True) - N_EXPERTS
    p_sel = 1.0 / jnp.sum(jnp.where(is_g, jnp.exp(gl - gmax), 0.0), axis=-1, keepdims=True)
    in_grp = (lane < N_EXPERTS) & ((lane >> 3) == gsel)
    el = jnp.where(in_grp, logits, ninf)
    v1 = jnp.max(el, axis=-1, keepdims=True)
    i1 = jnp.min(jnp.where(el == v1, lane, big), axis=-1, keepdims=True)
    el2 = jnp.where(lane == i1, ninf, el)
    v2 = jnp.max(el2, axis=-1, keepdims=True)
    i2 = jnp.min(jnp.where(el2 == v2, lane, big), axis=-1, keepdims=True)
    e2 = jnp.exp(v2 - v1)
    den = 1.0 + e2
    comb_ref[0] = jnp.where(lane == i1, (1.0 / den) * p_sel,
                            jnp.where(lane == i2, (e2 / den) * p_sel,
                                      jnp.where(lane == N_EXPERTS, gsel.astype(F32), 0.0)))


def _mixer(x, oa, g_mix, wc, wq, wg, bg, cw, cb, lng, lnb, gqm, kmt, vm, wa, wb, wm, wo, gffn, wrh, wrl, br, tt):
    B, S, D = x.shape
    M = kmt.shape[-1]
    const = lambda b, s: (0, 0)
    tile = lambda w: pl.BlockSpec((1, tt, w), lambda b, s: (b, s, 0))
    full = lambda a: pl.BlockSpec(a.shape, const)
    return pl.pallas_call(
        functools.partial(_mixer_kernel, tt=tt),
        grid=(B, S // tt),
        in_specs=[
            tile(D), tile(QA_W), full(g_mix), full(wc), full(wq), full(wg), full(bg), full(cw), full(cb),
            full(lng), full(lnb), full(gqm),
            pl.BlockSpec((1, MEM_HEADS, MEM_HEAD_DIM, M), lambda b, s: (b, 0, 0, 0)),
            pl.BlockSpec((1, MEM_HEADS, M, MEM_HEAD_DIM), lambda b, s: (b, 0, 0, 0)),
            full(wa), full(wb), full(wm), full(wo), full(gffn), full(wrh), full(wrl), full(br),
        ],
        out_specs=[tile(D), tile(D), tile(LANES)],
        out_shape=[
            jax.ShapeDtypeStruct((B, S, D), F32),
            jax.ShapeDtypeStruct((B, S, D), BF16),
            jax.ShapeDtypeStruct((B, S, LANES), F32),
        ],
        scratch_shapes=[pltpu.VMEM((CONV_HALO + tt, CONV_CH), F32),
                        pltpu.VMEM((CONV_HALO + tt, CONV_CH), F32)],
        compiler_params=pltpu.CompilerParams(dimension_semantics=("arbitrary", "arbitrary"),
                                             vmem_limit_bytes=VMEM_LIMIT),
        name="mixer",
    )(x, oa, g_mix, wc, wq, wg, bg, cw, cb, lng, lnb, gqm, kmt, vm, wa, wb, wm, wo, gffn, wrh, wrl, br)


MOE_BLK = 256
MOE_EXPERTS_PER_STEP = 4


def _moe_kernel(h2_ref, x1_ref, comb_ref, ltri_ref, wup_ref, wdn_ref, o_ref,
                hs_s, combs_s, ys_s, pt_s, seg_s, *, tm, rows):
    n = pl.program_id(1)
    lane = lax.broadcasted_iota(jnp.int32, (tm, LANES), 1)

    @pl.when(n == 0)
    def _():
        comb = comb_ref[...]
        gid = comb[:, N_EXPERTS:N_EXPERTS + 1].astype(jnp.int32)
        member = lane == gid
        onehot = jnp.where(member, 1.0, 0.0)
        before = _dot(ltri_ref[...], onehot.astype(BF16))
        rank = jnp.sum(jnp.where(member, before, 0.0), axis=-1, keepdims=True)
        count = jnp.sum(onehot, axis=0, keepdims=True)
        lane1 = lax.broadcasted_iota(jnp.int32, (1, LANES), 1)
        start_blk = jnp.int32(0)
        start_row = jnp.zeros((1, LANES), F32)
        for g in range(N_GROUPS):
            cnt_g = jnp.sum(jnp.where(lane1 == g, count, 0.0)).astype(jnp.int32)
            nblk_g = (cnt_g + (MOE_BLK - 1)) // MOE_BLK
            seg_s[g] = start_blk
            seg_s[N_GROUPS + g] = nblk_g
            start_row = jnp.where(lane1 == g, (start_blk * MOE_BLK).astype(F32), start_row)
            start_blk = start_blk + nblk_g
        dest = jnp.sum(jnp.where(member, start_row, 0.0), axis=-1, keepdims=True) + rank
        seg_s[2 * N_GROUPS] = start_blk
        dest_i = dest.astype(jnp.int32)
        dest_row = jnp.broadcast_to(dest, (tm, LANES)).T[0:1, :].astype(jnp.int32)
        c_hi, c_lo = _split_bf16(comb)
        col = lax.broadcasted_iota(jnp.int32, (tm, MOE_BLK), 1)
        row = lax.broadcasted_iota(jnp.int32, (MOE_BLK, tm), 0)

        def permute_block(b, carry):
            r0 = pl.multiple_of(b * MOE_BLK, MOE_BLK)
            pt_s[b] = jnp.where(col + r0 == dest_i, 1.0, 0.0).astype(BF16)
            perm = jnp.where(row + r0 == dest_row, 1.0, 0.0).astype(BF16)
            hs_s[pl.ds(r0, MOE_BLK), :] = _dot(perm, h2_ref[...]).astype(BF16)
            combs_s[pl.ds(r0, MOE_BLK), :] = _dot(perm, c_hi) + _dot(perm, c_lo)
            ys_s[pl.ds(r0, MOE_BLK), :] = jnp.zeros((MOE_BLK, ys_s.shape[1]), F32)
            return carry

        lax.fori_loop(0, start_blk, permute_block, 0)

    g = (n * MOE_EXPERTS_PER_STEP) // EXPERTS_PER_GROUP
    first_blk = seg_s[g]
    lane_b = lax.broadcasted_iota(jnp.int32, (MOE_BLK, LANES), 1)

    def expert_block(b, carry):
        r0 = pl.multiple_of((first_blk + b) * MOE_BLK, MOE_BLK)
        hb = hs_s[pl.ds(r0, MOE_BLK), :]
        cb = combs_s[pl.ds(r0, MOE_BLK), :]
        acts = []
        for i in range(MOE_EXPERTS_PER_STEP):
            up = _dot(hb, wup_ref[i])
            a = up[:, :D_FF_EXPERT]
            c = jnp.sum(jnp.where(lane_b == n * MOE_EXPERTS_PER_STEP + i, cb, 0.0), axis=-1, keepdims=True)
            acts.append((a * _sigmoid(a) * up[:, D_FF_EXPERT:] * c).astype(BF16))
        w_dn = wdn_ref[...].reshape(MOE_EXPERTS_PER_STEP * D_FF_EXPERT, wdn_ref.shape[-1])
        ys_s[pl.ds(r0, MOE_BLK), :] += _dot(jnp.concatenate(acts, axis=-1), w_dn)
        return carry

    lax.fori_loop(0, seg_s[N_GROUPS + g], expert_block, 0)

    @pl.when(n == N_EXPERTS // MOE_EXPERTS_PER_STEP - 1)
    def _():
        o_ref[...] = x1_ref[...]

        def unpermute_block(b, carry):
            r0 = pl.multiple_of(b * MOE_BLK, MOE_BLK)
            o_ref[...] += _dot(pt_s[b], ys_s[pl.ds(r0, MOE_BLK), :].astype(BF16))
            return carry

        lax.fori_loop(0, seg_s[2 * N_GROUPS], unpermute_block, 0)


def _moe(h2, x1, comb, w_up, w_down, tm):
    T, D = h2.shape
    rows = tm + N_GROUPS * MOE_BLK
    ltri = jnp.asarray(np.tril(np.ones((tm, tm), np.float32), -1), BF16)
    return pl.pallas_call(
        functools.partial(_moe_kernel, tm=tm, rows=rows),
        grid=(T // tm, N_EXPERTS // MOE_EXPERTS_PER_STEP),
        in_specs=[
            pl.BlockSpec((tm, D), lambda t, n: (t, 0)),
            pl.BlockSpec((tm, D), lambda t, n: (t, 0)),
            pl.BlockSpec((tm, LANES), lambda t, n: (t, 0)),
            pl.BlockSpec((tm, tm), lambda t, n: (0, 0)),
            pl.BlockSpec((MOE_EXPERTS_PER_STEP, D, 2 * D_FF_EXPERT), lambda t, n: (n, 0, 0)),
            pl.BlockSpec((MOE_EXPERTS_PER_STEP, D_FF_EXPERT, D), lambda t, n: (n, 0, 0)),
        ],
        out_specs=pl.BlockSpec((tm, D), lambda t, n: (t, 0)),
        out_shape=jax.ShapeDtypeStruct((T, D), F32),
        scratch_shapes=[
            pltpu.VMEM((rows, D), BF16),
            pltpu.VMEM((rows, LANES), F32),
            pltpu.VMEM((rows, D), F32),
            pltpu.VMEM((rows // MOE_BLK, tm, MOE_BLK), BF16),
            pltpu.SMEM((2 * N_GROUPS + 1,), jnp.int32),
        ],
        compiler_params=pltpu.CompilerParams(dimension_semantics=("arbitrary", "arbitrary"),
                                             vmem_limit_bytes=VMEM_LIMIT),
        name="moe",
    )(h2, x1, comb, ltri, w_up, w_down)


def _pick_tile(n, pref):
    t = min(n, pref)
    assert n % t == 0, (n, t)
    return t


def _rope_tables(S):
    half = HEAD_DIM_A // 2
    inv = ROPE_THETA ** (-jnp.arange(half, dtype=F32) / half)
    ang = jnp.arange(S, dtype=jnp.int32).astype(F32)[:, None] * inv[None, :]
    cos, sin = jnp.cos(ang), jnp.sin(ang)
    cos_t = jnp.concatenate([cos, cos, cos, cos], axis=-1)
    sin_t = jnp.concatenate([-sin, sin, -sin, sin], axis=-1)
    return cos_t, sin_t


def _layer(x, mem, g_mix, w_in, b_gate, g_qa, g_ka, g_idx_k, conv_w, conv_b, ln_g, ln_b,
           g_mem, w_mem_kv, g_qm, g_km, w_br_a, w_br_b, w_br_m, w_o, g_ffn,
           w_rg, b_rg, w_re, b_re, w_up, w_down):
    B, S, D = x.shape
    topk = min(TOPK_MAX, S // 4)
    tk = _pick_tile(S, 512)
    tq = _pick_tile(tk, 128)
    tt = _pick_tile(S, 512)
    tm = _pick_tile(B * S, 1024)

    sizes = (QA_W, N_KV_HEADS_A * HEAD_DIM_A, N_KV_HEADS_A * HEAD_DIM_A, QI_W, IDX_DIM, IDX_HEADS,
             CONV_IN_W, QM_W, N_BRANCHES * D)
    offs = np.concatenate([[0], np.cumsum(sizes)])
    col = lambda i: w_in[:, offs[i]:offs[i + 1]]
    w_qa, w_ka, w_va, w_qi, w_ki, w_wi, w_conv, w_qm, w_gate = (col(i) for i in range(9))
    hd = HEAD_DIM_A
    zeros_hd = jnp.zeros((D, hd), F32)
    ka_dup = [w_ka[:, g * hd:(g + 1) * hd] for g in range(N_KV_HEADS_A) for _ in range(2)]
    va_aug = [w for g in range(N_KV_HEADS_A) for w in (w_va[:, g * hd:(g + 1) * hd], zeros_hd)]
    w_dsa = jnp.concatenate(
        [w_qa, w_qi] + ka_dup + va_aug + [w_ki, w_ki, w_wi, jnp.zeros((D, WI_W - IDX_HEADS), F32)], axis=1).astype(BF16)
    vbias = jnp.tile(jnp.concatenate([jnp.zeros((hd,), F32), jnp.ones((hd,), F32)]), N_KV_HEADS_A).reshape(1, VA_W)
    seg = np.arange(QA_W) // hd
    bd = jnp.asarray(seg[:, None] == seg[None, :], BF16)
    cos_t, sin_t = _rope_tables(S)

    kmt, vm = _mem_kv(mem, g_mem, w_mem_kv, g_km)
    qat, qit, k, vt, ki, wit = _dsa_proj(
        x, g_mix.reshape(1, D), w_dsa, vbias,
        jnp.tile(g_qa, N_HEADS_A).reshape(1, QA_W), jnp.tile(g_ka, KA_W // hd).reshape(1, KA_W),
        jnp.tile(g_idx_k, KI_W // IDX_DIM).reshape(1, KI_W), cos_t, sin_t, bd, tk)
    o_a = _dsa_attn(qat, qit, wit, k, vt, ki, tq, topk)

    w_r = jnp.concatenate([w_re, w_rg, jnp.zeros((D, LANES - N_EXPERTS - N_GROUPS), F32)], axis=1)
    b_r = jnp.concatenate([b_re, b_rg, jnp.zeros((LANES - N_EXPERTS - N_GROUPS,), F32)]).reshape(1, LANES)
    w_r_hi = w_r.astype(BF16)
    w_r_lo = (w_r - w_r_hi.astype(F32)).astype(BF16)
    x1, h2, comb = _mixer(
        x, o_a, g_mix.reshape(1, D), w_conv.astype(BF16), w_qm.astype(BF16), w_gate.astype(BF16),
        b_gate.reshape(1, -1), conv_w.reshape(CONV_WIDTH, CONV_CH), conv_b.reshape(1, CONV_CH),
        ln_g.reshape(1, CONV_CH), ln_b.reshape(1, CONV_CH), g_qm.reshape(1, MEM_HEAD_DIM), kmt, vm,
        w_br_a.astype(BF16), w_br_b.astype(BF16), w_br_m.astype(BF16), w_o.astype(BF16),
        g_ffn.reshape(1, D), w_r_hi, w_r_lo, b_r, tt)

    out = _moe(h2.reshape(B * S, D), x1.reshape(B * S, D), comb.reshape(B * S, LANES),
               w_up.astype(BF16), w_down.astype(BF16), tm)
    return out.reshape(B, S, D)


def kernel(x, mem, g_mix, w_in, b_gate, g_qa, g_ka, g_idx_k, conv_w, conv_b, ln_g, ln_b, g_mem, w_mem_kv, g_qm, g_km,
           w_br_a, w_br_b, w_br_m, w_o, g_ffn, w_rg, b_rg, w_re, b_re, w_up, w_down):
    params = (g_mix, w_in, b_gate, g_qa, g_ka, g_idx_k, conv_w, conv_b, ln_g, ln_b, g_mem, w_mem_kv, g_qm, g_km,
              w_br_a, w_br_b, w_br_m, w_o, g_ffn, w_rg, b_rg, w_re, b_re, w_up, w_down)
    for l in range(g_mix.shape[0]):
        x = _layer(x, mem, *(p[l] for p in params))
    return x
```

```python
import functools

import jax
import jax.numpy as jnp
import numpy as np
from jax import lax
from jax.experimental import pallas as pl
from jax.experimental.pallas import tpu as pltpu

N_HEADS_A = 8
N_KV_HEADS_A = 2
HEAD_DIM_A = 64
IDX_HEADS = 8
IDX_DIM = 64
TOPK_MAX = 256
CONV_CH = 512
CONV_WIDTH = 31
MEM_HEADS = 4
MEM_HEAD_DIM = 128
N_BRANCHES = 3
N_GROUPS = 4
EXPERTS_PER_GROUP = 8
N_EXPERTS = N_GROUPS * EXPERTS_PER_GROUP
D_FF_EXPERT = 256
ROPE_THETA = 10000.0
EPS = 1e-6

LANES = 128
SUBLANES = 8
VMEM_LIMIT = 56 * 1024 * 1024

INT_MIN = -2 ** 31
LOG2_E = 1.4426950408889634
HALF_BIAS = 2 ** 15
HALF_WORD = 2 ** 16
NEG_BIG = -1e30
CONV_HALO = 32

BF16 = jnp.bfloat16
F32 = jnp.float32


def _dot(a, b):
    return jnp.dot(a, b, preferred_element_type=F32)


def _sigmoid(x):
    return 0.5 * jnp.tanh(0.5 * x) + 0.5


def _split_bf16(x):
    hi = x.astype(BF16)
    lo = (x - hi.astype(F32)).astype(BF16)
    return hi, lo


def _seg_sum(xsq, ones_bd):
    hi, lo = _split_bf16(xsq)
    return _dot(hi, ones_bd) + _dot(lo, ones_bd)


def _rot_half(x, head_dim):
    n = x.shape[-1]
    half = head_dim // 2
    lane = lax.broadcasted_iota(jnp.int32, x.shape, x.ndim - 1)
    is_lo = (lane & (head_dim - 1)) < half
    return jnp.where(is_lo, pltpu.roll(x, n - half, x.ndim - 1), pltpu.roll(x, half, x.ndim - 1))


def _tile_lanes(x, reps):
    return jnp.concatenate([x] * reps, axis=-1) if reps > 1 else x


def _mem_kv_kernel(mem_ref, g_mem_ref, w_ref, g_km_ref, kt_ref, v_ref):
    m = mem_ref[0]
    hm = m * lax.rsqrt(jnp.mean(m * m, axis=-1, keepdims=True) + EPS) * g_mem_ref[...]
    kv = _dot(hm.astype(BF16), w_ref[...])
    hd = MEM_HEADS * MEM_HEAD_DIM
    for h in range(MEM_HEADS):
        k = kv[:, h * MEM_HEAD_DIM:(h + 1) * MEM_HEAD_DIM]
        k = k * lax.rsqrt(jnp.mean(k * k, axis=-1, keepdims=True) + EPS) * g_km_ref[...]
        kt_ref[0, h] = k.T.astype(BF16)
        v_ref[0, h] = kv[:, hd + h * MEM_HEAD_DIM: hd + (h + 1) * MEM_HEAD_DIM].astype(BF16)


def _mem_kv(mem, g_mem, w_mem_kv, g_km):
    B, M, D = mem.shape
    hd2 = 2 * MEM_HEADS * MEM_HEAD_DIM
    return pl.pallas_call(
        _mem_kv_kernel,
        grid=(B,),
        in_specs=[
            pl.BlockSpec((1, M, D), lambda b: (b, 0, 0)),
            pl.BlockSpec((1, D), lambda b: (0, 0)),
            pl.BlockSpec((D, hd2), lambda b: (0, 0)),
            pl.BlockSpec((1, MEM_HEAD_DIM), lambda b: (0, 0)),
        ],
        out_specs=[
            pl.BlockSpec((1, MEM_HEADS, MEM_HEAD_DIM, M), lambda b: (b, 0, 0, 0)),
            pl.BlockSpec((1, MEM_HEADS, M, MEM_HEAD_DIM), lambda b: (b, 0, 0, 0)),
        ],
        out_shape=[
            jax.ShapeDtypeStruct((B, MEM_HEADS, MEM_HEAD_DIM, M), BF16),
            jax.ShapeDtypeStruct((B, MEM_HEADS, M, MEM_HEAD_DIM), BF16),
        ],
        compiler_params=pltpu.CompilerParams(dimension_semantics=("arbitrary",)),
        name="mem_kv",
    )(mem, g_mem.reshape(1, D), w_mem_kv.astype(BF16), g_km.reshape(1, MEM_HEAD_DIM))


QA_W = N_HEADS_A * HEAD_DIM_A
QI_W = IDX_HEADS * IDX_DIM
KA_W = 2 * N_KV_HEADS_A * HEAD_DIM_A
VA_W = 2 * N_KV_HEADS_A * HEAD_DIM_A
KI_W = 2 * IDX_DIM
WI_W = LANES
DSA_COLS = QA_W + QI_W + KA_W + VA_W + KI_W + WI_W


def _dsa_proj_kernel(x_ref, g_mix_ref, w_ref, vbias_ref, gq_ref, gk_ref, gki_ref, cos_ref, sin_ref, bd_ref,
                     qat_ref, qit_ref, k_ref, vt_ref, ki_ref, wit_ref):
    x = x_ref[0]
    h = x * lax.rsqrt(jnp.mean(x * x, axis=-1, keepdims=True) + EPS) * g_mix_ref[...]
    p = _dot(h.astype(BF16), w_ref[...])
    cos1 = cos_ref[...]
    sin1 = sin_ref[...]
    cos4, sin4 = _tile_lanes(cos1, 4), _tile_lanes(sin1, 4)
    cos2, sin2 = _tile_lanes(cos1, 2), _tile_lanes(sin1, 2)
    bd = bd_ref[...]
    inv_hd = 1.0 / HEAD_DIM_A

    def rope(v, c, s):
        return v * c + _rot_half(v, HEAD_DIM_A) * s

    o = 0
    qa = p[:, o:o + QA_W]; o += QA_W
    qa = qa * lax.rsqrt(_seg_sum(qa * qa, bd) * inv_hd + EPS) * gq_ref[...]
    qat_ref[0] = (rope(qa, cos4, sin4) * (HEAD_DIM_A ** -0.5 * LOG2_E)).T.astype(BF16)

    qi = p[:, o:o + QI_W]; o += QI_W
    qit_ref[0] = (rope(qi, cos4, sin4) * (IDX_DIM ** -0.5)).T.astype(BF16)

    ka = p[:, o:o + KA_W]; o += KA_W
    ka = ka * lax.rsqrt(_seg_sum(ka * ka, bd[:KA_W, :KA_W]) * inv_hd + EPS) * gk_ref[...]
    k_ref[0] = rope(ka, cos2, sin2).astype(BF16)

    vt_ref[0, 0] = (p[:, o:o + VA_W] + vbias_ref[...]).T.astype(BF16); o += VA_W

    ki = p[:, o:o + KI_W]; o += KI_W
    ki = ki * lax.rsqrt(_seg_sum(ki * ki, bd[:KI_W, :KI_W]) * (1.0 / IDX_DIM) + EPS) * gki_ref[...]
    ki_ref[0] = rope(ki, cos1, sin1).astype(BF16)

    wit_ref[0] = (p[:, o:o + WI_W] * (IDX_HEADS ** -0.5)).T[:IDX_HEADS, :]


def _dsa_proj(x, g_mix, w_dsa, vbias, g_qa, g_ka, g_idx_k, cos_t, sin_t, bd, tt):
    B, S, D = x.shape
    nt = S // tt
    const = lambda b, s: (0, 0)
    return pl.pallas_call(
        _dsa_proj_kernel,
        grid=(B, nt),
        in_specs=[
            pl.BlockSpec((1, tt, D), lambda b, s: (b, s, 0)),
            pl.BlockSpec((1, D), const),
            pl.BlockSpec((D, DSA_COLS), const),
            pl.BlockSpec((1, VA_W), const),
            pl.BlockSpec((1, QA_W), const),
            pl.BlockSpec((1, KA_W), const),
            pl.BlockSpec((1, KI_W), const),
            pl.BlockSpec((tt, LANES), lambda b, s: (s, 0)),
            pl.BlockSpec((tt, LANES), lambda b, s: (s, 0)),
            pl.BlockSpec((QA_W, QA_W), const),
        ],
        out_specs=[
            pl.BlockSpec((1, QA_W, tt), lambda b, s: (b, 0, s)),
            pl.BlockSpec((1, QI_W, tt), lambda b, s: (b, 0, s)),
            pl.BlockSpec((1, tt, KA_W), lambda b, s: (b, s, 0)),
            pl.BlockSpec((1, 1, VA_W, tt), lambda b, s: (b, s, 0, 0)),
            pl.BlockSpec((1, tt, KI_W), lambda b, s: (b, s, 0)),
            pl.BlockSpec((1, IDX_HEADS, tt), lambda b, s: (b, 0, s)),
        ],
        out_shape=[
            jax.ShapeDtypeStruct((B, QA_W, S), BF16),
            jax.ShapeDtypeStruct((B, QI_W, S), BF16),
            jax.ShapeDtypeStruct((B, S, KA_W), BF16),
            jax.ShapeDtypeStruct((B, nt, VA_W, tt), BF16),
            jax.ShapeDtypeStruct((B, S, KI_W), BF16),
            jax.ShapeDtypeStruct((B, IDX_HEADS, S), F32),
        ],
        compiler_params=pltpu.CompilerParams(dimension_semantics=("arbitrary", "arbitrary"),
                                             vmem_limit_bytes=VMEM_LIMIT),
        name="dsa_proj",
    )(x, g_mix, w_dsa, vbias, g_qa, g_ka, g_idx_k, cos_t, sin_t, bd)


def _key_of(f):
    bits = lax.bitcast_convert_type(f, jnp.int32)
    return bits ^ ((bits >> 31) & jnp.int32(0x7FFFFFFF))


def _pack_high_halves(key):
    m = key.shape[0] // 2
    return lax.shift_right_logical(key[:m], 16) | (key[m:] & jnp.int32(-HALF_WORD))


def _pack_low_halves(key):
    m = key.shape[0] // 2
    return ((key[:m] & 0xFFFF) | (key[m:] << 16)) ^ jnp.int32(-HALF_BIAS * HALF_WORD + HALF_BIAS)


def _halves_of(words):
    return pltpu.bitcast(words, jnp.int16)


def _pipelined_chunks(nchunks, last_chunk, matmul, finish):
    matmul(0, 0)

    def pair(i, carry):
        c0 = 2 * i
        matmul(c0 + 1, 1)
        finish(c0, 0)
        matmul(c0 + 2, 0)
        finish(c0 + 1, 1)
        return carry

    lax.fori_loop(0, last_chunk // 2, pair, 0)

    @pl.when(last_chunk % 2 == 0)
    def _():
        finish(last_chunk, 0)

    @pl.when(last_chunk % 2 == 1)
    def _():
        matmul(last_chunk, 1)
        finish(last_chunk - 1, 0)
        finish(last_chunk, 1)


def _dsa_attn_kernel(qat_ref, qit_ref, wit_ref, k_ref, vt_ref, ki_ref, o_ref,
                     qa_s, qi_s, keys_s, khi_s, klo_s, zlo_s, buf0, buf1, m_s, acc_s, *, tq, tk, topk, idx_bits):
    qt = pl.program_id(1)
    q0 = qt * tq
    nchunks = q0 // tk + 1
    last_chunk = nchunks - 1
    hpg = N_HEADS_A // N_KV_HEADS_A
    gw = hpg * tq
    half_pairs = hpg // 2
    bufs = (buf0, buf1)
    dim = lax.broadcasted_iota(jnp.int32, (LANES, tq), 0)
    lo_half = dim < HEAD_DIM_A

    for j in range(IDX_HEADS // 2):
        pair = qit_ref[0, j * LANES:(j + 1) * LANES, :]
        qi_s[:, (2 * j) * tq:(2 * j + 1) * tq] = jnp.where(lo_half, pair, jnp.zeros_like(pair))
        qi_s[:, (2 * j + 1) * tq:(2 * j + 2) * tq] = jnp.where(lo_half, jnp.zeros_like(pair), pair)
    for g in range(N_KV_HEADS_A):
        for jj in range(half_pairs):
            pair = qat_ref[0, (g * half_pairs + jj) * LANES:(g * half_pairs + jj + 1) * LANES, :]
            qa_s[:, g * gw + jj * tq:g * gw + (jj + 1) * tq] = jnp.where(lo_half, pair, jnp.zeros_like(pair))
            qa_s[:, g * gw + (half_pairs + jj) * tq:g * gw + (half_pairs + jj + 1) * tq] = (
                jnp.where(lo_half, jnp.zeros_like(pair), pair))

    key_in_chunk = lax.broadcasted_iota(jnp.int32, (tk, tq), 0)
    q_pos = q0 + lax.broadcasted_iota(jnp.int32, (tk, tq), 1)

    def score_chunk(c, diagonal=False):
        r0 = pl.multiple_of(c * tk, tk)
        logits = _dot(ki_ref[0, pl.ds(r0, tk), :], qi_s[...])
        score = jnp.zeros((tk, tq), F32)
        for h in range(IDX_HEADS):
            score = score + wit_ref[0, h:h + 1, :] * jnp.maximum(logits[:, h * tq:(h + 1) * tq], 0.0)
        key = _key_of(score)
        if diagonal:
            key = jnp.where(key_in_chunk + c * tk <= q_pos, key, jnp.int32(INT_MIN))
        keys_s[c] = key
        khi_s[c] = _pack_high_halves(key)
        klo_s[c] = _pack_low_halves(key)

    def score_pair(i, carry):
        score_chunk(2 * i)
        score_chunk(2 * i + 1)
        return carry

    lax.fori_loop(0, last_chunk // 2, score_pair, 0)

    @pl.when(last_chunk % 2 == 1)
    def _():
        score_chunk(last_chunk - 1)

    score_chunk(last_chunk, diagonal=True)

    def count_keys(pred_fn):
        def one(c, acc):
            hit = pred_fn(keys_s[c], c * tk).astype(jnp.int32)
            return acc + jnp.sum(hit.reshape(tk // 8, 8, tq), axis=0)

        def two(i, acc):
            return one(2 * i + 1, one(2 * i, acc))

        part = lax.fori_loop(0, nchunks // 2, two, jnp.zeros((8, tq), jnp.int32))
        part = lax.cond(nchunks % 2 == 1, lambda acc: one(last_chunk, acc), lambda acc: acc, part)
        return jnp.sum(part, axis=0, keepdims=True)

    total = q0 + lax.broadcasted_iota(jnp.int32, (1, tq), 1) + 1
    has_tau = total >= topk

    def count_halves(src_s, cand, odd):
        cand16 = _halves_of(jnp.broadcast_to((cand & 0xFFFF) | (cand << 16), (SUBLANES, tq)))

        def one(c, acc):
            half = _halves_of(src_s[c])
            hits = [jnp.where(half[j * 2 * SUBLANES:(j + 1) * 2 * SUBLANES, :] >= cand16, jnp.int16(1), jnp.int16(0))
                    for j in range(tk // (2 * SUBLANES))]
            while len(hits) > 1:
                hits = [a + b for a, b in zip(hits[::2], hits[1::2])]
            return acc + hits[0]

        def two(i, acc):
            return one(2 * i + 1, one(2 * i, acc))

        acc = lax.fori_loop(0, nchunks // 2, two, jnp.zeros((2 * SUBLANES, tq), jnp.int16))
        if odd:
            acc = one(last_chunk, acc)
        words = pltpu.bitcast(acc, jnp.int32)
        return jnp.sum((words & 0xFFFF) + (words >> 16), axis=0, keepdims=True)

    def radix16(src_s, want, odd):
        def step(i, state):
            v, cnt_ge, cnt_gt = state
            cand = v + (jnp.int32(1) << (15 - i))
            c = count_halves(src_s, cand, odd)
            up = c >= want
            return jnp.where(up, cand, v), jnp.where(up, c, cnt_ge), jnp.where(up, cnt_gt, c)
        zero = jnp.zeros((1, tq), jnp.int32)
        return lax.fori_loop(0, 16, step, (jnp.full((1, tq), -HALF_BIAS, jnp.int32), zero, zero))

    def select(odd):
        tau_hi, cnt_hi_ge, cnt_hi_gt = radix16(khi_s, topk, odd)
        tau_hi16 = _halves_of(jnp.broadcast_to((tau_hi & 0xFFFF) | (tau_hi << 16), (SUBLANES, tq)))

        def low_of_bucket(c, carry):
            hi, lo = _halves_of(khi_s[c]), _halves_of(klo_s[c])
            for j in range(tk // (2 * SUBLANES)):
                rows = slice(j * 2 * SUBLANES, (j + 1) * 2 * SUBLANES)
                zlo_s[c, j * SUBLANES:(j + 1) * SUBLANES, :] = pltpu.bitcast(
                    jnp.where(hi[rows, :] == tau_hi16, lo[rows, :], jnp.int16(-HALF_BIAS)), jnp.int32)
            return carry

        lax.fori_loop(0, nchunks, low_of_bucket, 0)
        tau_lo, cnt_lo_ge, _ = radix16(zlo_s, topk - cnt_hi_gt, odd)
        return tau_hi, cnt_hi_ge, cnt_hi_gt, tau_lo, cnt_lo_ge

    tau_hi, cnt_hi_ge, cnt_hi_gt, tau_lo, cnt_lo_ge = lax.cond(
        nchunks % 2 == 1, lambda: select(True), lambda: select(False))
    tau = (tau_hi << 16) | ((tau_lo + HALF_BIAS) & 0xFFFF)
    cnt = cnt_hi_gt + jnp.where(tau_lo == -HALF_BIAS, cnt_hi_ge - cnt_hi_gt, cnt_lo_ge)
    tau = jnp.where(has_tau, jnp.maximum(tau, jnp.int32(INT_MIN + 1)), jnp.int32(INT_MIN))

    tied = has_tau & (cnt > topk)

    def tie_cut():
        need = topk - count_keys(lambda k, _: k > tau)

        def idx_step(i, j):
            bit = jnp.int32(1) << (idx_bits - 1 - i)
            test = j + bit - 1
            f = count_keys(lambda k, base: (k == tau) & (base + key_in_chunk <= test))
            return jnp.where(f < need, j + bit, j)
        return lax.fori_loop(0, idx_bits, idx_step, jnp.zeros((1, tq), jnp.int32))

    jcut = lax.cond(jnp.max(jnp.where(tied, 1, 0)) > 0, tie_cut, lambda: jnp.zeros((1, tq), jnp.int32))
    jcut = jnp.where(tied, jcut, jnp.where(has_tau, jnp.int32(2 ** 30), jnp.int32(-1)))

    m_s[...] = jnp.full(m_s.shape, NEG_BIG, F32)
    acc_s[...] = jnp.zeros(acc_s.shape, F32)

    def attn_matmul(c, slot):
        r0 = pl.multiple_of(c * tk, tk)
        k = keys_s[c]
        kpos = key_in_chunk + c * tk
        bias = jnp.where(k > tau, 0.0, jnp.where(k == tau, jnp.where(kpos <= jcut, 0.0, NEG_BIG), NEG_BIG))
        bias4 = jnp.concatenate([bias] * hpg, axis=1)
        for g in range(N_KV_HEADS_A):
            bufs[slot][:, g * gw:(g + 1) * gw] = _dot(k_ref[0, pl.ds(r0, tk), g * LANES:(g + 1) * LANES],
                                                      qa_s[:, g * gw:(g + 1) * gw]) + bias4

    def attn_finish(c, slot):
        for g in range(N_KV_HEADS_A):
            m_old = m_s[g]
            m_new = jnp.maximum(m_old, jnp.max(bufs[slot][:, g * gw:(g + 1) * gw], axis=0, keepdims=True))
            p = jnp.exp2(bufs[slot][:, g * gw:(g + 1) * gw] - m_new)
            alpha = jnp.exp2(m_old - m_new)
            pv = _dot(vt_ref[0, c, g * LANES:(g + 1) * LANES, :], p.astype(BF16))
            acc_s[g] = alpha * acc_s[g] + pv
            m_s[g] = m_new

    _pipelined_chunks(nchunks, last_chunk, attn_matmul, attn_finish)

    for g in range(N_KV_HEADS_A):
        for jj in range(half_pairs):
            a_e = acc_s[g, :, jj * tq:(jj + 1) * tq]
            a_o = acc_s[g, :, (half_pairs + jj) * tq:(half_pairs + jj + 1) * tq]
            o_pair = jnp.concatenate([a_e[:HEAD_DIM_A] / a_e[HEAD_DIM_A:], a_o[:HEAD_DIM_A] / a_o[HEAD_DIM_A:]], axis=0)
            j = g * half_pairs + jj
            o_ref[0, :, j * LANES:(j + 1) * LANES] = o_pair.T.astype(BF16)


def _dsa_attn(qat, qit, wit, k, vt, ki, tq, topk):
    B, _, S = qat.shape
    nc, tk = vt.shape[1], vt.shape[3]
    hpg = N_HEADS_A // N_KV_HEADS_A
    idx_bits = max(1, int(np.ceil(np.log2(S))))
    assert IDX_HEADS == N_HEADS_A
    kernel = functools.partial(_dsa_attn_kernel, tq=tq, tk=tk, topk=topk, idx_bits=idx_bits)
    return pl.pallas_call(
        kernel,
        grid=(B, S // tq),
        in_specs=[
            pl.BlockSpec((1, QA_W, tq), lambda b, q: (b, 0, q)),
            pl.BlockSpec((1, QI_W, tq), lambda b, q: (b, 0, q)),
            pl.BlockSpec((1, IDX_HEADS, tq), lambda b, q: (b, 0, q)),
            pl.BlockSpec((1, S, KA_W), lambda b, q: (b, 0, 0)),
            pl.BlockSpec((1, nc, VA_W, tk), lambda b, q: (b, 0, 0, 0)),
            pl.BlockSpec((1, S, KI_W), lambda b, q: (b, 0, 0)),
        ],
        out_specs=pl.BlockSpec((1, tq, QA_W), lambda b, q: (b, q, 0)),
        out_shape=jax.ShapeDtypeStruct((B, S, QA_W), BF16),
        scratch_shapes=[
            pltpu.VMEM((LANES, N_HEADS_A * tq), BF16),
            pltpu.VMEM((LANES, IDX_HEADS * tq), BF16),
            pltpu.VMEM((nc, tk, tq), jnp.int32),
            pltpu.VMEM((nc, tk // 2, tq), jnp.int32),
            pltpu.VMEM((nc, tk // 2, tq), jnp.int32),
            pltpu.VMEM((nc, tk // 2, tq), jnp.int32),
            pltpu.VMEM((tk, IDX_HEADS * tq), F32),
            pltpu.VMEM((tk, IDX_HEADS * tq), F32),
            pltpu.VMEM((N_KV_HEADS_A, 1, hpg * tq), F32),
            pltpu.VMEM((N_KV_HEADS_A, LANES, hpg * tq), F32),
        ],
        compiler_params=pltpu.CompilerParams(dimension_semantics=("arbitrary", "arbitrary"),
                                             vmem_limit_bytes=VMEM_LIMIT),
        name="dsa_attn",
    )(qat, qit, wit, k, vt, ki)


CONV_IN_W = 2 * CONV_CH
QM_W = MEM_HEADS * MEM_HEAD_DIM


def _mixer_kernel(x_ref, oa_ref, g_mix_ref, wc_ref, wq_ref, wg_ref, bg_ref, cw_ref, cb_ref, lng_ref, lnb_ref,
                  gqm_ref, kmt_ref, vm_ref, wa_ref, wb_ref, wm_ref, wo_ref, gffn_ref, wrh_ref, wrl_ref, br_ref,
                  x1_ref, h2_ref, comb_ref, ubuf, wbuf, *, tt):
    D = x_ref.shape[-1]
    x = x_ref[0]
    h = (x * lax.rsqrt(jnp.mean(x * x, axis=-1, keepdims=True) + EPS) * g_mix_ref[...]).astype(BF16)

    cin = _dot(h, wc_ref[...])
    u = cin[:, :CONV_CH] * _sigmoid(cin[:, CONV_CH:])

    @pl.when(pl.program_id(1) == 0)
    def _():
        ubuf[0:CONV_HALO, :] = jnp.zeros((CONV_HALO, CONV_CH), F32)

    ubuf[CONV_HALO:CONV_HALO + tt, :] = u
    y = jnp.zeros((tt, CONV_CH), F32) + cb_ref[...]
    first = CONV_HALO - (CONV_WIDTH - 1)
    for b in range(SUBLANES):
        taps = [j for j in range(CONV_WIDTH) if (first + j) % SUBLANES == b]
        if not taps:
            continue
        base = first + taps[0]
        if b == 0:
            window, w0 = ubuf, base
        else:
            span = taps[-1] - taps[0] + tt
            wbuf[0:span, :] = ubuf[base:base + span, :]
            window, w0 = wbuf, 0
        for j in taps:
            off = w0 + j - taps[0]
            y = y + cw_ref[j:j + 1, :] * window[off:off + tt, :]
    halo = ubuf[tt:tt + CONV_HALO, :]
    ubuf[0:CONV_HALO, :] = halo
    mu = jnp.mean(y, axis=-1, keepdims=True)
    yc = y - mu
    var = jnp.mean(yc * yc, axis=-1, keepdims=True)
    yn = yc * lax.rsqrt(var + EPS) * lng_ref[...] + lnb_ref[...]
    o_b = (yn * _sigmoid(yn)).astype(BF16)

    qm = _dot(h, wq_ref[...])
    o_m = []
    for hh in range(MEM_HEADS):
        q = qm[:, hh * MEM_HEAD_DIM:(hh + 1) * MEM_HEAD_DIM]
        q = q * lax.rsqrt(jnp.mean(q * q, axis=-1, keepdims=True) + EPS) * gqm_ref[...] * (MEM_HEAD_DIM ** -0.5)
        s = _dot(q.astype(BF16), kmt_ref[0, hh])
        p = jnp.exp(s - jnp.max(s, axis=-1, keepdims=True))
        l = jnp.sum(p, axis=-1, keepdims=True)
        o_m.append((_dot(p.astype(BF16), vm_ref[0, hh]) / l).astype(BF16))
    o_m = jnp.concatenate(o_m, axis=-1)

    gates = _sigmoid(_dot(h, wg_ref[...]) + bg_ref[...])
    merged = (gates[:, :D] * _dot(oa_ref[0], wa_ref[...])
              + gates[:, D:2 * D] * _dot(o_b, wb_ref[...])
              + gates[:, 2 * D:] * _dot(o_m, wm_ref[...]))
    x1 = x + _dot(merged.astype(BF16), wo_ref[...])
    x1_ref[0] = x1

    h2 = x1 * lax.rsqrt(jnp.mean(x1 * x1, axis=-1, keepdims=True) + EPS) * gffn_ref[...]
    h2_ref[0] = h2.astype(BF16)
    hi, lo = _split_bf16(h2)
    logits = _dot(hi, wrh_ref[...]) + _dot(hi, wrl_ref[...]) + _dot(lo, wrh_ref[...]) + br_ref[...]
    lane = lax.broadcasted_iota(jnp.int32, logits.shape, 1)
    ninf = jnp.float32(-jnp.inf)
    big = jnp.int32(1 << 20)
    is_g = (lane >= N_EXPERTS) & (lane < N_EXPERTS + N_GROUPS)
    gl = jnp.where(is_g, logits, ninf)
    gmax = jnp.max(gl, axis=-1, keepdims=True)
    gsel = jnp.min(jnp.where(gl == gmax, lane, big), axis=-1, keepdims=True) - N_EXPERTS
    p_sel = 1.0 / jnp.sum(jnp.where(is_g, jnp.exp(gl - gmax), 0.0), axis=-1, keepdims=True)
    in_grp = (lane < N_EXPERTS) & ((lane >> 3) == gsel)
    el = jnp.where(in_grp, logits, ninf)
    v1 = jnp.max(el, axis=-1, keepdims=True)
    i1 = jnp.min(jnp.where(el == v1, lane, big), axis=-1, keepdims=True)
    el2 = jnp.where(lane == i1, ninf, el)
    v2 = jnp.max(el2, axis=-1, keepdims=True)
    i2 = jnp.min(jnp.where(el2 == v2, lane, big), axis=-1, keepdims=True)
    e2 = jnp.exp(v2 - v1)
    den = 1.0 + e2
    comb_ref[0] = jnp.where(lane == i1, (1.0 / den) * p_sel,
                            jnp.where(lane == i2, (e2 / den) * p_sel,
                                      jnp.where(lane == N_EXPERTS, gsel.astype(F32), 0.0)))


def _mixer(x, oa, g_mix, wc, wq, wg, bg, cw, cb, lng, lnb, gqm, kmt, vm, wa, wb, wm, wo, gffn, wrh, wrl, br, tt):
    B, S, D = x.shape
    M = kmt.shape[-1]
    const = lambda b, s: (0, 0)
    tile = lambda w: pl.BlockSpec((1, tt, w), lambda b, s: (b, s, 0))
    full = lambda a: pl.BlockSpec(a.shape, const)
    return pl.pallas_call(
        functools.partial(_mixer_kernel, tt=tt),
        grid=(B, S // tt),
        in_specs=[
            tile(D), tile(QA_W), full(g_mix), full(wc), full(wq), full(wg), full(bg), full(cw), full(cb),
            full(lng), full(lnb), full(gqm),
            pl.BlockSpec((1, MEM_HEADS, MEM_HEAD_DIM, M), lambda b, s: (b, 0, 0, 0)),
            pl.BlockSpec((1, MEM_HEADS, M, MEM_HEAD_DIM), lambda b, s: (b, 0, 0, 0)),
            full(wa), full(wb), full(wm), full(wo), full(gffn), full(wrh), full(wrl), full(br),
        ],
        out_specs=[tile(D), tile(D), tile(LANES)],
        out_shape=[
            jax.ShapeDtypeStruct((B, S, D), F32),
            jax.ShapeDtypeStruct((B, S, D), BF16),
            jax.ShapeDtypeStruct((B, S, LANES), F32),
        ],
        scratch_shapes=[pltpu.VMEM((CONV_HALO + tt, CONV_CH), F32),
                        pltpu.VMEM((CONV_HALO + tt, CONV_CH), F32)],
        compiler_params=pltpu.CompilerParams(dimension_semantics=("arbitrary", "arbitrary"),
                                             vmem_limit_bytes=VMEM_LIMIT),
        name="mixer",
    )(x, oa, g_mix, wc, wq, wg, bg, cw, cb, lng, lnb, gqm, kmt, vm, wa, wb, wm, wo, gffn, wrh, wrl, br)


MOE_BLK = 256
MOE_EXPERTS_PER_STEP = 4


def _moe_kernel(h2_ref, x1_ref, comb_ref, ltri_ref, wup_ref, wdn_ref, o_ref,
                hs_s, combs_s, ys_s, pt_s, seg_s, *, tm, rows):
    n = pl.program_id(1)
    lane = lax.broadcasted_iota(jnp.int32, (tm, LANES), 1)

    @pl.when(n == 0)
    def _():
        comb = comb_ref[...]
        gid = comb[:, N_EXPERTS:N_EXPERTS + 1].astype(jnp.int32)
        member = lane == gid
        onehot = jnp.where(member, 1.0, 0.0)
        before = _dot(ltri_ref[...], onehot.astype(BF16))
        rank = jnp.sum(jnp.where(member, before, 0.0), axis=-1, keepdims=True)
        count = jnp.sum(onehot, axis=0, keepdims=True)
        lane1 = lax.broadcasted_iota(jnp.int32, (1, LANES), 1)
        start_blk = jnp.int32(0)
        start_row = jnp.zeros((1, LANES), F32)
        for g in range(N_GROUPS):
            cnt_g = jnp.sum(jnp.where(lane1 == g, count, 0.0)).astype(jnp.int32)
            nblk_g = (cnt_g + (MOE_BLK - 1)) // MOE_BLK
            seg_s[g] = start_blk
            seg_s[N_GROUPS + g] = nblk_g
            start_row = jnp.where(lane1 == g, (start_blk * MOE_BLK).astype(F32), start_row)
            start_blk = start_blk + nblk_g
        dest = jnp.sum(jnp.where(member, start_row, 0.0), axis=-1, keepdims=True) + rank
        seg_s[2 * N_GROUPS] = start_blk
        dest_i = dest.astype(jnp.int32)
        dest_row = jnp.broadcast_to(dest, (tm, LANES)).T[0:1, :].astype(jnp.int32)
        c_hi, c_lo = _split_bf16(comb)
        col = lax.broadcasted_iota(jnp.int32, (tm, MOE_BLK), 1)
        row = lax.broadcasted_iota(jnp.int32, (MOE_BLK, tm), 0)

        def permute_block(b, carry):
            r0 = pl.multiple_of(b * MOE_BLK, MOE_BLK)
            pt_s[b] = jnp.where(col + r0 == dest_i, 1.0, 0.0).astype(BF16)
            perm = jnp.where(row + r0 == dest_row, 1.0, 0.0).astype(BF16)
            hs_s[pl.ds(r0, MOE_BLK), :] = _dot(perm, h2_ref[...]).astype(BF16)
            combs_s[pl.ds(r0, MOE_BLK), :] = _dot(perm, c_hi) + _dot(perm, c_lo)
            ys_s[pl.ds(r0, MOE_BLK), :] = jnp.zeros((MOE_BLK, ys_s.shape[1]), F32)
            return carry

        lax.fori_loop(0, start_blk, permute_block, 0)

    g = (n * MOE_EXPERTS_PER_STEP) // EXPERTS_PER_GROUP
    first_blk = seg_s[g]
    lane_b = lax.broadcasted_iota(jnp.int32, (MOE_BLK, LANES), 1)

    def expert_block(b, carry):
        r0 = pl.multiple_of((first_blk + b) * MOE_BLK, MOE_BLK)
        hb = hs_s[pl.ds(r0, MOE_BLK), :]
        cb = combs_s[pl.ds(r0, MOE_BLK), :]
        acts = []
        for i in range(MOE_EXPERTS_PER_STEP):
            up = _dot(hb, wup_ref[i])
            a = up[:, :D_FF_EXPERT]
            c = jnp.sum(jnp.where(lane_b == n * MOE_EXPERTS_PER_STEP + i, cb, 0.0), axis=-1, keepdims=True)
            acts.append((a * _sigmoid(a) * up[:, D_FF_EXPERT:] * c).astype(BF16))
        w_dn = wdn_ref[...].reshape(MOE_EXPERTS_PER_STEP * D_FF_EXPERT, wdn_ref.shape[-1])
        ys_s[pl.ds(r0, MOE_BLK), :] += _dot(jnp.concatenate(acts, axis=-1), w_dn)
        return carry

    lax.fori_loop(0, seg_s[N_GROUPS + g], expert_block, 0)

    @pl.when(n == N_EXPERTS // MOE_EXPERTS_PER_STEP - 1)
    def _():
        o_ref[...] = x1_ref[...]

        def unpermute_block(b, carry):
            r0 = pl.multiple_of(b * MOE_BLK, MOE_BLK)
            o_ref[...] += _dot(pt_s[b], ys_s[pl.ds(r0, MOE_BLK), :].astype(BF16))
            return carry

        lax.fori_loop(0, seg_s[2 * N_GROUPS], unpermute_block, 0)


def _moe(h2, x1, comb, w_up, w_down, tm):
    T, D = h2.shape
    rows = tm + N_GROUPS * MOE_BLK
    ltri = jnp.asarray(np.tril(np.ones((tm, tm), np.float32), -1), BF16)
    return pl.pallas_call(
        functools.partial(_moe_kernel, tm=tm, rows=rows),
        grid=(T // tm, N_EXPERTS // MOE_EXPERTS_PER_STEP),
        in_specs=[
            pl.BlockSpec((tm, D), lambda t, n: (t, 0)),
            pl.BlockSpec((tm, D), lambda t, n: (t, 0)),
            pl.BlockSpec((tm, LANES), lambda t, n: (t, 0)),
            pl.BlockSpec((tm, tm), lambda t, n: (0, 0)),
            pl.BlockSpec((MOE_EXPERTS_PER_STEP, D, 2 * D_FF_EXPERT), lambda t, n: (n, 0, 0)),
            pl.BlockSpec((MOE_EXPERTS_PER_STEP, D_FF_EXPERT, D), lambda t, n: (n, 0, 0)),
        ],
        out_specs=pl.BlockSpec((tm, D), lambda t, n: (t, 0)),
        out_shape=jax.ShapeDtypeStruct((T, D), F32),
        scratch_shapes=[
            pltpu.VMEM((rows, D), BF16),
            pltpu.VMEM((rows, LANES), F32),
            pltpu.VMEM((rows, D), F32),
            pltpu.VMEM((rows // MOE_BLK, tm, MOE_BLK), BF16),
            pltpu.SMEM((2 * N_GROUPS + 1,), jnp.int32),
        ],
        compiler_params=pltpu.CompilerParams(dimension_semantics=("arbitrary", "arbitrary"),
                                             vmem_limit_bytes=VMEM_LIMIT),
        name="moe",
    )(h2, x1, comb, ltri, w_up, w_down)


def _pick_tile(n, pref):
    t = min(n, pref)
    assert n % t == 0, (n, t)
    return t


def _rope_tables(S):
    half = HEAD_DIM_A // 2
    inv = ROPE_THETA ** (-jnp.arange(half, dtype=F32) / half)
    ang = jnp.arange(S, dtype=jnp.int32).astype(F32)[:, None] * inv[None, :]
    cos, sin = jnp.cos(ang), jnp.sin(ang)
    cos_t = jnp.concatenate([cos, cos, cos, cos], axis=-1)
    sin_t = jnp.concatenate([-sin, sin, -sin, sin], axis=-1)
    return cos_t, sin_t


def _layer(x, mem, g_mix, w_in, b_gate, g_qa, g_ka, g_idx_k, conv_w, conv_b, ln_g, ln_b,
           g_mem, w_mem_kv, g_qm, g_km, w_br_a, w_br_b, w_br_m, w_o, g_ffn,
           w_rg, b_rg, w_re, b_re, w_up, w_down):
    B, S, D = x.shape
    topk = min(TOPK_MAX, S // 4)
    tk = _pick_tile(S, 512)
    tq = _pick_tile(tk, 128)
    tt = _pick_tile(S, 512)
    tm = _pick_tile(B * S, 1024)

    sizes = (QA_W, N_KV_HEADS_A * HEAD_DIM_A, N_KV_HEADS_A * HEAD_DIM_A, QI_W, IDX_DIM, IDX_HEADS,
             CONV_IN_W, QM_W, N_BRANCHES * D)
    offs = np.concatenate([[0], np.cumsum(sizes)])
    col = lambda i: w_in[:, offs[i]:offs[i + 1]]
    w_qa, w_ka, w_va, w_qi, w_ki, w_wi, w_conv, w_qm, w_gate = (col(i) for i in range(9))
    hd = HEAD_DIM_A
    zeros_hd = jnp.zeros((D, hd), F32)
    ka_dup = [w_ka[:, g * hd:(g + 1) * hd] for g in range(N_KV_HEADS_A) for _ in range(2)]
    va_aug = [w for g in range(N_KV_HEADS_A) for w in (w_va[:, g * hd:(g + 1) * hd], zeros_hd)]
    w_dsa = jnp.concatenate(
        [w_qa, w_qi] + ka_dup + va_aug + [w_ki, w_ki, w_wi, jnp.zeros((D, WI_W - IDX_HEADS), F32)], axis=1).astype(BF16)
    vbias = jnp.tile(jnp.concatenate([jnp.zeros((hd,), F32), jnp.ones((hd,), F32)]), N_KV_HEADS_A).reshape(1, VA_W)
    seg = np.arange(QA_W) // hd
    bd = jnp.asarray(seg[:, None] == seg[None, :], BF16)
    cos_t, sin_t = _rope_tables(S)

    kmt, vm = _mem_kv(mem, g_mem, w_mem_kv, g_km)
    qat, qit, k, vt, ki, wit = _dsa_proj(
        x, g_mix.reshape(1, D), w_dsa, vbias,
        jnp.tile(g_qa, N_HEADS_A).reshape(1, QA_W), jnp.tile(g_ka, KA_W // hd).reshape(1, KA_W),
        jnp.tile(g_idx_k, KI_W // IDX_DIM).reshape(1, KI_W), cos_t, sin_t, bd, tk)
    o_a = _dsa_attn(qat, qit, wit, k, vt, ki, tq, topk)

    w_r = jnp.concatenate([w_re, w_rg, jnp.zeros((D, LANES - N_EXPERTS - N_GROUPS), F32)], axis=1)
    b_r = jnp.concatenate([b_re, b_rg, jnp.zeros((LANES - N_EXPERTS - N_GROUPS,), F32)]).reshape(1, LANES)
    w_r_hi = w_r.astype(BF16)
    w_r_lo = (w_r - w_r_hi.astype(F32)).astype(BF16)
    x1, h2, comb = _mixer(
        x, o_a, g_mix.reshape(1, D), w_conv.astype(BF16), w_qm.astype(BF16), w_gate.astype(BF16),
        b_gate.reshape(1, -1), conv_w.reshape(CONV_WIDTH, CONV_CH), conv_b.reshape(1, CONV_CH),
        ln_g.reshape(1, CONV_CH), ln_b.reshape(1, CONV_CH), g_qm.reshape(1, MEM_HEAD_DIM), kmt, vm,
        w_br_a.astype(BF16), w_br_b.astype(BF16), w_br_m.astype(BF16), w_o.astype(BF16),
        g_ffn.reshape(1, D), w_r_hi, w_r_lo, b_r, tt)

    out = _moe(h2.reshape(B * S, D), x1.reshape(B * S, D), comb.reshape(B * S, LANES),
               w_up.astype(BF16), w_down.astype(BF16), tm)
    return out.reshape(B, S, D)


def kernel(x, mem, g_mix, w_in, b_gate, g_qa, g_ka, g_idx_k, conv_w, conv_b, ln_g, ln_b, g_mem, w_mem_kv, g_qm, g_km,
           w_br_a, w_br_b, w_br_m, w_o, g_ffn, w_rg, b_rg, w_re, b_re, w_up, w_down):
    params = (g_mix, w_in, b_gate, g_qa, g_ka, g_idx_k, conv_w, conv_b, ln_g, ln_b, g_mem, w_mem_kv, g_qm, g_km,
              w_br_a, w_br_b, w_br_m, w_o, g_ffn, w_rg, b_rg, w_re, b_re, w_up, w_down)
    for l in range(g_mix.shape[0]):
        x = _layer(x, mem, *(p[l] for p in params))
    return x
```

```python
import functools

import jax
import jax.numpy as jnp
import numpy as np
from jax import lax
from jax.experimental import pallas as pl
from jax.experimental.pallas import tpu as pltpu

N_HEADS_A = 8
N_KV_HEADS_A = 2
HEAD_DIM_A = 64
IDX_HEADS = 8
IDX_DIM = 64
TOPK_MAX = 256
CONV_CH = 512
CONV_WIDTH = 31
MEM_HEADS = 4
MEM_HEAD_DIM = 128
N_BRANCHES = 3
N_GROUPS = 4
EXPERTS_PER_GROUP = 8
N_EXPERTS = N_GROUPS * EXPERTS_PER_GROUP
D_FF_EXPERT = 256
ROPE_THETA = 10000.0
EPS = 1e-6

LANES = 128
SUBLANES = 8
VMEM_LIMIT = 56 * 1024 * 1024

INT_MIN = -2 ** 31
LOG2_E = 1.4426950408889634
HALF_BIAS = 2 ** 15
HALF_WORD = 2 ** 16
NEG_BIG = -1e30
CONV_HALO = 32

BF16 = jnp.bfloat16
F32 = jnp.float32


def _dot(a, b):
    return jnp.dot(a, b, preferred_element_type=F32)


def _sigmoid(x):
    return 0.5 * jnp.tanh(0.5 * x) + 0.5


def _split_bf16(x):
    hi = x.astype(BF16)
    lo = (x - hi.astype(F32)).astype(BF16)
    return hi, lo


def _seg_sum(xsq, ones_bd):
    hi, lo = _split_bf16(xsq)
    return _dot(hi, ones_bd) + _dot(lo, ones_bd)


def _rot_half(x, head_dim):
    n = x.shape[-1]
    half = head_dim // 2
    lane = lax.broadcasted_iota(jnp.int32, x.shape, x.ndim - 1)
    is_lo = (lane & (head_dim - 1)) < half
    return jnp.where(is_lo, pltpu.roll(x, n - half, x.ndim - 1), pltpu.roll(x, half, x.ndim - 1))


def _tile_lanes(x, reps):
    return jnp.concatenate([x] * reps, axis=-1) if reps > 1 else x


def _mem_kv_kernel(mem_ref, g_mem_ref, w_ref, g_km_ref, kt_ref, v_ref):
    m = mem_ref[0]
    hm = m * lax.rsqrt(jnp.mean(m * m, axis=-1, keepdims=True) + EPS) * g_mem_ref[...]
    kv = _dot(hm.astype(BF16), w_ref[...])
    hd = MEM_HEADS * MEM_HEAD_DIM
    for h in range(MEM_HEADS):
        k = kv[:, h * MEM_HEAD_DIM:(h + 1) * MEM_HEAD_DIM]
        k = k * lax.rsqrt(jnp.mean(k * k, axis=-1, keepdims=True) + EPS) * g_km_ref[...]
        kt_ref[0, h] = k.T.astype(BF16)
        v_ref[0, h] = kv[:, hd + h * MEM_HEAD_DIM: hd + (h + 1) * MEM_HEAD_DIM].astype(BF16)


def _mem_kv(mem, g_mem, w_mem_kv, g_km):
    B, M, D = mem.shape
    hd2 = 2 * MEM_HEADS * MEM_HEAD_DIM
    return pl.pallas_call(
        _mem_kv_kernel,
        grid=(B,),
        in_specs=[
            pl.BlockSpec((1, M, D), lambda b: (b, 0, 0)),
            pl.BlockSpec((1, D), lambda b: (0, 0)),
            pl.BlockSpec((D, hd2), lambda b: (0, 0)),
            pl.BlockSpec((1, MEM_HEAD_DIM), lambda b: (0, 0)),
        ],
        out_specs=[
            pl.BlockSpec((1, MEM_HEADS, MEM_HEAD_DIM, M), lambda b: (b, 0, 0, 0)),
            pl.BlockSpec((1, MEM_HEADS, M, MEM_HEAD_DIM), lambda b: (b, 0, 0, 0)),
        ],
        out_shape=[
            jax.ShapeDtypeStruct((B, MEM_HEADS, MEM_HEAD_DIM, M), BF16),
            jax.ShapeDtypeStruct((B, MEM_HEADS, M, MEM_HEAD_DIM), BF16),
        ],
        compiler_params=pltpu.CompilerParams(dimension_semantics=("arbitrary",)),
        name="mem_kv",
    )(mem, g_mem.reshape(1, D), w_mem_kv.astype(BF16), g_km.reshape(1, MEM_HEAD_DIM))


QA_W = N_HEADS_A * HEAD_DIM_A
QI_W = IDX_HEADS * IDX_DIM
KA_W = 2 * N_KV_HEADS_A * HEAD_DIM_A
VA_W = 2 * N_KV_HEADS_A * HEAD_DIM_A
KI_W = 2 * IDX_DIM
WI_W = LANES
DSA_COLS = QA_W + QI_W + KA_W + VA_W + KI_W + WI_W


def _dsa_proj_kernel(x_ref, g_mix_ref, w_ref, vbias_ref, gq_ref, gk_ref, gki_ref, cos_ref, sin_ref, bd_ref,
                     qat_ref, qit_ref, k_ref, vt_ref, ki_ref, wit_ref):
    x = x_ref[0]
    h = x * lax.rsqrt(jnp.mean(x * x, axis=-1, keepdims=True) + EPS) * g_mix_ref[...]
    p = _dot(h.astype(BF16), w_ref[...])
    cos1 = cos_ref[...]
    sin1 = sin_ref[...]
    cos4, sin4 = _tile_lanes(cos1, 4), _tile_lanes(sin1, 4)
    cos2, sin2 = _tile_lanes(cos1, 2), _tile_lanes(sin1, 2)
    bd = bd_ref[...]
    inv_hd = 1.0 / HEAD_DIM_A

    def rope(v, c, s):
        return v * c + _rot_half(v, HEAD_DIM_A) * s

    o = 0
    qa = p[:, o:o + QA_W]; o += QA_W
    qa = qa * lax.rsqrt(_seg_sum(qa * qa, bd) * inv_hd + EPS) * gq_ref[...]
    qat_ref[0] = (rope(qa, cos4, sin4) * (HEAD_DIM_A ** -0.5 * LOG2_E)).T.astype(BF16)

    qi = p[:, o:o + QI_W]; o += QI_W
    qit_ref[0] = (rope(qi, cos4, sin4) * (IDX_DIM ** -0.5)).T.astype(BF16)

    ka = p[:, o:o + KA_W]; o += KA_W
    ka = ka * lax.rsqrt(_seg_sum(ka * ka, bd[:KA_W, :KA_W]) * inv_hd + EPS) * gk_ref[...]
    k_ref[0] = rope(ka, cos2, sin2).astype(BF16)

    vt_ref[0, 0] = (p[:, o:o + VA_W] + vbias_ref[...]).T.astype(BF16); o += VA_W

    ki = p[:, o:o + KI_W]; o += KI_W
    ki = ki * lax.rsqrt(_seg_sum(ki * ki, bd[:KI_W, :KI_W]) * (1.0 / IDX_DIM) + EPS) * gki_ref[...]
    ki_ref[0] = rope(ki, cos1, sin1).astype(BF16)

    wit_ref[0] = (p[:, o:o + WI_W] * (IDX_HEADS ** -0.5)).T[:IDX_HEADS, :]


def _dsa_proj(x, g_mix, w_dsa, vbias, g_qa, g_ka, g_idx_k, cos_t, sin_t, bd, tt):
    B, S, D = x.shape
    nt = S // tt
    const = lambda b, s: (0, 0)
    return pl.pallas_call(
        _dsa_proj_kernel,
        grid=(B, nt),
        in_specs=[
            pl.BlockSpec((1, tt, D), lambda b, s: (b, s, 0)),
            pl.BlockSpec((1, D), const),
            pl.BlockSpec((D, DSA_COLS), const),
            pl.BlockSpec((1, VA_W), const),
            pl.BlockSpec((1, QA_W), const),
            pl.BlockSpec((1, KA_W), const),
            pl.BlockSpec((1, KI_W), const),
            pl.BlockSpec((tt, LANES), lambda b, s: (s, 0)),
            pl.BlockSpec((tt, LANES), lambda b, s: (s, 0)),
            pl.BlockSpec((QA_W, QA_W), const),
        ],
        out_specs=[
            pl.BlockSpec((1, QA_W, tt), lambda b, s: (b, 0, s)),
            pl.BlockSpec((1, QI_W, tt), lambda b, s: (b, 0, s)),
            pl.BlockSpec((1, tt, KA_W), lambda b, s: (b, s, 0)),
            pl.BlockSpec((1, 1, VA_W, tt), lambda b, s: (b, s, 0, 0)),
            pl.BlockSpec((1, tt, KI_W), lambda b, s: (b, s, 0)),
            pl.BlockSpec((1, IDX_HEADS, tt), lambda b, s: (b, 0, s)),
        ],
        out_shape=[
            jax.ShapeDtypeStruct((B, QA_W, S), BF16),
            jax.ShapeDtypeStruct((B, QI_W, S), BF16),
            jax.ShapeDtypeStruct((B, S, KA_W), BF16),
            jax.ShapeDtypeStruct((B, nt, VA_W, tt), BF16),
            jax.ShapeDtypeStruct((B, S, KI_W), BF16),
            jax.ShapeDtypeStruct((B, IDX_HEADS, S), F32),
        ],
        compiler_params=pltpu.CompilerParams(dimension_semantics=("arbitrary", "arbitrary"),
                                             vmem_limit_bytes=VMEM_LIMIT),
        name="dsa_proj",
    )(x, g_mix, w_dsa, vbias, g_qa, g_ka, g_idx_k, cos_t, sin_t, bd)


def _key_of(f):
    bits = lax.bitcast_convert_type(f, jnp.int32)
    return bits ^ ((bits >> 31) & jnp.int32(0x7FFFFFFF))


def _pack_high_halves(key):
    m = key.shape[0] // 2
    return lax.shift_right_logical(key[:m], 16) | (key[m:] & jnp.int32(-HALF_WORD))


def _pack_low_halves(key):
    m = key.shape[0] // 2
    return ((key[:m] & 0xFFFF) | (key[m:] << 16)) ^ jnp.int32(-HALF_BIAS * HALF_WORD + HALF_BIAS)


def _halves_of(words):
    return pltpu.bitcast(words, jnp.int16)


def _pipelined_chunks(nchunks, last_chunk, matmul, finish):
    matmul(0, 0)

    def pair(i, carry):
        c0 = 2 * i
        matmul(c0 + 1, 1)
        finish(c0, 0)
        matmul(c0 + 2, 0)
        finish(c0 + 1, 1)
        return carry

    lax.fori_loop(0, last_chunk // 2, pair, 0)

    @pl.when(last_chunk % 2 == 0)
    def _():
        finish(last_chunk, 0)

    @pl.when(last_chunk % 2 == 1)
    def _():
        matmul(last_chunk, 1)
        finish(last_chunk - 1, 0)
        finish(last_chunk, 1)


def _dsa_attn_kernel(qat_ref, qit_ref, wit_ref, k_ref, vt_ref, ki_ref, o_ref,
                     qa_s, qi_s, keys_s, khi_s, klo_s, zlo_s, buf0, buf1, m_s, acc_s, *, tq, tk, topk, idx_bits):
    qt = pl.program_id(1)
    q0 = qt * tq
    nchunks = q0 // tk + 1
    last_chunk = nchunks - 1
    hpg = N_HEADS_A // N_KV_HEADS_A
    gw = hpg * tq
    half_pairs = hpg // 2
    bufs = (buf0, buf1)
    dim = lax.broadcasted_iota(jnp.int32, (LANES, tq), 0)
    lo_half = dim < HEAD_DIM_A

    for j in range(IDX_HEADS // 2):
        pair = qit_ref[0, j * LANES:(j + 1) * LANES, :]
        qi_s[:, (2 * j) * tq:(2 * j + 1) * tq] = jnp.where(lo_half, pair, jnp.zeros_like(pair))
        qi_s[:, (2 * j + 1) * tq:(2 * j + 2) * tq] = jnp.where(lo_half, jnp.zeros_like(pair), pair)
    for g in range(N_KV_HEADS_A):
        for jj in range(half_pairs):
            pair = qat_ref[0, (g * half_pairs + jj) * LANES:(g * half_pairs + jj + 1) * LANES, :]
            qa_s[:, g * gw + jj * tq:g * gw + (jj + 1) * tq] = jnp.where(lo_half, pair, jnp.zeros_like(pair))
            qa_s[:, g * gw + (half_pairs + jj) * tq:g * gw + (half_pairs + jj + 1) * tq] = (
                jnp.where(lo_half, jnp.zeros_like(pair), pair))

    key_in_chunk = lax.broadcasted_iota(jnp.int32, (tk, tq), 0)
    q_pos = q0 + lax.broadcasted_iota(jnp.int32, (tk, tq), 1)

    def score_chunk(c, diagonal=False):
        r0 = pl.multiple_of(c * tk, tk)
        logits = _dot(ki_ref[0, pl.ds(r0, tk), :], qi_s[...])
        score = jnp.zeros((tk, tq), F32)
        for h in range(IDX_HEADS):
            score = score + wit_ref[0, h:h + 1, :] * jnp.maximum(logits[:, h * tq:(h + 1) * tq], 0.0)
        key = _key_of(score)
        if diagonal:
            key = jnp.where(key_in_chunk + c * tk <= q_pos, key, jnp.int32(INT_MIN))
        keys_s[c] = key
        khi_s[c] = _pack_high_halves(key)
        klo_s[c] = _pack_low_halves(key)

    def score_pair(i, carry):
        score_chunk(2 * i)
        score_chunk(2 * i + 1)
        return carry

    lax.fori_loop(0, last_chunk // 2, score_pair, 0)

    @pl.when(last_chunk % 2 == 1)
    def _():
        score_chunk(last_chunk - 1)
        score_chunk(last_chunk, diagonal=True)

    @pl.when(last_chunk % 2 == 0)
    def _():
        score_chunk(last_chunk, diagonal=True)

    def count_keys(pred_fn):
        def one(c, acc):
            hit = pred_fn(keys_s[c], c * tk).astype(jnp.int32)
            return acc + jnp.sum(hit.reshape(tk // 8, 8, tq), axis=0)

        def two(i, acc):
            return one(2 * i + 1, one(2 * i, acc))

        part = lax.fori_loop(0, nchunks // 2, two, jnp.zeros((8, tq), jnp.int32))
        part = lax.cond(nchunks % 2 == 1, lambda acc: one(last_chunk, acc), lambda acc: acc, part)
        return jnp.sum(part, axis=0, keepdims=True)

    total = q0 + lax.broadcasted_iota(jnp.int32, (1, tq), 1) + 1
    has_tau = total >= topk

    def count_halves(src_s, cand, odd):
        cand16 = _halves_of(jnp.broadcast_to((cand & 0xFFFF) | (cand << 16), (SUBLANES, tq)))

        def one(c, acc):
            half = _halves_of(src_s[c])
            hits = [jnp.where(half[j * 2 * SUBLANES:(j + 1) * 2 * SUBLANES, :] >= cand16, jnp.int16(1), jnp.int16(0))
                    for j in range(tk // (2 * SUBLANES))]
            while len(hits) > 1:
                hits = [a + b for a, b in zip(hits[::2], hits[1::2])]
            return acc + hits[0]

        def two(i, acc):
            return one(2 * i + 1, one(2 * i, acc))

        acc = lax.fori_loop(0, nchunks // 2, two, jnp.zeros((2 * SUBLANES, tq), jnp.int16))
        if odd:
            acc = one(last_chunk, acc)
        words = pltpu.bitcast(acc, jnp.int32)
        return jnp.sum((words & 0xFFFF) + (words >> 16), axis=0, keepdims=True)

    def radix16(src_s, want, odd):
        def step(i, state):
            v, cnt_ge, cnt_gt = state
            cand = v + (jnp.int32(1) << (15 - i))
            c = count_halves(src_s, cand, odd)
            up = c >= want
            return jnp.where(up, cand, v), jnp.where(up, c, cnt_ge), jnp.where(up, cnt_gt, c)
        zero = jnp.zeros((1, tq), jnp.int32)
        return lax.fori_loop(0, 16, step, (jnp.full((1, tq), -HALF_BIAS, jnp.int32), zero, zero))

    def select(odd):
        tau_hi, cnt_hi_ge, cnt_hi_gt = radix16(khi_s, topk, odd)
        tau_hi16 = _halves_of(jnp.broadcast_to((tau_hi & 0xFFFF) | (tau_hi << 16), (SUBLANES, tq)))

        def low_of_bucket(c, carry):
            hi, lo = _halves_of(khi_s[c]), _halves_of(klo_s[c])
            for j in range(tk // (2 * SUBLANES)):
                rows = slice(j * 2 * SUBLANES, (j + 1) * 2 * SUBLANES)
                zlo_s[c, j * SUBLANES:(j + 1) * SUBLANES, :] = pltpu.bitcast(
                    jnp.where(hi[rows, :] == tau_hi16, lo[rows, :], jnp.int16(-HALF_BIAS)), jnp.int32)
            return carry

        lax.fori_loop(0, nchunks, low_of_bucket, 0)
        tau_lo, cnt_lo_ge, _ = radix16(zlo_s, topk - cnt_hi_gt, odd)
        return tau_hi, cnt_hi_ge, cnt_hi_gt, tau_lo, cnt_lo_ge

    tau_hi, cnt_hi_ge, cnt_hi_gt, tau_lo, cnt_lo_ge = lax.cond(
        nchunks % 2 == 1, lambda: select(True), lambda: select(False))
    tau = (tau_hi << 16) | ((tau_lo + HALF_BIAS) & 0xFFFF)
    cnt = cnt_hi_gt + jnp.where(tau_lo == -HALF_BIAS, cnt_hi_ge - cnt_hi_gt, cnt_lo_ge)
    tau = jnp.where(has_tau, jnp.maximum(tau, jnp.int32(INT_MIN + 1)), jnp.int32(INT_MIN))

    tied = has_tau & (cnt > topk)

    def tie_cut():
        need = topk - count_keys(lambda k, _: k > tau)

        def idx_step(i, j):
            bit = jnp.int32(1) << (idx_bits - 1 - i)
            test = j + bit - 1
            f = count_keys(lambda k, base: (k == tau) & (base + key_in_chunk <= test))
            return jnp.where(f < need, j + bit, j)
        return lax.fori_loop(0, idx_bits, idx_step, jnp.zeros((1, tq), jnp.int32))

    jcut = lax.cond(jnp.max(jnp.where(tied, 1, 0)) > 0, tie_cut, lambda: jnp.zeros((1, tq), jnp.int32))
    jcut = jnp.where(tied, jcut, jnp.where(has_tau, jnp.int32(2 ** 30), jnp.int32(-1)))

    m_s[...] = jnp.full(m_s.shape, NEG_BIG, F32)
    acc_s[...] = jnp.zeros(acc_s.shape, F32)

    def attn_matmul(c, slot):
        r0 = pl.multiple_of(c * tk, tk)
        k = keys_s[c]
        kpos = key_in_chunk + c * tk
        bias = jnp.where(k > tau, 0.0, jnp.where(k == tau, jnp.where(kpos <= jcut, 0.0, NEG_BIG), NEG_BIG))
        bias4 = jnp.concatenate([bias] * hpg, axis=1)
        for g in range(N_KV_HEADS_A):
            bufs[slot][:, g * gw:(g + 1) * gw] = _dot(k_ref[0, pl.ds(r0, tk), g * LANES:(g + 1) * LANES],
                                                      qa_s[:, g * gw:(g + 1) * gw]) + bias4

    def attn_finish(c, slot):
        for g in range(N_KV_HEADS_A):
            m_old = m_s[g]
            m_new = jnp.maximum(m_old, jnp.max(bufs[slot][:, g * gw:(g + 1) * gw], axis=0, keepdims=True))
            p = jnp.exp2(bufs[slot][:, g * gw:(g + 1) * gw] - m_new)
            alpha = jnp.exp2(m_old - m_new)
            pv = _dot(vt_ref[0, c, g * LANES:(g + 1) * LANES, :], p.astype(BF16))
            acc_s[g] = alpha * acc_s[g] + pv
            m_s[g] = m_new

    _pipelined_chunks(nchunks, last_chunk, attn_matmul, attn_finish)

    for g in range(N_KV_HEADS_A):
        for jj in range(half_pairs):
            a_e = acc_s[g, :, jj * tq:(jj + 1) * tq]
            a_o = acc_s[g, :, (half_pairs + jj) * tq:(half_pairs + jj + 1) * tq]
            o_pair = jnp.concatenate([a_e[:HEAD_DIM_A] / a_e[HEAD_DIM_A:], a_o[:HEAD_DIM_A] / a_o[HEAD_DIM_A:]], axis=0)
            j = g * half_pairs + jj
            o_ref[0, :, j * LANES:(j + 1) * LANES] = o_pair.T.astype(BF16)


def _dsa_attn(qat, qit, wit, k, vt, ki, tq, topk):
    B, _, S = qat.shape
    nc, tk = vt.shape[1], vt.shape[3]
    hpg = N_HEADS_A // N_KV_HEADS_A
    idx_bits = max(1, int(np.ceil(np.log2(S))))
    assert IDX_HEADS == N_HEADS_A
    kernel = functools.partial(_dsa_attn_kernel, tq=tq, tk=tk, topk=topk, idx_bits=idx_bits)
    return pl.pallas_call(
        kernel,
        grid=(B, S // tq),
        in_specs=[
            pl.BlockSpec((1, QA_W, tq), lambda b, q: (b, 0, q)),
            pl.BlockSpec((1, QI_W, tq), lambda b, q: (b, 0, q)),
            pl.BlockSpec((1, IDX_HEADS, tq), lambda b, q: (b, 0, q)),
            pl.BlockSpec((1, S, KA_W), lambda b, q: (b, 0, 0)),
            pl.BlockSpec((1, nc, VA_W, tk), lambda b, q: (b, 0, 0, 0)),
            pl.BlockSpec((1, S, KI_W), lambda b, q: (b, 0, 0)),
        ],
        out_specs=pl.BlockSpec((1, tq, QA_W), lambda b, q: (b, q, 0)),
        out_shape=jax.ShapeDtypeStruct((B, S, QA_W), BF16),
        scratch_shapes=[
            pltpu.VMEM((LANES, N_HEADS_A * tq), BF16),
            pltpu.VMEM((LANES, IDX_HEADS * tq), BF16),
            pltpu.VMEM((nc, tk, tq), jnp.int32),
            pltpu.VMEM((nc, tk // 2, tq), jnp.int32),
            pltpu.VMEM((nc, tk // 2, tq), jnp.int32),
            pltpu.VMEM((nc, tk // 2, tq), jnp.int32),
            pltpu.VMEM((tk, IDX_HEADS * tq), F32),
            pltpu.VMEM((tk, IDX_HEADS * tq), F32),
            pltpu.VMEM((N_KV_HEADS_A, 1, hpg * tq), F32),
            pltpu.VMEM((N_KV_HEADS_A, LANES, hpg * tq), F32),
        ],
        compiler_params=pltpu.CompilerParams(dimension_semantics=("arbitrary", "arbitrary"),
                                             vmem_limit_bytes=VMEM_LIMIT),
        name="dsa_attn",
    )(qat, qit, wit, k, vt, ki)


CONV_IN_W = 2 * CONV_CH
QM_W = MEM_HEADS * MEM_HEAD_DIM


def _mixer_kernel(x_ref, oa_ref, g_mix_ref, wc_ref, wq_ref, wg_ref, bg_ref, cw_ref, cb_ref, lng_ref, lnb_ref,
                  gqm_ref, kmt_ref, vm_ref, wa_ref, wb_ref, wm_ref, wo_ref, gffn_ref, wrh_ref, wrl_ref, br_ref,
                  x1_ref, h2_ref, comb_ref, ubuf, wbuf, *, tt):
    D = x_ref.shape[-1]

    @pl.when(pl.program_id(1) == 0)
    def _():
        ubuf[0:CONV_HALO, :] = jnp.zeros((CONV_HALO, CONV_CH), F32)

    x = x_ref[0]
    h = (x * lax.rsqrt(jnp.mean(x * x, axis=-1, keepdims=True) + EPS) * g_mix_ref[...]).astype(BF16)

    cin = _dot(h, wc_ref[...])
    u = cin[:, :CONV_CH] * _sigmoid(cin[:, CONV_CH:])

    ubuf[CONV_HALO:CONV_HALO + tt, :] = u
    y = jnp.zeros((tt, CONV_CH), F32) + cb_ref[...]
    first = CONV_HALO - (CONV_WIDTH - 1)
    for b in range(SUBLANES):
        taps = [j for j in range(CONV_WIDTH) if (first + j) % SUBLANES == b]
        if not taps:
            continue
        base = first + taps[0]
        if b == 0:
            window, w0 = ubuf, base
        else:
            span = taps[-1] - taps[0] + tt
            wbuf[0:span, :] = ubuf[base:base + span, :]
            window, w0 = wbuf, 0
        for j in taps:
            off = w0 + j - taps[0]
            y = y + cw_ref[j:j + 1, :] * window[off:off + tt, :]
    halo = ubuf[tt:tt + CONV_HALO, :]
    ubuf[0:CONV_HALO, :] = halo
    mu = jnp.mean(y, axis=-1, keepdims=True)
    yc = y - mu
    var = jnp.mean(yc * yc, axis=-1, keepdims=True)
    yn = yc * lax.rsqrt(var + EPS) * lng_ref[...] + lnb_ref[...]
    o_b = (yn * _sigmoid(yn)).astype(BF16)

    qm = _dot(h, wq_ref[...])
    o_m = []
    for hh in range(MEM_HEADS):
        q = qm[:, hh * MEM_HEAD_DIM:(hh + 1) * MEM_HEAD_DIM]
        q = q * lax.rsqrt(jnp.mean(q * q, axis=-1, keepdims=True) + EPS) * gqm_ref[...] * (MEM_HEAD_DIM ** -0.5)
        s = _dot(q.astype(BF16), kmt_ref[0, hh])
        p = jnp.exp(s - jnp.max(s, axis=-1, keepdims=True))
        l = jnp.sum(p, axis=-1, keepdims=True)
        o_m.append((_dot(p.astype(BF16), vm_ref[0, hh]) / l).astype(BF16))
    o_m = jnp.concatenate(o_m, axis=-1)

    gates = _sigmoid(_dot(h, wg_ref[...]) + bg_ref[...])
    merged = (gates[:, :D] * _dot(oa_ref[0], wa_ref[...])
              + gates[:, D:2 * D] * _dot(o_b, wb_ref[...])
              + gates[:, 2 * D:] * _dot(o_m, wm_ref[...]))
    x1 = x + _dot(merged.astype(BF16), wo_ref[...])
    x1_ref[0] = x1

    h2 = x1 * lax.rsqrt(jnp.mean(x1 * x1, axis=-1, keepdims=True) + EPS) * gffn_ref[...]
    h2_ref[0] = h2.astype(BF16)
    hi, lo = _split_bf16(h2)
    logits = _dot(hi, wrh_ref[...]) + _dot(hi, wrl_ref[...]) + _dot(lo, wrh_ref[...]) + br_ref[...]
    lane = lax.broadcasted_iota(jnp.int32, logits.shape, 1)
    ninf = jnp.float32(-jnp.inf)
    big = jnp.int32(1 << 20)
    is_g = (lane >= N_EXPERTS) & (lane < N_EXPERTS + N_GROUPS)
    gl = jnp.where(is_g, logits, ninf)
    gmax = jnp.max(gl, axis=-1, keepdims=True)
    gsel = jnp.min(jnp.where(gl == gmax, lane, big), axis=-1, keepdims=True) - N_EXPERTS
    p_sel = 1.0 / jnp.sum(jnp.where(is_g, jnp.exp(gl - gmax), 0.0), axis=-1, keepdims=True)
    in_grp = (lane < N_EXPERTS) & ((lane >> 3) == gsel)
    el = jnp.where(in_grp, logits, ninf)
    v1 = jnp.max(el, axis=-1, keepdims=True)
    i1 = jnp.min(jnp.where(el == v1, lane, big), axis=-1, keepdims=True)
    el2 = jnp.where(lane == i1, ninf, el)
    v2 = jnp.max(el2, axis=-1, keepdims=True)
    i2 = jnp.min(jnp.where(el2 == v2, lane, big), axis=-1, keepdims=True)
    e2 = jnp.exp(v2 - v1)
    den = 1.0 + e2
    comb_ref[0] = jnp.where(lane == i1, (1.0 / den) * p_sel,
                            jnp.where(lane == i2, (e2 / den) * p_sel,
                                      jnp.where(lane == N_EXPERTS, gsel.astype(F32), 0.0)))


def _mixer(x, oa, g_mix, wc, wq, wg, bg, cw, cb, lng, lnb, gqm, kmt, vm, wa, wb, wm, wo, gffn, wrh, wrl, br, tt):
    B, S, D = x.shape
    M = kmt.shape[-1]
    const = lambda b, s: (0, 0)
    tile = lambda w: pl.BlockSpec((1, tt, w), lambda b, s: (b, s, 0))
    full = lambda a: pl.BlockSpec(a.shape, const)
    return pl.pallas_call(
        functools.partial(_mixer_kernel, tt=tt),
        grid=(B, S // tt),
        in_specs=[
            tile(D), tile(QA_W), full(g_mix), full(wc), full(wq), full(wg), full(bg), full(cw), full(cb),
            full(lng), full(lnb), full(gqm),
            pl.BlockSpec((1, MEM_HEADS, MEM_HEAD_DIM, M), lambda b, s: (b, 0, 0, 0)),
            pl.BlockSpec((1, MEM_HEADS, M, MEM_HEAD_DIM), lambda b, s: (b, 0, 0, 0)),
            full(wa), full(wb), full(wm), full(wo), full(gffn), full(wrh), full(wrl), full(br),
        ],
        out_specs=[tile(D), tile(D), tile(LANES)],
        out_shape=[
            jax.ShapeDtypeStruct((B, S, D), F32),
            jax.ShapeDtypeStruct((B, S, D), BF16),
            jax.ShapeDtypeStruct((B, S, LANES), F32),
        ],
        scratch_shapes=[pltpu.VMEM((CONV_HALO + tt, CONV_CH), F32),
                        pltpu.VMEM((CONV_HALO + tt, CONV_CH), F32)],
        compiler_params=pltpu.CompilerParams(dimension_semantics=("arbitrary", "arbitrary"),
                                             vmem_limit_bytes=VMEM_LIMIT),
        name="mixer",
    )(x, oa, g_mix, wc, wq, wg, bg, cw, cb, lng, lnb, gqm, kmt, vm, wa, wb, wm, wo, gffn, wrh, wrl, br)


MOE_BLK = 256
MOE_EXPERTS_PER_STEP = 4


def _moe_kernel(h2_ref, x1_ref, comb_ref, ltri_ref, wup_ref, wdn_ref, o_ref,
                hs_s, combs_s, ys_s, pt_s, seg_s, *, tm, rows):
    n = pl.program_id(1)
    lane = lax.broadcasted_iota(jnp.int32, (tm, LANES), 1)

    @pl.when(n == 0)
    def _():
        comb = comb_ref[...]
        gid = comb[:, N_EXPERTS:N_EXPERTS + 1].astype(jnp.int32)
        member = lane == gid
        onehot = jnp.where(member, 1.0, 0.0)
        before = _dot(ltri_ref[...], onehot.astype(BF16))
        rank = jnp.sum(jnp.where(member, before, 0.0), axis=-1, keepdims=True)
        count = jnp.sum(onehot, axis=0, keepdims=True)
        lane1 = lax.broadcasted_iota(jnp.int32, (1, LANES), 1)
        start_blk = jnp.int32(0)
        start_row = jnp.zeros((1, LANES), F32)
        for g in range(N_GROUPS):
            cnt_g = jnp.sum(jnp.where(lane1 == g, count, 0.0)).astype(jnp.int32)
            nblk_g = (cnt_g + (MOE_BLK - 1)) // MOE_BLK
            seg_s[g] = start_blk
            seg_s[N_GROUPS + g] = nblk_g
            start_row = jnp.where(lane1 == g, (start_blk * MOE_BLK).astype(F32), start_row)
            start_blk = start_blk + nblk_g
        dest = jnp.sum(jnp.where(member, start_row, 0.0), axis=-1, keepdims=True) + rank
        seg_s[2 * N_GROUPS] = start_blk
        dest_i = dest.astype(jnp.int32)
        dest_row = jnp.broadcast_to(dest, (tm, LANES)).T[0:1, :].astype(jnp.int32)
        c_hi, c_lo = _split_bf16(comb)
        col = lax.broadcasted_iota(jnp.int32, (tm, MOE_BLK), 1)
        row = lax.broadcasted_iota(jnp.int32, (MOE_BLK, tm), 0)

        def permute_block(b, carry):
            r0 = pl.multiple_of(b * MOE_BLK, MOE_BLK)
            pt_s[b] = jnp.where(col + r0 == dest_i, 1.0, 0.0).astype(BF16)
            perm = jnp.where(row + r0 == dest_row, 1.0, 0.0).astype(BF16)
            hs_s[pl.ds(r0, MOE_BLK), :] = _dot(perm, h2_ref[...]).astype(BF16)
            combs_s[pl.ds(r0, MOE_BLK), :] = _dot(perm, c_hi) + _dot(perm, c_lo)
            ys_s[pl.ds(r0, MOE_BLK), :] = jnp.zeros((MOE_BLK, ys_s.shape[1]), F32)
            return carry

        lax.fori_loop(0, start_blk, permute_block, 0)

    g = (n * MOE_EXPERTS_PER_STEP) // EXPERTS_PER_GROUP
    first_blk = seg_s[g]
    lane_b = lax.broadcasted_iota(jnp.int32, (MOE_BLK, LANES), 1)

    def expert_block(b, carry):
        r0 = pl.multiple_of((first_blk + b) * MOE_BLK, MOE_BLK)
        hb = hs_s[pl.ds(r0, MOE_BLK), :]
        cb = combs_s[pl.ds(r0, MOE_BLK), :]
        acts = []
        for i in range(MOE_EXPERTS_PER_STEP):
            up = _dot(hb, wup_ref[i])
            a = up[:, :D_FF_EXPERT]
            c = jnp.sum(jnp.where(lane_b == n * MOE_EXPERTS_PER_STEP + i, cb, 0.0), axis=-1, keepdims=True)
            acts.append((a * _sigmoid(a) * up[:, D_FF_EXPERT:] * c).astype(BF16))
        w_dn = wdn_ref[...].reshape(MOE_EXPERTS_PER_STEP * D_FF_EXPERT, wdn_ref.shape[-1])
        ys_s[pl.ds(r0, MOE_BLK), :] += _dot(jnp.concatenate(acts, axis=-1), w_dn)
        return carry

    lax.fori_loop(0, seg_s[N_GROUPS + g], expert_block, 0)

    @pl.when(n == N_EXPERTS // MOE_EXPERTS_PER_STEP - 1)
    def _():
        o_ref[...] = x1_ref[...]

        def unpermute_block(b, carry):
            r0 = pl.multiple_of(b * MOE_BLK, MOE_BLK)
            o_ref[...] += _dot(pt_s[b], ys_s[pl.ds(r0, MOE_BLK), :].astype(BF16))
            return carry

        lax.fori_loop(0, seg_s[2 * N_GROUPS], unpermute_block, 0)


def _moe(h2, x1, comb, w_up, w_down, tm):
    T, D = h2.shape
    rows = tm + N_GROUPS * MOE_BLK
    ltri = jnp.asarray(np.tril(np.ones((tm, tm), np.float32), -1), BF16)
    return pl.pallas_call(
        functools.partial(_moe_kernel, tm=tm, rows=rows),
        grid=(T // tm, N_EXPERTS // MOE_EXPERTS_PER_STEP),
        in_specs=[
            pl.BlockSpec((tm, D), lambda t, n: (t, 0)),
            pl.BlockSpec((tm, D), lambda t, n: (t, 0)),
            pl.BlockSpec((tm, LANES), lambda t, n: (t, 0)),
            pl.BlockSpec((tm, tm), lambda t, n: (0, 0)),
            pl.BlockSpec((MOE_EXPERTS_PER_STEP, D, 2 * D_FF_EXPERT), lambda t, n: (n, 0, 0)),
            pl.BlockSpec((MOE_EXPERTS_PER_STEP, D_FF_EXPERT, D), lambda t, n: (n, 0, 0)),
        ],
        out_specs=pl.BlockSpec((tm, D), lambda t, n: (t, 0)),
        out_shape=jax.ShapeDtypeStruct((T, D), F32),
        scratch_shapes=[
            pltpu.VMEM((rows, D), BF16),
            pltpu.VMEM((rows, LANES), F32),
            pltpu.VMEM((rows, D), F32),
            pltpu.VMEM((rows // MOE_BLK, tm, MOE_BLK), BF16),
            pltpu.SMEM((2 * N_GROUPS + 1,), jnp.int32),
        ],
        compiler_params=pltpu.CompilerParams(dimension_semantics=("arbitrary", "arbitrary"),
                                             vmem_limit_bytes=VMEM_LIMIT),
        name="moe",
    )(h2, x1, comb, ltri, w_up, w_down)


def _pick_tile(n, pref):
    t = min(n, pref)
    assert n % t == 0, (n, t)
    return t


def _rope_tables(S):
    half = HEAD_DIM_A // 2
    inv = ROPE_THETA ** (-jnp.arange(half, dtype=F32) / half)
    ang = jnp.arange(S, dtype=jnp.int32).astype(F32)[:, None] * inv[None, :]
    cos, sin = jnp.cos(ang), jnp.sin(ang)
    cos_t = jnp.concatenate([cos, cos, cos, cos], axis=-1)
    sin_t = jnp.concatenate([-sin, sin, -sin, sin], axis=-1)
    return cos_t, sin_t


def _layer(x, mem, g_mix, w_in, b_gate, g_qa, g_ka, g_idx_k, conv_w, conv_b, ln_g, ln_b,
           g_mem, w_mem_kv, g_qm, g_km, w_br_a, w_br_b, w_br_m, w_o, g_ffn,
           w_rg, b_rg, w_re, b_re, w_up, w_down):
    B, S, D = x.shape
    topk = min(TOPK_MAX, S // 4)
    tk = _pick_tile(S, 512)
    tq = _pick_tile(tk, 128)
    tt = _pick_tile(S, 512)
    tm = _pick_tile(B * S, 1024)

    sizes = (QA_W, N_KV_HEADS_A * HEAD_DIM_A, N_KV_HEADS_A * HEAD_DIM_A, QI_W, IDX_DIM, IDX_HEADS,
             CONV_IN_W, QM_W, N_BRANCHES * D)
    offs = np.concatenate([[0], np.cumsum(sizes)])
    col = lambda i: w_in[:, offs[i]:offs[i + 1]]
    w_qa, w_ka, w_va, w_qi, w_ki, w_wi, w_conv, w_qm, w_gate = (col(i) for i in range(9))
    hd = HEAD_DIM_A
    zeros_hd = jnp.zeros((D, hd), F32)
    ka_dup = [w_ka[:, g * hd:(g + 1) * hd] for g in range(N_KV_HEADS_A) for _ in range(2)]
    va_aug = [w for g in range(N_KV_HEADS_A) for w in (w_va[:, g * hd:(g + 1) * hd], zeros_hd)]
    w_dsa = jnp.concatenate(
        [w_qa, w_qi] + ka_dup + va_aug + [w_ki, w_ki, w_wi, jnp.zeros((D, WI_W - IDX_HEADS), F32)], axis=1).astype(BF16)
    vbias = jnp.tile(jnp.concatenate([jnp.zeros((hd,), F32), jnp.ones((hd,), F32)]), N_KV_HEADS_A).reshape(1, VA_W)
    seg = np.arange(QA_W) // hd
    bd = jnp.asarray(seg[:, None] == seg[None, :], BF16)
    cos_t, sin_t = _rope_tables(S)

    kmt, vm = _mem_kv(mem, g_mem, w_mem_kv, g_km)
    qat, qit, k, vt, ki, wit = _dsa_proj(
        x, g_mix.reshape(1, D), w_dsa, vbias,
        jnp.tile(g_qa, N_HEADS_A).reshape(1, QA_W), jnp.tile(g_ka, KA_W // hd).reshape(1, KA_W),
        jnp.tile(g_idx_k, KI_W // IDX_DIM).reshape(1, KI_W), cos_t, sin_t, bd, tk)
    o_a = _dsa_attn(qat, qit, wit, k, vt, ki, tq, topk)

    w_r = jnp.concatenate([w_re, w_rg, jnp.zeros((D, LANES - N_EXPERTS - N_GROUPS), F32)], axis=1)
    b_r = jnp.concatenate([b_re, b_rg, jnp.zeros((LANES - N_EXPERTS - N_GROUPS,), F32)]).reshape(1, LANES)
    w_r_hi = w_r.astype(BF16)
    w_r_lo = (w_r - w_r_hi.astype(F32)).astype(BF16)
    x1, h2, comb = _mixer(
        x, o_a, g_mix.reshape(1, D), w_conv.astype(BF16), w_qm.astype(BF16), w_gate.astype(BF16),
        b_gate.reshape(1, -1), conv_w.reshape(CONV_WIDTH, CONV_CH), conv_b.reshape(1, CONV_CH),
        ln_g.reshape(1, CONV_CH), ln_b.reshape(1, CONV_CH), g_qm.reshape(1, MEM_HEAD_DIM), kmt, vm,
        w_br_a.astype(BF16), w_br_b.astype(BF16), w_br_m.astype(BF16), w_o.astype(BF16),
        g_ffn.reshape(1, D), w_r_hi, w_r_lo, b_r, tt)

    out = _moe(h2.reshape(B * S, D), x1.reshape(B * S, D), comb.reshape(B * S, LANES),
               w_up.astype(BF16), w_down.astype(BF16), tm)
    return out.reshape(B, S, D)


def kernel(x, mem, g_mix, w_in, b_gate, g_qa, g_ka, g_idx_k, conv_w, conv_b, ln_g, ln_b, g_mem, w_mem_kv, g_qm, g_km,
           w_br_a, w_br_b, w_br_m, w_o, g_ffn, w_rg, b_rg, w_re, b_re, w_up, w_down):
    params = (g_mix, w_in, b_gate, g_qa, g_ka, g_idx_k, conv_w, conv_b, ln_g, ln_b, g_mem, w_mem_kv, g_qm, g_km,
              w_br_a, w_br_b, w_br_m, w_o, g_ffn, w_rg, b_rg, w_re, b_re, w_up, w_down)
    for l in range(g_mix.shape[0]):
        x = _layer(x, mem, *(p[l] for p in params))
    return x
```

```python
import functools

import jax
import jax.numpy as jnp
import numpy as np
from jax import lax
from jax.experimental import pallas as pl
from jax.experimental.pallas import tpu as pltpu

N_HEADS_A = 8
N_KV_HEADS_A = 2
HEAD_DIM_A = 64
IDX_HEADS = 8
IDX_DIM = 64
TOPK_MAX = 256
CONV_CH = 512
CONV_WIDTH = 31
MEM_HEADS = 4
MEM_HEAD_DIM = 128
N_BRANCHES = 3
N_GROUPS = 4
EXPERTS_PER_GROUP = 8
N_EXPERTS = N_GROUPS * EXPERTS_PER_GROUP
D_FF_EXPERT = 256
ROPE_THETA = 10000.0
EPS = 1e-6

LANES = 128
SUBLANES = 8
VMEM_LIMIT = 56 * 1024 * 1024

INT_MIN = -2 ** 31
LOG2_E = 1.4426950408889634
HALF_BIAS = 2 ** 15
HALF_WORD = 2 ** 16
CHUNKS_PER_TRIP = 4
NEG_BIG = -1e30
CONV_HALO = 32

BF16 = jnp.bfloat16
F32 = jnp.float32


def _dot(a, b):
    return jnp.dot(a, b, preferred_element_type=F32)


def _sigmoid(x):
    return 0.5 * jnp.tanh(0.5 * x) + 0.5


def _split_bf16(x):
    hi = x.astype(BF16)
    lo = (x - hi.astype(F32)).astype(BF16)
    return hi, lo


def _seg_sum(xsq, ones_bd):
    hi, lo = _split_bf16(xsq)
    return _dot(hi, ones_bd) + _dot(lo, ones_bd)


def _rot_half(x, head_dim):
    n = x.shape[-1]
    half = head_dim // 2
    lane = lax.broadcasted_iota(jnp.int32, x.shape, x.ndim - 1)
    is_lo = (lane & (head_dim - 1)) < half
    return jnp.where(is_lo, pltpu.roll(x, n - half, x.ndim - 1), pltpu.roll(x, half, x.ndim - 1))


def _tile_lanes(x, reps):
    return jnp.concatenate([x] * reps, axis=-1) if reps > 1 else x


def _mem_kv_kernel(mem_ref, g_mem_ref, w_ref, g_km_ref, kt_ref, v_ref):
    m = mem_ref[0]
    hm = m * lax.rsqrt(jnp.mean(m * m, axis=-1, keepdims=True) + EPS) * g_mem_ref[...]
    kv = _dot(hm.astype(BF16), w_ref[...])
    hd = MEM_HEADS * MEM_HEAD_DIM
    for h in range(MEM_HEADS):
        k = kv[:, h * MEM_HEAD_DIM:(h + 1) * MEM_HEAD_DIM]
        k = k * lax.rsqrt(jnp.mean(k * k, axis=-1, keepdims=True) + EPS) * g_km_ref[...]
        kt_ref[0, h] = k.T.astype(BF16)
        v_ref[0, h] = kv[:, hd + h * MEM_HEAD_DIM: hd + (h + 1) * MEM_HEAD_DIM].astype(BF16)


def _mem_kv(mem, g_mem, w_mem_kv, g_km):
    B, M, D = mem.shape
    hd2 = 2 * MEM_HEADS * MEM_HEAD_DIM
    return pl.pallas_call(
        _mem_kv_kernel,
        grid=(B,),
        in_specs=[
            pl.BlockSpec((1, M, D), lambda b: (b, 0, 0)),
            pl.BlockSpec((1, D), lambda b: (0, 0)),
            pl.BlockSpec((D, hd2), lambda b: (0, 0)),
            pl.BlockSpec((1, MEM_HEAD_DIM), lambda b: (0, 0)),
        ],
        out_specs=[
            pl.BlockSpec((1, MEM_HEADS, MEM_HEAD_DIM, M), lambda b: (b, 0, 0, 0)),
            pl.BlockSpec((1, MEM_HEADS, M, MEM_HEAD_DIM), lambda b: (b, 0, 0, 0)),
        ],
        out_shape=[
            jax.ShapeDtypeStruct((B, MEM_HEADS, MEM_HEAD_DIM, M), BF16),
            jax.ShapeDtypeStruct((B, MEM_HEADS, M, MEM_HEAD_DIM), BF16),
        ],
        compiler_params=pltpu.CompilerParams(dimension_semantics=("arbitrary",)),
        name="mem_kv",
    )(mem, g_mem.reshape(1, D), w_mem_kv.astype(BF16), g_km.reshape(1, MEM_HEAD_DIM))


QA_W = N_HEADS_A * HEAD_DIM_A
QI_W = IDX_HEADS * IDX_DIM
KA_W = 2 * N_KV_HEADS_A * HEAD_DIM_A
VA_W = 2 * N_KV_HEADS_A * HEAD_DIM_A
KI_W = 2 * IDX_DIM
WI_W = LANES
DSA_COLS = QA_W + QI_W + KA_W + VA_W + KI_W + WI_W


def _dsa_proj_kernel(x_ref, g_mix_ref, w_ref, vbias_ref, gq_ref, gk_ref, gki_ref, cos_ref, sin_ref, bd_ref,
                     qat_ref, qit_ref, k_ref, vt_ref, ki_ref, wit_ref):
    x = x_ref[0]
    h = x * lax.rsqrt(jnp.mean(x * x, axis=-1, keepdims=True) + EPS) * g_mix_ref[...]
    p = _dot(h.astype(BF16), w_ref[...])
    cos1 = cos_ref[...]
    sin1 = sin_ref[...]
    cos4, sin4 = _tile_lanes(cos1, 4), _tile_lanes(sin1, 4)
    cos2, sin2 = _tile_lanes(cos1, 2), _tile_lanes(sin1, 2)
    bd = bd_ref[...]
    inv_hd = 1.0 / HEAD_DIM_A

    def rope(v, c, s):
        return v * c + _rot_half(v, HEAD_DIM_A) * s

    o = 0
    qa = p[:, o:o + QA_W]; o += QA_W
    qa = qa * lax.rsqrt(_seg_sum(qa * qa, bd) * inv_hd + EPS) * gq_ref[...]
    qat_ref[0] = (rope(qa, cos4, sin4) * (HEAD_DIM_A ** -0.5 * LOG2_E)).T.astype(BF16)

    qi = p[:, o:o + QI_W]; o += QI_W
    qit_ref[0] = (rope(qi, cos4, sin4) * (IDX_DIM ** -0.5)).T.astype(BF16)

    ka = p[:, o:o + KA_W]; o += KA_W
    ka = ka * lax.rsqrt(_seg_sum(ka * ka, bd[:KA_W, :KA_W]) * inv_hd + EPS) * gk_ref[...]
    k_ref[0] = rope(ka, cos2, sin2).astype(BF16)

    vt_ref[0, 0] = (p[:, o:o + VA_W] + vbias_ref[...]).T.astype(BF16); o += VA_W

    ki = p[:, o:o + KI_W]; o += KI_W
    ki = ki * lax.rsqrt(_seg_sum(ki * ki, bd[:KI_W, :KI_W]) * (1.0 / IDX_DIM) + EPS) * gki_ref[...]
    ki_ref[0] = rope(ki, cos1, sin1).astype(BF16)

    wit_ref[0] = (p[:, o:o + WI_W] * (IDX_HEADS ** -0.5)).T[:IDX_HEADS, :]


def _dsa_proj(x, g_mix, w_dsa, vbias, g_qa, g_ka, g_idx_k, cos_t, sin_t, bd, tt):
    B, S, D = x.shape
    nt = S // tt
    const = lambda b, s: (0, 0)
    return pl.pallas_call(
        _dsa_proj_kernel,
        grid=(B, nt),
        in_specs=[
            pl.BlockSpec((1, tt, D), lambda b, s: (b, s, 0)),
            pl.BlockSpec((1, D), const),
            pl.BlockSpec((D, DSA_COLS), const),
            pl.BlockSpec((1, VA_W), const),
            pl.BlockSpec((1, QA_W), const),
            pl.BlockSpec((1, KA_W), const),
            pl.BlockSpec((1, KI_W), const),
            pl.BlockSpec((tt, LANES), lambda b, s: (s, 0)),
            pl.BlockSpec((tt, LANES), lambda b, s: (s, 0)),
            pl.BlockSpec((QA_W, QA_W), const),
        ],
        out_specs=[
            pl.BlockSpec((1, QA_W, tt), lambda b, s: (b, 0, s)),
            pl.BlockSpec((1, QI_W, tt), lambda b, s: (b, 0, s)),
            pl.BlockSpec((1, tt, KA_W), lambda b, s: (b, s, 0)),
            pl.BlockSpec((1, 1, VA_W, tt), lambda b, s: (b, s, 0, 0)),
            pl.BlockSpec((1, tt, KI_W), lambda b, s: (b, s, 0)),
            pl.BlockSpec((1, IDX_HEADS, tt), lambda b, s: (b, 0, s)),
        ],
        out_shape=[
            jax.ShapeDtypeStruct((B, QA_W, S), BF16),
            jax.ShapeDtypeStruct((B, QI_W, S), BF16),
            jax.ShapeDtypeStruct((B, S, KA_W), BF16),
            jax.ShapeDtypeStruct((B, nt, VA_W, tt), BF16),
            jax.ShapeDtypeStruct((B, S, KI_W), BF16),
            jax.ShapeDtypeStruct((B, IDX_HEADS, S), F32),
        ],
        compiler_params=pltpu.CompilerParams(dimension_semantics=("arbitrary", "arbitrary"),
                                             vmem_limit_bytes=VMEM_LIMIT),
        name="dsa_proj",
    )(x, g_mix, w_dsa, vbias, g_qa, g_ka, g_idx_k, cos_t, sin_t, bd)


def _key_of(f):
    bits = lax.bitcast_convert_type(f, jnp.int32)
    return bits ^ ((bits >> 31) & jnp.int32(0x7FFFFFFF))


def _pack_high_halves(key):
    m = key.shape[0] // 2
    return lax.shift_right_logical(key[:m], 16) | (key[m:] & jnp.int32(-HALF_WORD))


def _pack_low_halves(key):
    m = key.shape[0] // 2
    return ((key[:m] & 0xFFFF) | (key[m:] << 16)) ^ jnp.int32(-HALF_BIAS * HALF_WORD + HALF_BIAS)


def _halves_of(words):
    return pltpu.bitcast(words, jnp.int16)


def _pipelined_chunks(nchunks, last_chunk, matmul, finish):
    matmul(0, 0)

    def straight(c0, n, last):
        for j in range(n):
            if j + 1 < n or not last:
                matmul(c0 + j + 1, (j + 1) % 2)
            finish(c0 + j, j % 2)

    def trip(i, carry):
        straight(CHUNKS_PER_TRIP * i, CHUNKS_PER_TRIP, False)
        return carry

    trips = last_chunk // CHUNKS_PER_TRIP
    lax.fori_loop(0, trips, trip, 0)
    for r in range(CHUNKS_PER_TRIP):
        @pl.when(last_chunk % CHUNKS_PER_TRIP == r)
        def _(r=r):
            straight(trips * CHUNKS_PER_TRIP, r + 1, True)


def _dsa_attn_kernel(qat_ref, qit_ref, wit_ref, k_ref, vt_ref, ki_ref, o_ref,
                     qa_s, qi_s, keys_s, khi_s, klo_s, zlo_s, buf0, buf1, m_s, acc_s, *, tq, tk, topk, idx_bits):
    qt = pl.program_id(1)
    q0 = qt * tq
    nchunks = q0 // tk + 1
    last_chunk = nchunks - 1
    hpg = N_HEADS_A // N_KV_HEADS_A
    gw = hpg * tq
    half_pairs = hpg // 2
    bufs = (buf0, buf1)
    dim = lax.broadcasted_iota(jnp.int32, (LANES, tq), 0)
    lo_half = dim < HEAD_DIM_A

    for j in range(IDX_HEADS // 2):
        pair = qit_ref[0, j * LANES:(j + 1) * LANES, :]
        qi_s[:, (2 * j) * tq:(2 * j + 1) * tq] = jnp.where(lo_half, pair, jnp.zeros_like(pair))
        qi_s[:, (2 * j + 1) * tq:(2 * j + 2) * tq] = jnp.where(lo_half, jnp.zeros_like(pair), pair)
    for g in range(N_KV_HEADS_A):
        for jj in range(half_pairs):
            pair = qat_ref[0, (g * half_pairs + jj) * LANES:(g * half_pairs + jj + 1) * LANES, :]
            qa_s[:, g * gw + jj * tq:g * gw + (jj + 1) * tq] = jnp.where(lo_half, pair, jnp.zeros_like(pair))
            qa_s[:, g * gw + (half_pairs + jj) * tq:g * gw + (half_pairs + jj + 1) * tq] = (
                jnp.where(lo_half, jnp.zeros_like(pair), pair))

    key_in_chunk = lax.broadcasted_iota(jnp.int32, (tk, tq), 0)
    q_pos = q0 + lax.broadcasted_iota(jnp.int32, (tk, tq), 1)

    def score_chunk(c, diagonal=False):
        r0 = pl.multiple_of(c * tk, tk)
        logits = _dot(ki_ref[0, pl.ds(r0, tk), :], qi_s[...])
        score = jnp.zeros((tk, tq), F32)
        for h in range(IDX_HEADS):
            score = score + wit_ref[0, h:h + 1, :] * jnp.maximum(logits[:, h * tq:(h + 1) * tq], 0.0)
        key = _key_of(score)
        if diagonal:
            key = jnp.where(key_in_chunk + c * tk <= q_pos, key, jnp.int32(INT_MIN))
        keys_s[c] = key
        khi_s[c] = _pack_high_halves(key)
        klo_s[c] = _pack_low_halves(key)

    def score_trip(i, carry):
        for j in range(CHUNKS_PER_TRIP):
            score_chunk(CHUNKS_PER_TRIP * i + j)
        return carry

    lax.fori_loop(0, last_chunk // CHUNKS_PER_TRIP, score_trip, 0)
    for r in range(CHUNKS_PER_TRIP):
        @pl.when(last_chunk % CHUNKS_PER_TRIP == r)
        def _(r=r):
            for j in range(r):
                score_chunk(last_chunk - r + j)
            score_chunk(last_chunk, diagonal=True)

    def count_keys(pred_fn):
        def one(c, acc):
            hit = pred_fn(keys_s[c], c * tk).astype(jnp.int32)
            return acc + jnp.sum(hit.reshape(tk // 8, 8, tq), axis=0)

        def two(i, acc):
            return one(2 * i + 1, one(2 * i, acc))

        part = lax.fori_loop(0, nchunks // 2, two, jnp.zeros((8, tq), jnp.int32))
        part = lax.cond(nchunks % 2 == 1, lambda acc: one(last_chunk, acc), lambda acc: acc, part)
        return jnp.sum(part, axis=0, keepdims=True)

    total = q0 + lax.broadcasted_iota(jnp.int32, (1, tq), 1) + 1
    has_tau = total >= topk

    def count_halves(src_s, cand, odd):
        cand16 = _halves_of(jnp.broadcast_to((cand & 0xFFFF) | (cand << 16), (SUBLANES, tq)))

        def one(c, acc):
            half = _halves_of(src_s[c])
            hits = [jnp.where(half[j * 2 * SUBLANES:(j + 1) * 2 * SUBLANES, :] >= cand16, jnp.int16(1), jnp.int16(0))
                    for j in range(tk // (2 * SUBLANES))]
            while len(hits) > 1:
                hits = [a + b for a, b in zip(hits[::2], hits[1::2])]
            return acc + hits[0]

        def two(i, acc):
            return one(2 * i + 1, one(2 * i, acc))

        acc = lax.fori_loop(0, nchunks // 2, two, jnp.zeros((2 * SUBLANES, tq), jnp.int16))
        if odd:
            acc = one(last_chunk, acc)
        words = pltpu.bitcast(acc, jnp.int32)
        return jnp.sum((words & 0xFFFF) + (words >> 16), axis=0, keepdims=True)

    def radix16(src_s, want, odd):
        def step(i, state):
            v, cnt_ge, cnt_gt = state
            cand = v + (jnp.int32(1) << (15 - i))
            c = count_halves(src_s, cand, odd)
            up = c >= want
            return jnp.where(up, cand, v), jnp.where(up, c, cnt_ge), jnp.where(up, cnt_gt, c)
        zero = jnp.zeros((1, tq), jnp.int32)
        return lax.fori_loop(0, 16, step, (jnp.full((1, tq), -HALF_BIAS, jnp.int32), zero, zero))

    def select(odd):
        tau_hi, cnt_hi_ge, cnt_hi_gt = radix16(khi_s, topk, odd)
        tau_hi16 = _halves_of(jnp.broadcast_to((tau_hi & 0xFFFF) | (tau_hi << 16), (SUBLANES, tq)))

        def low_of_bucket(c, carry):
            hi, lo = _halves_of(khi_s[c]), _halves_of(klo_s[c])
            for j in range(tk // (2 * SUBLANES)):
                rows = slice(j * 2 * SUBLANES, (j + 1) * 2 * SUBLANES)
                zlo_s[c, j * SUBLANES:(j + 1) * SUBLANES, :] = pltpu.bitcast(
                    jnp.where(hi[rows, :] == tau_hi16, lo[rows, :], jnp.int16(-HALF_BIAS)), jnp.int32)
            return carry

        lax.fori_loop(0, nchunks, low_of_bucket, 0)
        tau_lo, cnt_lo_ge, _ = radix16(zlo_s, topk - cnt_hi_gt, odd)
        return tau_hi, cnt_hi_ge, cnt_hi_gt, tau_lo, cnt_lo_ge

    tau_hi, cnt_hi_ge, cnt_hi_gt, tau_lo, cnt_lo_ge = lax.cond(
        nchunks % 2 == 1, lambda: select(True), lambda: select(False))
    tau = (tau_hi << 16) | ((tau_lo + HALF_BIAS) & 0xFFFF)
    cnt = cnt_hi_gt + jnp.where(tau_lo == -HALF_BIAS, cnt_hi_ge - cnt_hi_gt, cnt_lo_ge)
    tau = jnp.where(has_tau, jnp.maximum(tau, jnp.int32(INT_MIN + 1)), jnp.int32(INT_MIN))

    tied = has_tau & (cnt > topk)

    def tie_cut():
        need = topk - count_keys(lambda k, _: k > tau)

        def idx_step(i, j):
            bit = jnp.int32(1) << (idx_bits - 1 - i)
            test = j + bit - 1
            f = count_keys(lambda k, base: (k == tau) & (base + key_in_chunk <= test))
            return jnp.where(f < need, j + bit, j)
        return lax.fori_loop(0, idx_bits, idx_step, jnp.zeros((1, tq), jnp.int32))

    jcut = lax.cond(jnp.max(jnp.where(tied, 1, 0)) > 0, tie_cut, lambda: jnp.zeros((1, tq), jnp.int32))
    jcut = jnp.where(tied, jcut, jnp.where(has_tau, jnp.int32(2 ** 30), jnp.int32(-1)))

    m_s[...] = jnp.full(m_s.shape, NEG_BIG, F32)
    acc_s[...] = jnp.zeros(acc_s.shape, F32)

    def attn_matmul(c, slot):
        r0 = pl.multiple_of(c * tk, tk)
        k = keys_s[c]
        kpos = key_in_chunk + c * tk
        bias = jnp.where(k > tau, 0.0, jnp.where(k == tau, jnp.where(kpos <= jcut, 0.0, NEG_BIG), NEG_BIG))
        bias4 = jnp.concatenate([bias] * hpg, axis=1)
        for g in range(N_KV_HEADS_A):
            bufs[slot][:, g * gw:(g + 1) * gw] = _dot(k_ref[0, pl.ds(r0, tk), g * LANES:(g + 1) * LANES],
                                                      qa_s[:, g * gw:(g + 1) * gw]) + bias4

    def attn_finish(c, slot):
        for g in range(N_KV_HEADS_A):
            m_old = m_s[g]
            m_new = jnp.maximum(m_old, jnp.max(bufs[slot][:, g * gw:(g + 1) * gw], axis=0, keepdims=True))
            p = jnp.exp2(bufs[slot][:, g * gw:(g + 1) * gw] - m_new)
            alpha = jnp.exp2(m_old - m_new)
            pv = _dot(vt_ref[0, c, g * LANES:(g + 1) * LANES, :], p.astype(BF16))
            acc_s[g] = alpha * acc_s[g] + pv
            m_s[g] = m_new

    _pipelined_chunks(nchunks, last_chunk, attn_matmul, attn_finish)

    for g in range(N_KV_HEADS_A):
        for jj in range(half_pairs):
            a_e = acc_s[g, :, jj * tq:(jj + 1) * tq]
            a_o = acc_s[g, :, (half_pairs + jj) * tq:(half_pairs + jj + 1) * tq]
            o_pair = jnp.concatenate([a_e[:HEAD_DIM_A] / a_e[HEAD_DIM_A:], a_o[:HEAD_DIM_A] / a_o[HEAD_DIM_A:]], axis=0)
            j = g * half_pairs + jj
            o_ref[0, :, j * LANES:(j + 1) * LANES] = o_pair.T.astype(BF16)


def _dsa_attn(qat, qit, wit, k, vt, ki, tq, topk):
    B, _, S = qat.shape
    nc, tk = vt.shape[1], vt.shape[3]
    hpg = N_HEADS_A // N_KV_HEADS_A
    idx_bits = max(1, int(np.ceil(np.log2(S))))
    assert IDX_HEADS == N_HEADS_A
    assert CHUNKS_PER_TRIP % 2 == 0
    kernel = functools.partial(_dsa_attn_kernel, tq=tq, tk=tk, topk=topk, idx_bits=idx_bits)
    return pl.pallas_call(
        kernel,
        grid=(B, S // tq),
        in_specs=[
            pl.BlockSpec((1, QA_W, tq), lambda b, q: (b, 0, q)),
            pl.BlockSpec((1, QI_W, tq), lambda b, q: (b, 0, q)),
            pl.BlockSpec((1, IDX_HEADS, tq), lambda b, q: (b, 0, q)),
            pl.BlockSpec((1, S, KA_W), lambda b, q: (b, 0, 0)),
            pl.BlockSpec((1, nc, VA_W, tk), lambda b, q: (b, 0, 0, 0)),
            pl.BlockSpec((1, S, KI_W), lambda b, q: (b, 0, 0)),
        ],
        out_specs=pl.BlockSpec((1, tq, QA_W), lambda b, q: (b, q, 0)),
        out_shape=jax.ShapeDtypeStruct((B, S, QA_W), BF16),
        scratch_shapes=[
            pltpu.VMEM((LANES, N_HEADS_A * tq), BF16),
            pltpu.VMEM((LANES, IDX_HEADS * tq), BF16),
            pltpu.VMEM((nc, tk, tq), jnp.int32),
            pltpu.VMEM((nc, tk // 2, tq), jnp.int32),
            pltpu.VMEM((nc, tk // 2, tq), jnp.int32),
            pltpu.VMEM((nc, tk // 2, tq), jnp.int32),
            pltpu.VMEM((tk, IDX_HEADS * tq), F32),
            pltpu.VMEM((tk, IDX_HEADS * tq), F32),
            pltpu.VMEM((N_KV_HEADS_A, 1, hpg * tq), F32),
            pltpu.VMEM((N_KV_HEADS_A, LANES, hpg * tq), F32),
        ],
        compiler_params=pltpu.CompilerParams(dimension_semantics=("arbitrary", "arbitrary"),
                                             vmem_limit_bytes=VMEM_LIMIT),
        name="dsa_attn",
    )(qat, qit, wit, k, vt, ki)


CONV_IN_W = 2 * CONV_CH
QM_W = MEM_HEADS * MEM_HEAD_DIM


def _mixer_kernel(x_ref, oa_ref, g_mix_ref, wc_ref, wq_ref, wg_ref, bg_ref, cw_ref, cb_ref, lng_ref, lnb_ref,
                  gqm_ref, kmt_ref, vm_ref, wa_ref, wb_ref, wm_ref, wo_ref, gffn_ref, wrh_ref, wrl_ref, br_ref,
                  x1_ref, h2_ref, comb_ref, ubuf, wbuf, *, tt):
    D = x_ref.shape[-1]

    @pl.when(pl.program_id(1) == 0)
    def _():
        ubuf[0:CONV_HALO, :] = jnp.zeros((CONV_HALO, CONV_CH), F32)

    x = x_ref[0]
    h = (x * lax.rsqrt(jnp.mean(x * x, axis=-1, keepdims=True) + EPS) * g_mix_ref[...]).astype(BF16)

    cin = _dot(h, wc_ref[...])
    u = cin[:, :CONV_CH] * _sigmoid(cin[:, CONV_CH:])

    ubuf[CONV_HALO:CONV_HALO + tt, :] = u
    y = jnp.zeros((tt, CONV_CH), F32) + cb_ref[...]
    first = CONV_HALO - (CONV_WIDTH - 1)
    for b in range(SUBLANES):
        taps = [j for j in range(CONV_WIDTH) if (first + j) % SUBLANES == b]
        if not taps:
            continue
        base = first + taps[0]
        if b == 0:
            window, w0 = ubuf, base
        else:
            span = taps[-1] - taps[0] + tt
            wbuf[0:span, :] = ubuf[base:base + span, :]
            window, w0 = wbuf, 0
        for j in taps:
            off = w0 + j - taps[0]
            y = y + cw_ref[j:j + 1, :] * window[off:off + tt, :]
    halo = ubuf[tt:tt + CONV_HALO, :]
    ubuf[0:CONV_HALO, :] = halo
    mu = jnp.mean(y, axis=-1, keepdims=True)
    yc = y - mu
    var = jnp.mean(yc * yc, axis=-1, keepdims=True)
    yn = yc * lax.rsqrt(var + EPS) * lng_ref[...] + lnb_ref[...]
    o_b = (yn * _sigmoid(yn)).astype(BF16)

    qm = _dot(h, wq_ref[...])
    o_m = []
    for hh in range(MEM_HEADS):
        q = qm[:, hh * MEM_HEAD_DIM:(hh + 1) * MEM_HEAD_DIM]
        q = q * lax.rsqrt(jnp.mean(q * q, axis=-1, keepdims=True) + EPS) * gqm_ref[...] * (MEM_HEAD_DIM ** -0.5)
        s = _dot(q.astype(BF16), kmt_ref[0, hh])
        p = jnp.exp(s - jnp.max(s, axis=-1, keepdims=True))
        l = jnp.sum(p, axis=-1, keepdims=True)
        o_m.append((_dot(p.astype(BF16), vm_ref[0, hh]) / l).astype(BF16))
    o_m = jnp.concatenate(o_m, axis=-1)

    gates = _sigmoid(_dot(h, wg_ref[...]) + bg_ref[...])
    merged = (gates[:, :D] * _dot(oa_ref[0], wa_ref[...])
              + gates[:, D:2 * D] * _dot(o_b, wb_ref[...])
              + gates[:, 2 * D:] * _dot(o_m, wm_ref[...]))
    x1 = x + _dot(merged.astype(BF16), wo_ref[...])
    x1_ref[0] = x1

    h2 = x1 * lax.rsqrt(jnp.mean(x1 * x1, axis=-1, keepdims=True) + EPS) * gffn_ref[...]
    h2_ref[0] = h2.astype(BF16)
    hi, lo = _split_bf16(h2)
    logits = _dot(hi, wrh_ref[...]) + _dot(hi, wrl_ref[...]) + _dot(lo, wrh_ref[...]) + br_ref[...]
    lane = lax.broadcasted_iota(jnp.int32, logits.shape, 1)
    ninf = jnp.float32(-jnp.inf)
    big = jnp.int32(1 << 20)
    is_g = (lane >= N_EXPERTS) & (lane < N_EXPERTS + N_GROUPS)
    gl = jnp.where(is_g, logits, ninf)
    gmax = jnp.max(gl, axis=-1, keepdims=True)
    gsel = jnp.min(jnp.where(gl == gmax, lane, big), axis=-1, keepdims=True) - N_EXPERTS
    p_sel = 1.0 / jnp.sum(jnp.where(is_g, jnp.exp(gl - gmax), 0.0), axis=-1, keepdims=True)
    in_grp = (lane < N_EXPERTS) & ((lane >> 3) == gsel)
    el = jnp.where(in_grp, logits, ninf)
    v1 = jnp.max(el, axis=-1, keepdims=True)
    i1 = jnp.min(jnp.where(el == v1, lane, big), axis=-1, keepdims=True)
    el2 = jnp.where(lane == i1, ninf, el)
    v2 = jnp.max(el2, axis=-1, keepdims=True)
    i2 = jnp.min(jnp.where(el2 == v2, lane, big), axis=-1, keepdims=True)
    e2 = jnp.exp(v2 - v1)
    den = 1.0 + e2
    comb_ref[0] = jnp.where(lane == i1, (1.0 / den) * p_sel,
                            jnp.where(lane == i2, (e2 / den) * p_sel,
                                      jnp.where(lane == N_EXPERTS, gsel.astype(F32), 0.0)))


def _mixer(x, oa, g_mix, wc, wq, wg, bg, cw, cb, lng, lnb, gqm, kmt, vm, wa, wb, wm, wo, gffn, wrh, wrl, br, tt):
    B, S, D = x.shape
    M = kmt.shape[-1]
    const = lambda b, s: (0, 0)
    tile = lambda w: pl.BlockSpec((1, tt, w), lambda b, s: (b, s, 0))
    full = lambda a: pl.BlockSpec(a.shape, const)
    return pl.pallas_call(
        functools.partial(_mixer_kernel, tt=tt),
        grid=(B, S // tt),
        in_specs=[
            tile(D), tile(QA_W), full(g_mix), full(wc), full(wq), full(wg), full(bg), full(cw), full(cb),
            full(lng), full(lnb), full(gqm),
            pl.BlockSpec((1, MEM_HEADS, MEM_HEAD_DIM, M), lambda b, s: (b, 0, 0, 0)),
            pl.BlockSpec((1, MEM_HEADS, M, MEM_HEAD_DIM), lambda b, s: (b, 0, 0, 0)),
            full(wa), full(wb), full(wm), full(wo), full(gffn), full(wrh), full(wrl), full(br),
        ],
        out_specs=[tile(D), tile(D), tile(LANES)],
        out_shape=[
            jax.ShapeDtypeStruct((B, S, D), F32),
            jax.ShapeDtypeStruct((B, S, D), BF16),
            jax.ShapeDtypeStruct((B, S, LANES), F32),
        ],
        scratch_shapes=[pltpu.VMEM((CONV_HALO + tt, CONV_CH), F32),
                        pltpu.VMEM((CONV_HALO + tt, CONV_CH), F32)],
        compiler_params=pltpu.CompilerParams(dimension_semantics=("arbitrary", "arbitrary"),
                                             vmem_limit_bytes=VMEM_LIMIT),
        name="mixer",
    )(x, oa, g_mix, wc, wq, wg, bg, cw, cb, lng, lnb, gqm, kmt, vm, wa, wb, wm, wo, gffn, wrh, wrl, br)


MOE_BLK = 256
MOE_EXPERTS_PER_STEP = 4


def _moe_kernel(h2_ref, x1_ref, comb_ref, ltri_ref, wup_ref, wdn_ref, o_ref,
                hs_s, combs_s, ys_s, pt_s, seg_s, *, tm, rows):
    n = pl.program_id(1)
    lane = lax.broadcasted_iota(jnp.int32, (tm, LANES), 1)

    @pl.when(n == 0)
    def _():
        comb = comb_ref[...]
        gid = comb[:, N_EXPERTS:N_EXPERTS + 1].astype(jnp.int32)
        member = lane == gid
        onehot = jnp.where(member, 1.0, 0.0)
        before = _dot(ltri_ref[...], onehot.astype(BF16))
        rank = jnp.sum(jnp.where(member, before, 0.0), axis=-1, keepdims=True)
        count = jnp.sum(onehot, axis=0, keepdims=True)
        lane1 = lax.broadcasted_iota(jnp.int32, (1, LANES), 1)
        start_blk = jnp.int32(0)
        start_row = jnp.zeros((1, LANES), F32)
        for g in range(N_GROUPS):
            cnt_g = jnp.sum(jnp.where(lane1 == g, count, 0.0)).astype(jnp.int32)
            nblk_g = (cnt_g + (MOE_BLK - 1)) // MOE_BLK
            seg_s[g] = start_blk
            seg_s[N_GROUPS + g] = nblk_g
            start_row = jnp.where(lane1 == g, (start_blk * MOE_BLK).astype(F32), start_row)
            start_blk = start_blk + nblk_g
        dest = jnp.sum(jnp.where(member, start_row, 0.0), axis=-1, keepdims=True) + rank
        seg_s[2 * N_GROUPS] = start_blk
        dest_i = dest.astype(jnp.int32)
        dest_row = jnp.broadcast_to(dest, (tm, LANES)).T[0:1, :].astype(jnp.int32)
        c_hi, c_lo = _split_bf16(comb)
        col = lax.broadcasted_iota(jnp.int32, (tm, MOE_BLK), 1)
        row = lax.broadcasted_iota(jnp.int32, (MOE_BLK, tm), 0)

        def permute_block(b, carry):
            r0 = pl.multiple_of(b * MOE_BLK, MOE_BLK)
            pt_s[b] = jnp.where(col + r0 == dest_i, 1.0, 0.0).astype(BF16)
            perm = jnp.where(row + r0 == dest_row, 1.0, 0.0).astype(BF16)
            hs_s[pl.ds(r0, MOE_BLK), :] = _dot(perm, h2_ref[...]).astype(BF16)
            combs_s[pl.ds(r0, MOE_BLK), :] = _dot(perm, c_hi) + _dot(perm, c_lo)
            ys_s[pl.ds(r0, MOE_BLK), :] = jnp.zeros((MOE_BLK, ys_s.shape[1]), F32)
            return carry

        lax.fori_loop(0, start_blk, permute_block, 0)

    g = (n * MOE_EXPERTS_PER_STEP) // EXPERTS_PER_GROUP
    first_blk = seg_s[g]
    lane_b = lax.broadcasted_iota(jnp.int32, (MOE_BLK, LANES), 1)

    def expert_block(b, carry):
        r0 = pl.multiple_of((first_blk + b) * MOE_BLK, MOE_BLK)
        hb = hs_s[pl.ds(r0, MOE_BLK), :]
        cb = combs_s[pl.ds(r0, MOE_BLK), :]
        acts = []
        for i in range(MOE_EXPERTS_PER_STEP):
            up = _dot(hb, wup_ref[i])
            a = up[:, :D_FF_EXPERT]
            c = jnp.sum(jnp.where(lane_b == n * MOE_EXPERTS_PER_STEP + i, cb, 0.0), axis=-1, keepdims=True)
            acts.append((a * _sigmoid(a) * up[:, D_FF_EXPERT:] * c).astype(BF16))
        w_dn = wdn_ref[...].reshape(MOE_EXPERTS_PER_STEP * D_FF_EXPERT, wdn_ref.shape[-1])
        ys_s[pl.ds(r0, MOE_BLK), :] += _dot(jnp.concatenate(acts, axis=-1), w_dn)
        return carry

    lax.fori_loop(0, seg_s[N_GROUPS + g], expert_block, 0)

    @pl.when(n == N_EXPERTS // MOE_EXPERTS_PER_STEP - 1)
    def _():
        o_ref[...] = x1_ref[...]

        def unpermute_block(b, carry):
            r0 = pl.multiple_of(b * MOE_BLK, MOE_BLK)
            o_ref[...] += _dot(pt_s[b], ys_s[pl.ds(r0, MOE_BLK), :].astype(BF16))
            return carry

        lax.fori_loop(0, seg_s[2 * N_GROUPS], unpermute_block, 0)


def _moe(h2, x1, comb, w_up, w_down, tm):
    T, D = h2.shape
    rows = tm + N_GROUPS * MOE_BLK
    ltri = jnp.asarray(np.tril(np.ones((tm, tm), np.float32), -1), BF16)
    return pl.pallas_call(
        functools.partial(_moe_kernel, tm=tm, rows=rows),
        grid=(T // tm, N_EXPERTS // MOE_EXPERTS_PER_STEP),
        in_specs=[
            pl.BlockSpec((tm, D), lambda t, n: (t, 0)),
            pl.BlockSpec((tm, D), lambda t, n: (t, 0)),
            pl.BlockSpec((tm, LANES), lambda t, n: (t, 0)),
            pl.BlockSpec((tm, tm), lambda t, n: (0, 0)),
            pl.BlockSpec((MOE_EXPERTS_PER_STEP, D, 2 * D_FF_EXPERT), lambda t, n: (n, 0, 0)),
            pl.BlockSpec((MOE_EXPERTS_PER_STEP, D_FF_EXPERT, D), lambda t, n: (n, 0, 0)),
        ],
        out_specs=pl.BlockSpec((tm, D), lambda t, n: (t, 0)),
        out_shape=jax.ShapeDtypeStruct((T, D), F32),
        scratch_shapes=[
            pltpu.VMEM((rows, D), BF16),
            pltpu.VMEM((rows, LANES), F32),
            pltpu.VMEM((rows, D), F32),
            pltpu.VMEM((rows // MOE_BLK, tm, MOE_BLK), BF16),
            pltpu.SMEM((2 * N_GROUPS + 1,), jnp.int32),
        ],
        compiler_params=pltpu.CompilerParams(dimension_semantics=("arbitrary", "arbitrary"),
                                             vmem_limit_bytes=VMEM_LIMIT),
        name="moe",
    )(h2, x1, comb, ltri, w_up, w_down)


def _pick_tile(n, pref):
    t = min(n, pref)
    assert n % t == 0, (n, t)
    return t


def _rope_tables(S):
    half = HEAD_DIM_A // 2
    inv = ROPE_THETA ** (-jnp.arange(half, dtype=F32) / half)
    ang = jnp.arange(S, dtype=jnp.int32).astype(F32)[:, None] * inv[None, :]
    cos, sin = jnp.cos(ang), jnp.sin(ang)
    cos_t = jnp.concatenate([cos, cos, cos, cos], axis=-1)
    sin_t = jnp.concatenate([-sin, sin, -sin, sin], axis=-1)
    return cos_t, sin_t


def _layer(x, mem, g_mix, w_in, b_gate, g_qa, g_ka, g_idx_k, conv_w, conv_b, ln_g, ln_b,
           g_mem, w_mem_kv, g_qm, g_km, w_br_a, w_br_b, w_br_m, w_o, g_ffn,
           w_rg, b_rg, w_re, b_re, w_up, w_down):
    B, S, D = x.shape
    topk = min(TOPK_MAX, S // 4)
    tk = _pick_tile(S, 512)
    tq = _pick_tile(tk, 128)
    tt = _pick_tile(S, 512)
    tm = _pick_tile(B * S, 1024)

    sizes = (QA_W, N_KV_HEADS_A * HEAD_DIM_A, N_KV_HEADS_A * HEAD_DIM_A, QI_W, IDX_DIM, IDX_HEADS,
             CONV_IN_W, QM_W, N_BRANCHES * D)
    offs = np.concatenate([[0], np.cumsum(sizes)])
    col = lambda i: w_in[:, offs[i]:offs[i + 1]]
    w_qa, w_ka, w_va, w_qi, w_ki, w_wi, w_conv, w_qm, w_gate = (col(i) for i in range(9))
    hd = HEAD_DIM_A
    zeros_hd = jnp.zeros((D, hd), F32)
    ka_dup = [w_ka[:, g * hd:(g + 1) * hd] for g in range(N_KV_HEADS_A) for _ in range(2)]
    va_aug = [w for g in range(N_KV_HEADS_A) for w in (w_va[:, g * hd:(g + 1) * hd], zeros_hd)]
    w_dsa = jnp.concatenate(
        [w_qa, w_qi] + ka_dup + va_aug + [w_ki, w_ki, w_wi, jnp.zeros((D, WI_W - IDX_HEADS), F32)], axis=1).astype(BF16)
    vbias = jnp.tile(jnp.concatenate([jnp.zeros((hd,), F32), jnp.ones((hd,), F32)]), N_KV_HEADS_A).reshape(1, VA_W)
    seg = np.arange(QA_W) // hd
    bd = jnp.asarray(seg[:, None] == seg[None, :], BF16)
    cos_t, sin_t = _rope_tables(S)

    kmt, vm = _mem_kv(mem, g_mem, w_mem_kv, g_km)
    qat, qit, k, vt, ki, wit = _dsa_proj(
        x, g_mix.reshape(1, D), w_dsa, vbias,
        jnp.tile(g_qa, N_HEADS_A).reshape(1, QA_W), jnp.tile(g_ka, KA_W // hd).reshape(1, KA_W),
        jnp.tile(g_idx_k, KI_W // IDX_DIM).reshape(1, KI_W), cos_t, sin_t, bd, tk)
    o_a = _dsa_attn(qat, qit, wit, k, vt, ki, tq, topk)

    w_r = jnp.concatenate([w_re, w_rg, jnp.zeros((D, LANES - N_EXPERTS - N_GROUPS), F32)], axis=1)
    b_r = jnp.concatenate([b_re, b_rg, jnp.zeros((LANES - N_EXPERTS - N_GROUPS,), F32)]).reshape(1, LANES)
    w_r_hi = w_r.astype(BF16)
    w_r_lo = (w_r - w_r_hi.astype(F32)).astype(BF16)
    x1, h2, comb = _mixer(
        x, o_a, g_mix.reshape(1, D), w_conv.astype(BF16), w_qm.astype(BF16), w_gate.astype(BF16),
        b_gate.reshape(1, -1), conv_w.reshape(CONV_WIDTH, CONV_CH), conv_b.reshape(1, CONV_CH),
        ln_g.reshape(1, CONV_CH), ln_b.reshape(1, CONV_CH), g_qm.reshape(1, MEM_HEAD_DIM), kmt, vm,
        w_br_a.astype(BF16), w_br_b.astype(BF16), w_br_m.astype(BF16), w_o.astype(BF16),
        g_ffn.reshape(1, D), w_r_hi, w_r_lo, b_r, tt)

    out = _moe(h2.reshape(B * S, D), x1.reshape(B * S, D), comb.reshape(B * S, LANES),
               w_up.astype(BF16), w_down.astype(BF16), tm)
    return out.reshape(B, S, D)


def kernel(x, mem, g_mix, w_in, b_gate, g_qa, g_ka, g_idx_k, conv_w, conv_b, ln_g, ln_b, g_mem, w_mem_kv, g_qm, g_km,
           w_br_a, w_br_b, w_br_m, w_o, g_ffn, w_rg, b_rg, w_re, b_re, w_up, w_down):
    params = (g_mix, w_in, b_gate, g_qa, g_ka, g_idx_k, conv_w, conv_b, ln_g, ln_b, g_mem, w_mem_kv, g_qm, g_km,
              w_br_a, w_br_b, w_br_m, w_o, g_ffn, w_rg, b_rg, w_re, b_re, w_up, w_down)
    for l in range(g_mix.shape[0]):
        x = _layer(x, mem, *(p[l] for p in params))
    return x
```

```python
import functools

import jax
import jax.numpy as jnp
import numpy as np
from jax import lax
from jax.experimental import pallas as pl
from jax.experimental.pallas import tpu as pltpu

N_HEADS_A = 8
N_KV_HEADS_A = 2
HEAD_DIM_A = 64
IDX_HEADS = 8
IDX_DIM = 64
TOPK_MAX = 256
CONV_CH = 512
CONV_WIDTH = 31
MEM_HEADS = 4
MEM_HEAD_DIM = 128
N_BRANCHES = 3
N_GROUPS = 4
EXPERTS_PER_GROUP = 8
N_EXPERTS = N_GROUPS * EXPERTS_PER_GROUP
D_FF_EXPERT = 256
ROPE_THETA = 10000.0
EPS = 1e-6

LANES = 128
SUBLANES = 8
VMEM_LIMIT = 56 * 1024 * 1024

INT_MIN = -2 ** 31
LOG2_E = 1.4426950408889634
HALF_BIAS = 2 ** 15
HALF_WORD = 2 ** 16
CHUNKS_PER_TRIP = 4
NEG_BIG = -1e30
CONV_HALO = 32

BF16 = jnp.bfloat16
F32 = jnp.float32


def _dot(a, b):
    return jnp.dot(a, b, preferred_element_type=F32)


def _sigmoid(x):
    return 0.5 * jnp.tanh(0.5 * x) + 0.5


def _split_bf16(x):
    hi = x.astype(BF16)
    lo = (x - hi.astype(F32)).astype(BF16)
    return hi, lo


def _seg_sum(xsq, ones_bd):
    hi, lo = _split_bf16(xsq)
    return _dot(hi, ones_bd) + _dot(lo, ones_bd)


def _rot_half(x, head_dim):
    n = x.shape[-1]
    half = head_dim // 2
    lane = lax.broadcasted_iota(jnp.int32, x.shape, x.ndim - 1)
    is_lo = (lane & (head_dim - 1)) < half
    return jnp.where(is_lo, pltpu.roll(x, n - half, x.ndim - 1), pltpu.roll(x, half, x.ndim - 1))


def _tile_lanes(x, reps):
    return jnp.concatenate([x] * reps, axis=-1) if reps > 1 else x


def _mem_kv_kernel(mem_ref, g_mem_ref, w_ref, g_km_ref, kt_ref, v_ref):
    m = mem_ref[0]
    hm = m * lax.rsqrt(jnp.mean(m * m, axis=-1, keepdims=True) + EPS) * g_mem_ref[...]
    kv = _dot(hm.astype(BF16), w_ref[...])
    hd = MEM_HEADS * MEM_HEAD_DIM
    for h in range(MEM_HEADS):
        k = kv[:, h * MEM_HEAD_DIM:(h + 1) * MEM_HEAD_DIM]
        k = k * lax.rsqrt(jnp.mean(k * k, axis=-1, keepdims=True) + EPS) * g_km_ref[...]
        kt_ref[0, h] = k.T.astype(BF16)
        v_ref[0, h] = kv[:, hd + h * MEM_HEAD_DIM: hd + (h + 1) * MEM_HEAD_DIM].astype(BF16)


def _mem_kv(mem, g_mem, w_mem_kv, g_km):
    B, M, D = mem.shape
    hd2 = 2 * MEM_HEADS * MEM_HEAD_DIM
    return pl.pallas_call(
        _mem_kv_kernel,
        grid=(B,),
        in_specs=[
            pl.BlockSpec((1, M, D), lambda b: (b, 0, 0)),
            pl.BlockSpec((1, D), lambda b: (0, 0)),
            pl.BlockSpec((D, hd2), lambda b: (0, 0)),
            pl.BlockSpec((1, MEM_HEAD_DIM), lambda b: (0, 0)),
        ],
        out_specs=[
            pl.BlockSpec((1, MEM_HEADS, MEM_HEAD_DIM, M), lambda b: (b, 0, 0, 0)),
            pl.BlockSpec((1, MEM_HEADS, M, MEM_HEAD_DIM), lambda b: (b, 0, 0, 0)),
        ],
        out_shape=[
            jax.ShapeDtypeStruct((B, MEM_HEADS, MEM_HEAD_DIM, M), BF16),
            jax.ShapeDtypeStruct((B, MEM_HEADS, M, MEM_HEAD_DIM), BF16),
        ],
        compiler_params=pltpu.CompilerParams(dimension_semantics=("arbitrary",)),
        name="mem_kv",
    )(mem, g_mem.reshape(1, D), w_mem_kv.astype(BF16), g_km.reshape(1, MEM_HEAD_DIM))


QA_W = N_HEADS_A * HEAD_DIM_A
QI_W = IDX_HEADS * IDX_DIM
KA_W = 2 * N_KV_HEADS_A * HEAD_DIM_A
VA_W = 2 * N_KV_HEADS_A * HEAD_DIM_A
KI_W = 2 * IDX_DIM
WI_W = LANES
DSA_COLS = QA_W + QI_W + KA_W + VA_W + KI_W + WI_W


def _dsa_proj_kernel(x_ref, g_mix_ref, w_ref, vbias_ref, gq_ref, gk_ref, gki_ref, cos_ref, sin_ref, bd_ref,
                     qat_ref, qit_ref, k_ref, vt_ref, ki_ref, wit_ref):
    x = x_ref[0]
    h = x * lax.rsqrt(jnp.mean(x * x, axis=-1, keepdims=True) + EPS) * g_mix_ref[...]
    p = _dot(h.astype(BF16), w_ref[...])
    cos1 = cos_ref[...]
    sin1 = sin_ref[...]
    cos4, sin4 = _tile_lanes(cos1, 4), _tile_lanes(sin1, 4)
    cos2, sin2 = _tile_lanes(cos1, 2), _tile_lanes(sin1, 2)
    bd = bd_ref[...]
    inv_hd = 1.0 / HEAD_DIM_A

    def rope(v, c, s):
        return v * c + _rot_half(v, HEAD_DIM_A) * s

    o = 0
    qa = p[:, o:o + QA_W]; o += QA_W
    qa = qa * lax.rsqrt(_seg_sum(qa * qa, bd) * inv_hd + EPS) * gq_ref[...]
    qat_ref[0] = (rope(qa, cos4, sin4) * (HEAD_DIM_A ** -0.5 * LOG2_E)).T.astype(BF16)

    qi = p[:, o:o + QI_W]; o += QI_W
    qit_ref[0] = (rope(qi, cos4, sin4) * (IDX_DIM ** -0.5)).T.astype(BF16)

    ka = p[:, o:o + KA_W]; o += KA_W
    ka = ka * lax.rsqrt(_seg_sum(ka * ka, bd[:KA_W, :KA_W]) * inv_hd + EPS) * gk_ref[...]
    k_ref[0] = rope(ka, cos2, sin2).astype(BF16)

    vt_ref[0, 0] = (p[:, o:o + VA_W] + vbias_ref[...]).T.astype(BF16); o += VA_W

    ki = p[:, o:o + KI_W]; o += KI_W
    ki = ki * lax.rsqrt(_seg_sum(ki * ki, bd[:KI_W, :KI_W]) * (1.0 / IDX_DIM) + EPS) * gki_ref[...]
    ki_ref[0] = rope(ki, cos1, sin1).astype(BF16)

    wit_ref[0] = (p[:, o:o + WI_W] * (IDX_HEADS ** -0.5)).T[:IDX_HEADS, :]


def _dsa_proj(x, g_mix, w_dsa, vbias, g_qa, g_ka, g_idx_k, cos_t, sin_t, bd, tt):
    B, S, D = x.shape
    nt = S // tt
    const = lambda b, s: (0, 0)
    return pl.pallas_call(
        _dsa_proj_kernel,
        grid=(B, nt),
        in_specs=[
            pl.BlockSpec((1, tt, D), lambda b, s: (b, s, 0)),
            pl.BlockSpec((1, D), const),
            pl.BlockSpec((D, DSA_COLS), const),
            pl.BlockSpec((1, VA_W), const),
            pl.BlockSpec((1, QA_W), const),
            pl.BlockSpec((1, KA_W), const),
            pl.BlockSpec((1, KI_W), const),
            pl.BlockSpec((tt, LANES), lambda b, s: (s, 0)),
            pl.BlockSpec((tt, LANES), lambda b, s: (s, 0)),
            pl.BlockSpec((QA_W, QA_W), const),
        ],
        out_specs=[
            pl.BlockSpec((1, QA_W, tt), lambda b, s: (b, 0, s)),
            pl.BlockSpec((1, QI_W, tt), lambda b, s: (b, 0, s)),
            pl.BlockSpec((1, tt, KA_W), lambda b, s: (b, s, 0)),
            pl.BlockSpec((1, 1, VA_W, tt), lambda b, s: (b, s, 0, 0)),
            pl.BlockSpec((1, tt, KI_W), lambda b, s: (b, s, 0)),
            pl.BlockSpec((1, IDX_HEADS, tt), lambda b, s: (b, 0, s)),
        ],
        out_shape=[
            jax.ShapeDtypeStruct((B, QA_W, S), BF16),
            jax.ShapeDtypeStruct((B, QI_W, S), BF16),
            jax.ShapeDtypeStruct((B, S, KA_W), BF16),
            jax.ShapeDtypeStruct((B, nt, VA_W, tt), BF16),
            jax.ShapeDtypeStruct((B, S, KI_W), BF16),
            jax.ShapeDtypeStruct((B, IDX_HEADS, S), F32),
        ],
        compiler_params=pltpu.CompilerParams(dimension_semantics=("arbitrary", "arbitrary"),
                                             vmem_limit_bytes=VMEM_LIMIT),
        name="dsa_proj",
    )(x, g_mix, w_dsa, vbias, g_qa, g_ka, g_idx_k, cos_t, sin_t, bd)


def _key_of(f):
    bits = lax.bitcast_convert_type(f, jnp.int32)
    return bits ^ ((bits >> 31) & jnp.int32(0x7FFFFFFF))


def _pack_high_halves(key):
    m = key.shape[0] // 2
    return lax.shift_right_logical(key[:m], 16) | (key[m:] & jnp.int32(-HALF_WORD))


def _pack_low_halves(key):
    m = key.shape[0] // 2
    return ((key[:m] & 0xFFFF) | (key[m:] << 16)) ^ jnp.int32(-HALF_BIAS * HALF_WORD + HALF_BIAS)


def _halves_of(words):
    return pltpu.bitcast(words, jnp.int16)


def _pipelined_chunks(nchunks, last_chunk, matmul, finish):
    matmul(0, 0)

    def straight(c0, n, last):
        for j in range(n):
            if j + 1 < n or not last:
                matmul(c0 + j + 1, (j + 1) % 2)
            finish(c0 + j, j % 2)

    def trip(i, carry):
        straight(CHUNKS_PER_TRIP * i, CHUNKS_PER_TRIP, False)
        return carry

    trips = last_chunk // CHUNKS_PER_TRIP
    lax.fori_loop(0, trips, trip, 0)
    for r in range(CHUNKS_PER_TRIP):
        @pl.when(last_chunk % CHUNKS_PER_TRIP == r)
        def _(r=r):
            straight(trips * CHUNKS_PER_TRIP, r + 1, True)


def _dsa_attn_kernel(qat_ref, qit_ref, wit_ref, k_ref, vt_ref, ki_ref, o_ref,
                     qa_s, qi_s, keys_s, khi_s, klo_s, zlo_s, buf0, buf1, m_s, acc_s, *, tq, tk, topk, idx_bits):
    qt = pl.program_id(1)
    q0 = qt * tq
    nchunks = q0 // tk + 1
    last_chunk = nchunks - 1
    hpg = N_HEADS_A // N_KV_HEADS_A
    gw = hpg * tq
    half_pairs = hpg // 2
    bufs = (buf0, buf1)
    dim = lax.broadcasted_iota(jnp.int32, (LANES, tq), 0)
    lo_half = dim < HEAD_DIM_A

    for j in range(IDX_HEADS // 2):
        pair = qit_ref[0, j * LANES:(j + 1) * LANES, :]
        qi_s[:, (2 * j) * tq:(2 * j + 1) * tq] = jnp.where(lo_half, pair, jnp.zeros_like(pair))
        qi_s[:, (2 * j + 1) * tq:(2 * j + 2) * tq] = jnp.where(lo_half, jnp.zeros_like(pair), pair)
    for g in range(N_KV_HEADS_A):
        for jj in range(half_pairs):
            pair = qat_ref[0, (g * half_pairs + jj) * LANES:(g * half_pairs + jj + 1) * LANES, :]
            qa_s[:, g * gw + jj * tq:g * gw + (jj + 1) * tq] = jnp.where(lo_half, pair, jnp.zeros_like(pair))
            qa_s[:, g * gw + (half_pairs + jj) * tq:g * gw + (half_pairs + jj + 1) * tq] = (
                jnp.where(lo_half, jnp.zeros_like(pair), pair))

    key_in_chunk = lax.broadcasted_iota(jnp.int32, (tk, tq), 0)
    q_pos = q0 + lax.broadcasted_iota(jnp.int32, (tk, tq), 1)

    def score_chunk(c, diagonal=False):
        r0 = pl.multiple_of(c * tk, tk)
        logits = _dot(ki_ref[0, pl.ds(r0, tk), :], qi_s[...])
        score = jnp.zeros((tk, tq), F32)
        for h in range(IDX_HEADS):
            score = score + wit_ref[0, h:h + 1, :] * jnp.maximum(logits[:, h * tq:(h + 1) * tq], 0.0)
        key = _key_of(score)
        if diagonal:
            key = jnp.where(key_in_chunk + c * tk <= q_pos, key, jnp.int32(INT_MIN))
        keys_s[c] = key
        khi_s[c] = _pack_high_halves(key)
        klo_s[c] = _pack_low_halves(key)

    def score_trip(i, carry):
        for j in range(CHUNKS_PER_TRIP):
            score_chunk(CHUNKS_PER_TRIP * i + j)
        return carry

    lax.fori_loop(0, last_chunk // CHUNKS_PER_TRIP, score_trip, 0)
    for r in range(CHUNKS_PER_TRIP):
        @pl.when(last_chunk % CHUNKS_PER_TRIP == r)
        def _(r=r):
            for j in range(r):
                score_chunk(last_chunk - r + j)
            score_chunk(last_chunk, diagonal=True)

    def count_keys(pred_fn):
        def one(c, acc):
            hit = pred_fn(keys_s[c], c * tk).astype(jnp.int32)
            return acc + jnp.sum(hit.reshape(tk // 8, 8, tq), axis=0)

        def two(i, acc):
            return one(2 * i + 1, one(2 * i, acc))

        part = lax.fori_loop(0, nchunks // 2, two, jnp.zeros((8, tq), jnp.int32))
        part = lax.cond(nchunks % 2 == 1, lambda acc: one(last_chunk, acc), lambda acc: acc, part)
        return jnp.sum(part, axis=0, keepdims=True)

    total = q0 + lax.broadcasted_iota(jnp.int32, (1, tq), 1) + 1
    has_tau = total >= topk

    def count_halves(src_s, cand, left):
        cand16 = _halves_of(jnp.broadcast_to((cand & 0xFFFF) | (cand << 16), (SUBLANES, tq)))

        def one(c, acc):
            half = _halves_of(src_s[c])
            hits = [jnp.where(half[j * 2 * SUBLANES:(j + 1) * 2 * SUBLANES, :] >= cand16, jnp.int16(1), jnp.int16(0))
                    for j in range(tk // (2 * SUBLANES))]
            while len(hits) > 1:
                hits = [a + b for a, b in zip(hits[::2], hits[1::2])]
            return acc + hits[0]

        def trip(i, acc):
            for j in range(CHUNKS_PER_TRIP):
                acc = one(CHUNKS_PER_TRIP * i + j, acc)
            return acc

        acc = lax.fori_loop(0, nchunks // CHUNKS_PER_TRIP, trip, jnp.zeros((2 * SUBLANES, tq), jnp.int16))
        for j in range(left):
            acc = one(nchunks - left + j, acc)
        words = pltpu.bitcast(acc, jnp.int32)
        return jnp.sum((words & 0xFFFF) + (words >> 16), axis=0, keepdims=True)

    def radix16(src_s, want, left):
        def step(i, state):
            v, cnt_ge, cnt_gt = state
            cand = v + (jnp.int32(1) << (15 - i))
            c = count_halves(src_s, cand, left)
            up = c >= want
            return jnp.where(up, cand, v), jnp.where(up, c, cnt_ge), jnp.where(up, cnt_gt, c)
        zero = jnp.zeros((1, tq), jnp.int32)
        return lax.fori_loop(0, 16, step, (jnp.full((1, tq), -HALF_BIAS, jnp.int32), zero, zero))

    def select(left):
        tau_hi, cnt_hi_ge, cnt_hi_gt = radix16(khi_s, topk, left)
        tau_hi16 = _halves_of(jnp.broadcast_to((tau_hi & 0xFFFF) | (tau_hi << 16), (SUBLANES, tq)))

        def low_of_bucket(c, carry):
            hi, lo = _halves_of(khi_s[c]), _halves_of(klo_s[c])
            for j in range(tk // (2 * SUBLANES)):
                rows = slice(j * 2 * SUBLANES, (j + 1) * 2 * SUBLANES)
                zlo_s[c, j * SUBLANES:(j + 1) * SUBLANES, :] = pltpu.bitcast(
                    jnp.where(hi[rows, :] == tau_hi16, lo[rows, :], jnp.int16(-HALF_BIAS)), jnp.int32)
            return carry

        lax.fori_loop(0, nchunks, low_of_bucket, 0)
        tau_lo, cnt_lo_ge, _ = radix16(zlo_s, topk - cnt_hi_gt, left)
        return tau_hi, cnt_hi_ge, cnt_hi_gt, tau_lo, cnt_lo_ge

    left = nchunks % CHUNKS_PER_TRIP
    tau_hi, cnt_hi_ge, cnt_hi_gt, tau_lo, cnt_lo_ge = lax.cond(
        left < 2,
        lambda: lax.cond(left == 0, lambda: select(0), lambda: select(1)),
        lambda: lax.cond(left == 2, lambda: select(2), lambda: select(3)))
    tau = (tau_hi << 16) | ((tau_lo + HALF_BIAS) & 0xFFFF)
    cnt = cnt_hi_gt + jnp.where(tau_lo == -HALF_BIAS, cnt_hi_ge - cnt_hi_gt, cnt_lo_ge)
    tau = jnp.where(has_tau, jnp.maximum(tau, jnp.int32(INT_MIN + 1)), jnp.int32(INT_MIN))

    tied = has_tau & (cnt > topk)

    def tie_cut():
        need = topk - count_keys(lambda k, _: k > tau)

        def idx_step(i, j):
            bit = jnp.int32(1) << (idx_bits - 1 - i)
            test = j + bit - 1
            f = count_keys(lambda k, base: (k == tau) & (base + key_in_chunk <= test))
            return jnp.where(f < need, j + bit, j)
        return lax.fori_loop(0, idx_bits, idx_step, jnp.zeros((1, tq), jnp.int32))

    jcut = lax.cond(jnp.max(jnp.where(tied, 1, 0)) > 0, tie_cut, lambda: jnp.zeros((1, tq), jnp.int32))
    jcut = jnp.where(tied, jcut, jnp.where(has_tau, jnp.int32(2 ** 30), jnp.int32(-1)))

    m_s[...] = jnp.full(m_s.shape, NEG_BIG, F32)
    acc_s[...] = jnp.zeros(acc_s.shape, F32)

    def attn_matmul(c, slot):
        r0 = pl.multiple_of(c * tk, tk)
        k = keys_s[c]
        kpos = key_in_chunk + c * tk
        bias = jnp.where(k > tau, 0.0, jnp.where(k == tau, jnp.where(kpos <= jcut, 0.0, NEG_BIG), NEG_BIG))
        bias4 = jnp.concatenate([bias] * hpg, axis=1)
        for g in range(N_KV_HEADS_A):
            bufs[slot][:, g * gw:(g + 1) * gw] = _dot(k_ref[0, pl.ds(r0, tk), g * LANES:(g + 1) * LANES],
                                                      qa_s[:, g * gw:(g + 1) * gw]) + bias4

    def attn_finish(c, slot):
        for g in range(N_KV_HEADS_A):
            m_old = m_s[g]
            m_new = jnp.maximum(m_old, jnp.max(bufs[slot][:, g * gw:(g + 1) * gw], axis=0, keepdims=True))
            p = jnp.exp2(bufs[slot][:, g * gw:(g + 1) * gw] - m_new)
            alpha = jnp.exp2(m_old - m_new)
            pv = _dot(vt_ref[0, c, g * LANES:(g + 1) * LANES, :], p.astype(BF16))
            acc_s[g] = alpha * acc_s[g] + pv
            m_s[g] = m_new

    _pipelined_chunks(nchunks, last_chunk, attn_matmul, attn_finish)

    for g in range(N_KV_HEADS_A):
        for jj in range(half_pairs):
            a_e = acc_s[g, :, jj * tq:(jj + 1) * tq]
            a_o = acc_s[g, :, (half_pairs + jj) * tq:(half_pairs + jj + 1) * tq]
            o_pair = jnp.concatenate([a_e[:HEAD_DIM_A] / a_e[HEAD_DIM_A:], a_o[:HEAD_DIM_A] / a_o[HEAD_DIM_A:]], axis=0)
            j = g * half_pairs + jj
            o_ref[0, :, j * LANES:(j + 1) * LANES] = o_pair.T.astype(BF16)


def _dsa_attn(qat, qit, wit, k, vt, ki, tq, topk):
    B, _, S = qat.shape
    nc, tk = vt.shape[1], vt.shape[3]
    hpg = N_HEADS_A // N_KV_HEADS_A
    idx_bits = max(1, int(np.ceil(np.log2(S))))
    assert IDX_HEADS == N_HEADS_A
    assert CHUNKS_PER_TRIP == 4
    kernel = functools.partial(_dsa_attn_kernel, tq=tq, tk=tk, topk=topk, idx_bits=idx_bits)
    return pl.pallas_call(
        kernel,
        grid=(B, S // tq),
        in_specs=[
            pl.BlockSpec((1, QA_W, tq), lambda b, q: (b, 0, q)),
            pl.BlockSpec((1, QI_W, tq), lambda b, q: (b, 0, q)),
            pl.BlockSpec((1, IDX_HEADS, tq), lambda b, q: (b, 0, q)),
            pl.BlockSpec((1, S, KA_W), lambda b, q: (b, 0, 0)),
            pl.BlockSpec((1, nc, VA_W, tk), lambda b, q: (b, 0, 0, 0)),
            pl.BlockSpec((1, S, KI_W), lambda b, q: (b, 0, 0)),
        ],
        out_specs=pl.BlockSpec((1, tq, QA_W), lambda b, q: (b, q, 0)),
        out_shape=jax.ShapeDtypeStruct((B, S, QA_W), BF16),
        scratch_shapes=[
            pltpu.VMEM((LANES, N_HEADS_A * tq), BF16),
            pltpu.VMEM((LANES, IDX_HEADS * tq), BF16),
            pltpu.VMEM((nc, tk, tq), jnp.int32),
            pltpu.VMEM((nc, tk // 2, tq), jnp.int32),
            pltpu.VMEM((nc, tk // 2, tq), jnp.int32),
            pltpu.VMEM((nc, tk // 2, tq), jnp.int32),
            pltpu.VMEM((tk, IDX_HEADS * tq), F32),
            pltpu.VMEM((tk, IDX_HEADS * tq), F32),
            pltpu.VMEM((N_KV_HEADS_A, 1, hpg * tq), F32),
            pltpu.VMEM((N_KV_HEADS_A, LANES, hpg * tq), F32),
        ],
        compiler_params=pltpu.CompilerParams(dimension_semantics=("arbitrary", "arbitrary"),
                                             vmem_limit_bytes=VMEM_LIMIT),
        name="dsa_attn",
    )(qat, qit, wit, k, vt, ki)


CONV_IN_W = 2 * CONV_CH
QM_W = MEM_HEADS * MEM_HEAD_DIM


def _mixer_kernel(x_ref, oa_ref, g_mix_ref, wc_ref, wq_ref, wg_ref, bg_ref, cw_ref, cb_ref, lng_ref, lnb_ref,
                  gqm_ref, kmt_ref, vm_ref, wa_ref, wb_ref, wm_ref, wo_ref, gffn_ref, wrh_ref, wrl_ref, br_ref,
                  x1_ref, h2_ref, comb_ref, ubuf, wbuf, *, tt):
    D = x_ref.shape[-1]

    @pl.when(pl.program_id(1) == 0)
    def _():
        ubuf[0:CONV_HALO, :] = jnp.zeros((CONV_HALO, CONV_CH), F32)

    x = x_ref[0]
    h = (x * lax.rsqrt(jnp.mean(x * x, axis=-1, keepdims=True) + EPS) * g_mix_ref[...]).astype(BF16)

    cin = _dot(h, wc_ref[...])
    u = cin[:, :CONV_CH] * _sigmoid(cin[:, CONV_CH:])

    ubuf[CONV_HALO:CONV_HALO + tt, :] = u
    y = jnp.zeros((tt, CONV_CH), F32) + cb_ref[...]
    first = CONV_HALO - (CONV_WIDTH - 1)
    for b in range(SUBLANES):
        taps = [j for j in range(CONV_WIDTH) if (first + j) % SUBLANES == b]
        if not taps:
            continue
        base = first + taps[0]
        if b == 0:
            window, w0 = ubuf, base
        else:
            span = taps[-1] - taps[0] + tt
            wbuf[0:span, :] = ubuf[base:base + span, :]
            window, w0 = wbuf, 0
        for j in taps:
            off = w0 + j - taps[0]
            y = y + cw_ref[j:j + 1, :] * window[off:off + tt, :]
    halo = ubuf[tt:tt + CONV_HALO, :]
    ubuf[0:CONV_HALO, :] = halo
    mu = jnp.mean(y, axis=-1, keepdims=True)
    yc = y - mu
    var = jnp.mean(yc * yc, axis=-1, keepdims=True)
    yn = yc * lax.rsqrt(var + EPS) * lng_ref[...] + lnb_ref[...]
    o_b = (yn * _sigmoid(yn)).astype(BF16)

    qm = _dot(h, wq_ref[...])
    o_m = []
    for hh in range(MEM_HEADS):
        q = qm[:, hh * MEM_HEAD_DIM:(hh + 1) * MEM_HEAD_DIM]
        q = q * lax.rsqrt(jnp.mean(q * q, axis=-1, keepdims=True) + EPS) * gqm_ref[...] * (MEM_HEAD_DIM ** -0.5)
        s = _dot(q.astype(BF16), kmt_ref[0, hh])
        p = jnp.exp(s - jnp.max(s, axis=-1, keepdims=True))
        l = jnp.sum(p, axis=-1, keepdims=True)
        o_m.append((_dot(p.astype(BF16), vm_ref[0, hh]) / l).astype(BF16))
    o_m = jnp.concatenate(o_m, axis=-1)

    gates = _sigmoid(_dot(h, wg_ref[...]) + bg_ref[...])
    merged = (gates[:, :D] * _dot(oa_ref[0], wa_ref[...])
              + gates[:, D:2 * D] * _dot(o_b, wb_ref[...])
              + gates[:, 2 * D:] * _dot(o_m, wm_ref[...]))
    x1 = x + _dot(merged.astype(BF16), wo_ref[...])
    x1_ref[0] = x1

    h2 = x1 * lax.rsqrt(jnp.mean(x1 * x1, axis=-1, keepdims=True) + EPS) * gffn_ref[...]
    h2_ref[0] = h2.astype(BF16)
    hi, lo = _split_bf16(h2)
    logits = _dot(hi, wrh_ref[...]) + _dot(hi, wrl_ref[...]) + _dot(lo, wrh_ref[...]) + br_ref[...]
    lane = lax.broadcasted_iota(jnp.int32, logits.shape, 1)
    ninf = jnp.float32(-jnp.inf)
    big = jnp.int32(1 << 20)
    is_g = (lane >= N_EXPERTS) & (lane < N_EXPERTS + N_GROUPS)
    gl = jnp.where(is_g, logits, ninf)
    gmax = jnp.max(gl, axis=-1, keepdims=True)
    gsel = jnp.min(jnp.where(gl == gmax, lane, big), axis=-1, keepdims=True) - N_EXPERTS
    p_sel = 1.0 / jnp.sum(jnp.where(is_g, jnp.exp(gl - gmax), 0.0), axis=-1, keepdims=True)
    in_grp = (lane < N_EXPERTS) & ((lane >> 3) == gsel)
    el = jnp.where(in_grp, logits, ninf)
    v1 = jnp.max(el, axis=-1, keepdims=True)
    i1 = jnp.min(jnp.where(el == v1, lane, big), axis=-1, keepdims=True)
    el2 = jnp.where(lane == i1, ninf, el)
    v2 = jnp.max(el2, axis=-1, keepdims=True)
    i2 = jnp.min(jnp.where(el2 == v2, lane, big), axis=-1, keepdims=True)
    e2 = jnp.exp(v2 - v1)
    den = 1.0 + e2
    comb_ref[0] = jnp.where(lane == i1, (1.0 / den) * p_sel,
                            jnp.where(lane == i2, (e2 / den) * p_sel,
                                      jnp.where(lane == N_EXPERTS, gsel.astype(F32), 0.0)))


def _mixer(x, oa, g_mix, wc, wq, wg, bg, cw, cb, lng, lnb, gqm, kmt, vm, wa, wb, wm, wo, gffn, wrh, wrl, br, tt):
    B, S, D = x.shape
    M = kmt.shape[-1]
    const = lambda b, s: (0, 0)
    tile = lambda w: pl.BlockSpec((1, tt, w), lambda b, s: (b, s, 0))
    full = lambda a: pl.BlockSpec(a.shape, const)
    return pl.pallas_call(
        functools.partial(_mixer_kernel, tt=tt),
        grid=(B, S // tt),
        in_specs=[
            tile(D), tile(QA_W), full(g_mix), full(wc), full(wq), full(wg), full(bg), full(cw), full(cb),
            full(lng), full(lnb), full(gqm),
            pl.BlockSpec((1, MEM_HEADS, MEM_HEAD_DIM, M), lambda b, s: (b, 0, 0, 0)),
            pl.BlockSpec((1, MEM_HEADS, M, MEM_HEAD_DIM), lambda b, s: (b, 0, 0, 0)),
            full(wa), full(wb), full(wm), full(wo), full(gffn), full(wrh), full(wrl), full(br),
        ],
        out_specs=[tile(D), tile(D), tile(LANES)],
        out_shape=[
            jax.ShapeDtypeStruct((B, S, D), F32),
            jax.ShapeDtypeStruct((B, S, D), BF16),
            jax.ShapeDtypeStruct((B, S, LANES), F32),
        ],
        scratch_shapes=[pltpu.VMEM((CONV_HALO + tt, CONV_CH), F32),
                        pltpu.VMEM((CONV_HALO + tt, CONV_CH), F32)],
        compiler_params=pltpu.CompilerParams(dimension_semantics=("arbitrary", "arbitrary"),
                                             vmem_limit_bytes=VMEM_LIMIT),
        name="mixer",
    )(x, oa, g_mix, wc, wq, wg, bg, cw, cb, lng, lnb, gqm, kmt, vm, wa, wb, wm, wo, gffn, wrh, wrl, br)


MOE_BLK = 256
MOE_EXPERTS_PER_STEP = 4


def _moe_kernel(h2_ref, x1_ref, comb_ref, ltri_ref, wup_ref, wdn_ref, o_ref,
                hs_s, combs_s, ys_s, pt_s, seg_s, *, tm, rows):
    n = pl.program_id(1)
    lane = lax.broadcasted_iota(jnp.int32, (tm, LANES), 1)

    @pl.when(n == 0)
    def _():
        comb = comb_ref[...]
        gid = comb[:, N_EXPERTS:N_EXPERTS + 1].astype(jnp.int32)
        member = lane == gid
        onehot = jnp.where(member, 1.0, 0.0)
        before = _dot(ltri_ref[...], onehot.astype(BF16))
        rank = jnp.sum(jnp.where(member, before, 0.0), axis=-1, keepdims=True)
        count = jnp.sum(onehot, axis=0, keepdims=True)
        lane1 = lax.broadcasted_iota(jnp.int32, (1, LANES), 1)
        start_blk = jnp.int32(0)
        start_row = jnp.zeros((1, LANES), F32)
        for g in range(N_GROUPS):
            cnt_g = jnp.sum(jnp.where(lane1 == g, count, 0.0)).astype(jnp.int32)
            nblk_g = (cnt_g + (MOE_BLK - 1)) // MOE_BLK
            seg_s[g] = start_blk
            seg_s[N_GROUPS + g] = nblk_g
            start_row = jnp.where(lane1 == g, (start_blk * MOE_BLK).astype(F32), start_row)
            start_blk = start_blk + nblk_g
        dest = jnp.sum(jnp.where(member, start_row, 0.0), axis=-1, keepdims=True) + rank
        seg_s[2 * N_GROUPS] = start_blk
        dest_i = dest.astype(jnp.int32)
        dest_row = jnp.broadcast_to(dest, (tm, LANES)).T[0:1, :].astype(jnp.int32)
        c_hi, c_lo = _split_bf16(comb)
        col = lax.broadcasted_iota(jnp.int32, (tm, MOE_BLK), 1)
        row = lax.broadcasted_iota(jnp.int32, (MOE_BLK, tm), 0)

        def permute_block(b, carry):
            r0 = pl.multiple_of(b * MOE_BLK, MOE_BLK)
            pt_s[b] = jnp.where(col + r0 == dest_i, 1.0, 0.0).astype(BF16)
            perm = jnp.where(row + r0 == dest_row, 1.0, 0.0).astype(BF16)
            hs_s[pl.ds(r0, MOE_BLK), :] = _dot(perm, h2_ref[...]).astype(BF16)
            combs_s[pl.ds(r0, MOE_BLK), :] = _dot(perm, c_hi) + _dot(perm, c_lo)
            ys_s[pl.ds(r0, MOE_BLK), :] = jnp.zeros((MOE_BLK, ys_s.shape[1]), F32)
            return carry

        lax.fori_loop(0, start_blk, permute_block, 0)

    g = (n * MOE_EXPERTS_PER_STEP) // EXPERTS_PER_GROUP
    first_blk = seg_s[g]
    lane_b = lax.broadcasted_iota(jnp.int32, (MOE_BLK, LANES), 1)

    def expert_block(b, carry):
        r0 = pl.multiple_of((first_blk + b) * MOE_BLK, MOE_BLK)
        hb = hs_s[pl.ds(r0, MOE_BLK), :]
        cb = combs_s[pl.ds(r0, MOE_BLK), :]
        acts = []
        for i in range(MOE_EXPERTS_PER_STEP):
            up = _dot(hb, wup_ref[i])
            a = up[:, :D_FF_EXPERT]
            c = jnp.sum(jnp.where(lane_b == n * MOE_EXPERTS_PER_STEP + i, cb, 0.0), axis=-1, keepdims=True)
            acts.append((a * _sigmoid(a) * up[:, D_FF_EXPERT:] * c).astype(BF16))
        w_dn = wdn_ref[...].reshape(MOE_EXPERTS_PER_STEP * D_FF_EXPERT, wdn_ref.shape[-1])
        ys_s[pl.ds(r0, MOE_BLK), :] += _dot(jnp.concatenate(acts, axis=-1), w_dn)
        return carry

    lax.fori_loop(0, seg_s[N_GROUPS + g], expert_block, 0)

    @pl.when(n == N_EXPERTS // MOE_EXPERTS_PER_STEP - 1)
    def _():
        o_ref[...] = x1_ref[...]

        def unpermute_block(b, carry):
            r0 = pl.multiple_of(b * MOE_BLK, MOE_BLK)
            o_ref[...] += _dot(pt_s[b], ys_s[pl.ds(r0, MOE_BLK), :].astype(BF16))
            return carry

        lax.fori_loop(0, seg_s[2 * N_GROUPS], unpermute_block, 0)


def _moe(h2, x1, comb, w_up, w_down, tm):
    T, D = h2.shape
    rows = tm + N_GROUPS * MOE_BLK
    ltri = jnp.asarray(np.tril(np.ones((tm, tm), np.float32), -1), BF16)
    return pl.pallas_call(
        functools.partial(_moe_kernel, tm=tm, rows=rows),
        grid=(T // tm, N_EXPERTS // MOE_EXPERTS_PER_STEP),
        in_specs=[
            pl.BlockSpec((tm, D), lambda t, n: (t, 0)),
            pl.BlockSpec((tm, D), lambda t, n: (t, 0)),
            pl.BlockSpec((tm, LANES), lambda t, n: (t, 0)),
            pl.BlockSpec((tm, tm), lambda t, n: (0, 0)),
            pl.BlockSpec((MOE_EXPERTS_PER_STEP, D, 2 * D_FF_EXPERT), lambda t, n: (n, 0, 0)),
            pl.BlockSpec((MOE_EXPERTS_PER_STEP, D_FF_EXPERT, D), lambda t, n: (n, 0, 0)),
        ],
        out_specs=pl.BlockSpec((tm, D), lambda t, n: (t, 0)),
        out_shape=jax.ShapeDtypeStruct((T, D), F32),
        scratch_shapes=[
            pltpu.VMEM((rows, D), BF16),
            pltpu.VMEM((rows, LANES), F32),
            pltpu.VMEM((rows, D), F32),
            pltpu.VMEM((rows // MOE_BLK, tm, MOE_BLK), BF16),
            pltpu.SMEM((2 * N_GROUPS + 1,), jnp.int32),
        ],
        compiler_params=pltpu.CompilerParams(dimension_semantics=("arbitrary", "arbitrary"),
                                             vmem_limit_bytes=VMEM_LIMIT),
        name="moe",
    )(h2, x1, comb, ltri, w_up, w_down)


def _pick_tile(n, pref):
    t = min(n, pref)
    assert n % t == 0, (n, t)
    return t


def _rope_tables(S):
    half = HEAD_DIM_A // 2
    inv = ROPE_THETA ** (-jnp.arange(half, dtype=F32) / half)
    ang = jnp.arange(S, dtype=jnp.int32).astype(F32)[:, None] * inv[None, :]
    cos, sin = jnp.cos(ang), jnp.sin(ang)
    cos_t = jnp.concatenate([cos, cos, cos, cos], axis=-1)
    sin_t = jnp.concatenate([-sin, sin, -sin, sin], axis=-1)
    return cos_t, sin_t


def _layer(x, mem, g_mix, w_in, b_gate, g_qa, g_ka, g_idx_k, conv_w, conv_b, ln_g, ln_b,
           g_mem, w_mem_kv, g_qm, g_km, w_br_a, w_br_b, w_br_m, w_o, g_ffn,
           w_rg, b_rg, w_re, b_re, w_up, w_down):
    B, S, D = x.shape
    topk = min(TOPK_MAX, S // 4)
    tk = _pick_tile(S, 512)
    tq = _pick_tile(tk, 128)
    tt = _pick_tile(S, 512)
    tm = _pick_tile(B * S, 1024)

    sizes = (QA_W, N_KV_HEADS_A * HEAD_DIM_A, N_KV_HEADS_A * HEAD_DIM_A, QI_W, IDX_DIM, IDX_HEADS,
             CONV_IN_W, QM_W, N_BRANCHES * D)
    offs = np.concatenate([[0], np.cumsum(sizes)])
    col = lambda i: w_in[:, offs[i]:offs[i + 1]]
    w_qa, w_ka, w_va, w_qi, w_ki, w_wi, w_conv, w_qm, w_gate = (col(i) for i in range(9))
    hd = HEAD_DIM_A
    zeros_hd = jnp.zeros((D, hd), F32)
    ka_dup = [w_ka[:, g * hd:(g + 1) * hd] for g in range(N_KV_HEADS_A) for _ in range(2)]
    va_aug = [w for g in range(N_KV_HEADS_A) for w in (w_va[:, g * hd:(g + 1) * hd], zeros_hd)]
    w_dsa = jnp.concatenate(
        [w_qa, w_qi] + ka_dup + va_aug + [w_ki, w_ki, w_wi, jnp.zeros((D, WI_W - IDX_HEADS), F32)], axis=1).astype(BF16)
    vbias = jnp.tile(jnp.concatenate([jnp.zeros((hd,), F32), jnp.ones((hd,), F32)]), N_KV_HEADS_A).reshape(1, VA_W)
    seg = np.arange(QA_W) // hd
    bd = jnp.asarray(seg[:, None] == seg[None, :], BF16)
    cos_t, sin_t = _rope_tables(S)

    kmt, vm = _mem_kv(mem, g_mem, w_mem_kv, g_km)
    qat, qit, k, vt, ki, wit = _dsa_proj(
        x, g_mix.reshape(1, D), w_dsa, vbias,
        jnp.tile(g_qa, N_HEADS_A).reshape(1, QA_W), jnp.tile(g_ka, KA_W // hd).reshape(1, KA_W),
        jnp.tile(g_idx_k, KI_W // IDX_DIM).reshape(1, KI_W), cos_t, sin_t, bd, tk)
    o_a = _dsa_attn(qat, qit, wit, k, vt, ki, tq, topk)

    w_r = jnp.concatenate([w_re, w_rg, jnp.zeros((D, LANES - N_EXPERTS - N_GROUPS), F32)], axis=1)
    b_r = jnp.concatenate([b_re, b_rg, jnp.zeros((LANES - N_EXPERTS - N_GROUPS,), F32)]).reshape(1, LANES)
    w_r_hi = w_r.astype(BF16)
    w_r_lo = (w_r - w_r_hi.astype(F32)).astype(BF16)
    x1, h2, comb = _mixer(
        x, o_a, g_mix.reshape(1, D), w_conv.astype(BF16), w_qm.astype(BF16), w_gate.astype(BF16),
        b_gate.reshape(1, -1), conv_w.reshape(CONV_WIDTH, CONV_CH), conv_b.reshape(1, CONV_CH),
        ln_g.reshape(1, CONV_CH), ln_b.reshape(1, CONV_CH), g_qm.reshape(1, MEM_HEAD_DIM), kmt, vm,
        w_br_a.astype(BF16), w_br_b.astype(BF16), w_br_m.astype(BF16), w_o.astype(BF16),
        g_ffn.reshape(1, D), w_r_hi, w_r_lo, b_r, tt)

    out = _moe(h2.reshape(B * S, D), x1.reshape(B * S, D), comb.reshape(B * S, LANES),
               w_up.astype(BF16), w_down.astype(BF16), tm)
    return out.reshape(B, S, D)


def kernel(x, mem, g_mix, w_in, b_gate, g_qa, g_ka, g_idx_k, conv_w, conv_b, ln_g, ln_b, g_mem, w_mem_kv, g_qm, g_km,
           w_br_a, w_br_b, w_br_m, w_o, g_ffn, w_rg, b_rg, w_re, b_re, w_up, w_down):
    params = (g_mix, w_in, b_gate, g_qa, g_ka, g_idx_k, conv_w, conv_b, ln_g, ln_b, g_mem, w_mem_kv, g_qm, g_km,
              w_br_a, w_br_b, w_br_m, w_o, g_ffn, w_rg, b_rg, w_re, b_re, w_up, w_down)
    for l in range(g_mix.shape[0]):
        x = _layer(x, mem, *(p[l] for p in params))
    return x
```

```python
import functools

import jax
import jax.numpy as jnp
import numpy as np
from jax import lax
from jax.experimental import pallas as pl
from jax.experimental.pallas import tpu as pltpu

N_HEADS_A = 8
N_KV_HEADS_A = 2
HEAD_DIM_A = 64
IDX_HEADS = 8
IDX_DIM = 64
TOPK_MAX = 256
CONV_CH = 512
CONV_WIDTH = 31
MEM_HEADS = 4
MEM_HEAD_DIM = 128
N_BRANCHES = 3
N_GROUPS = 4
EXPERTS_PER_GROUP = 8
N_EXPERTS = N_GROUPS * EXPERTS_PER_GROUP
D_FF_EXPERT = 256
ROPE_THETA = 10000.0
EPS = 1e-6

LANES = 128
SUBLANES = 8
VMEM_LIMIT = 56 * 1024 * 1024

INT_MIN = -2 ** 31
LOG2_E = 1.4426950408889634
HALF_BIAS = 2 ** 15
HALF_WORD = 2 ** 16
CHUNKS_PER_TRIP = 4
NEG_BIG = -1e30
CONV_HALO = 32

BF16 = jnp.bfloat16
F32 = jnp.float32


def _dot(a, b):
    return jnp.dot(a, b, preferred_element_type=F32)


def _sigmoid(x):
    return 0.5 * jnp.tanh(0.5 * x) + 0.5


def _split_bf16(x):
    hi = x.astype(BF16)
    lo = (x - hi.astype(F32)).astype(BF16)
    return hi, lo


def _seg_sum(xsq, ones_bd):
    hi, lo = _split_bf16(xsq)
    return _dot(hi, ones_bd) + _dot(lo, ones_bd)


def _rot_half(x, head_dim):
    n = x.shape[-1]
    half = head_dim // 2
    lane = lax.broadcasted_iota(jnp.int32, x.shape, x.ndim - 1)
    is_lo = (lane & (head_dim - 1)) < half
    return jnp.where(is_lo, pltpu.roll(x, n - half, x.ndim - 1), pltpu.roll(x, half, x.ndim - 1))


def _tile_lanes(x, reps):
    return jnp.concatenate([x] * reps, axis=-1) if reps > 1 else x


def _mem_kv_kernel(mem_ref, g_mem_ref, w_ref, g_km_ref, kt_ref, v_ref):
    m = mem_ref[0]
    hm = m * lax.rsqrt(jnp.mean(m * m, axis=-1, keepdims=True) + EPS) * g_mem_ref[...]
    kv = _dot(hm.astype(BF16), w_ref[...])
    hd = MEM_HEADS * MEM_HEAD_DIM
    for h in range(MEM_HEADS):
        k = kv[:, h * MEM_HEAD_DIM:(h + 1) * MEM_HEAD_DIM]
        k = k * lax.rsqrt(jnp.mean(k * k, axis=-1, keepdims=True) + EPS) * g_km_ref[...]
        kt_ref[0, h] = k.T.astype(BF16)
        v_ref[0, h] = kv[:, hd + h * MEM_HEAD_DIM: hd + (h + 1) * MEM_HEAD_DIM].astype(BF16)


def _mem_kv(mem, g_mem, w_mem_kv, g_km):
    B, M, D = mem.shape
    hd2 = 2 * MEM_HEADS * MEM_HEAD_DIM
    return pl.pallas_call(
        _mem_kv_kernel,
        grid=(B,),
        in_specs=[
            pl.BlockSpec((1, M, D), lambda b: (b, 0, 0)),
            pl.BlockSpec((1, D), lambda b: (0, 0)),
            pl.BlockSpec((D, hd2), lambda b: (0, 0)),
            pl.BlockSpec((1, MEM_HEAD_DIM), lambda b: (0, 0)),
        ],
        out_specs=[
            pl.BlockSpec((1, MEM_HEADS, MEM_HEAD_DIM, M), lambda b: (b, 0, 0, 0)),
            pl.BlockSpec((1, MEM_HEADS, M, MEM_HEAD_DIM), lambda b: (b, 0, 0, 0)),
        ],
        out_shape=[
            jax.ShapeDtypeStruct((B, MEM_HEADS, MEM_HEAD_DIM, M), BF16),
            jax.ShapeDtypeStruct((B, MEM_HEADS, M, MEM_HEAD_DIM), BF16),
        ],
        compiler_params=pltpu.CompilerParams(dimension_semantics=("arbitrary",)),
        name="mem_kv",
    )(mem, g_mem.reshape(1, D), w_mem_kv.astype(BF16), g_km.reshape(1, MEM_HEAD_DIM))


QA_W = N_HEADS_A * HEAD_DIM_A
QI_W = IDX_HEADS * IDX_DIM
KA_W = 2 * N_KV_HEADS_A * HEAD_DIM_A
VA_W = 2 * N_KV_HEADS_A * HEAD_DIM_A
KI_W = 2 * IDX_DIM
WI_W = LANES
DSA_COLS = QA_W + QI_W + KA_W + VA_W + KI_W + WI_W


def _dsa_proj_kernel(x_ref, g_mix_ref, w_ref, vbias_ref, gq_ref, gk_ref, gki_ref, cos_ref, sin_ref, bd_ref,
                     qat_ref, qit_ref, k_ref, vt_ref, ki_ref, wit_ref):
    x = x_ref[0]
    h = x * lax.rsqrt(jnp.mean(x * x, axis=-1, keepdims=True) + EPS) * g_mix_ref[...]
    p = _dot(h.astype(BF16), w_ref[...])
    cos1 = cos_ref[...]
    sin1 = sin_ref[...]
    cos4, sin4 = _tile_lanes(cos1, 4), _tile_lanes(sin1, 4)
    cos2, sin2 = _tile_lanes(cos1, 2), _tile_lanes(sin1, 2)
    bd = bd_ref[...]
    inv_hd = 1.0 / HEAD_DIM_A

    def rope(v, c, s):
        return v * c + _rot_half(v, HEAD_DIM_A) * s

    o = 0
    qa = p[:, o:o + QA_W]; o += QA_W
    qa = qa * lax.rsqrt(_seg_sum(qa * qa, bd) * inv_hd + EPS) * gq_ref[...]
    qat_ref[0] = (rope(qa, cos4, sin4) * (HEAD_DIM_A ** -0.5 * LOG2_E)).T.astype(BF16)

    qi = p[:, o:o + QI_W]; o += QI_W
    qit_ref[0] = (rope(qi, cos4, sin4) * (IDX_DIM ** -0.5)).T.astype(BF16)

    ka = p[:, o:o + KA_W]; o += KA_W
    ka = ka * lax.rsqrt(_seg_sum(ka * ka, bd[:KA_W, :KA_W]) * inv_hd + EPS) * gk_ref[...]
    k_ref[0] = rope(ka, cos2, sin2).astype(BF16)

    vt_ref[0, 0] = (p[:, o:o + VA_W] + vbias_ref[...]).T.astype(BF16); o += VA_W

    ki = p[:, o:o + KI_W]; o += KI_W
    ki = ki * lax.rsqrt(_seg_sum(ki * ki, bd[:KI_W, :KI_W]) * (1.0 / IDX_DIM) + EPS) * gki_ref[...]
    ki_ref[0] = rope(ki, cos1, sin1).astype(BF16)

    wit_ref[0] = (p[:, o:o + WI_W] * (IDX_HEADS ** -0.5)).T[:IDX_HEADS, :]


def _dsa_proj(x, g_mix, w_dsa, vbias, g_qa, g_ka, g_idx_k, cos_t, sin_t, bd, tt):
    B, S, D = x.shape
    nt = S // tt
    const = lambda b, s: (0, 0)
    return pl.pallas_call(
        _dsa_proj_kernel,
        grid=(B, nt),
        in_specs=[
            pl.BlockSpec((1, tt, D), lambda b, s: (b, s, 0)),
            pl.BlockSpec((1, D), const),
            pl.BlockSpec((D, DSA_COLS), const),
            pl.BlockSpec((1, VA_W), const),
            pl.BlockSpec((1, QA_W), const),
            pl.BlockSpec((1, KA_W), const),
            pl.BlockSpec((1, KI_W), const),
            pl.BlockSpec((tt, LANES), lambda b, s: (s, 0)),
            pl.BlockSpec((tt, LANES), lambda b, s: (s, 0)),
            pl.BlockSpec((QA_W, QA_W), const),
        ],
        out_specs=[
            pl.BlockSpec((1, QA_W, tt), lambda b, s: (b, 0, s)),
            pl.BlockSpec((1, QI_W, tt), lambda b, s: (b, 0, s)),
            pl.BlockSpec((1, tt, KA_W), lambda b, s: (b, s, 0)),
            pl.BlockSpec((1, 1, VA_W, tt), lambda b, s: (b, s, 0, 0)),
            pl.BlockSpec((1, tt, KI_W), lambda b, s: (b, s, 0)),
            pl.BlockSpec((1, IDX_HEADS, tt), lambda b, s: (b, 0, s)),
        ],
        out_shape=[
            jax.ShapeDtypeStruct((B, QA_W, S), BF16),
            jax.ShapeDtypeStruct((B, QI_W, S), BF16),
            jax.ShapeDtypeStruct((B, S, KA_W), BF16),
            jax.ShapeDtypeStruct((B, nt, VA_W, tt), BF16),
            jax.ShapeDtypeStruct((B, S, KI_W), BF16),
            jax.ShapeDtypeStruct((B, IDX_HEADS, S), F32),
        ],
        compiler_params=pltpu.CompilerParams(dimension_semantics=("arbitrary", "arbitrary"),
                                             vmem_limit_bytes=VMEM_LIMIT),
        name="dsa_proj",
    )(x, g_mix, w_dsa, vbias, g_qa, g_ka, g_idx_k, cos_t, sin_t, bd)


def _key_of(f):
    bits = lax.bitcast_convert_type(f, jnp.int32)
    return bits ^ ((bits >> 31) & jnp.int32(0x7FFFFFFF))


def _pack_high_halves(key):
    m = key.shape[0] // 2
    return lax.shift_right_logical(key[:m], 16) | (key[m:] & jnp.int32(-HALF_WORD))


def _pack_low_halves(key):
    m = key.shape[0] // 2
    return ((key[:m] & 0xFFFF) | (key[m:] << 16)) ^ jnp.int32(-HALF_BIAS * HALF_WORD + HALF_BIAS)


def _halves_of(words):
    return pltpu.bitcast(words, jnp.int16)


def _pipelined_chunks(nchunks, last_chunk, matmul, finish):
    matmul(0, 0)

    def straight(c0, n, last):
        for j in range(n):
            if j + 1 < n or not last:
                matmul(c0 + j + 1, (j + 1) % 2)
            finish(c0 + j, j % 2)

    def trip(i, carry):
        straight(CHUNKS_PER_TRIP * i, CHUNKS_PER_TRIP, False)
        return carry

    trips = last_chunk // CHUNKS_PER_TRIP
    lax.fori_loop(0, trips, trip, 0)
    for r in range(CHUNKS_PER_TRIP):
        @pl.when(last_chunk % CHUNKS_PER_TRIP == r)
        def _(r=r):
            straight(trips * CHUNKS_PER_TRIP, r + 1, True)


def _dsa_attn_kernel(qat_ref, qit_ref, wit_ref, k_ref, vt_ref, ki_ref, o_ref,
                     qa_s, qi_s, keys_s, khi_s, klo_s, zlo_s, buf0, buf1, m_s, acc_s, *, tq, tk, topk, idx_bits):
    qt = pl.program_id(1)
    q0 = qt * tq
    nchunks = q0 // tk + 1
    last_chunk = nchunks - 1
    hpg = N_HEADS_A // N_KV_HEADS_A
    gw = hpg * tq
    half_pairs = hpg // 2
    bufs = (buf0, buf1)
    dim = lax.broadcasted_iota(jnp.int32, (LANES, tq), 0)
    lo_half = dim < HEAD_DIM_A

    for j in range(IDX_HEADS // 2):
        pair = qit_ref[0, j * LANES:(j + 1) * LANES, :]
        qi_s[:, (2 * j) * tq:(2 * j + 1) * tq] = jnp.where(lo_half, pair, jnp.zeros_like(pair))
        qi_s[:, (2 * j + 1) * tq:(2 * j + 2) * tq] = jnp.where(lo_half, jnp.zeros_like(pair), pair)
    for g in range(N_KV_HEADS_A):
        for jj in range(half_pairs):
            pair = qat_ref[0, (g * half_pairs + jj) * LANES:(g * half_pairs + jj + 1) * LANES, :]
            qa_s[:, g * gw + jj * tq:g * gw + (jj + 1) * tq] = jnp.where(lo_half, pair, jnp.zeros_like(pair))
            qa_s[:, g * gw + (half_pairs + jj) * tq:g * gw + (half_pairs + jj + 1) * tq] = (
                jnp.where(lo_half, jnp.zeros_like(pair), pair))

    key_in_chunk = lax.broadcasted_iota(jnp.int32, (tk, tq), 0)
    q_pos = q0 + lax.broadcasted_iota(jnp.int32, (tk, tq), 1)

    def score_chunk(c, diagonal=False):
        r0 = pl.multiple_of(c * tk, tk)
        logits = _dot(ki_ref[0, pl.ds(r0, tk), :], qi_s[...])
        score = jnp.zeros((tk, tq), F32)
        for h in range(IDX_HEADS):
            score = score + wit_ref[0, h:h + 1, :] * jnp.maximum(logits[:, h * tq:(h + 1) * tq], 0.0)
        key = _key_of(score)
        if diagonal:
            key = jnp.where(key_in_chunk + c * tk <= q_pos, key, jnp.int32(INT_MIN))
        keys_s[c] = key
        khi_s[c] = _pack_high_halves(key)
        klo_s[c] = _pack_low_halves(key)

    def score_trip(i, carry):
        for j in range(CHUNKS_PER_TRIP):
            score_chunk(CHUNKS_PER_TRIP * i + j)
        return carry

    lax.fori_loop(0, last_chunk // CHUNKS_PER_TRIP, score_trip, 0)
    for r in range(CHUNKS_PER_TRIP):
        @pl.when(last_chunk % CHUNKS_PER_TRIP == r)
        def _(r=r):
            for j in range(r):
                score_chunk(last_chunk - r + j)
            score_chunk(last_chunk, diagonal=True)

    def count_keys(pred_fn):
        def one(c, acc):
            hit = pred_fn(keys_s[c], c * tk).astype(jnp.int32)
            return acc + jnp.sum(hit.reshape(tk // 8, 8, tq), axis=0)

        def two(i, acc):
            return one(2 * i + 1, one(2 * i, acc))

        part = lax.fori_loop(0, nchunks // 2, two, jnp.zeros((8, tq), jnp.int32))
        part = lax.cond(nchunks % 2 == 1, lambda acc: one(last_chunk, acc), lambda acc: acc, part)
        return jnp.sum(part, axis=0, keepdims=True)

    total = q0 + lax.broadcasted_iota(jnp.int32, (1, tq), 1) + 1
    has_tau = total >= topk

    def count_halves(src_s, cand, left):
        cand16 = _halves_of(jnp.broadcast_to((cand & 0xFFFF) | (cand << 16), (SUBLANES, tq)))

        def one(c, acc):
            half = _halves_of(src_s[c])
            hits = [jnp.where(half[j * 2 * SUBLANES:(j + 1) * 2 * SUBLANES, :] >= cand16, jnp.int16(1), jnp.int16(0))
                    for j in range(tk // (2 * SUBLANES))]
            while len(hits) > 1:
                hits = [a + b for a, b in zip(hits[::2], hits[1::2])]
            return acc + hits[0]

        def trip(i, acc):
            for j in range(CHUNKS_PER_TRIP):
                acc = one(CHUNKS_PER_TRIP * i + j, acc)
            return acc

        acc = lax.fori_loop(0, nchunks // CHUNKS_PER_TRIP, trip, jnp.zeros((2 * SUBLANES, tq), jnp.int16))
        for j in range(left):
            acc = one(nchunks - left + j, acc)
        words = pltpu.bitcast(acc, jnp.int32)
        return jnp.sum((words & 0xFFFF) + (words >> 16), axis=0, keepdims=True)

    def radix16(src_s, want, left):
        def step(i, state):
            v, cnt_ge, cnt_gt = state
            cand = v + (jnp.int32(1) << (15 - i))
            c = count_halves(src_s, cand, left)
            up = c >= want
            return jnp.where(up, cand, v), jnp.where(up, c, cnt_ge), jnp.where(up, cnt_gt, c)
        zero = jnp.zeros((1, tq), jnp.int32)
        return lax.fori_loop(0, 16, step, (jnp.full((1, tq), -HALF_BIAS, jnp.int32), zero, zero))

    def select(left):
        tau_hi, cnt_hi_ge, cnt_hi_gt = radix16(khi_s, topk, left)
        tau_hi16 = _halves_of(jnp.broadcast_to((tau_hi & 0xFFFF) | (tau_hi << 16), (SUBLANES, tq)))

        def low_of_bucket(c, carry):
            hi, lo = _halves_of(khi_s[c]), _halves_of(klo_s[c])
            for j in range(tk // (2 * SUBLANES)):
                rows = slice(j * 2 * SUBLANES, (j + 1) * 2 * SUBLANES)
                zlo_s[c, j * SUBLANES:(j + 1) * SUBLANES, :] = pltpu.bitcast(
                    jnp.where(hi[rows, :] == tau_hi16, lo[rows, :], jnp.int16(-HALF_BIAS)), jnp.int32)
            return carry

        lax.fori_loop(0, nchunks, low_of_bucket, 0)
        tau_lo, cnt_lo_ge, _ = radix16(zlo_s, topk - cnt_hi_gt, left)
        return tau_hi, cnt_hi_ge, cnt_hi_gt, tau_lo, cnt_lo_ge

    left = nchunks % CHUNKS_PER_TRIP
    tau_hi, cnt_hi_ge, cnt_hi_gt, tau_lo, cnt_lo_ge = lax.cond(
        left < 2,
        lambda: lax.cond(left == 0, lambda: select(0), lambda: select(1)),
        lambda: lax.cond(left == 2, lambda: select(2), lambda: select(3)))
    tau = (tau_hi << 16) | ((tau_lo + HALF_BIAS) & 0xFFFF)
    cnt = cnt_hi_gt + jnp.where(tau_lo == -HALF_BIAS, cnt_hi_ge - cnt_hi_gt, cnt_lo_ge)
    tau = jnp.where(has_tau, jnp.maximum(tau, jnp.int32(INT_MIN + 1)), jnp.int32(INT_MIN))

    tied = has_tau & (cnt > topk)

    def tie_cut():
        need = topk - count_keys(lambda k, _: k > tau)

        def idx_step(i, j):
            bit = jnp.int32(1) << (idx_bits - 1 - i)
            test = j + bit - 1
            f = count_keys(lambda k, base: (k == tau) & (base + key_in_chunk <= test))
            return jnp.where(f < need, j + bit, j)
        return lax.fori_loop(0, idx_bits, idx_step, jnp.zeros((1, tq), jnp.int32))

    jcut = lax.cond(jnp.max(jnp.where(tied, 1, 0)) > 0, tie_cut, lambda: jnp.zeros((1, tq), jnp.int32))
    jcut = jnp.where(tied, jcut, jnp.where(has_tau, jnp.int32(2 ** 30), jnp.int32(-1)))

    m_s[...] = jnp.full(m_s.shape, NEG_BIG, F32)
    acc_s[...] = jnp.zeros(acc_s.shape, F32)

    def attn_matmul(c, slot):
        r0 = pl.multiple_of(c * tk, tk)
        k = keys_s[c]
        kpos = key_in_chunk + c * tk
        bias = jnp.where(k > tau, 0.0, jnp.where(k == tau, jnp.where(kpos <= jcut, 0.0, NEG_BIG), NEG_BIG))
        bias4 = jnp.concatenate([bias] * hpg, axis=1)
        for g in range(N_KV_HEADS_A):
            bufs[slot][:, g * gw:(g + 1) * gw] = _dot(k_ref[0, pl.ds(r0, tk), g * LANES:(g + 1) * LANES],
                                                      qa_s[:, g * gw:(g + 1) * gw]) + bias4

    def attn_finish(c, slot):
        for g in range(N_KV_HEADS_A):
            m_old = m_s[g]
            m_new = jnp.maximum(m_old, jnp.max(bufs[slot][:, g * gw:(g + 1) * gw], axis=0, keepdims=True))
            p = jnp.exp2(bufs[slot][:, g * gw:(g + 1) * gw] - m_new)
            alpha = jnp.exp2(m_old - m_new)
            pv = _dot(vt_ref[0, c, g * LANES:(g + 1) * LANES, :], p.astype(BF16))
            acc_s[g] = alpha * acc_s[g] + pv
            m_s[g] = m_new

    _pipelined_chunks(nchunks, last_chunk, attn_matmul, attn_finish)

    for g in range(N_KV_HEADS_A):
        for jj in range(half_pairs):
            a_e = acc_s[g, :, jj * tq:(jj + 1) * tq]
            a_o = acc_s[g, :, (half_pairs + jj) * tq:(half_pairs + jj + 1) * tq]
            o_pair = jnp.concatenate([a_e[:HEAD_DIM_A] / a_e[HEAD_DIM_A:], a_o[:HEAD_DIM_A] / a_o[HEAD_DIM_A:]], axis=0)
            j = g * half_pairs + jj
            o_ref[0, :, j * LANES:(j + 1) * LANES] = o_pair.T.astype(BF16)


def _dsa_attn(qat, qit, wit, k, vt, ki, tq, topk):
    B, _, S = qat.shape
    nc, tk = vt.shape[1], vt.shape[3]
    hpg = N_HEADS_A // N_KV_HEADS_A
    idx_bits = max(1, int(np.ceil(np.log2(S))))
    assert IDX_HEADS == N_HEADS_A
    assert CHUNKS_PER_TRIP == 4
    kernel = functools.partial(_dsa_attn_kernel, tq=tq, tk=tk, topk=topk, idx_bits=idx_bits)
    return pl.pallas_call(
        kernel,
        grid=(B, S // tq),
        in_specs=[
            pl.BlockSpec((1, QA_W, tq), lambda b, q: (b, 0, q)),
            pl.BlockSpec((1, QI_W, tq), lambda b, q: (b, 0, q)),
            pl.BlockSpec((1, IDX_HEADS, tq), lambda b, q: (b, 0, q)),
            pl.BlockSpec((1, S, KA_W), lambda b, q: (b, 0, 0)),
            pl.BlockSpec((1, nc, VA_W, tk), lambda b, q: (b, 0, 0, 0)),
            pl.BlockSpec((1, S, KI_W), lambda b, q: (b, 0, 0)),
        ],
        out_specs=pl.BlockSpec((1, tq, QA_W), lambda b, q: (b, q, 0)),
        out_shape=jax.ShapeDtypeStruct((B, S, QA_W), BF16),
        scratch_shapes=[
            pltpu.VMEM((LANES, N_HEADS_A * tq), BF16),
            pltpu.VMEM((LANES, IDX_HEADS * tq), BF16),
            pltpu.VMEM((nc, tk, tq), jnp.int32),
            pltpu.VMEM((nc, tk // 2, tq), jnp.int32),
            pltpu.VMEM((nc, tk // 2, tq), jnp.int32),
            pltpu.VMEM((nc, tk // 2, tq), jnp.int32),
            pltpu.VMEM((tk, IDX_HEADS * tq), F32),
            pltpu.VMEM((tk, IDX_HEADS * tq), F32),
            pltpu.VMEM((N_KV_HEADS_A, 1, hpg * tq), F32),
            pltpu.VMEM((N_KV_HEADS_A, LANES, hpg * tq), F32),
        ],
        compiler_params=pltpu.CompilerParams(dimension_semantics=("arbitrary", "arbitrary"),
                                             vmem_limit_bytes=VMEM_LIMIT),
        name="dsa_attn",
    )(qat, qit, wit, k, vt, ki)


CONV_IN_W = 2 * CONV_CH
QM_W = MEM_HEADS * MEM_HEAD_DIM


def _mixer_kernel(x_ref, oa_ref, g_mix_ref, wc_ref, wq_ref, wg_ref, bg_ref, cw_ref, cb_ref, lng_ref, lnb_ref,
                  gqm_ref, kmt_ref, vm_ref, wa_ref, wb_ref, wm_ref, wo_ref, gffn_ref, wrh_ref, wrl_ref, br_ref,
                  x1_ref, h2_ref, comb_ref, ubuf, wbuf, *, tt):
    D = x_ref.shape[-1]

    @pl.when(pl.program_id(1) == 0)
    def _():
        ubuf[0:CONV_HALO, :] = jnp.zeros((CONV_HALO, CONV_CH), F32)

    x = x_ref[0]
    h = (x * lax.rsqrt(jnp.mean(x * x, axis=-1, keepdims=True) + EPS) * g_mix_ref[...]).astype(BF16)

    cin = _dot(h, wc_ref[...])
    u = cin[:, :CONV_CH] * _sigmoid(cin[:, CONV_CH:])

    ubuf[CONV_HALO:CONV_HALO + tt, :] = u
    y = jnp.zeros((tt, CONV_CH), F32) + cb_ref[...]
    first = CONV_HALO - (CONV_WIDTH - 1)
    for b in range(SUBLANES):
        taps = [j for j in range(CONV_WIDTH) if (first + j) % SUBLANES == b]
        if not taps:
            continue
        base = first + taps[0]
        if b == 0:
            window, w0 = ubuf, base
        else:
            span = taps[-1] - taps[0] + tt
            wbuf[0:span, :] = ubuf[base:base + span, :]
            window, w0 = wbuf, 0
        for j in taps:
            off = w0 + j - taps[0]
            y = y + cw_ref[j:j + 1, :] * window[off:off + tt, :]
    halo = ubuf[tt:tt + CONV_HALO, :]
    ubuf[0:CONV_HALO, :] = halo
    mu = jnp.mean(y, axis=-1, keepdims=True)
    yc = y - mu
    var = jnp.mean(yc * yc, axis=-1, keepdims=True)
    yn = yc * lax.rsqrt(var + EPS) * lng_ref[...] + lnb_ref[...]
    o_b = (yn * _sigmoid(yn)).astype(BF16)

    qm = _dot(h, wq_ref[...])
    o_m = []
    for hh in range(MEM_HEADS):
        q = qm[:, hh * MEM_HEAD_DIM:(hh + 1) * MEM_HEAD_DIM]
        q = q * lax.rsqrt(jnp.mean(q * q, axis=-1, keepdims=True) + EPS) * gqm_ref[...] * (MEM_HEAD_DIM ** -0.5)
        s = _dot(q.astype(BF16), kmt_ref[0, hh])
        p = jnp.exp(s - jnp.max(s, axis=-1, keepdims=True))
        l = jnp.sum(p, axis=-1, keepdims=True)
        o_m.append((_dot(p.astype(BF16), vm_ref[0, hh]) / l).astype(BF16))
    o_m = jnp.concatenate(o_m, axis=-1)

    gates = _sigmoid(_dot(h, wg_ref[...]) + bg_ref[...])
    merged = (gates[:, :D] * _dot(oa_ref[0], wa_ref[...])
              + gates[:, D:2 * D] * _dot(o_b, wb_ref[...])
              + gates[:, 2 * D:] * _dot(o_m, wm_ref[...]))
    x1 = x + _dot(merged.astype(BF16), wo_ref[...])
    x1_ref[0] = x1

    h2 = x1 * lax.rsqrt(jnp.mean(x1 * x1, axis=-1, keepdims=True) + EPS) * gffn_ref[...]
    h2_ref[0] = h2.astype(BF16)
    hi, lo = _split_bf16(h2)
    logits = _dot(hi, wrh_ref[...]) + _dot(hi, wrl_ref[...]) + _dot(lo, wrh_ref[...]) + br_ref[...]
    lane = lax.broadcasted_iota(jnp.int32, logits.shape, 1)
    ninf = jnp.float32(-jnp.inf)
    big = jnp.int32(1 << 20)
    is_g = (lane >= N_EXPERTS) & (lane < N_EXPERTS + N_GROUPS)
    gl = jnp.where(is_g, logits, ninf)
    gmax = jnp.max(gl, axis=-1, keepdims=True)
    gsel = jnp.min(jnp.where(gl == gmax, lane, big), axis=-1, keepdims=True) - N_EXPERTS
    p_sel = 1.0 / jnp.sum(jnp.where(is_g, jnp.exp(gl - gmax), 0.0), axis=-1, keepdims=True)
    in_grp = (lane < N_EXPERTS) & ((lane >> 3) == gsel)
    el = jnp.where(in_grp, logits, ninf)
    v1 = jnp.max(el, axis=-1, keepdims=True)
    i1 = jnp.min(jnp.where(el == v1, lane, big), axis=-1, keepdims=True)
    el2 = jnp.where(lane == i1, ninf, el)
    v2 = jnp.max(el2, axis=-1, keepdims=True)
    i2 = jnp.min(jnp.where(el2 == v2, lane, big), axis=-1, keepdims=True)
    e2 = jnp.exp(v2 - v1)
    den = 1.0 + e2
    comb_ref[0] = jnp.where(lane == i1, (1.0 / den) * p_sel,
                            jnp.where(lane == i2, (e2 / den) * p_sel,
                                      jnp.where(lane == N_EXPERTS, gsel.astype(F32), 0.0)))


def _mixer(x, oa, g_mix, wc, wq, wg, bg, cw, cb, lng, lnb, gqm, kmt, vm, wa, wb, wm, wo, gffn, wrh, wrl, br, tt):
    B, S, D = x.shape
    M = kmt.shape[-1]
    const = lambda b, s: (0, 0)
    tile = lambda w: pl.BlockSpec((1, tt, w), lambda b, s: (b, s, 0))
    full = lambda a: pl.BlockSpec(a.shape, const)
    return pl.pallas_call(
        functools.partial(_mixer_kernel, tt=tt),
        grid=(B, S // tt),
        in_specs=[
            tile(D), tile(QA_W), full(g_mix), full(wc), full(wq), full(wg), full(bg), full(cw), full(cb),
            full(lng), full(lnb), full(gqm),
            pl.BlockSpec((1, MEM_HEADS, MEM_HEAD_DIM, M), lambda b, s: (b, 0, 0, 0)),
            pl.BlockSpec((1, MEM_HEADS, M, MEM_HEAD_DIM), lambda b, s: (b, 0, 0, 0)),
            full(wa), full(wb), full(wm), full(wo), full(gffn), full(wrh), full(wrl), full(br),
        ],
        out_specs=[tile(D), tile(D), tile(LANES)],
        out_shape=[
            jax.ShapeDtypeStruct((B, S, D), F32),
            jax.ShapeDtypeStruct((B, S, D), BF16),
            jax.ShapeDtypeStruct((B, S, LANES), F32),
        ],
        scratch_shapes=[pltpu.VMEM((CONV_HALO + tt, CONV_CH), F32),
                        pltpu.VMEM((CONV_HALO + tt, CONV_CH), F32)],
        compiler_params=pltpu.CompilerParams(dimension_semantics=("arbitrary", "arbitrary"),
                                             vmem_limit_bytes=VMEM_LIMIT),
        name="mixer",
    )(x, oa, g_mix, wc, wq, wg, bg, cw, cb, lng, lnb, gqm, kmt, vm, wa, wb, wm, wo, gffn, wrh, wrl, br)


MOE_BLK = 256
MOE_EXPERTS_PER_STEP = 4


def _loop_in_pairs(n, body):
    def two(i, carry):
        body(2 * i)
        body(2 * i + 1)
        return carry

    lax.fori_loop(0, n // 2, two, 0)

    @pl.when(n % 2 == 1)
    def _():
        body(n - 1)


def _moe_kernel(h2_ref, x1_ref, comb_ref, ltri_ref, wup_ref, wdn_ref, o_ref,
                hs_s, combs_s, ys_s, pt_s, seg_s, *, tm, rows):
    n = pl.program_id(1)
    lane = lax.broadcasted_iota(jnp.int32, (tm, LANES), 1)

    @pl.when(n == 0)
    def _():
        comb = comb_ref[...]
        gid = comb[:, N_EXPERTS:N_EXPERTS + 1].astype(jnp.int32)
        member = lane == gid
        onehot = jnp.where(member, 1.0, 0.0)
        before = _dot(ltri_ref[...], onehot.astype(BF16))
        rank = jnp.sum(jnp.where(member, before, 0.0), axis=-1, keepdims=True)
        count = jnp.sum(onehot, axis=0, keepdims=True)
        lane1 = lax.broadcasted_iota(jnp.int32, (1, LANES), 1)
        start_blk = jnp.int32(0)
        start_row = jnp.zeros((1, LANES), F32)
        for g in range(N_GROUPS):
            cnt_g = jnp.sum(jnp.where(lane1 == g, count, 0.0)).astype(jnp.int32)
            nblk_g = (cnt_g + (MOE_BLK - 1)) // MOE_BLK
            seg_s[g] = start_blk
            seg_s[N_GROUPS + g] = nblk_g
            start_row = jnp.where(lane1 == g, (start_blk * MOE_BLK).astype(F32), start_row)
            start_blk = start_blk + nblk_g
        dest = jnp.sum(jnp.where(member, start_row, 0.0), axis=-1, keepdims=True) + rank
        seg_s[2 * N_GROUPS] = start_blk
        dest_i = dest.astype(jnp.int32)
        dest_row = jnp.broadcast_to(dest, (tm, LANES)).T[0:1, :].astype(jnp.int32)
        c_hi, c_lo = _split_bf16(comb)
        col = lax.broadcasted_iota(jnp.int32, (tm, MOE_BLK), 1)
        row = lax.broadcasted_iota(jnp.int32, (MOE_BLK, tm), 0)

        def permute_block(b):
            r0 = pl.multiple_of(b * MOE_BLK, MOE_BLK)
            pt_s[b] = jnp.where(col + r0 == dest_i, 1.0, 0.0).astype(BF16)
            perm = jnp.where(row + r0 == dest_row, 1.0, 0.0).astype(BF16)
            hs_s[pl.ds(r0, MOE_BLK), :] = _dot(perm, h2_ref[...]).astype(BF16)
            combs_s[pl.ds(r0, MOE_BLK), :] = _dot(perm, c_hi) + _dot(perm, c_lo)
            ys_s[pl.ds(r0, MOE_BLK), :] = jnp.zeros((MOE_BLK, ys_s.shape[1]), F32)

        _loop_in_pairs(start_blk, permute_block)

    g = (n * MOE_EXPERTS_PER_STEP) // EXPERTS_PER_GROUP
    first_blk = seg_s[g]
    lane_b = lax.broadcasted_iota(jnp.int32, (MOE_BLK, LANES), 1)

    def expert_block(b):
        r0 = pl.multiple_of((first_blk + b) * MOE_BLK, MOE_BLK)
        hb = hs_s[pl.ds(r0, MOE_BLK), :]
        cb = combs_s[pl.ds(r0, MOE_BLK), :]
        acts = []
        for i in range(MOE_EXPERTS_PER_STEP):
            up = _dot(hb, wup_ref[i])
            a = up[:, :D_FF_EXPERT]
            c = jnp.sum(jnp.where(lane_b == n * MOE_EXPERTS_PER_STEP + i, cb, 0.0), axis=-1, keepdims=True)
            acts.append((a * _sigmoid(a) * up[:, D_FF_EXPERT:] * c).astype(BF16))
        w_dn = wdn_ref[...].reshape(MOE_EXPERTS_PER_STEP * D_FF_EXPERT, wdn_ref.shape[-1])
        ys_s[pl.ds(r0, MOE_BLK), :] += _dot(jnp.concatenate(acts, axis=-1), w_dn)

    _loop_in_pairs(seg_s[N_GROUPS + g], expert_block)

    @pl.when(n == N_EXPERTS // MOE_EXPERTS_PER_STEP - 1)
    def _():
        o_ref[...] = x1_ref[...]

        def token_order(b):
            r0 = pl.multiple_of(b * MOE_BLK, MOE_BLK)
            return _dot(pt_s[b], ys_s[pl.ds(r0, MOE_BLK), :].astype(BF16))

        def unpermute_pair(i, carry):
            o_ref[...] += token_order(2 * i) + token_order(2 * i + 1)
            return carry

        used = seg_s[2 * N_GROUPS]
        lax.fori_loop(0, used // 2, unpermute_pair, 0)

        @pl.when(used % 2 == 1)
        def _():
            o_ref[...] += token_order(used - 1)


def _moe(h2, x1, comb, w_up, w_down, tm):
    T, D = h2.shape
    rows = tm + N_GROUPS * MOE_BLK
    ltri = jnp.asarray(np.tril(np.ones((tm, tm), np.float32), -1), BF16)
    return pl.pallas_call(
        functools.partial(_moe_kernel, tm=tm, rows=rows),
        grid=(T // tm, N_EXPERTS // MOE_EXPERTS_PER_STEP),
        in_specs=[
            pl.BlockSpec((tm, D), lambda t, n: (t, 0)),
            pl.BlockSpec((tm, D), lambda t, n: (t, 0)),
            pl.BlockSpec((tm, LANES), lambda t, n: (t, 0)),
            pl.BlockSpec((tm, tm), lambda t, n: (0, 0)),
            pl.BlockSpec((MOE_EXPERTS_PER_STEP, D, 2 * D_FF_EXPERT), lambda t, n: (n, 0, 0)),
            pl.BlockSpec((MOE_EXPERTS_PER_STEP, D_FF_EXPERT, D), lambda t, n: (n, 0, 0)),
        ],
        out_specs=pl.BlockSpec((tm, D), lambda t, n: (t, 0)),
        out_shape=jax.ShapeDtypeStruct((T, D), F32),
        scratch_shapes=[
            pltpu.VMEM((rows, D), BF16),
            pltpu.VMEM((rows, LANES), F32),
            pltpu.VMEM((rows, D), F32),
            pltpu.VMEM((rows // MOE_BLK, tm, MOE_BLK), BF16),
            pltpu.SMEM((2 * N_GROUPS + 1,), jnp.int32),
        ],
        compiler_params=pltpu.CompilerParams(dimension_semantics=("arbitrary", "arbitrary"),
                                             vmem_limit_bytes=VMEM_LIMIT),
        name="moe",
    )(h2, x1, comb, ltri, w_up, w_down)


def _pick_tile(n, pref):
    t = min(n, pref)
    assert n % t == 0, (n, t)
    return t


def _rope_tables(S):
    half = HEAD_DIM_A // 2
    inv = ROPE_THETA ** (-jnp.arange(half, dtype=F32) / half)
    ang = jnp.arange(S, dtype=jnp.int32).astype(F32)[:, None] * inv[None, :]
    cos, sin = jnp.cos(ang), jnp.sin(ang)
    cos_t = jnp.concatenate([cos, cos, cos, cos], axis=-1)
    sin_t = jnp.concatenate([-sin, sin, -sin, sin], axis=-1)
    return cos_t, sin_t


def _layer(x, mem, g_mix, w_in, b_gate, g_qa, g_ka, g_idx_k, conv_w, conv_b, ln_g, ln_b,
           g_mem, w_mem_kv, g_qm, g_km, w_br_a, w_br_b, w_br_m, w_o, g_ffn,
           w_rg, b_rg, w_re, b_re, w_up, w_down):
    B, S, D = x.shape
    topk = min(TOPK_MAX, S // 4)
    tk = _pick_tile(S, 512)
    tq = _pick_tile(tk, 128)
    tt = _pick_tile(S, 512)
    tm = _pick_tile(B * S, 1024)

    sizes = (QA_W, N_KV_HEADS_A * HEAD_DIM_A, N_KV_HEADS_A * HEAD_DIM_A, QI_W, IDX_DIM, IDX_HEADS,
             CONV_IN_W, QM_W, N_BRANCHES * D)
    offs = np.concatenate([[0], np.cumsum(sizes)])
    col = lambda i: w_in[:, offs[i]:offs[i + 1]]
    w_qa, w_ka, w_va, w_qi, w_ki, w_wi, w_conv, w_qm, w_gate = (col(i) for i in range(9))
    hd = HEAD_DIM_A
    zeros_hd = jnp.zeros((D, hd), F32)
    ka_dup = [w_ka[:, g * hd:(g + 1) * hd] for g in range(N_KV_HEADS_A) for _ in range(2)]
    va_aug = [w for g in range(N_KV_HEADS_A) for w in (w_va[:, g * hd:(g + 1) * hd], zeros_hd)]
    w_dsa = jnp.concatenate(
        [w_qa, w_qi] + ka_dup + va_aug + [w_ki, w_ki, w_wi, jnp.zeros((D, WI_W - IDX_HEADS), F32)], axis=1).astype(BF16)
    vbias = jnp.tile(jnp.concatenate([jnp.zeros((hd,), F32), jnp.ones((hd,), F32)]), N_KV_HEADS_A).reshape(1, VA_W)
    seg = np.arange(QA_W) // hd
    bd = jnp.asarray(seg[:, None] == seg[None, :], BF16)
    cos_t, sin_t = _rope_tables(S)

    kmt, vm = _mem_kv(mem, g_mem, w_mem_kv, g_km)
    qat, qit, k, vt, ki, wit = _dsa_proj(
        x, g_mix.reshape(1, D), w_dsa, vbias,
        jnp.tile(g_qa, N_HEADS_A).reshape(1, QA_W), jnp.tile(g_ka, KA_W // hd).reshape(1, KA_W),
        jnp.tile(g_idx_k, KI_W // IDX_DIM).reshape(1, KI_W), cos_t, sin_t, bd, tk)
    o_a = _dsa_attn(qat, qit, wit, k, vt, ki, tq, topk)

    w_r = jnp.concatenate([w_re, w_rg, jnp.zeros((D, LANES - N_EXPERTS - N_GROUPS), F32)], axis=1)
    b_r = jnp.concatenate([b_re, b_rg, jnp.zeros((LANES - N_EXPERTS - N_GROUPS,), F32)]).reshape(1, LANES)
    w_r_hi = w_r.astype(BF16)
    w_r_lo = (w_r - w_r_hi.astype(F32)).astype(BF16)
    x1, h2, comb = _mixer(
        x, o_a, g_mix.reshape(1, D), w_conv.astype(BF16), w_qm.astype(BF16), w_gate.astype(BF16),
        b_gate.reshape(1, -1), conv_w.reshape(CONV_WIDTH, CONV_CH), conv_b.reshape(1, CONV_CH),
        ln_g.reshape(1, CONV_CH), ln_b.reshape(1, CONV_CH), g_qm.reshape(1, MEM_HEAD_DIM), kmt, vm,
        w_br_a.astype(BF16), w_br_b.astype(BF16), w_br_m.astype(BF16), w_o.astype(BF16),
        g_ffn.reshape(1, D), w_r_hi, w_r_lo, b_r, tt)

    out = _moe(h2.reshape(B * S, D), x1.reshape(B * S, D), comb.reshape(B * S, LANES),
               w_up.astype(BF16), w_down.astype(BF16), tm)
    return out.reshape(B, S, D)


def kernel(x, mem, g_mix, w_in, b_gate, g_qa, g_ka, g_idx_k, conv_w, conv_b, ln_g, ln_b, g_mem, w_mem_kv, g_qm, g_km,
           w_br_a, w_br_b, w_br_m, w_o, g_ffn, w_rg, b_rg, w_re, b_re, w_up, w_down):
    params = (g_mix, w_in, b_gate, g_qa, g_ka, g_idx_k, conv_w, conv_b, ln_g, ln_b, g_mem, w_mem_kv, g_qm, g_km,
              w_br_a, w_br_b, w_br_m, w_o, g_ffn, w_rg, b_rg, w_re, b_re, w_up, w_down)
    for l in range(g_mix.shape[0]):
        x = _layer(x, mem, *(p[l] for p in params))
    return x
```

```python
import functools

import jax
import jax.numpy as jnp
import numpy as np
from jax import lax
from jax.experimental import pallas as pl
from jax.experimental.pallas import tpu as pltpu

N_HEADS_A = 8
N_KV_HEADS_A = 2
HEAD_DIM_A = 64
IDX_HEADS = 8
IDX_DIM = 64
TOPK_MAX = 256
CONV_CH = 512
CONV_WIDTH = 31
MEM_HEADS = 4
MEM_HEAD_DIM = 128
N_BRANCHES = 3
N_GROUPS = 4
EXPERTS_PER_GROUP = 8
N_EXPERTS = N_GROUPS * EXPERTS_PER_GROUP
D_FF_EXPERT = 256
ROPE_THETA = 10000.0
EPS = 1e-6

LANES = 128
SUBLANES = 8
VMEM_LIMIT = 56 * 1024 * 1024

INT_MIN = -2 ** 31
LOG2_E = 1.4426950408889634
HALF_BIAS = 2 ** 15
HALF_WORD = 2 ** 16
CHUNKS_PER_TRIP = 4
NEG_BIG = -1e30
CONV_HALO = 32

BF16 = jnp.bfloat16
F32 = jnp.float32


def _dot(a, b):
    return jnp.dot(a, b, preferred_element_type=F32)


def _sigmoid(x):
    return 0.5 * jnp.tanh(0.5 * x) + 0.5


def _split_bf16(x):
    hi = x.astype(BF16)
    lo = (x - hi.astype(F32)).astype(BF16)
    return hi, lo


def _seg_sum(xsq, ones_bd):
    hi, lo = _split_bf16(xsq)
    return _dot(hi, ones_bd) + _dot(lo, ones_bd)


def _rot_half(x, head_dim):
    n = x.shape[-1]
    half = head_dim // 2
    lane = lax.broadcasted_iota(jnp.int32, x.shape, x.ndim - 1)
    is_lo = (lane & (head_dim - 1)) < half
    return jnp.where(is_lo, pltpu.roll(x, n - half, x.ndim - 1), pltpu.roll(x, half, x.ndim - 1))


def _tile_lanes(x, reps):
    return jnp.concatenate([x] * reps, axis=-1) if reps > 1 else x


def _mem_kv_kernel(mem_ref, g_mem_ref, w_ref, g_km_ref, kt_ref, v_ref):
    m = mem_ref[0]
    hm = m * lax.rsqrt(jnp.mean(m * m, axis=-1, keepdims=True) + EPS) * g_mem_ref[...]
    kv = _dot(hm.astype(BF16), w_ref[...])
    hd = MEM_HEADS * MEM_HEAD_DIM
    for h in range(MEM_HEADS):
        k = kv[:, h * MEM_HEAD_DIM:(h + 1) * MEM_HEAD_DIM]
        k = k * lax.rsqrt(jnp.mean(k * k, axis=-1, keepdims=True) + EPS) * g_km_ref[...]
        kt_ref[0, h] = k.T.astype(BF16)
        v_ref[0, h] = kv[:, hd + h * MEM_HEAD_DIM: hd + (h + 1) * MEM_HEAD_DIM].astype(BF16)


def _mem_kv(mem, g_mem, w_mem_kv, g_km):
    B, M, D = mem.shape
    hd2 = 2 * MEM_HEADS * MEM_HEAD_DIM
    return pl.pallas_call(
        _mem_kv_kernel,
        grid=(B,),
        in_specs=[
            pl.BlockSpec((1, M, D), lambda b: (b, 0, 0)),
            pl.BlockSpec((1, D), lambda b: (0, 0)),
            pl.BlockSpec((D, hd2), lambda b: (0, 0)),
            pl.BlockSpec((1, MEM_HEAD_DIM), lambda b: (0, 0)),
        ],
        out_specs=[
            pl.BlockSpec((1, MEM_HEADS, MEM_HEAD_DIM, M), lambda b: (b, 0, 0, 0)),
            pl.BlockSpec((1, MEM_HEADS, M, MEM_HEAD_DIM), lambda b: (b, 0, 0, 0)),
        ],
        out_shape=[
            jax.ShapeDtypeStruct((B, MEM_HEADS, MEM_HEAD_DIM, M), BF16),
            jax.ShapeDtypeStruct((B, MEM_HEADS, M, MEM_HEAD_DIM), BF16),
        ],
        compiler_params=pltpu.CompilerParams(dimension_semantics=("arbitrary",)),
        name="mem_kv",
    )(mem, g_mem.reshape(1, D), w_mem_kv.astype(BF16), g_km.reshape(1, MEM_HEAD_DIM))


QA_W = N_HEADS_A * HEAD_DIM_A
QI_W = IDX_HEADS * IDX_DIM
KA_W = 2 * N_KV_HEADS_A * HEAD_DIM_A
VA_W = 2 * N_KV_HEADS_A * HEAD_DIM_A
KI_W = 2 * IDX_DIM
WI_W = LANES
DSA_COLS = QA_W + QI_W + KA_W + VA_W + KI_W + WI_W


def _dsa_proj_kernel(x_ref, g_mix_ref, w_ref, vbias_ref, gq_ref, gk_ref, gki_ref, cos_ref, sin_ref, bd_ref,
                     qat_ref, qit_ref, k_ref, vt_ref, ki_ref, wit_ref):
    x = x_ref[0]
    h = x * lax.rsqrt(jnp.mean(x * x, axis=-1, keepdims=True) + EPS) * g_mix_ref[...]
    p = _dot(h.astype(BF16), w_ref[...])
    cos1 = cos_ref[...]
    sin1 = sin_ref[...]
    cos4, sin4 = _tile_lanes(cos1, 4), _tile_lanes(sin1, 4)
    cos2, sin2 = _tile_lanes(cos1, 2), _tile_lanes(sin1, 2)
    bd = bd_ref[...]
    inv_hd = 1.0 / HEAD_DIM_A

    def rope(v, c, s):
        return v * c + _rot_half(v, HEAD_DIM_A) * s

    o = 0
    qa = p[:, o:o + QA_W]; o += QA_W
    qa = qa * lax.rsqrt(_seg_sum(qa * qa, bd) * inv_hd + EPS) * gq_ref[...]
    qat_ref[0] = (rope(qa, cos4, sin4) * (HEAD_DIM_A ** -0.5 * LOG2_E)).T.astype(BF16)

    qi = p[:, o:o + QI_W]; o += QI_W
    qit_ref[0] = (rope(qi, cos4, sin4) * (IDX_DIM ** -0.5)).T.astype(BF16)

    ka = p[:, o:o + KA_W]; o += KA_W
    ka = ka * lax.rsqrt(_seg_sum(ka * ka, bd[:KA_W, :KA_W]) * inv_hd + EPS) * gk_ref[...]
    k_ref[0] = rope(ka, cos2, sin2).astype(BF16)

    vt_ref[0, 0] = (p[:, o:o + VA_W] + vbias_ref[...]).T.astype(BF16); o += VA_W

    ki = p[:, o:o + KI_W]; o += KI_W
    ki = ki * lax.rsqrt(_seg_sum(ki * ki, bd[:KI_W, :KI_W]) * (1.0 / IDX_DIM) + EPS) * gki_ref[...]
    ki_ref[0] = rope(ki, cos1, sin1).astype(BF16)

    wit_ref[0] = (p[:, o:o + WI_W] * (IDX_HEADS ** -0.5)).T[:IDX_HEADS, :]


def _dsa_proj(x, g_mix, w_dsa, vbias, g_qa, g_ka, g_idx_k, cos_t, sin_t, bd, tt):
    B, S, D = x.shape
    nt = S // tt
    const = lambda b, s: (0, 0)
    return pl.pallas_call(
        _dsa_proj_kernel,
        grid=(B, nt),
        in_specs=[
            pl.BlockSpec((1, tt, D), lambda b, s: (b, s, 0)),
            pl.BlockSpec((1, D), const),
            pl.BlockSpec((D, DSA_COLS), const),
            pl.BlockSpec((1, VA_W), const),
            pl.BlockSpec((1, QA_W), const),
            pl.BlockSpec((1, KA_W), const),
            pl.BlockSpec((1, KI_W), const),
            pl.BlockSpec((tt, LANES), lambda b, s: (s, 0)),
            pl.BlockSpec((tt, LANES), lambda b, s: (s, 0)),
            pl.BlockSpec((QA_W, QA_W), const),
        ],
        out_specs=[
            pl.BlockSpec((1, QA_W, tt), lambda b, s: (b, 0, s)),
            pl.BlockSpec((1, QI_W, tt), lambda b, s: (b, 0, s)),
            pl.BlockSpec((1, tt, KA_W), lambda b, s: (b, s, 0)),
            pl.BlockSpec((1, 1, VA_W, tt), lambda b, s: (b, s, 0, 0)),
            pl.BlockSpec((1, tt, KI_W), lambda b, s: (b, s, 0)),
            pl.BlockSpec((1, IDX_HEADS, tt), lambda b, s: (b, 0, s)),
        ],
        out_shape=[
            jax.ShapeDtypeStruct((B, QA_W, S), BF16),
            jax.ShapeDtypeStruct((B, QI_W, S), BF16),
            jax.ShapeDtypeStruct((B, S, KA_W), BF16),
            jax.ShapeDtypeStruct((B, nt, VA_W, tt), BF16),
            jax.ShapeDtypeStruct((B, S, KI_W), BF16),
            jax.ShapeDtypeStruct((B, IDX_HEADS, S), F32),
        ],
        compiler_params=pltpu.CompilerParams(dimension_semantics=("arbitrary", "arbitrary"),
                                             vmem_limit_bytes=VMEM_LIMIT),
        name="dsa_proj",
    )(x, g_mix, w_dsa, vbias, g_qa, g_ka, g_idx_k, cos_t, sin_t, bd)


def _key_of(f):
    bits = lax.bitcast_convert_type(f, jnp.int32)
    return bits ^ ((bits >> 31) & jnp.int32(0x7FFFFFFF))


def _pack_high_halves(key):
    m = key.shape[0] // 2
    return lax.shift_right_logical(key[:m], 16) | (key[m:] & jnp.int32(-HALF_WORD))


def _pack_low_halves(key):
    m = key.shape[0] // 2
    return ((key[:m] & 0xFFFF) | (key[m:] << 16)) ^ jnp.int32(-HALF_BIAS * HALF_WORD + HALF_BIAS)


def _halves_of(words):
    return pltpu.bitcast(words, jnp.int16)


def _pipelined_chunks(nchunks, last_chunk, matmul, finish):
    matmul(0, 0)

    def straight(c0, n, last):
        for j in range(n):
            if j + 1 < n or not last:
                matmul(c0 + j + 1, (j + 1) % 2)
            finish(c0 + j, j % 2)

    def trip(i, carry):
        straight(CHUNKS_PER_TRIP * i, CHUNKS_PER_TRIP, False)
        return carry

    trips = last_chunk // CHUNKS_PER_TRIP
    lax.fori_loop(0, trips, trip, 0)
    for r in range(CHUNKS_PER_TRIP):
        @pl.when(last_chunk % CHUNKS_PER_TRIP == r)
        def _(r=r):
            straight(trips * CHUNKS_PER_TRIP, r + 1, True)


def _dsa_attn_kernel(qat_ref, qit_ref, wit_ref, k_ref, vt_ref, ki_ref, o_ref,
                     qa_s, qi_s, keys_s, khi_s, klo_s, zlo_s, buf0, buf1, m_s, acc_s, *, tq, tk, topk, idx_bits):
    qt = pl.program_id(1)
    q0 = qt * tq
    nchunks = q0 // tk + 1
    last_chunk = nchunks - 1
    hpg = N_HEADS_A // N_KV_HEADS_A
    gw = hpg * tq
    half_pairs = hpg // 2
    bufs = (buf0, buf1)
    dim = lax.broadcasted_iota(jnp.int32, (LANES, tq), 0)
    lo_half = dim < HEAD_DIM_A

    for j in range(IDX_HEADS // 2):
        pair = qit_ref[0, j * LANES:(j + 1) * LANES, :]
        qi_s[:, (2 * j) * tq:(2 * j + 1) * tq] = jnp.where(lo_half, pair, jnp.zeros_like(pair))
        qi_s[:, (2 * j + 1) * tq:(2 * j + 2) * tq] = jnp.where(lo_half, jnp.zeros_like(pair), pair)
    for g in range(N_KV_HEADS_A):
        for jj in range(half_pairs):
            pair = qat_ref[0, (g * half_pairs + jj) * LANES:(g * half_pairs + jj + 1) * LANES, :]
            qa_s[:, g * gw + jj * tq:g * gw + (jj + 1) * tq] = jnp.where(lo_half, pair, jnp.zeros_like(pair))
            qa_s[:, g * gw + (half_pairs + jj) * tq:g * gw + (half_pairs + jj + 1) * tq] = (
                jnp.where(lo_half, jnp.zeros_like(pair), pair))

    key_in_chunk = lax.broadcasted_iota(jnp.int32, (tk, tq), 0)
    q_pos = q0 + lax.broadcasted_iota(jnp.int32, (tk, tq), 1)

    def score_chunk(c, diagonal=False):
        r0 = pl.multiple_of(c * tk, tk)
        logits = _dot(ki_ref[0, pl.ds(r0, tk), :], qi_s[...])
        score = jnp.zeros((tk, tq), F32)
        for h in range(IDX_HEADS):
            score = score + wit_ref[0, h:h + 1, :] * jnp.maximum(logits[:, h * tq:(h + 1) * tq], 0.0)
        key = _key_of(score)
        if diagonal:
            key = jnp.where(key_in_chunk + c * tk <= q_pos, key, jnp.int32(INT_MIN))
        keys_s[c] = key
        khi_s[c] = _pack_high_halves(key)
        klo_s[c] = _pack_low_halves(key)

    def score_trip(i, carry):
        for j in range(CHUNKS_PER_TRIP):
            score_chunk(CHUNKS_PER_TRIP * i + j)
        return carry

    lax.fori_loop(0, last_chunk // CHUNKS_PER_TRIP, score_trip, 0)
    for r in range(CHUNKS_PER_TRIP):
        @pl.when(last_chunk % CHUNKS_PER_TRIP == r)
        def _(r=r):
            for j in range(r):
                score_chunk(last_chunk - r + j)
            score_chunk(last_chunk, diagonal=True)

    def count_keys(pred_fn):
        def one(c, acc):
            hit = pred_fn(keys_s[c], c * tk).astype(jnp.int32)
            return acc + jnp.sum(hit.reshape(tk // 8, 8, tq), axis=0)

        def two(i, acc):
            return one(2 * i + 1, one(2 * i, acc))

        part = lax.fori_loop(0, nchunks // 2, two, jnp.zeros((8, tq), jnp.int32))
        part = lax.cond(nchunks % 2 == 1, lambda acc: one(last_chunk, acc), lambda acc: acc, part)
        return jnp.sum(part, axis=0, keepdims=True)

    total = q0 + lax.broadcasted_iota(jnp.int32, (1, tq), 1) + 1
    has_tau = total >= topk

    def count_halves(src_s, cand, left):
        cand16 = _halves_of(jnp.broadcast_to((cand & 0xFFFF) | (cand << 16), (SUBLANES, tq)))

        def one(c, acc):
            half = _halves_of(src_s[c])
            hits = [jnp.where(half[j * 2 * SUBLANES:(j + 1) * 2 * SUBLANES, :] >= cand16, jnp.int16(1), jnp.int16(0))
                    for j in range(tk // (2 * SUBLANES))]
            while len(hits) > 1:
                hits = [a + b for a, b in zip(hits[::2], hits[1::2])]
            return acc + hits[0]

        def trip(i, acc):
            for j in range(CHUNKS_PER_TRIP):
                acc = one(CHUNKS_PER_TRIP * i + j, acc)
            return acc

        acc = lax.fori_loop(0, nchunks // CHUNKS_PER_TRIP, trip, jnp.zeros((2 * SUBLANES, tq), jnp.int16))
        for j in range(left):
            acc = one(nchunks - left + j, acc)
        words = pltpu.bitcast(acc, jnp.int32)
        return jnp.sum((words & 0xFFFF) + (words >> 16), axis=0, keepdims=True)

    def radix16(src_s, want, left):
        def step(i, state):
            v, cnt_ge, cnt_gt = state
            cand = v + (jnp.int32(1) << (15 - i))
            c = count_halves(src_s, cand, left)
            up = c >= want
            return jnp.where(up, cand, v), jnp.where(up, c, cnt_ge), jnp.where(up, cnt_gt, c)
        zero = jnp.zeros((1, tq), jnp.int32)
        return lax.fori_loop(0, 16, step, (jnp.full((1, tq), -HALF_BIAS, jnp.int32), zero, zero))

    def select(left):
        tau_hi, cnt_hi_ge, cnt_hi_gt = radix16(khi_s, topk, left)
        tau_hi16 = _halves_of(jnp.broadcast_to((tau_hi & 0xFFFF) | (tau_hi << 16), (SUBLANES, tq)))

        def low_of_bucket(c, carry):
            hi, lo = _halves_of(khi_s[c]), _halves_of(klo_s[c])
            for j in range(tk // (2 * SUBLANES)):
                rows = slice(j * 2 * SUBLANES, (j + 1) * 2 * SUBLANES)
                zlo_s[c, j * SUBLANES:(j + 1) * SUBLANES, :] = pltpu.bitcast(
                    jnp.where(hi[rows, :] == tau_hi16, lo[rows, :], jnp.int16(-HALF_BIAS)), jnp.int32)
            return carry

        lax.fori_loop(0, nchunks, low_of_bucket, 0)
        tau_lo, cnt_lo_ge, _ = radix16(zlo_s, topk - cnt_hi_gt, left)
        return tau_hi, cnt_hi_ge, cnt_hi_gt, tau_lo, cnt_lo_ge

    left = nchunks % CHUNKS_PER_TRIP
    tau_hi, cnt_hi_ge, cnt_hi_gt, tau_lo, cnt_lo_ge = lax.cond(
        left < 2,
        lambda: lax.cond(left == 0, lambda: select(0), lambda: select(1)),
        lambda: lax.cond(left == 2, lambda: select(2), lambda: select(3)))
    tau = (tau_hi << 16) | ((tau_lo + HALF_BIAS) & 0xFFFF)
    cnt = cnt_hi_gt + jnp.where(tau_lo == -HALF_BIAS, cnt_hi_ge - cnt_hi_gt, cnt_lo_ge)
    tau = jnp.where(has_tau, jnp.maximum(tau, jnp.int32(INT_MIN + 1)), jnp.int32(INT_MIN))

    tied = has_tau & (cnt > topk)

    def tie_cut():
        need = topk - count_keys(lambda k, _: k > tau)

        def idx_step(i, j):
            bit = jnp.int32(1) << (idx_bits - 1 - i)
            test = j + bit - 1
            f = count_keys(lambda k, base: (k == tau) & (base + key_in_chunk <= test))
            return jnp.where(f < need, j + bit, j)
        return lax.fori_loop(0, idx_bits, idx_step, jnp.zeros((1, tq), jnp.int32))

    jcut = lax.cond(jnp.max(jnp.where(tied, 1, 0)) > 0, tie_cut, lambda: jnp.zeros((1, tq), jnp.int32))
    jcut = jnp.where(tied, jcut, jnp.where(has_tau, jnp.int32(2 ** 30), jnp.int32(-1)))

    m_s[...] = jnp.full(m_s.shape, NEG_BIG, F32)
    acc_s[...] = jnp.zeros(acc_s.shape, F32)

    def attn_matmul(c, slot):
        r0 = pl.multiple_of(c * tk, tk)
        k = keys_s[c]
        kpos = key_in_chunk + c * tk
        bias = jnp.where(k > tau, 0.0, jnp.where(k == tau, jnp.where(kpos <= jcut, 0.0, NEG_BIG), NEG_BIG))
        bias4 = jnp.concatenate([bias] * hpg, axis=1)
        for g in range(N_KV_HEADS_A):
            bufs[slot][:, g * gw:(g + 1) * gw] = _dot(k_ref[0, pl.ds(r0, tk), g * LANES:(g + 1) * LANES],
                                                      qa_s[:, g * gw:(g + 1) * gw]) + bias4

    def attn_finish(c, slot):
        for g in range(N_KV_HEADS_A):
            m_old = m_s[g]
            m_new = jnp.maximum(m_old, jnp.max(bufs[slot][:, g * gw:(g + 1) * gw], axis=0, keepdims=True))
            p = jnp.exp2(bufs[slot][:, g * gw:(g + 1) * gw] - m_new)
            alpha = jnp.exp2(m_old - m_new)
            pv = _dot(vt_ref[0, c, g * LANES:(g + 1) * LANES, :], p.astype(BF16))
            acc_s[g] = alpha * acc_s[g] + pv
            m_s[g] = m_new

    _pipelined_chunks(nchunks, last_chunk, attn_matmul, attn_finish)

    for g in range(N_KV_HEADS_A):
        for jj in range(half_pairs):
            a_e = acc_s[g, :, jj * tq:(jj + 1) * tq]
            a_o = acc_s[g, :, (half_pairs + jj) * tq:(half_pairs + jj + 1) * tq]
            o_pair = jnp.concatenate([a_e[:HEAD_DIM_A] / a_e[HEAD_DIM_A:], a_o[:HEAD_DIM_A] / a_o[HEAD_DIM_A:]], axis=0)
            j = g * half_pairs + jj
            o_ref[0, :, j * LANES:(j + 1) * LANES] = o_pair.T.astype(BF16)


def _dsa_attn(qat, qit, wit, k, vt, ki, tq, topk):
    B, _, S = qat.shape
    nc, tk = vt.shape[1], vt.shape[3]
    hpg = N_HEADS_A // N_KV_HEADS_A
    idx_bits = max(1, int(np.ceil(np.log2(S))))
    assert IDX_HEADS == N_HEADS_A
    assert CHUNKS_PER_TRIP == 4
    kernel = functools.partial(_dsa_attn_kernel, tq=tq, tk=tk, topk=topk, idx_bits=idx_bits)
    return pl.pallas_call(
        kernel,
        grid=(B, S // tq),
        in_specs=[
            pl.BlockSpec((1, QA_W, tq), lambda b, q: (b, 0, q)),
            pl.BlockSpec((1, QI_W, tq), lambda b, q: (b, 0, q)),
            pl.BlockSpec((1, IDX_HEADS, tq), lambda b, q: (b, 0, q)),
            pl.BlockSpec((1, S, KA_W), lambda b, q: (b, 0, 0)),
            pl.BlockSpec((1, nc, VA_W, tk), lambda b, q: (b, 0, 0, 0)),
            pl.BlockSpec((1, S, KI_W), lambda b, q: (b, 0, 0)),
        ],
        out_specs=pl.BlockSpec((1, tq, QA_W), lambda b, q: (b, q, 0)),
        out_shape=jax.ShapeDtypeStruct((B, S, QA_W), BF16),
        scratch_shapes=[
            pltpu.VMEM((LANES, N_HEADS_A * tq), BF16),
            pltpu.VMEM((LANES, IDX_HEADS * tq), BF16),
            pltpu.VMEM((nc, tk, tq), jnp.int32),
            pltpu.VMEM((nc, tk // 2, tq), jnp.int32),
            pltpu.VMEM((nc, tk // 2, tq), jnp.int32),
            pltpu.VMEM((nc, tk // 2, tq), jnp.int32),
            pltpu.VMEM((tk, IDX_HEADS * tq), F32),
            pltpu.VMEM((tk, IDX_HEADS * tq), F32),
            pltpu.VMEM((N_KV_HEADS_A, 1, hpg * tq), F32),
            pltpu.VMEM((N_KV_HEADS_A, LANES, hpg * tq), F32),
        ],
        compiler_params=pltpu.CompilerParams(dimension_semantics=("arbitrary", "arbitrary"),
                                             vmem_limit_bytes=VMEM_LIMIT),
        name="dsa_attn",
    )(qat, qit, wit, k, vt, ki)


CONV_IN_W = 2 * CONV_CH
QM_W = MEM_HEADS * MEM_HEAD_DIM


def _mixer_kernel(x_ref, oa_ref, g_mix_ref, wc_ref, wq_ref, wg_ref, bg_ref, cw_ref, cb_ref, lng_ref, lnb_ref,
                  gqm_ref, kmt_ref, vm_ref, wa_ref, wb_ref, wm_ref, wo_ref, gffn_ref, wrh_ref, wrl_ref, br_ref,
                  x1_ref, h2_ref, comb_ref, ubuf, wbuf, *, tt):
    D = x_ref.shape[-1]

    @pl.when(pl.program_id(1) == 0)
    def _():
        ubuf[0:CONV_HALO, :] = jnp.zeros((CONV_HALO, CONV_CH), F32)

    x = x_ref[0]
    h = (x * lax.rsqrt(jnp.mean(x * x, axis=-1, keepdims=True) + EPS) * g_mix_ref[...]).astype(BF16)

    cin = _dot(h, wc_ref[...])
    u = cin[:, :CONV_CH] * _sigmoid(cin[:, CONV_CH:])

    ubuf[CONV_HALO:CONV_HALO + tt, :] = u
    y = jnp.zeros((tt, CONV_CH), F32) + cb_ref[...]
    first = CONV_HALO - (CONV_WIDTH - 1)
    for b in range(SUBLANES):
        taps = [j for j in range(CONV_WIDTH) if (first + j) % SUBLANES == b]
        if not taps:
            continue
        base = first + taps[0]
        if b == 0:
            window, w0 = ubuf, base
        else:
            span = taps[-1] - taps[0] + tt
            wbuf[0:span, :] = ubuf[base:base + span, :]
            window, w0 = wbuf, 0
        for j in taps:
            off = w0 + j - taps[0]
            y = y + cw_ref[j:j + 1, :] * window[off:off + tt, :]
    halo = ubuf[tt:tt + CONV_HALO, :]
    ubuf[0:CONV_HALO, :] = halo
    mu = jnp.mean(y, axis=-1, keepdims=True)
    yc = y - mu
    var = jnp.mean(yc * yc, axis=-1, keepdims=True)
    yn = yc * lax.rsqrt(var + EPS) * lng_ref[...] + lnb_ref[...]
    o_b = (yn * _sigmoid(yn)).astype(BF16)

    qm = _dot(h, wq_ref[...])
    o_m = []
    for hh in range(MEM_HEADS):
        q = qm[:, hh * MEM_HEAD_DIM:(hh + 1) * MEM_HEAD_DIM]
        q = q * lax.rsqrt(jnp.mean(q * q, axis=-1, keepdims=True) + EPS) * gqm_ref[...] * (MEM_HEAD_DIM ** -0.5)
        s = _dot(q.astype(BF16), kmt_ref[0, hh])
        p = jnp.exp(s - jnp.max(s, axis=-1, keepdims=True))
        l = jnp.sum(p, axis=-1, keepdims=True)
        o_m.append((_dot(p.astype(BF16), vm_ref[0, hh]) / l).astype(BF16))
    o_m = jnp.concatenate(o_m, axis=-1)

    gates = 0.5 * jnp.tanh(_dot(h, wg_ref[...]) + bg_ref[...]) + 0.5
    merged = (gates[:, :D] * _dot(oa_ref[0], wa_ref[...])
              + gates[:, D:2 * D] * _dot(o_b, wb_ref[...])
              + gates[:, 2 * D:] * _dot(o_m, wm_ref[...]))
    x1 = x + _dot(merged.astype(BF16), wo_ref[...])
    x1_ref[0] = x1

    h2 = x1 * lax.rsqrt(jnp.mean(x1 * x1, axis=-1, keepdims=True) + EPS) * gffn_ref[...]
    h2_ref[0] = h2.astype(BF16)
    hi, lo = _split_bf16(h2)
    logits = _dot(hi, wrh_ref[...]) + _dot(hi, wrl_ref[...]) + _dot(lo, wrh_ref[...]) + br_ref[...]
    lane = lax.broadcasted_iota(jnp.int32, logits.shape, 1)
    ninf = jnp.float32(-jnp.inf)
    big = jnp.int32(1 << 20)
    is_g = (lane >= N_EXPERTS) & (lane < N_EXPERTS + N_GROUPS)
    gl = jnp.where(is_g, logits, ninf)
    gmax = jnp.max(gl, axis=-1, keepdims=True)
    gsel = jnp.min(jnp.where(gl == gmax, lane, big), axis=-1, keepdims=True) - N_EXPERTS
    p_sel = 1.0 / jnp.sum(jnp.where(is_g, jnp.exp(gl - gmax), 0.0), axis=-1, keepdims=True)
    in_grp = (lane < N_EXPERTS) & ((lane >> 3) == gsel)
    el = jnp.where(in_grp, logits, ninf)
    v1 = jnp.max(el, axis=-1, keepdims=True)
    i1 = jnp.min(jnp.where(el == v1, lane, big), axis=-1, keepdims=True)
    el2 = jnp.where(lane == i1, ninf, el)
    v2 = jnp.max(el2, axis=-1, keepdims=True)
    i2 = jnp.min(jnp.where(el2 == v2, lane, big), axis=-1, keepdims=True)
    e2 = jnp.exp(v2 - v1)
    den = 1.0 + e2
    comb_ref[0] = jnp.where(lane == i1, (1.0 / den) * p_sel,
                            jnp.where(lane == i2, (e2 / den) * p_sel,
                                      jnp.where(lane == N_EXPERTS, gsel.astype(F32), 0.0)))


def _mixer(x, oa, g_mix, wc, wq, wg, bg, cw, cb, lng, lnb, gqm, kmt, vm, wa, wb, wm, wo, gffn, wrh, wrl, br, tt):
    B, S, D = x.shape
    M = kmt.shape[-1]
    const = lambda b, s: (0, 0)
    tile = lambda w: pl.BlockSpec((1, tt, w), lambda b, s: (b, s, 0))
    full = lambda a: pl.BlockSpec(a.shape, const)
    return pl.pallas_call(
        functools.partial(_mixer_kernel, tt=tt),
        grid=(B, S // tt),
        in_specs=[
            tile(D), tile(QA_W), full(g_mix), full(wc), full(wq), full(wg), full(bg), full(cw), full(cb),
            full(lng), full(lnb), full(gqm),
            pl.BlockSpec((1, MEM_HEADS, MEM_HEAD_DIM, M), lambda b, s: (b, 0, 0, 0)),
            pl.BlockSpec((1, MEM_HEADS, M, MEM_HEAD_DIM), lambda b, s: (b, 0, 0, 0)),
            full(wa), full(wb), full(wm), full(wo), full(gffn), full(wrh), full(wrl), full(br),
        ],
        out_specs=[tile(D), tile(D), tile(LANES)],
        out_shape=[
            jax.ShapeDtypeStruct((B, S, D), F32),
            jax.ShapeDtypeStruct((B, S, D), BF16),
            jax.ShapeDtypeStruct((B, S, LANES), F32),
        ],
        scratch_shapes=[pltpu.VMEM((CONV_HALO + tt, CONV_CH), F32),
                        pltpu.VMEM((CONV_HALO + tt, CONV_CH), F32)],
        compiler_params=pltpu.CompilerParams(dimension_semantics=("arbitrary", "arbitrary"),
                                             vmem_limit_bytes=VMEM_LIMIT),
        name="mixer",
    )(x, oa, g_mix, wc, wq, wg, bg, cw, cb, lng, lnb, gqm, kmt, vm, wa, wb, wm, wo, gffn, wrh, wrl, br)


MOE_BLK = 256
MOE_EXPERTS_PER_STEP = 4


def _loop_in_pairs(n, body):
    def two(i, carry):
        body(2 * i)
        body(2 * i + 1)
        return carry

    lax.fori_loop(0, n // 2, two, 0)

    @pl.when(n % 2 == 1)
    def _():
        body(n - 1)


def _moe_kernel(h2_ref, x1_ref, comb_ref, ltri_ref, wup_ref, wdn_ref, o_ref,
                hs_s, combs_s, ys_s, pt_s, seg_s, *, tm, rows):
    n = pl.program_id(1)
    lane = lax.broadcasted_iota(jnp.int32, (tm, LANES), 1)

    @pl.when(n == 0)
    def _():
        comb = comb_ref[...]
        gid = comb[:, N_EXPERTS:N_EXPERTS + 1].astype(jnp.int32)
        member = lane == gid
        onehot = jnp.where(member, 1.0, 0.0)
        before = _dot(ltri_ref[...], onehot.astype(BF16))
        rank = jnp.sum(jnp.where(member, before, 0.0), axis=-1, keepdims=True)
        count = jnp.sum(onehot, axis=0, keepdims=True)
        lane1 = lax.broadcasted_iota(jnp.int32, (1, LANES), 1)
        start_blk = jnp.int32(0)
        start_row = jnp.zeros((1, LANES), F32)
        for g in range(N_GROUPS):
            cnt_g = jnp.sum(jnp.where(lane1 == g, count, 0.0)).astype(jnp.int32)
            nblk_g = (cnt_g + (MOE_BLK - 1)) // MOE_BLK
            seg_s[g] = start_blk
            seg_s[N_GROUPS + g] = nblk_g
            start_row = jnp.where(lane1 == g, (start_blk * MOE_BLK).astype(F32), start_row)
            start_blk = start_blk + nblk_g
        dest = jnp.sum(jnp.where(member, start_row, 0.0), axis=-1, keepdims=True) + rank
        seg_s[2 * N_GROUPS] = start_blk
        dest_i = dest.astype(jnp.int32)
        dest_row = jnp.broadcast_to(dest, (tm, LANES)).T[0:1, :].astype(jnp.int32)
        c_hi, c_lo = _split_bf16(comb)
        col = lax.broadcasted_iota(jnp.int32, (tm, MOE_BLK), 1)
        row = lax.broadcasted_iota(jnp.int32, (MOE_BLK, tm), 0)

        def permute_block(b):
            r0 = pl.multiple_of(b * MOE_BLK, MOE_BLK)
            pt_s[b] = jnp.where(col + r0 == dest_i, 1.0, 0.0).astype(BF16)
            perm = jnp.where(row + r0 == dest_row, 1.0, 0.0).astype(BF16)
            hs_s[pl.ds(r0, MOE_BLK), :] = _dot(perm, h2_ref[...]).astype(BF16)
            combs_s[pl.ds(r0, MOE_BLK), :] = _dot(perm, c_hi) + _dot(perm, c_lo)
            ys_s[pl.ds(r0, MOE_BLK), :] = jnp.zeros((MOE_BLK, ys_s.shape[1]), F32)

        _loop_in_pairs(start_blk, permute_block)

    g = (n * MOE_EXPERTS_PER_STEP) // EXPERTS_PER_GROUP
    first_blk = seg_s[g]
    lane_b = lax.broadcasted_iota(jnp.int32, (MOE_BLK, LANES), 1)

    def expert_block(b):
        r0 = pl.multiple_of((first_blk + b) * MOE_BLK, MOE_BLK)
        hb = hs_s[pl.ds(r0, MOE_BLK), :]
        cb = combs_s[pl.ds(r0, MOE_BLK), :]
        acts = []
        for i in range(MOE_EXPERTS_PER_STEP):
            up = _dot(hb, wup_ref[i])
            a = up[:, :D_FF_EXPERT]
            c = jnp.sum(jnp.where(lane_b == n * MOE_EXPERTS_PER_STEP + i, cb, 0.0), axis=-1, keepdims=True)
            acts.append((a * _sigmoid(a) * up[:, D_FF_EXPERT:] * c).astype(BF16))
        w_dn = wdn_ref[...].reshape(MOE_EXPERTS_PER_STEP * D_FF_EXPERT, wdn_ref.shape[-1])
        ys_s[pl.ds(r0, MOE_BLK), :] += _dot(jnp.concatenate(acts, axis=-1), w_dn)

    _loop_in_pairs(seg_s[N_GROUPS + g], expert_block)

    @pl.when(n == N_EXPERTS // MOE_EXPERTS_PER_STEP - 1)
    def _():
        o_ref[...] = x1_ref[...]

        def token_order(b):
            r0 = pl.multiple_of(b * MOE_BLK, MOE_BLK)
            return _dot(pt_s[b], ys_s[pl.ds(r0, MOE_BLK), :].astype(BF16))

        def unpermute_pair(i, carry):
            o_ref[...] += token_order(2 * i) + token_order(2 * i + 1)
            return carry

        used = seg_s[2 * N_GROUPS]
        lax.fori_loop(0, used // 2, unpermute_pair, 0)

        @pl.when(used % 2 == 1)
        def _():
            o_ref[...] += token_order(used - 1)


def _moe(h2, x1, comb, w_up, w_down, tm):
    T, D = h2.shape
    rows = tm + N_GROUPS * MOE_BLK
    ltri = jnp.asarray(np.tril(np.ones((tm, tm), np.float32), -1), BF16)
    return pl.pallas_call(
        functools.partial(_moe_kernel, tm=tm, rows=rows),
        grid=(T // tm, N_EXPERTS // MOE_EXPERTS_PER_STEP),
        in_specs=[
            pl.BlockSpec((tm, D), lambda t, n: (t, 0)),
            pl.BlockSpec((tm, D), lambda t, n: (t, 0)),
            pl.BlockSpec((tm, LANES), lambda t, n: (t, 0)),
            pl.BlockSpec((tm, tm), lambda t, n: (0, 0)),
            pl.BlockSpec((MOE_EXPERTS_PER_STEP, D, 2 * D_FF_EXPERT), lambda t, n: (n, 0, 0)),
            pl.BlockSpec((MOE_EXPERTS_PER_STEP, D_FF_EXPERT, D), lambda t, n: (n, 0, 0)),
        ],
        out_specs=pl.BlockSpec((tm, D), lambda t, n: (t, 0)),
        out_shape=jax.ShapeDtypeStruct((T, D), F32),
        scratch_shapes=[
            pltpu.VMEM((rows, D), BF16),
            pltpu.VMEM((rows, LANES), F32),
            pltpu.VMEM((rows, D), F32),
            pltpu.VMEM((rows // MOE_BLK, tm, MOE_BLK), BF16),
            pltpu.SMEM((2 * N_GROUPS + 1,), jnp.int32),
        ],
        compiler_params=pltpu.CompilerParams(dimension_semantics=("arbitrary", "arbitrary"),
                                             vmem_limit_bytes=VMEM_LIMIT),
        name="moe",
    )(h2, x1, comb, ltri, w_up, w_down)


def _pick_tile(n, pref):
    t = min(n, pref)
    assert n % t == 0, (n, t)
    return t


def _rope_tables(S):
    half = HEAD_DIM_A // 2
    inv = ROPE_THETA ** (-jnp.arange(half, dtype=F32) / half)
    ang = jnp.arange(S, dtype=jnp.int32).astype(F32)[:, None] * inv[None, :]
    cos, sin = jnp.cos(ang), jnp.sin(ang)
    cos_t = jnp.concatenate([cos, cos, cos, cos], axis=-1)
    sin_t = jnp.concatenate([-sin, sin, -sin, sin], axis=-1)
    return cos_t, sin_t


def _layer(x, mem, g_mix, w_in, b_gate, g_qa, g_ka, g_idx_k, conv_w, conv_b, ln_g, ln_b,
           g_mem, w_mem_kv, g_qm, g_km, w_br_a, w_br_b, w_br_m, w_o, g_ffn,
           w_rg, b_rg, w_re, b_re, w_up, w_down):
    B, S, D = x.shape
    topk = min(TOPK_MAX, S // 4)
    tk = _pick_tile(S, 512)
    tq = _pick_tile(tk, 128)
    tt = _pick_tile(S, 512)
    tm = _pick_tile(B * S, 1024)

    sizes = (QA_W, N_KV_HEADS_A * HEAD_DIM_A, N_KV_HEADS_A * HEAD_DIM_A, QI_W, IDX_DIM, IDX_HEADS,
             CONV_IN_W, QM_W, N_BRANCHES * D)
    offs = np.concatenate([[0], np.cumsum(sizes)])
    col = lambda i: w_in[:, offs[i]:offs[i + 1]]
    w_qa, w_ka, w_va, w_qi, w_ki, w_wi, w_conv, w_qm, w_gate = (col(i) for i in range(9))
    hd = HEAD_DIM_A
    zeros_hd = jnp.zeros((D, hd), F32)
    ka_dup = [w_ka[:, g * hd:(g + 1) * hd] for g in range(N_KV_HEADS_A) for _ in range(2)]
    va_aug = [w for g in range(N_KV_HEADS_A) for w in (w_va[:, g * hd:(g + 1) * hd], zeros_hd)]
    w_dsa = jnp.concatenate(
        [w_qa, w_qi] + ka_dup + va_aug + [w_ki, w_ki, w_wi, jnp.zeros((D, WI_W - IDX_HEADS), F32)], axis=1).astype(BF16)
    vbias = jnp.tile(jnp.concatenate([jnp.zeros((hd,), F32), jnp.ones((hd,), F32)]), N_KV_HEADS_A).reshape(1, VA_W)
    seg = np.arange(QA_W) // hd
    bd = jnp.asarray(seg[:, None] == seg[None, :], BF16)
    cos_t, sin_t = _rope_tables(S)

    kmt, vm = _mem_kv(mem, g_mem, w_mem_kv, g_km)
    qat, qit, k, vt, ki, wit = _dsa_proj(
        x, g_mix.reshape(1, D), w_dsa, vbias,
        jnp.tile(g_qa, N_HEADS_A).reshape(1, QA_W), jnp.tile(g_ka, KA_W // hd).reshape(1, KA_W),
        jnp.tile(g_idx_k, KI_W // IDX_DIM).reshape(1, KI_W), cos_t, sin_t, bd, tk)
    o_a = _dsa_attn(qat, qit, wit, k, vt, ki, tq, topk)

    w_r = jnp.concatenate([w_re, w_rg, jnp.zeros((D, LANES - N_EXPERTS - N_GROUPS), F32)], axis=1)
    b_r = jnp.concatenate([b_re, b_rg, jnp.zeros((LANES - N_EXPERTS - N_GROUPS,), F32)]).reshape(1, LANES)
    w_r_hi = w_r.astype(BF16)
    w_r_lo = (w_r - w_r_hi.astype(F32)).astype(BF16)
    x1, h2, comb = _mixer(
        x, o_a, g_mix.reshape(1, D), w_conv.astype(BF16), w_qm.astype(BF16), (0.5 * w_gate).astype(BF16),
        (0.5 * b_gate).reshape(1, -1), conv_w.reshape(CONV_WIDTH, CONV_CH), conv_b.reshape(1, CONV_CH),
        ln_g.reshape(1, CONV_CH), ln_b.reshape(1, CONV_CH), g_qm.reshape(1, MEM_HEAD_DIM), kmt, vm,
        w_br_a.astype(BF16), w_br_b.astype(BF16), w_br_m.astype(BF16), w_o.astype(BF16),
        g_ffn.reshape(1, D), w_r_hi, w_r_lo, b_r, tt)

    out = _moe(h2.reshape(B * S, D), x1.reshape(B * S, D), comb.reshape(B * S, LANES),
               w_up.astype(BF16), w_down.astype(BF16), tm)
    return out.reshape(B, S, D)


def kernel(x, mem, g_mix, w_in, b_gate, g_qa, g_ka, g_idx_k, conv_w, conv_b, ln_g, ln_b, g_mem, w_mem_kv, g_qm, g_km,
           w_br_a, w_br_b, w_br_m, w_o, g_ffn, w_rg, b_rg, w_re, b_re, w_up, w_down):
    params = (g_mix, w_in, b_gate, g_qa, g_ka, g_idx_k, conv_w, conv_b, ln_g, ln_b, g_mem, w_mem_kv, g_qm, g_km,
              w_br_a, w_br_b, w_br_m, w_o, g_ffn, w_rg, b_rg, w_re, b_re, w_up, w_down)
    for l in range(g_mix.shape[0]):
        x = _layer(x, mem, *(p[l] for p in params))
    return x
```
